```python
import jax, jax.numpy as jnp
from jax import lax
import numpy as np

D_MODEL = 1024
BATCH = 8
SEQ = 2048
DEPTH = 1

MEM_LEN = 256
D_MIX = D_MODEL
GLA_WIDTH = D_MIX // 2
GLA_HEADS = 4
GLA_DV = GLA_WIDTH // GLA_HEADS
GLA_DK = GLA_DV // 2
GLA_KEY_WIDTH = GLA_HEADS * GLA_DK
GATE_RANK = 16
GATE_TAU = 16.0
CHUNK = 64
CONV_WIDTH = D_MIX - GLA_WIDTH
CONV_KERNEL = 31
IN_SIZES = (GLA_KEY_WIDTH, GLA_KEY_WIDTH, GLA_WIDTH, GLA_WIDTH, GATE_RANK, GATE_RANK, 2 * CONV_WIDTH)
IN_WIDTH = sum(IN_SIZES)
XATTN_HEADS = 4
XATTN_DH = D_MODEL // XATTN_HEADS
N_EXPERTS = 32
TOP_K = 4
D_FF_EXPERT = D_MODEL
SWIGLU_ALPHA = 1.702
SWIGLU_LIMIT = 7.0
MOE_BLOCK = 128
RMS_EPS = 1e-6
LN_EPS = 1e-5

kernel_name = "hybrid_gla_conformer_moe_encoder"


def rms_norm(x, g):
    xf = x.astype(jnp.float32)
    y = xf * lax.rsqrt(jnp.mean(xf * xf, axis=-1, keepdims=True) + RMS_EPS)
    return (y * g.astype(jnp.float32)).astype(x.dtype)


def layer_norm(x, g, b):
    xf = x.astype(jnp.float32)
    mu = jnp.mean(xf, axis=-1, keepdims=True)
    var = jnp.mean(jnp.square(xf - mu), axis=-1, keepdims=True)
    y = (xf - mu) * lax.rsqrt(var + LN_EPS)
    return (y * g.astype(jnp.float32) + b.astype(jnp.float32)).astype(x.dtype)


def gla_direction(q, k, v, log_a, strict):
    B, H, S, dk = q.shape
    dv = v.shape[-1]
    n = S // CHUNK
    qc = q.astype(jnp.float32).reshape(B, H, n, CHUNK, dk)
    kc = k.astype(jnp.float32).reshape(B, H, n, CHUNK, dk)
    vc = v.astype(jnp.float32).reshape(B, H, n, CHUNK, dv)
    bcum = jnp.cumsum(log_a.astype(jnp.float32).reshape(B, H, n, CHUNK, dk), axis=-2)
    b_last = bcum[..., -1:, :]
    idx = jnp.arange(CHUNK)
    mask = (idx[:, None] > idx[None, :]) if strict else (idx[:, None] >= idx[None, :])
    diff = bcum[..., :, None, :] - bcum[..., None, :, :]
    decay = jnp.exp(jnp.where(mask[:, :, None], diff, -jnp.inf))
    scores = jnp.einsum('bhnid,bhnjd,bhnijd->bhnij', qc, kc, decay)
    o_intra = jnp.einsum('bhnij,bhnjv->bhniv', scores, vc)
    u = jnp.einsum('bhncd,bhncv->bhndv', kc * jnp.exp(b_last - bcum), vc)
    a_chunk = jnp.exp(b_last[..., 0, :])

    def step(state, inp):
        a, uc = inp
        return a[..., None] * state + uc, state

    init = jnp.zeros((B, H, dk, dv), jnp.float32)
    _, s_in = lax.scan(step, init, (jnp.moveaxis(a_chunk, 2, 0), jnp.moveaxis(u, 2, 0)))
    s_in = jnp.moveaxis(s_in, 0, 2)
    o_inter = jnp.einsum('bhncd,bhndv->bhncv', qc * jnp.exp(bcum), s_in)
    return (o_intra + o_inter).reshape(B, H, S, dv)


def gla_group(q, k, v, r, gf, gb, w2_f, b_f, w2_b, b_b, norm_g):
    B, S, _ = q.shape

    def heads(t, d):
        return t.reshape(B, S, GLA_HEADS, d).transpose(0, 2, 1, 3)

    qh = heads(q, GLA_DK) * (GLA_DK ** -0.5)
    kh = heads(k, GLA_DK)
    vh = heads(v, GLA_DV)
    la_f = heads(jax.nn.log_sigmoid((gf @ w2_f + b_f).astype(jnp.float32)) / GATE_TAU, GLA_DK)
    la_b = heads(jax.nn.log_sigmoid((gb @ w2_b + b_b).astype(jnp.float32)) / GATE_TAU, GLA_DK)
    flip = lambda t: t[:, :, ::-1]
    o_f = gla_direction(qh, kh, vh, la_f, strict=False)
    o_b = flip(gla_direction(flip(qh), flip(kh), flip(vh), flip(la_b), strict=True))
    o = rms_norm(o_f + o_b, norm_g)
    o = o.transpose(0, 2, 1, 3).reshape(B, S, GLA_WIDTH)
    return (o * jax.nn.silu(r.astype(jnp.float32))).astype(q.dtype)


def conv_group(glu_in, conv_w, conv_b, ln_g, ln_b):
    a, g = jnp.split(glu_in, 2, axis=-1)
    u = a * jax.nn.sigmoid(g)
    pad = (CONV_KERNEL - 1) // 2
    y = lax.conv_general_dilated(
        u, conv_w[:, None, :].astype(u.dtype), window_strides=(1,), padding=[(pad, pad)],
        dimension_numbers=('NWC', 'WIO', 'NWC'), feature_group_count=CONV_WIDTH) + conv_b
    return jax.nn.silu(layer_norm(y, ln_g, ln_b))


def memory_cross_attention(h, m, wq, wk, wv, wo):
    B, S, _ = h.shape
    M = m.shape[1]
    q = (h @ wq).reshape(B, S, XATTN_HEADS, XATTN_DH)
    k = (m @ wk).reshape(B, M, XATTN_HEADS, XATTN_DH)
    v = (m @ wv).reshape(B, M, XATTN_HEADS, XATTN_DH)
    s = jnp.einsum('bqhd,bkhd->bhqk', q, k).astype(jnp.float32) * (XATTN_DH ** -0.5)
    p = jax.nn.softmax(s, axis=-1).astype(v.dtype)
    o = jnp.einsum('bhqk,bkhd->bqhd', p, v).reshape(B, S, D_MODEL)
    return o @ wo


def moe_ffn(h, router_w, router_b, w_up, b_up, w_down, b_down):
    B, S, D = h.shape
    T = B * S
    ht = h.reshape(T, D)
    logits = (ht @ router_w + router_b).astype(jnp.float32)
    top_vals, top_idx = lax.top_k(logits, TOP_K)
    gates = jax.nn.softmax(top_vals, axis=-1)
    A = T * TOP_K
    flat_e = top_idx.reshape(A)
    flat_tok = jnp.arange(A, dtype=jnp.int32) // TOP_K
    order = jnp.argsort(flat_e, stable=True)
    sorted_e = flat_e[order]
    sorted_tok = flat_tok[order]
    counts = jnp.bincount(flat_e, length=N_EXPERTS)
    padded = ((counts + MOE_BLOCK - 1) // MOE_BLOCK) * MOE_BLOCK
    starts = jnp.cumsum(counts) - counts
    pends = jnp.cumsum(padded)
    pstarts = pends - padded
    rank = jnp.arange(A, dtype=jnp.int32) - starts[sorted_e]
    dest = pstarts[sorted_e] + rank
    n_blocks = (A + N_EXPERTS * (MOE_BLOCK - 1) + MOE_BLOCK - 1) // MOE_BLOCK
    n_rows = n_blocks * MOE_BLOCK
    xs = jnp.zeros((n_rows, D), ht.dtype).at[dest].set(ht[sorted_tok])
    block_start = jnp.arange(n_blocks, dtype=pends.dtype) * MOE_BLOCK
    block_e = jnp.clip(jnp.searchsorted(pends, block_start, side='right'), 0, N_EXPERTS - 1)

    def expert_block(args):
        xb, e = args
        gu = xb @ w_up[e] + b_up[e]
        x_glu = jnp.minimum(gu[:, :D_FF_EXPERT], SWIGLU_LIMIT)
        x_lin = jnp.clip(gu[:, D_FF_EXPERT:], -SWIGLU_LIMIT, SWIGLU_LIMIT)
        hid = x_glu * jax.nn.sigmoid(SWIGLU_ALPHA * x_glu) * (x_lin + 1)
        return hid @ w_down[e] + b_down[e]

    ys = lax.map(expert_block, (xs.reshape(n_blocks, MOE_BLOCK, D), block_e)).reshape(n_rows, D)
    w_sorted = gates.reshape(A)[order].astype(ys.dtype)
    out = jax.ops.segment_sum(ys[dest] * w_sorted[:, None], sorted_tok, num_segments=T)
    return out.reshape(B, S, D).astype(h.dtype)


def setup_inputs(seed: int = 0) -> dict:
    key = jax.random.key(seed)
    ks = jax.random.split(key, 32)
    f32 = jnp.float32
    nrm = lambda k, shape, s: jax.random.normal(k, shape, f32) * s
    L = DEPTH
    return {
        "x": nrm(ks[0], (BATCH, SEQ, D_MODEL), 1.0),
        "mem": nrm(ks[1], (BATCH, MEM_LEN, D_MODEL), 1.0),
        "norm_mix_g": 1.0 + nrm(ks[2], (L, D_MODEL), 0.02),
        "w_in": nrm(ks[3], (L, D_MODEL, IN_WIDTH), D_MODEL ** -0.5),
        "gate_w2_fwd": nrm(ks[4], (L, GATE_RANK, GLA_KEY_WIDTH), GATE_RANK ** -0.5),
        "gate_b_fwd": nrm(ks[5], (L, GLA_KEY_WIDTH), 0.1),
        "gate_w2_bwd": nrm(ks[6], (L, GATE_RANK, GLA_KEY_WIDTH), GATE_RANK ** -0.5),
        "gate_b_bwd": nrm(ks[7], (L, GLA_KEY_WIDTH), 0.1),
        "gla_norm_g": 1.0 + nrm(ks[8], (L, GLA_DV), 0.02),
        "conv_w": nrm(ks[9], (L, CONV_KERNEL, CONV_WIDTH), CONV_KERNEL ** -0.5),
        "conv_b": nrm(ks[10], (L, CONV_WIDTH), 0.02),
        "conv_ln_g": 1.0 + nrm(ks[11], (L, CONV_WIDTH), 0.02),
        "conv_ln_b": nrm(ks[12], (L, CONV_WIDTH), 0.02),
        "w_out": nrm(ks[13], (L, D_MIX, D_MODEL), D_MIX ** -0.5),
        "norm_xattn_g": 1.0 + nrm(ks[14], (L, D_MODEL), 0.02),
        "norm_mem_g": 1.0 + nrm(ks[15], (L, D_MODEL), 0.02),
        "xattn_wq": nrm(ks[16], (L, D_MODEL, D_MODEL), D_MODEL ** -0.5),
        "xattn_wk": nrm(ks[17], (L, D_MODEL, D_MODEL), D_MODEL ** -0.5),
        "xattn_wv": nrm(ks[18], (L, D_MODEL, D_MODEL), D_MODEL ** -0.5),
        "xattn_wo": nrm(ks[19], (L, D_MODEL, D_MODEL), D_MODEL ** -0.5),
        "norm_ffn_g": 1.0 + nrm(ks[20], (L, D_MODEL), 0.02),
        "router_w": nrm(ks[21], (L, D_MODEL, N_EXPERTS), D_MODEL ** -0.5),
        "router_b": nrm(ks[22], (L, N_EXPERTS), 0.01),
        "exp_w_up": nrm(ks[23], (L, N_EXPERTS, D_MODEL, 2 * D_FF_EXPERT), D_MODEL ** -0.5),
        "exp_b_up": nrm(ks[24], (L, N_EXPERTS, 2 * D_FF_EXPERT), 0.02),
        "exp_w_down": nrm(ks[25], (L, N_EXPERTS, D_FF_EXPERT, D_MODEL), D_FF_EXPERT ** -0.5),
        "exp_b_down": nrm(ks[26], (L, N_EXPERTS, D_MODEL), 0.02),
        "final_norm_g": 1.0 + nrm(ks[27], (D_MODEL,), 0.02),
    }


def reference(x, mem, norm_mix_g, w_in, gate_w2_fwd, gate_b_fwd, gate_w2_bwd, gate_b_bwd,
              gla_norm_g, conv_w, conv_b, conv_ln_g, conv_ln_b, w_out, norm_xattn_g,
              norm_mem_g, xattn_wq, xattn_wk, xattn_wv, xattn_wo, norm_ffn_g, router_w,
              router_b, exp_w_up, exp_b_up, exp_w_down, exp_b_down, final_norm_g):
    splits = []
    off = 0
    for size in IN_SIZES[:-1]:
        off += size
        splits.append(off)
    for l in range(DEPTH):
        h = rms_norm(x, norm_mix_g[l])
        q, k, v, r, gf, gb, glu = jnp.split(h @ w_in[l], splits, axis=-1)
        a_out = gla_group(q, k, v, r, gf, gb, gate_w2_fwd[l], gate_b_fwd[l],
                          gate_w2_bwd[l], gate_b_bwd[l], gla_norm_g[l])
        b_out = conv_group(glu, conv_w[l], conv_b[l], conv_ln_g[l], conv_ln_b[l])
        x = x + (jnp.concatenate([a_out, b_out], axis=-1) @ w_out[l]).astype(x.dtype)
        h = rms_norm(x, norm_xattn_g[l])
        m = rms_norm(mem, norm_mem_g[l])
        x = x + memory_cross_attention(h, m, xattn_wq[l], xattn_wk[l], xattn_wv[l],
                                       xattn_wo[l]).astype(x.dtype)
        h = rms_norm(x, norm_ffn_g[l])
        x = x + moe_ffn(h, router_w[l], router_b[l], exp_w_up[l], exp_b_up[l],
                        exp_w_down[l], exp_b_down[l]).astype(x.dtype)
    return rms_norm(x, final_norm_g)
```

```python
import functools

import numpy as np
import jax
import jax.numpy as jnp
from jax import lax
from jax.experimental import pallas as pl
from jax.experimental.pallas import tpu as pltpu

F32 = jnp.float32
BF16 = jnp.bfloat16

GLA_HEADS = 4
GLA_DK = 64
GLA_DV = 128
GLA_CHUNK = 64
GLA_TILE = 8
GATE_TAU = 16.0
CONV_KERNEL = 31
XATTN_HEADS = 4
TOP_K = 4
SWIGLU_ALPHA = 1.702
SWIGLU_LIMIT = 7.0
RMS_EPS = 1e-6
LN_EPS = 1e-5

LANES = 128
MOE_ROWS = 256
VMEM_LIMIT = 56 * 1024 * 1024


def _dot(a, b):
    return jnp.dot(a, b, preferred_element_type=F32)


def _dot_nt(a, b):
    return lax.dot_general(a, b, (((1,), (1,)), ((), ())), preferred_element_type=F32)


def _dot_tn(a, b):
    return lax.dot_general(a, b, (((0,), (0,)), ((), ())), preferred_element_type=F32)


def _rms(x, g):
    return x * lax.rsqrt(jnp.mean(x * x, axis=-1, keepdims=True) + RMS_EPS) * g


def _params(sem):
    return pltpu.CompilerParams(dimension_semantics=sem, vmem_limit_bytes=VMEM_LIMIT)


def _inproj_kernel(x_ref, g_ref, w_ref, w2_ref, b2_ref,
                   q_ref, k_ref, v_ref, gr_ref, la_ref, u_ref, *, kw, vw, cw):
    h = _rms(x_ref[...], g_ref[...]).astype(BF16)
    acc = _dot(h, w_ref[...])
    o = 0
    q_ref[...] = (acc[:, o:o + kw] * (GLA_DK ** -0.5)).astype(BF16); o += kw
    k_ref[...] = acc[:, o:o + kw].astype(BF16); o += kw
    v_ref[...] = acc[:, o:o + vw].astype(BF16); o += vw
    r = acc[:, o:o + vw]; o += vw
    gr_ref[...] = (r * jax.nn.sigmoid(r)).astype(BF16)
    a = acc[:, o:o + cw]; o += cw
    g = acc[:, o:o + cw]; o += cw
    u_ref[...] = a * jax.nn.sigmoid(g)
    z = _dot(acc[:, o:o + LANES].astype(BF16), w2_ref[...]) + b2_ref[...]
    la_ref[...] = (jnp.minimum(z, 0.0) - jnp.log1p(jnp.exp(-jnp.abs(z)))) * (1.0 / GATE_TAU)


def _inproj(x2d, g, w, w2, b2, kw, vw, cw, tm):
    T, D = x2d.shape
    W = w.shape[1]
    row = lambda n: pl.BlockSpec((tm, n), lambda i: (i, 0))
    full = lambda a: pl.BlockSpec(a.shape, lambda i: (0,) * a.ndim)
    return pl.pallas_call(
        functools.partial(_inproj_kernel, kw=kw, vw=vw, cw=cw),
        grid=(T // tm,),
        in_specs=[row(D), full(g), full(w), full(w2), full(b2)],
        out_specs=[row(kw), row(kw), row(vw), row(vw), row(2 * kw), row(cw)],
        out_shape=[jax.ShapeDtypeStruct((T, kw), BF16), jax.ShapeDtypeStruct((T, kw), BF16),
                   jax.ShapeDtypeStruct((T, vw), BF16), jax.ShapeDtypeStruct((T, vw), BF16),
                   jax.ShapeDtypeStruct((T, 2 * kw), F32), jax.ShapeDtypeStruct((T, cw), F32)],
        compiler_params=_params(("parallel",)),
        name="inproj",
    )(x2d, g, w, w2, b2)


def _gla_constants(bwd):
    C, H = GLA_CHUNK, GLA_HEADS
    t = np.arange(C)[:, None]
    s = np.arange(C)[None, :]
    if not bwd:
        blocks = [s <= t, s > t]
    else:
        blocks = [s >= t, s < t]
    lev_q, lev_k, lev_mask = [], [], []
    m = C // 2
    while m >= GLA_TILE:
        g = 2 * m
        pos = t % g
        start = t - pos
        second = pos >= m
        if not bwd:
            ref = start + m - 1
            blocks.append(np.where(second, (s > ref) & (s <= t), (s > t) & (s <= ref)))
            qrow, krow = second, ~second
            mask = (t // g == s // g) & ((t % g) >= m) & ((s % g) < m)
        else:
            mid = start + m
            blocks.append(np.where(~second, (s >= t) & (s < mid), (s >= mid) & (s < t)))
            qrow, krow = ~second, second
            mask = (t // g == s // g) & ((t % g) < m) & ((s % g) >= m)
        lev_q.append(np.broadcast_to(qrow, (C, H * GLA_DK)))
        lev_k.append(np.broadcast_to(krow, (C, H * GLA_DK)))
        lev_mask.append(np.tile(mask, (1, H)))
        m //= 2
    shifts = range(0, GLA_TILE) if not bwd else range(1, GLA_TILE)
    sh_mask = []
    for sh in shifts:
        j = t - sh if not bwd else t + sh
        sh_mask.append(np.tile((s == j) & (t // GLA_TILE == s // GLA_TILE), (1, H)))
    cmat = np.concatenate(blocks, axis=0).astype(np.float32)
    f = lambda xs: np.stack(xs).astype(np.float32)
    return cmat, f(lev_q), f(lev_k), f(lev_mask), f(sh_mask)


def _gla_head_masks():
    C, H = GLA_CHUNK, GLA_HEADS
    rk = np.arange(H * C)[:, None] // C
    hm_k = (rk == np.arange(H * GLA_DK)[None, :] // GLA_DK)
    hm_v = (rk == np.arange(H * GLA_DV)[None, :] // GLA_DV)
    bd = (np.arange(H * GLA_DV)[:, None] // GLA_DV == np.arange(H * GLA_DK)[None, :] // GLA_DK)
    e1 = (np.arange(H * GLA_DK)[:, None] // GLA_DK == np.arange(H * C)[None, :] // C)
    return (hm_k.astype(np.float32), hm_v.astype(np.float32), bd.astype(np.float32),
            e1.astype(np.float32))


def _gla_chunk(la, q, k, v, st_ref, cm_ref, lq_ref, lk_ref, lm_ref, sm_ref,
               hmk_ref, hmv_ref, bd_ref, e1_ref, bwd):
    C, H = GLA_CHUNK, GLA_HEADS
    cm = cm_ref[...]
    hi = la.astype(BF16)
    r1 = la - hi.astype(F32)
    mid = r1.astype(BF16)
    lo = (r1 - mid.astype(F32)).astype(BF16)
    x = _dot(cm, hi) + _dot(cm, mid) + _dot(cm, lo)
    e = jnp.exp(x)
    e_q, e_k = e[0:C], e[C:2 * C]
    if not bwd:
        cum, a_tot = x[0:C], e[C - 1:C]
    else:
        cum, a_tot = x[C:2 * C], e[0:1]

    st = st_ref[...]
    o = _dot_nt((q * e_q).astype(BF16), st.astype(BF16))
    u_t = _dot_tn(v, (k * e_k).astype(BF16))
    st_ref[...] = st * a_tot + u_t * bd_ref[...]

    hmk = hmk_ref[...]
    a = jnp.zeros((C, H * C), F32)
    for lv in range(lq_ref.shape[0]):
        em = e[(2 + lv) * C:(3 + lv) * C]
        qm = (q * (em * lq_ref[lv])).astype(BF16)
        km = k * (em * lk_ref[lv])
        kst = (jnp.concatenate([km] * H, axis=0) * hmk).astype(BF16)
        a = a + _dot_nt(qm, kst) * lm_ref[lv]
    ps = []
    for si in range(sm_ref.shape[0]):
        sh = si if not bwd else si + 1
        if sh == 0:
            ps.append((q * k).astype(BF16))
            continue
        amt = sh if not bwd else C - sh
        kr = pltpu.roll(k, amt, 0)
        cr = pltpu.roll(cum, amt, 0)
        d = (cum - cr) if not bwd else (cr - cum)
        ps.append((q * kr * jnp.exp(jnp.minimum(d, 0.0))).astype(BF16))
    sc = _dot(jnp.concatenate(ps, axis=0), e1_ref[...])
    for si in range(sm_ref.shape[0]):
        a = a + sc[si * C:(si + 1) * C] * sm_ref[si]
    vbd = jnp.concatenate([v] * H, axis=0) * hmv_ref[...]
    return o + _dot(a.astype(BF16), vbd)


def _gla_kernel(q_ref, k_ref, v_ref, la_ref, gr_ref, ng_ref,
                cmf_ref, lqf_ref, lkf_ref, lmf_ref, smf_ref,
                cmb_ref, lqb_ref, lkb_ref, lmb_ref, smb_ref,
                hmk_ref, hmv_ref, bd_ref, e1_ref,
                out_ref, of_ref, ob_ref, stf_ref, stb_ref):
    C = GLA_CHUNK
    S = q_ref.shape[0]
    kw = q_ref.shape[1]
    n = S // C
    stf_ref[...] = jnp.zeros_like(stf_ref)
    stb_ref[...] = jnp.zeros_like(stb_ref)

    def body(i, carry):
        rf = pl.ds(pl.multiple_of(i * C, C), C)
        rb = pl.ds(pl.multiple_of((n - 1 - i) * C, C), C)
        of_ref[rf, :] = _gla_chunk(
            la_ref[rf, 0:kw], q_ref[rf, :].astype(F32), k_ref[rf, :].astype(F32), v_ref[rf, :],
            stf_ref, cmf_ref, lqf_ref, lkf_ref, lmf_ref, smf_ref,
            hmk_ref, hmv_ref, bd_ref, e1_ref, False)
        ob_ref[rb, :] = _gla_chunk(
            la_ref[rb, kw:2 * kw], q_ref[rb, :].astype(F32), k_ref[rb, :].astype(F32), v_ref[rb, :],
            stb_ref, cmb_ref, lqb_ref, lkb_ref, lmb_ref, smb_ref,
            hmk_ref, hmv_ref, bd_ref, e1_ref, True)
        return carry

    lax.fori_loop(0, n, body, 0)

    rows = 256
    def fin(i, carry):
        r = pl.ds(pl.multiple_of(i * rows, rows), rows)
        o = of_ref[r, :] + ob_ref[r, :]
        ys = []
        for h in range(GLA_HEADS):
            oh = o[:, h * GLA_DV:(h + 1) * GLA_DV]
            ys.append(_rms(oh, ng_ref[...]))
        y = jnp.concatenate(ys, axis=1) * gr_ref[r, :].astype(F32)
        out_ref[r, :] = y.astype(BF16)
        return carry

    lax.fori_loop(0, S // rows, fin, 0)


def _gla(q, k, v, la, gr, ng, B, S):
    kw, vw = q.shape[1], v.shape[1]
    cf = _gla_constants(False)
    cb = _gla_constants(True)
    hm_k, hm_v, bd, e1 = _gla_head_masks()
    consts = [jnp.asarray(cf[0], BF16)] + [jnp.asarray(c) for c in cf[1:]] + \
             [jnp.asarray(cb[0], BF16)] + [jnp.asarray(c) for c in cb[1:]] + \
             [jnp.asarray(hm_k), jnp.asarray(hm_v, BF16), jnp.asarray(bd), jnp.asarray(e1, BF16)]
    seq = lambda n: pl.BlockSpec((S, n), lambda b: (b, 0))
    full = lambda a: pl.BlockSpec(a.shape, lambda b: (0,) * a.ndim)
    return pl.pallas_call(
        _gla_kernel,
        grid=(B,),
        in_specs=[seq(kw), seq(kw), seq(vw), seq(2 * kw), seq(vw), full(ng)] + [full(c) for c in consts],
        out_specs=seq(vw),
        out_shape=jax.ShapeDtypeStruct((B * S, vw), BF16),
        scratch_shapes=[pltpu.VMEM((S, vw), F32), pltpu.VMEM((S, vw), F32),
                        pltpu.VMEM((vw, kw), F32), pltpu.VMEM((vw, kw), F32)],
        compiler_params=_params(("parallel",)),
        name="gla",
    )(q, k, v, la, gr, ng, *consts)


def _conv_kernel(u_ref, w_ref, cb_ref, lg_ref, lb_ref, out_ref, *, tm, halo):
    S = u_ref.shape[0]
    j = pl.program_id(1)
    t0 = pl.multiple_of(j * tm, tm)
    top = u_ref[pl.ds(pl.multiple_of(jnp.maximum(t0 - halo, 0), halo), halo), :]
    bot = u_ref[pl.ds(pl.multiple_of(jnp.minimum(t0 + tm, S - halo), halo), halo), :]
    top = jnp.where(j > 0, top, 0.0)
    bot = jnp.where(j < pl.num_programs(1) - 1, bot, 0.0)
    win = jnp.concatenate([top, u_ref[pl.ds(t0, tm), :], bot], axis=0)
    rows = tm + 2 * halo
    pad = (CONV_KERNEL - 1) // 2
    w = w_ref[...]
    acc = jnp.zeros((tm, u_ref.shape[1]), F32)
    for r in range(8):
        offs = [k for k in range(CONV_KERNEL) if (halo - pad + k) % 8 == r]
        if not offs:
            continue
        xr = win if r == 0 else pltpu.roll(win, rows - r, 0)
        for k in offs:
            a0 = (halo - pad + k) - r
            acc = acc + xr[a0:a0 + tm, :] * w[k:k + 1, :]
    y = acc + cb_ref[...]
    mu = jnp.mean(y, axis=-1, keepdims=True)
    yc = y - mu
    var = jnp.mean(yc * yc, axis=-1, keepdims=True)
    z = yc * lax.rsqrt(var + LN_EPS) * lg_ref[...] + lb_ref[...]
    out_ref[...] = (z * jax.nn.sigmoid(z)).astype(BF16)


def _conv(u, w, cb, lg, lb, B, S, tm):
    cw = u.shape[1]
    halo = 16
    full = lambda a: pl.BlockSpec(a.shape, lambda b, j: (0,) * a.ndim)
    nj = S // tm
    return pl.pallas_call(
        functools.partial(_conv_kernel, tm=tm, halo=halo),
        grid=(B, nj),
        in_specs=[pl.BlockSpec((S, cw), lambda b, j: (b, 0)), full(w), full(cb), full(lg), full(lb)],
        out_specs=pl.BlockSpec((tm, cw), lambda b, j: (b * nj + j, 0)),
        out_shape=jax.ShapeDtypeStruct((B * S, cw), BF16),
        compiler_params=_params(("parallel", "parallel")),
        name="conv",
    )(u, w, cb, lg, lb)


def _memkv_kernel(m_ref, g_ref, wk_ref, wv_ref, k_ref, v_ref):
    m = _rms(m_ref[...], g_ref[...]).astype(BF16)
    k_ref[...] = _dot(m, wk_ref[...]).astype(BF16)
    v_ref[...] = _dot(m, wv_ref[...]).astype(BF16)


def _memkv(mem2d, g, wk, wv, M):
    R, D = mem2d.shape
    full = lambda a: pl.BlockSpec(a.shape, lambda i: (0,) * a.ndim)
    blk = pl.BlockSpec((M, D), lambda i: (i, 0))
    return pl.pallas_call(
        _memkv_kernel,
        grid=(R // M,),
        in_specs=[blk, full(g), full(wk), full(wv)],
        out_specs=[blk, blk],
        out_shape=[jax.ShapeDtypeStruct((R, D), BF16)] * 2,
        compiler_params=_params(("parallel",)),
        name="memkv",
    )(mem2d, g, wk, wv)


def _mid_kernel(x_ref, a_ref, b_ref, wout_ref, gx_ref, wq_ref, km_ref, vm_ref, wo_ref,
                gf_ref, rw_ref, rb_ref, x2_ref, h3_ref, idx_ref, gate_ref, *, n_exp):
    ab = jnp.concatenate([a_ref[...], b_ref[...]], axis=1)
    x1 = x_ref[...] + _dot(ab, wout_ref[...])
    q2 = _dot(_rms(x1, gx_ref[...]).astype(BF16), wq_ref[...])
    D = q2.shape[1]
    dh = D // XATTN_HEADS
    outs = []
    for h in range(XATTN_HEADS):
        sl = slice(h * dh, (h + 1) * dh)
        s = _dot_nt(q2[:, sl].astype(BF16), km_ref[:, sl]) * (dh ** -0.5)
        p = jnp.exp(s - jnp.max(s, axis=-1, keepdims=True))
        p = p / jnp.sum(p, axis=-1, keepdims=True)
        outs.append(_dot(p.astype(BF16), vm_ref[:, sl]))
    o = jnp.concatenate(outs, axis=1).astype(BF16)
    x2 = x1 + _dot(o, wo_ref[...])
    x2_ref[...] = x2
    h3 = _rms(x2, gf_ref[...])
    h3_ref[...] = h3
    lane = lax.broadcasted_iota(jnp.int32, (x2.shape[0], LANES), 1)
    logits = _dot(h3.astype(BF16), rw_ref[...]) + rb_ref[...]
    cur = jnp.where(lane < n_exp, logits, -jnp.inf)
    lane_f = lane.astype(F32)
    vals, idxs = [], []
    for _ in range(TOP_K):
        m = jnp.max(cur, axis=-1, keepdims=True)
        ik = jnp.min(jnp.where(cur == m, lane_f, float(LANES)), axis=-1, keepdims=True)
        vals.append(m)
        idxs.append(ik)
        cur = jnp.where(lane_f == ik, -jnp.inf, cur)
    es = [jnp.exp(vk - vals[0]) for vk in vals]
    den = es[0]
    for ek in es[1:]:
        den = den + ek
    idx_out = jnp.zeros(lane.shape, F32)
    gate_out = jnp.zeros(lane.shape, F32)
    for kk in range(TOP_K):
        idx_out = jnp.where(lane == kk, idxs[kk], idx_out)
        gate_out = jnp.where(lane == kk, es[kk] / den, gate_out)
    idx_ref[...] = idx_out.astype(jnp.int32)
    gate_ref[...] = gate_out


def _mid(x2d, a, b, wout, gx, wq, km, vm, wo, gf, rw, rb, S, M, n_exp, tm):
    T, D = x2d.shape
    half = a.shape[1]
    per_b = S // tm
    row = lambda n: pl.BlockSpec((tm, n), lambda i: (i, 0))
    full = lambda arr: pl.BlockSpec(arr.shape, lambda i: (0,) * arr.ndim)
    memb = pl.BlockSpec((M, D), lambda i: (i // per_b, 0))
    return pl.pallas_call(
        functools.partial(_mid_kernel, n_exp=n_exp),
        grid=(T // tm,),
        in_specs=[row(D), row(half), row(half), full(wout), full(gx), full(wq), memb, memb,
                  full(wo), full(gf), full(rw), full(rb)],
        out_specs=[row(D), row(D), row(LANES), row(LANES)],
        out_shape=[jax.ShapeDtypeStruct((T, D), F32), jax.ShapeDtypeStruct((T, D), F32),
                   jax.ShapeDtypeStruct((T, LANES), jnp.int32),
                   jax.ShapeDtypeStruct((T, LANES), F32)],
        compiler_params=_params(("parallel",)),
        name="mid",
    )(x2d, a, b, wout, gx, wq, km, vm, wo, gf, rw, rb)


def _onehot(idx_ref, kk, lane):
    return (lane == idx_ref[:, kk:kk + 1]).astype(F32)


def _rank_kernel(idx_ref, tri_ref, rank_ref, cnt_ref, carry_ref):
    i = pl.program_id(0)

    @pl.when(i == 0)
    def _():
        carry_ref[...] = jnp.zeros_like(carry_ref)

    tm = idx_ref.shape[0]
    lane = lax.broadcasted_iota(jnp.int32, (tm, LANES), 1)
    hots = [_onehot(idx_ref, kk, lane) for kk in range(TOP_K)]
    mask = hots[0]
    for hk in hots[1:]:
        mask = mask + hk
    incl = _dot(tri_ref[...], mask.astype(BF16))
    before = incl - mask + carry_ref[0:1, :]
    out = jnp.zeros((tm, LANES), F32)
    for kk in range(TOP_K):
        rk = jnp.sum(hots[kk] * before, axis=-1, keepdims=True)
        out = jnp.where(lane == kk, rk, out)
    rank_ref[...] = out.astype(jnp.int32)
    carry_ref[...] = carry_ref[...] + incl[tm - 1:tm, :]
    cnt_ref[...] = carry_ref[...]


def _rank(idx, tm):
    T = idx.shape[0]
    tri = jnp.asarray(np.tril(np.ones((tm, tm), np.float32)), BF16)
    row = pl.BlockSpec((tm, LANES), lambda i: (i, 0))
    return pl.pallas_call(
        _rank_kernel,
        grid=(T // tm,),
        in_specs=[row, pl.BlockSpec((tm, tm), lambda i: (0, 0))],
        out_specs=[row, pl.BlockSpec((8, LANES), lambda i: (0, 0))],
        out_shape=[jax.ShapeDtypeStruct((T, LANES), jnp.int32),
                   jax.ShapeDtypeStruct((8, LANES), F32)],
        scratch_shapes=[pltpu.VMEM((8, LANES), F32)],
        compiler_params=_params(("arbitrary",)),
        name="rank",
    )(idx, tri)


def _dest_kernel(idx_ref, rank_ref, pstart_ref, dest_ref):
    tm = idx_ref.shape[0]
    lane = lax.broadcasted_iota(jnp.int32, (tm, LANES), 1)
    out = jnp.zeros((tm, LANES), F32)
    for kk in range(TOP_K):
        base = jnp.sum(_onehot(idx_ref, kk, lane) * pstart_ref[0:1, :], axis=-1, keepdims=True)
        out = jnp.where(lane == kk, base, out)
    dest_ref[...] = out.astype(jnp.int32) + rank_ref[...]


def _dest(idx, rank, pstart, tm):
    T = idx.shape[0]
    row = pl.BlockSpec((tm, LANES), lambda i: (i, 0))
    return pl.pallas_call(
        _dest_kernel,
        grid=(T // tm,),
        in_specs=[row, row, pl.BlockSpec((8, LANES), lambda i: (0, 0))],
        out_specs=row,
        out_shape=jax.ShapeDtypeStruct((T, LANES), jnp.int32),
        compiler_params=_params(("parallel",)),
        name="dest",
    )(idx, rank, pstart)


def _row_copy(src_ref, src_row, dst_ref, dst_row, sem):
    return pltpu.make_async_copy(src_ref.at[pl.ds(src_row, 1)], dst_ref.at[pl.ds(dst_row, 1)], sem)


def _dispatch_kernel(dest_ref, h_ref, xs_in_ref, xs_ref, sem):
    del xs_in_ref
    tm = h_ref.shape[0]

    def issue(t, carry):
        for kk in range(TOP_K):
            _row_copy(h_ref, t, xs_ref, dest_ref[t * TOP_K + kk], sem).start()
        return carry

    lax.fori_loop(0, tm, issue, 0)

    def drain(t, carry):
        for kk in range(TOP_K):
            _row_copy(h_ref, 0, xs_ref, 0, sem).wait()
        return carry

    lax.fori_loop(0, tm, drain, 0)


def _dispatch(dest_flat, h3, n_rows, tm):
    T, D = h3.shape
    xs0 = jnp.zeros((n_rows, D), F32)
    return pl.pallas_call(
        _dispatch_kernel,
        grid=(T // tm,),
        in_specs=[pl.BlockSpec((tm * TOP_K,), lambda i: (i,), memory_space=pltpu.SMEM),
                  pl.BlockSpec((tm, D), lambda i: (i, 0)),
                  pl.BlockSpec(memory_space=pl.ANY)],
        out_specs=pl.BlockSpec(memory_space=pl.ANY),
        out_shape=jax.ShapeDtypeStruct((n_rows, D), F32),
        scratch_shapes=[pltpu.SemaphoreType.DMA(())],
        input_output_aliases={2: 0},
        compiler_params=_params(("arbitrary",)),
        name="dispatch",
    )(dest_flat, h3, xs0)


def _experts_kernel(be_ref, nb_ref, xs_ref, wu_ref, bu_ref, wd_ref, bd_ref, ys_ref):
    i = pl.program_id(0)

    @pl.when(i < nb_ref[0])
    def _():
        F = wd_ref.shape[1]
        gu = _dot(xs_ref[...].astype(BF16), wu_ref[0]) + bu_ref[0]
        x_glu = jnp.minimum(gu[:, :F], SWIGLU_LIMIT)
        x_lin = jnp.clip(gu[:, F:], -SWIGLU_LIMIT, SWIGLU_LIMIT)
        hid = x_glu * jax.nn.sigmoid(SWIGLU_ALPHA * x_glu) * (x_lin + 1.0)
        ys_ref[...] = _dot(hid.astype(BF16), wd_ref[0]) + bd_ref[0]

    @pl.when(i >= nb_ref[0])
    def _():
        ys_ref[...] = jnp.zeros_like(ys_ref)


def _experts(block_e, n_used, xs, wu, bu, wd, bd):
    n_rows, D = xs.shape
    E, _, F2 = wu.shape
    F = wd.shape[1]
    nb = n_rows // MOE_ROWS
    grid_spec = pltpu.PrefetchScalarGridSpec(
        num_scalar_prefetch=2,
        grid=(nb,),
        in_specs=[pl.BlockSpec((MOE_ROWS, D), lambda i, be, nu: (i, 0)),
                  pl.BlockSpec((1, D, F2), lambda i, be, nu: (be[i], 0, 0)),
                  pl.BlockSpec((1, 1, F2), lambda i, be, nu: (be[i], 0, 0)),
                  pl.BlockSpec((1, F, D), lambda i, be, nu: (be[i], 0, 0)),
                  pl.BlockSpec((1, 1, D), lambda i, be, nu: (be[i], 0, 0))],
        out_specs=pl.BlockSpec((MOE_ROWS, D), lambda i, be, nu: (i, 0)),
    )
    return pl.pallas_call(
        _experts_kernel,
        grid_spec=grid_spec,
        out_shape=jax.ShapeDtypeStruct((n_rows, D), F32),
        compiler_params=_params(("arbitrary",)),
        name="experts",
    )(block_e, n_used, xs, wu, bu, wd, bd)


def _combine_kernel(dest_ref, x2_ref, gate_ref, g_ref, ys_ref, out_ref, buf_ref, sem, *, apply_norm):
    tm = x2_ref.shape[0]

    def issue(t, carry):
        for kk in range(TOP_K):
            _row_copy(ys_ref, dest_ref[t * TOP_K + kk], buf_ref.at[kk], t, sem).start()
        return carry

    lax.fori_loop(0, tm, issue, 0)

    def drain(t, carry):
        for kk in range(TOP_K):
            _row_copy(ys_ref, 0, buf_ref.at[kk], 0, sem).wait()
        return carry

    lax.fori_loop(0, tm, drain, 0)

    x = x2_ref[...]
    for kk in range(TOP_K):
        x = x + buf_ref[kk] * gate_ref[:, kk:kk + 1]
    out_ref[...] = _rms(x, g_ref[...]) if apply_norm else x


def _combine(dest_flat, x2, gates, g, ys, tm, apply_norm):
    T, D = x2.shape
    return pl.pallas_call(
        functools.partial(_combine_kernel, apply_norm=apply_norm),
        grid=(T // tm,),
        in_specs=[pl.BlockSpec((tm * TOP_K,), lambda i: (i,), memory_space=pltpu.SMEM),
                  pl.BlockSpec((tm, D), lambda i: (i, 0)),
                  pl.BlockSpec((tm, LANES), lambda i: (i, 0)),
                  pl.BlockSpec(g.shape, lambda i: (0, 0)),
                  pl.BlockSpec(memory_space=pl.ANY)],
        out_specs=pl.BlockSpec((tm, D), lambda i: (i, 0)),
        out_shape=jax.ShapeDtypeStruct((T, D), F32),
        scratch_shapes=[pltpu.VMEM((TOP_K, tm, D), F32), pltpu.SemaphoreType.DMA(())],
        compiler_params=_params(("arbitrary",)),
        name="combine",
    )(dest_flat, x2, gates, g, ys)


def _row(v):
    return v.reshape(1, -1).astype(F32)


def kernel(x, mem, norm_mix_g, w_in, gate_w2_fwd, gate_b_fwd, gate_w2_bwd, gate_b_bwd,
           gla_norm_g, conv_w, conv_b, conv_ln_g, conv_ln_b, w_out, norm_xattn_g,
           norm_mem_g, xattn_wq, xattn_wk, xattn_wv, xattn_wo, norm_ffn_g, router_w,
           router_b, exp_w_up, exp_b_up, exp_w_down, exp_b_down, final_norm_g):
    B, S, D = x.shape
    M = mem.shape[1]
    T = B * S
    depth = w_in.shape[0]
    kw = GLA_HEADS * GLA_DK
    vw = GLA_HEADS * GLA_DV
    rank_w = gate_w2_fwd.shape[1]
    cw = conv_w.shape[2]
    E = router_w.shape[2]
    assert vw + cw == w_out.shape[1] and 2 * rank_w <= LANES and E <= LANES

    xc = x.reshape(T, D)
    for l in range(depth):
        o_gf = 2 * kw + 2 * vw
        o_glu = o_gf + 2 * rank_w
        wl = w_in[l]
        w_perm = jnp.concatenate(
            [wl[:, :o_gf], wl[:, o_glu:], wl[:, o_gf:o_glu],
             jnp.zeros((D, LANES - 2 * rank_w), wl.dtype)], axis=1).astype(BF16)
        w2 = jnp.zeros((LANES, 2 * kw), F32)
        w2 = w2.at[:rank_w, :kw].set(gate_w2_fwd[l]).at[rank_w:2 * rank_w, kw:].set(gate_w2_bwd[l])
        b2 = jnp.concatenate([gate_b_fwd[l], gate_b_bwd[l]]).reshape(1, -1)

        q, k, v, gr, la, u = _inproj(xc, _row(norm_mix_g[l]), w_perm, w2.astype(BF16), b2,
                                     kw, vw, cw, tm=512)
        a_out = _gla(q, k, v, la, gr, _row(gla_norm_g[l]), B, S)
        b_out = _conv(u, conv_w[l], _row(conv_b[l]), _row(conv_ln_g[l]), _row(conv_ln_b[l]),
                      B, S, tm=512)
        km, vm = _memkv(mem.reshape(B * M, D), _row(norm_mem_g[l]),
                        xattn_wk[l].astype(BF16), xattn_wv[l].astype(BF16), M)

        rw = jnp.zeros((D, LANES), F32).at[:, :E].set(router_w[l]).astype(BF16)
        rb = jnp.zeros((1, LANES), F32).at[0, :E].set(router_b[l])
        x2, h3, idx, gates = _mid(
            xc, a_out, b_out, w_out[l].astype(BF16), _row(norm_xattn_g[l]),
            xattn_wq[l].astype(BF16), km, vm, xattn_wo[l].astype(BF16),
            _row(norm_ffn_g[l]), rw, rb, S, M, E, tm=256)

        rank, cnt = _rank(idx, tm=512)
        counts = cnt[0, :E].astype(jnp.int32)
        padded = ((counts + MOE_ROWS - 1) // MOE_ROWS) * MOE_ROWS
        pends = jnp.cumsum(padded)
        pstart = jnp.zeros((8, LANES), F32).at[0, :E].set((pends - padded).astype(F32))
        dest = _dest(idx, rank, pstart, tm=512)
        dest_flat = dest[:, :TOP_K].reshape(T * TOP_K)
        n_blocks = (T * TOP_K + E * (MOE_ROWS - 1)) // MOE_ROWS
        n_rows = n_blocks * MOE_ROWS
        blk_start = jnp.arange(n_blocks, dtype=jnp.int32) * MOE_ROWS
        block_e = jnp.minimum(jnp.sum(pends[None, :] <= blk_start[:, None], axis=1), E - 1)
        n_used = (pends[E - 1] // MOE_ROWS).reshape(1)

        xs = _dispatch(dest_flat, h3, n_rows, tm=256)
        ys = _experts(block_e.astype(jnp.int32), n_used.astype(jnp.int32), xs,
                      exp_w_up[l].astype(BF16), exp_b_up[l][:, None, :],
                      exp_w_down[l].astype(BF16), exp_b_down[l][:, None, :])
        xc = _combine(dest_flat, x2, gates, _row(final_norm_g), ys, tm=128,
                      apply_norm=(l == depth - 1))
    return xc.reshape(B, S, D)
```

```python
import functools

import numpy as np
import jax
import jax.numpy as jnp
from jax import lax
from jax.experimental import pallas as pl
from jax.experimental.pallas import tpu as pltpu

F32 = jnp.float32
BF16 = jnp.bfloat16

GLA_HEADS = 4
GLA_DK = 64
GLA_DV = 128
GLA_CHUNK = 64
GLA_TILE = 8
GATE_TAU = 16.0
CONV_KERNEL = 31
XATTN_HEADS = 4
TOP_K = 4
SWIGLU_ALPHA = 1.702
SWIGLU_LIMIT = 7.0
RMS_EPS = 1e-6
LN_EPS = 1e-5

LANES = 128
SUBLANES = 8
MOE_ROWS = 256
VMEM_LIMIT = 56 * 1024 * 1024


def _dot(a, b):
    return jnp.dot(a, b, preferred_element_type=F32)


def _dot_nt(a, b):
    return lax.dot_general(a, b, (((1,), (1,)), ((), ())), preferred_element_type=F32)


def _dot_tn(a, b):
    return lax.dot_general(a, b, (((0,), (0,)), ((), ())), preferred_element_type=F32)


def _rms(x, g):
    return x * lax.rsqrt(jnp.mean(x * x, axis=-1, keepdims=True) + RMS_EPS) * g


def _params(sem):
    return pltpu.CompilerParams(dimension_semantics=sem, vmem_limit_bytes=VMEM_LIMIT)


def _store_rows_as_tiles(ref, val):
    n, d = val.shape
    assert d == SUBLANES * LANES and ref.shape == (n * SUBLANES, LANES)
    for s in range(SUBLANES):
        ref[pl.ds(s, n, stride=SUBLANES), :] = val[:, s * LANES:(s + 1) * LANES]


def _load_rows_from_tiles(ref):
    n = ref.shape[0] // SUBLANES
    return jnp.concatenate([ref[pl.ds(s, n, stride=SUBLANES), :] for s in range(SUBLANES)], axis=1)


def _inproj_kernel(x_ref, g_ref, w_ref, w2_ref, b2_ref,
                   q_ref, k_ref, v_ref, gr_ref, la_ref, u_ref, *, kw, vw, cw):
    h = _rms(x_ref[...], g_ref[...]).astype(BF16)
    acc = _dot(h, w_ref[...])
    o = 0
    q_ref[...] = (acc[:, o:o + kw] * (GLA_DK ** -0.5)).astype(BF16); o += kw
    k_ref[...] = acc[:, o:o + kw].astype(BF16); o += kw
    v_ref[...] = acc[:, o:o + vw].astype(BF16); o += vw
    r = acc[:, o:o + vw]; o += vw
    gr_ref[...] = (r * jax.nn.sigmoid(r)).astype(BF16)
    a = acc[:, o:o + cw]; o += cw
    g = acc[:, o:o + cw]; o += cw
    u_ref[...] = a * jax.nn.sigmoid(g)
    z = _dot(acc[:, o:o + LANES].astype(BF16), w2_ref[...]) + b2_ref[...]
    la_ref[...] = (jnp.minimum(z, 0.0) - jnp.log1p(jnp.exp(-jnp.abs(z)))) * (1.0 / GATE_TAU)


def _inproj(x2d, g, w, w2, b2, kw, vw, cw, tm):
    T, D = x2d.shape
    W = w.shape[1]
    row = lambda n: pl.BlockSpec((tm, n), lambda i: (i, 0))
    full = lambda a: pl.BlockSpec(a.shape, lambda i: (0,) * a.ndim)
    return pl.pallas_call(
        functools.partial(_inproj_kernel, kw=kw, vw=vw, cw=cw),
        grid=(T // tm,),
        in_specs=[row(D), full(g), full(w), full(w2), full(b2)],
        out_specs=[row(kw), row(kw), row(vw), row(vw), row(2 * kw), row(cw)],
        out_shape=[jax.ShapeDtypeStruct((T, kw), BF16), jax.ShapeDtypeStruct((T, kw), BF16),
                   jax.ShapeDtypeStruct((T, vw), BF16), jax.ShapeDtypeStruct((T, vw), BF16),
                   jax.ShapeDtypeStruct((T, 2 * kw), F32), jax.ShapeDtypeStruct((T, cw), F32)],
        compiler_params=_params(("parallel",)),
        name="inproj",
    )(x2d, g, w, w2, b2)


def _gla_constants(bwd):
    C, H = GLA_CHUNK, GLA_HEADS
    t = np.arange(C)[:, None]
    s = np.arange(C)[None, :]
    if not bwd:
        blocks = [s <= t, s > t]
    else:
        blocks = [s >= t, s < t]
    lev_q, lev_k, lev_mask = [], [], []
    m = C // 2
    while m >= GLA_TILE:
        g = 2 * m
        pos = t % g
        start = t - pos
        second = pos >= m
        if not bwd:
            ref = start + m - 1
            blocks.append(np.where(second, (s > ref) & (s <= t), (s > t) & (s <= ref)))
            qrow, krow = second, ~second
            mask = (t // g == s // g) & ((t % g) >= m) & ((s % g) < m)
        else:
            mid = start + m
            blocks.append(np.where(~second, (s >= t) & (s < mid), (s >= mid) & (s < t)))
            qrow, krow = ~second, second
            mask = (t // g == s // g) & ((t % g) < m) & ((s % g) >= m)
        lev_q.append(np.broadcast_to(qrow, (C, H * GLA_DK)))
        lev_k.append(np.broadcast_to(krow, (C, H * GLA_DK)))
        lev_mask.append(np.tile(mask, (1, H)))
        m //= 2
    shifts = range(0, GLA_TILE) if not bwd else range(1, GLA_TILE)
    sh_mask = []
    for sh in shifts:
        j = t - sh if not bwd else t + sh
        sh_mask.append(np.tile((s == j) & (t // GLA_TILE == s // GLA_TILE), (1, H)))
    cmat = np.concatenate(blocks, axis=0).astype(np.float32)
    f = lambda xs: np.stack(xs).astype(np.float32)
    return cmat, f(lev_q), f(lev_k), f(lev_mask), f(sh_mask)


def _gla_head_masks():
    C, H = GLA_CHUNK, GLA_HEADS
    rk = np.arange(H * C)[:, None] // C
    hm_k = (rk == np.arange(H * GLA_DK)[None, :] // GLA_DK)
    hm_v = (rk == np.arange(H * GLA_DV)[None, :] // GLA_DV)
    bd = (np.arange(H * GLA_DV)[:, None] // GLA_DV == np.arange(H * GLA_DK)[None, :] // GLA_DK)
    e1 = (np.arange(H * GLA_DK)[:, None] // GLA_DK == np.arange(H * C)[None, :] // C)
    return (hm_k.astype(np.float32), hm_v.astype(np.float32), bd.astype(np.float32),
            e1.astype(np.float32))


def _gla_chunk(la, q, k, v, st_ref, cm_ref, lq_ref, lk_ref, lm_ref, sm_ref,
               hmk_ref, hmv_ref, bd_ref, e1_ref, bwd):
    C, H = GLA_CHUNK, GLA_HEADS
    cm = cm_ref[...]
    hi = la.astype(BF16)
    r1 = la - hi.astype(F32)
    mid = r1.astype(BF16)
    lo = (r1 - mid.astype(F32)).astype(BF16)
    x = _dot(cm, hi) + _dot(cm, mid) + _dot(cm, lo)
    e = jnp.exp(x)
    e_q, e_k = e[0:C], e[C:2 * C]
    if not bwd:
        cum, a_tot = x[0:C], e[C - 1:C]
    else:
        cum, a_tot = x[C:2 * C], e[0:1]

    st = st_ref[...]
    o = _dot_nt((q * e_q).astype(BF16), st.astype(BF16))
    u_t = _dot_tn(v, (k * e_k).astype(BF16))
    st_ref[...] = st * a_tot + u_t * bd_ref[...]

    hmk = hmk_ref[...]
    a = jnp.zeros((C, H * C), F32)
    for lv in range(lq_ref.shape[0]):
        em = e[(2 + lv) * C:(3 + lv) * C]
        qm = (q * (em * lq_ref[lv])).astype(BF16)
        km = k * (em * lk_ref[lv])
        kst = (jnp.concatenate([km] * H, axis=0) * hmk).astype(BF16)
        a = a + _dot_nt(qm, kst) * lm_ref[lv]
    ps = []
    for si in range(sm_ref.shape[0]):
        sh = si if not bwd else si + 1
        if sh == 0:
            ps.append((q * k).astype(BF16))
            continue
        amt = sh if not bwd else C - sh
        kr = pltpu.roll(k, amt, 0)
        cr = pltpu.roll(cum, amt, 0)
        d = (cum - cr) if not bwd else (cr - cum)
        ps.append((q * kr * jnp.exp(jnp.minimum(d, 0.0))).astype(BF16))
    sc = _dot(jnp.concatenate(ps, axis=0), e1_ref[...])
    for si in range(sm_ref.shape[0]):
        a = a + sc[si * C:(si + 1) * C] * sm_ref[si]
    vbd = jnp.concatenate([v] * H, axis=0) * hmv_ref[...]
    return o + _dot(a.astype(BF16), vbd)


def _gla_kernel(q_ref, k_ref, v_ref, la_ref, gr_ref, ng_ref,
                cmf_ref, lqf_ref, lkf_ref, lmf_ref, smf_ref,
                cmb_ref, lqb_ref, lkb_ref, lmb_ref, smb_ref,
                hmk_ref, hmv_ref, bd_ref, e1_ref,
                out_ref, of_ref, ob_ref, stf_ref, stb_ref):
    C = GLA_CHUNK
    S = q_ref.shape[0]
    kw = q_ref.shape[1]
    n = S // C
    stf_ref[...] = jnp.zeros_like(stf_ref)
    stb_ref[...] = jnp.zeros_like(stb_ref)

    def body(i, carry):
        rf = pl.ds(pl.multiple_of(i * C, C), C)
        rb = pl.ds(pl.multiple_of((n - 1 - i) * C, C), C)
        of_ref[rf, :] = _gla_chunk(
            la_ref[rf, 0:kw], q_ref[rf, :].astype(F32), k_ref[rf, :].astype(F32), v_ref[rf, :],
            stf_ref, cmf_ref, lqf_ref, lkf_ref, lmf_ref, smf_ref,
            hmk_ref, hmv_ref, bd_ref, e1_ref, False)
        ob_ref[rb, :] = _gla_chunk(
            la_ref[rb, kw:2 * kw], q_ref[rb, :].astype(F32), k_ref[rb, :].astype(F32), v_ref[rb, :],
            stb_ref, cmb_ref, lqb_ref, lkb_ref, lmb_ref, smb_ref,
            hmk_ref, hmv_ref, bd_ref, e1_ref, True)
        return carry

    lax.fori_loop(0, n, body, 0)

    rows = 256
    def fin(i, carry):
        r = pl.ds(pl.multiple_of(i * rows, rows), rows)
        o = of_ref[r, :] + ob_ref[r, :]
        ys = []
        for h in range(GLA_HEADS):
            oh = o[:, h * GLA_DV:(h + 1) * GLA_DV]
            ys.append(_rms(oh, ng_ref[...]))
        y = jnp.concatenate(ys, axis=1) * gr_ref[r, :].astype(F32)
        out_ref[r, :] = y.astype(BF16)
        return carry

    lax.fori_loop(0, S // rows, fin, 0)


def _gla(q, k, v, la, gr, ng, B, S):
    kw, vw = q.shape[1], v.shape[1]
    cf = _gla_constants(False)
    cb = _gla_constants(True)
    hm_k, hm_v, bd, e1 = _gla_head_masks()
    consts = [jnp.asarray(cf[0], BF16)] + [jnp.asarray(c) for c in cf[1:]] + \
             [jnp.asarray(cb[0], BF16)] + [jnp.asarray(c) for c in cb[1:]] + \
             [jnp.asarray(hm_k), jnp.asarray(hm_v, BF16), jnp.asarray(bd), jnp.asarray(e1, BF16)]
    seq = lambda n: pl.BlockSpec((S, n), lambda b: (b, 0))
    full = lambda a: pl.BlockSpec(a.shape, lambda b: (0,) * a.ndim)
    return pl.pallas_call(
        _gla_kernel,
        grid=(B,),
        in_specs=[seq(kw), seq(kw), seq(vw), seq(2 * kw), seq(vw), full(ng)] + [full(c) for c in consts],
        out_specs=seq(vw),
        out_shape=jax.ShapeDtypeStruct((B * S, vw), BF16),
        scratch_shapes=[pltpu.VMEM((S, vw), F32), pltpu.VMEM((S, vw), F32),
                        pltpu.VMEM((vw, kw), F32), pltpu.VMEM((vw, kw), F32)],
        compiler_params=_params(("parallel",)),
        name="gla",
    )(q, k, v, la, gr, ng, *consts)


def _conv_kernel(u_ref, w_ref, cb_ref, lg_ref, lb_ref, out_ref, *, tm, halo):
    S = u_ref.shape[0]
    j = pl.program_id(1)
    t0 = pl.multiple_of(j * tm, tm)
    top = u_ref[pl.ds(pl.multiple_of(jnp.maximum(t0 - halo, 0), halo), halo), :]
    bot = u_ref[pl.ds(pl.multiple_of(jnp.minimum(t0 + tm, S - halo), halo), halo), :]
    top = jnp.where(j > 0, top, 0.0)
    bot = jnp.where(j < pl.num_programs(1) - 1, bot, 0.0)
    win = jnp.concatenate([top, u_ref[pl.ds(t0, tm), :], bot], axis=0)
    rows = tm + 2 * halo
    pad = (CONV_KERNEL - 1) // 2
    w = w_ref[...]
    acc = jnp.zeros((tm, u_ref.shape[1]), F32)
    for r in range(8):
        offs = [k for k in range(CONV_KERNEL) if (halo - pad + k) % 8 == r]
        if not offs:
            continue
        xr = win if r == 0 else pltpu.roll(win, rows - r, 0)
        for k in offs:
            a0 = (halo - pad + k) - r
            acc = acc + xr[a0:a0 + tm, :] * w[k:k + 1, :]
    y = acc + cb_ref[...]
    mu = jnp.mean(y, axis=-1, keepdims=True)
    yc = y - mu
    var = jnp.mean(yc * yc, axis=-1, keepdims=True)
    z = yc * lax.rsqrt(var + LN_EPS) * lg_ref[...] + lb_ref[...]
    out_ref[...] = (z * jax.nn.sigmoid(z)).astype(BF16)


def _conv(u, w, cb, lg, lb, B, S, tm):
    cw = u.shape[1]
    halo = 16
    full = lambda a: pl.BlockSpec(a.shape, lambda b, j: (0,) * a.ndim)
    nj = S // tm
    return pl.pallas_call(
        functools.partial(_conv_kernel, tm=tm, halo=halo),
        grid=(B, nj),
        in_specs=[pl.BlockSpec((S, cw), lambda b, j: (b, 0)), full(w), full(cb), full(lg), full(lb)],
        out_specs=pl.BlockSpec((tm, cw), lambda b, j: (b * nj + j, 0)),
        out_shape=jax.ShapeDtypeStruct((B * S, cw), BF16),
        compiler_params=_params(("parallel", "parallel")),
        name="conv",
    )(u, w, cb, lg, lb)


def _memkv_kernel(m_ref, g_ref, wk_ref, wv_ref, k_ref, v_ref):
    m = _rms(m_ref[...], g_ref[...]).astype(BF16)
    k_ref[...] = _dot(m, wk_ref[...]).astype(BF16)
    v_ref[...] = _dot(m, wv_ref[...]).astype(BF16)


def _memkv(mem2d, g, wk, wv, M):
    R, D = mem2d.shape
    full = lambda a: pl.BlockSpec(a.shape, lambda i: (0,) * a.ndim)
    blk = pl.BlockSpec((M, D), lambda i: (i, 0))
    return pl.pallas_call(
        _memkv_kernel,
        grid=(R // M,),
        in_specs=[blk, full(g), full(wk), full(wv)],
        out_specs=[blk, blk],
        out_shape=[jax.ShapeDtypeStruct((R, D), BF16)] * 2,
        compiler_params=_params(("parallel",)),
        name="memkv",
    )(mem2d, g, wk, wv)


def _mid_kernel(x_ref, a_ref, b_ref, wout_ref, gx_ref, wq_ref, km_ref, vm_ref, wo_ref,
                gf_ref, rw_ref, rb_ref, x2_ref, h3_ref, idx_ref, gate_ref, *, n_exp):
    ab = jnp.concatenate([a_ref[...], b_ref[...]], axis=1)
    x1 = x_ref[...] + _dot(ab, wout_ref[...])
    q2 = _dot(_rms(x1, gx_ref[...]).astype(BF16), wq_ref[...])
    D = q2.shape[1]
    dh = D // XATTN_HEADS
    outs = []
    for h in range(XATTN_HEADS):
        sl = slice(h * dh, (h + 1) * dh)
        s = _dot_nt(q2[:, sl].astype(BF16), km_ref[:, sl]) * (dh ** -0.5)
        p = jnp.exp(s - jnp.max(s, axis=-1, keepdims=True))
        p = p / jnp.sum(p, axis=-1, keepdims=True)
        outs.append(_dot(p.astype(BF16), vm_ref[:, sl]))
    o = jnp.concatenate(outs, axis=1).astype(BF16)
    x2 = x1 + _dot(o, wo_ref[...])
    x2_ref[...] = x2
    h3 = _rms(x2, gf_ref[...])
    _store_rows_as_tiles(h3_ref, h3)
    lane = lax.broadcasted_iota(jnp.int32, (x2.shape[0], LANES), 1)
    logits = _dot(h3.astype(BF16), rw_ref[...]) + rb_ref[...]
    cur = jnp.where(lane < n_exp, logits, -jnp.inf)
    lane_f = lane.astype(F32)
    vals, idxs = [], []
    for _ in range(TOP_K):
        m = jnp.max(cur, axis=-1, keepdims=True)
        ik = jnp.min(jnp.where(cur == m, lane_f, float(LANES)), axis=-1, keepdims=True)
        vals.append(m)
        idxs.append(ik)
        cur = jnp.where(lane_f == ik, -jnp.inf, cur)
    es = [jnp.exp(vk - vals[0]) for vk in vals]
    den = es[0]
    for ek in es[1:]:
        den = den + ek
    idx_out = jnp.zeros(lane.shape, F32)
    gate_out = jnp.zeros(lane.shape, F32)
    for kk in range(TOP_K):
        idx_out = jnp.where(lane == kk, idxs[kk], idx_out)
        gate_out = jnp.where(lane == kk, es[kk] / den, gate_out)
    idx_ref[...] = idx_out.astype(jnp.int32)
    gate_ref[...] = gate_out


def _mid(x2d, a, b, wout, gx, wq, km, vm, wo, gf, rw, rb, S, M, n_exp, tm):
    T, D = x2d.shape
    half = a.shape[1]
    per_b = S // tm
    row = lambda n: pl.BlockSpec((tm, n), lambda i: (i, 0))
    full = lambda arr: pl.BlockSpec(arr.shape, lambda i: (0,) * arr.ndim)
    memb = pl.BlockSpec((M, D), lambda i: (i // per_b, 0))
    return pl.pallas_call(
        functools.partial(_mid_kernel, n_exp=n_exp),
        grid=(T // tm,),
        in_specs=[row(D), row(half), row(half), full(wout), full(gx), full(wq), memb, memb,
                  full(wo), full(gf), full(rw), full(rb)],
        out_specs=[row(D), pl.BlockSpec((tm * SUBLANES, LANES), lambda i: (i, 0)),
                   row(LANES), row(LANES)],
        out_shape=[jax.ShapeDtypeStruct((T, D), F32),
                   jax.ShapeDtypeStruct((T * SUBLANES, LANES), F32),
                   jax.ShapeDtypeStruct((T, LANES), jnp.int32),
                   jax.ShapeDtypeStruct((T, LANES), F32)],
        compiler_params=_params(("parallel",)),
        name="mid",
    )(x2d, a, b, wout, gx, wq, km, vm, wo, gf, rw, rb)


def _onehot(idx_ref, kk, lane):
    return (lane == idx_ref[:, kk:kk + 1]).astype(F32)


def _rank_kernel(idx_ref, tri_ref, rank_ref, cnt_ref, carry_ref):
    i = pl.program_id(0)

    @pl.when(i == 0)
    def _():
        carry_ref[...] = jnp.zeros_like(carry_ref)

    tm = idx_ref.shape[0]
    lane = lax.broadcasted_iota(jnp.int32, (tm, LANES), 1)
    hots = [_onehot(idx_ref, kk, lane) for kk in range(TOP_K)]
    mask = hots[0]
    for hk in hots[1:]:
        mask = mask + hk
    incl = _dot(tri_ref[...], mask.astype(BF16))
    before = incl - mask + carry_ref[0:1, :]
    out = jnp.zeros((tm, LANES), F32)
    for kk in range(TOP_K):
        rk = jnp.sum(hots[kk] * before, axis=-1, keepdims=True)
        out = jnp.where(lane == kk, rk, out)
    rank_ref[...] = out.astype(jnp.int32)
    carry_ref[...] = carry_ref[...] + incl[tm - 1:tm, :]
    cnt_ref[...] = carry_ref[...]


def _rank(idx, tm):
    T = idx.shape[0]
    tri = jnp.asarray(np.tril(np.ones((tm, tm), np.float32)), BF16)
    row = pl.BlockSpec((tm, LANES), lambda i: (i, 0))
    return pl.pallas_call(
        _rank_kernel,
        grid=(T // tm,),
        in_specs=[row, pl.BlockSpec((tm, tm), lambda i: (0, 0))],
        out_specs=[row, pl.BlockSpec((8, LANES), lambda i: (0, 0))],
        out_shape=[jax.ShapeDtypeStruct((T, LANES), jnp.int32),
                   jax.ShapeDtypeStruct((8, LANES), F32)],
        scratch_shapes=[pltpu.VMEM((8, LANES), F32)],
        compiler_params=_params(("arbitrary",)),
        name="rank",
    )(idx, tri)


def _dest_kernel(idx_ref, rank_ref, pstart_ref, dest_ref):
    tm = idx_ref.shape[0]
    lane = lax.broadcasted_iota(jnp.int32, (tm, LANES), 1)
    out = jnp.zeros((tm, LANES), F32)
    for kk in range(TOP_K):
        base = jnp.sum(_onehot(idx_ref, kk, lane) * pstart_ref[0:1, :], axis=-1, keepdims=True)
        out = jnp.where(lane == kk, base, out)
    dest_ref[...] = out.astype(jnp.int32) + rank_ref[...]


def _dest(idx, rank, pstart, tm):
    T = idx.shape[0]
    row = pl.BlockSpec((tm, LANES), lambda i: (i, 0))
    return pl.pallas_call(
        _dest_kernel,
        grid=(T // tm,),
        in_specs=[row, row, pl.BlockSpec((8, LANES), lambda i: (0, 0))],
        out_specs=row,
        out_shape=jax.ShapeDtypeStruct((T, LANES), jnp.int32),
        compiler_params=_params(("parallel",)),
        name="dest",
    )(idx, rank, pstart)


ROW_UNROLL = 8


def _tile(ref, row):
    return ref.at[pl.ds(pl.multiple_of(row * SUBLANES, SUBLANES), SUBLANES)]


def _wait_rows(buf_ref, sem):
    pltpu.make_async_copy(buf_ref, buf_ref, sem).wait()


def _dispatch_kernel(dest_ref, h_ref, xs_in_ref, xs_ref, sem):
    del xs_in_ref
    tm = h_ref.shape[0] // SUBLANES

    def issue(i, carry):
        for u in range(ROW_UNROLL):
            t = i * ROW_UNROLL + u
            for kk in range(TOP_K):
                pltpu.make_async_copy(_tile(h_ref, t), _tile(xs_ref, dest_ref[t * TOP_K + kk]),
                                      sem).start(priority=kk % 2)
        return carry

    lax.fori_loop(0, tm // ROW_UNROLL, issue, 0)
    for kk in range(TOP_K):
        _wait_rows(h_ref, sem)


def _dispatch(dest_flat, h3, n_rows, tm):
    T = h3.shape[0] // SUBLANES
    xs0 = jnp.zeros((n_rows * SUBLANES, LANES), F32)
    return pl.pallas_call(
        _dispatch_kernel,
        grid=(T // tm,),
        in_specs=[pl.BlockSpec((tm * TOP_K,), lambda i: (i,), memory_space=pltpu.SMEM),
                  pl.BlockSpec((tm * SUBLANES, LANES), lambda i: (i, 0)),
                  pl.BlockSpec(memory_space=pl.ANY)],
        out_specs=pl.BlockSpec(memory_space=pl.ANY),
        out_shape=jax.ShapeDtypeStruct((n_rows * SUBLANES, LANES), F32),
        scratch_shapes=[pltpu.SemaphoreType.DMA(())],
        input_output_aliases={2: 0},
        compiler_params=_params(("arbitrary",)),
        name="dispatch",
    )(dest_flat, h3, xs0)


def _experts_kernel(be_ref, nb_ref, xs_ref, wu_ref, bu_ref, wd_ref, bd_ref, ys_ref):
    i = pl.program_id(0)

    @pl.when(i < nb_ref[0])
    def _():
        F = wd_ref.shape[1]
        x = _load_rows_from_tiles(xs_ref).astype(BF16)
        gu = _dot(x, wu_ref[0]) + bu_ref[0]
        x_glu = jnp.minimum(gu[:, :F], SWIGLU_LIMIT)
        x_lin = jnp.clip(gu[:, F:], -SWIGLU_LIMIT, SWIGLU_LIMIT)
        hid = x_glu * jax.nn.sigmoid(SWIGLU_ALPHA * x_glu) * (x_lin + 1.0)
        _store_rows_as_tiles(ys_ref, _dot(hid.astype(BF16), wd_ref[0]) + bd_ref[0])

    @pl.when(i >= nb_ref[0])
    def _():
        ys_ref[...] = jnp.zeros_like(ys_ref)


def _experts(block_e, n_used, xs, wu, bu, wd, bd):
    E, D, F2 = wu.shape
    F = wd.shape[1]
    nb = xs.shape[0] // (MOE_ROWS * SUBLANES)
    rows = pl.BlockSpec((MOE_ROWS * SUBLANES, LANES), lambda i, be, nu: (i, 0))
    grid_spec = pltpu.PrefetchScalarGridSpec(
        num_scalar_prefetch=2,
        grid=(nb,),
        in_specs=[rows,
                  pl.BlockSpec((1, D, F2), lambda i, be, nu: (be[i], 0, 0)),
                  pl.BlockSpec((1, 1, F2), lambda i, be, nu: (be[i], 0, 0)),
                  pl.BlockSpec((1, F, D), lambda i, be, nu: (be[i], 0, 0)),
                  pl.BlockSpec((1, 1, D), lambda i, be, nu: (be[i], 0, 0))],
        out_specs=rows,
    )
    return pl.pallas_call(
        _experts_kernel,
        grid_spec=grid_spec,
        out_shape=jax.ShapeDtypeStruct(xs.shape, F32),
        compiler_params=_params(("arbitrary",)),
        name="experts",
    )(block_e, n_used, xs, wu, bu, wd, bd)


def _combine_kernel(dest_ref, x2_ref, gate_ref, g_ref, ys_ref, out_ref, buf_ref, sem, *, apply_norm):
    tm = x2_ref.shape[0]

    def issue(i, carry):
        for u in range(ROW_UNROLL):
            t = i * ROW_UNROLL + u
            for kk in range(TOP_K):
                pltpu.make_async_copy(_tile(ys_ref, dest_ref[t * TOP_K + kk]),
                                      _tile(buf_ref.at[kk], t), sem).start(priority=kk % 2)
        return carry

    lax.fori_loop(0, tm // ROW_UNROLL, issue, 0)
    _wait_rows(buf_ref, sem)

    x = x2_ref[...]
    for kk in range(TOP_K):
        x = x + _load_rows_from_tiles(buf_ref.at[kk]) * gate_ref[:, kk:kk + 1]
    out_ref[...] = _rms(x, g_ref[...]) if apply_norm else x


def _combine(dest_flat, x2, gates, g, ys, tm, apply_norm):
    T, D = x2.shape
    return pl.pallas_call(
        functools.partial(_combine_kernel, apply_norm=apply_norm),
        grid=(T // tm,),
        in_specs=[pl.BlockSpec((tm * TOP_K,), lambda i: (i,), memory_space=pltpu.SMEM),
                  pl.BlockSpec((tm, D), lambda i: (i, 0)),
                  pl.BlockSpec((tm, LANES), lambda i: (i, 0)),
                  pl.BlockSpec(g.shape, lambda i: (0, 0)),
                  pl.BlockSpec(memory_space=pl.ANY)],
        out_specs=pl.BlockSpec((tm, D), lambda i: (i, 0)),
        out_shape=jax.ShapeDtypeStruct((T, D), F32),
        scratch_shapes=[pltpu.VMEM((TOP_K, tm * SUBLANES, LANES), F32), pltpu.SemaphoreType.DMA(())],
        compiler_params=_params(("arbitrary",)),
        name="combine",
    )(dest_flat, x2, gates, g, ys)


def _row(v):
    return v.reshape(1, -1).astype(F32)


def kernel(x, mem, norm_mix_g, w_in, gate_w2_fwd, gate_b_fwd, gate_w2_bwd, gate_b_bwd,
           gla_norm_g, conv_w, conv_b, conv_ln_g, conv_ln_b, w_out, norm_xattn_g,
           norm_mem_g, xattn_wq, xattn_wk, xattn_wv, xattn_wo, norm_ffn_g, router_w,
           router_b, exp_w_up, exp_b_up, exp_w_down, exp_b_down, final_norm_g):
    B, S, D = x.shape
    M = mem.shape[1]
    T = B * S
    depth = w_in.shape[0]
    kw = GLA_HEADS * GLA_DK
    vw = GLA_HEADS * GLA_DV
    rank_w = gate_w2_fwd.shape[1]
    cw = conv_w.shape[2]
    E = router_w.shape[2]
    assert vw + cw == w_out.shape[1] and 2 * rank_w <= LANES and E <= LANES

    xc = x.reshape(T, D)
    for l in range(depth):
        o_gf = 2 * kw + 2 * vw
        o_glu = o_gf + 2 * rank_w
        wl = w_in[l]
        w_perm = jnp.concatenate(
            [wl[:, :o_gf], wl[:, o_glu:], wl[:, o_gf:o_glu],
             jnp.zeros((D, LANES - 2 * rank_w), wl.dtype)], axis=1).astype(BF16)
        w2 = jnp.zeros((LANES, 2 * kw), F32)
        w2 = w2.at[:rank_w, :kw].set(gate_w2_fwd[l]).at[rank_w:2 * rank_w, kw:].set(gate_w2_bwd[l])
        b2 = jnp.concatenate([gate_b_fwd[l], gate_b_bwd[l]]).reshape(1, -1)

        q, k, v, gr, la, u = _inproj(xc, _row(norm_mix_g[l]), w_perm, w2.astype(BF16), b2,
                                     kw, vw, cw, tm=512)
        a_out = _gla(q, k, v, la, gr, _row(gla_norm_g[l]), B, S)
        b_out = _conv(u, conv_w[l], _row(conv_b[l]), _row(conv_ln_g[l]), _row(conv_ln_b[l]),
                      B, S, tm=512)
        km, vm = _memkv(mem.reshape(B * M, D), _row(norm_mem_g[l]),
                        xattn_wk[l].astype(BF16), xattn_wv[l].astype(BF16), M)

        rw = jnp.zeros((D, LANES), F32).at[:, :E].set(router_w[l]).astype(BF16)
        rb = jnp.zeros((1, LANES), F32).at[0, :E].set(router_b[l])
        x2, h3, idx, gates = _mid(
            xc, a_out, b_out, w_out[l].astype(BF16), _row(norm_xattn_g[l]),
            xattn_wq[l].astype(BF16), km, vm, xattn_wo[l].astype(BF16),
            _row(norm_ffn_g[l]), rw, rb, S, M, E, tm=256)

        rank, cnt = _rank(idx, tm=512)
        counts = cnt[0, :E].astype(jnp.int32)
        padded = ((counts + MOE_ROWS - 1) // MOE_ROWS) * MOE_ROWS
        pends = jnp.cumsum(padded)
        pstart = jnp.zeros((8, LANES), F32).at[0, :E].set((pends - padded).astype(F32))
        dest = _dest(idx, rank, pstart, tm=512)
        dest_flat = dest[:, :TOP_K].reshape(T * TOP_K)
        n_blocks = (T * TOP_K + E * (MOE_ROWS - 1)) // MOE_ROWS
        blk_start = jnp.arange(n_blocks, dtype=jnp.int32) * MOE_ROWS
        block_e = jnp.minimum(jnp.sum(pends[None, :] <= blk_start[:, None], axis=1), E - 1)
        n_used = (pends[E - 1] // MOE_ROWS).reshape(1)

        xs = _dispatch(dest_flat, h3, n_blocks * MOE_ROWS, tm=256)
        ys = _experts(block_e.astype(jnp.int32), n_used.astype(jnp.int32), xs,
                      exp_w_up[l].astype(BF16), exp_b_up[l][:, None, :],
                      exp_w_down[l].astype(BF16), exp_b_down[l][:, None, :])
        xc = _combine(dest_flat, x2, gates, _row(final_norm_g), ys, tm=128,
                      apply_norm=(l == depth - 1))
    return xc.reshape(B, S, D)
```

```python
import functools

import numpy as np
import jax
import jax.numpy as jnp
from jax import lax
from jax.experimental import pallas as pl
from jax.experimental.pallas import tpu as pltpu

F32 = jnp.float32
BF16 = jnp.bfloat16

GLA_HEADS = 4
GLA_DK = 64
GLA_DV = 128
GLA_CHUNK = 64
GLA_TILE = 4
GATE_TAU = 16.0
CONV_KERNEL = 31
XATTN_HEADS = 4
TOP_K = 4
SWIGLU_ALPHA = 1.702
SWIGLU_LIMIT = 7.0
RMS_EPS = 1e-6
LN_EPS = 1e-5

LANES = 128
SUBLANES = 8
MOE_ROWS = 256
VMEM_LIMIT = 56 * 1024 * 1024


def _dot(a, b):
    return jnp.dot(a, b, preferred_element_type=F32)


def _dot_nt(a, b):
    return lax.dot_general(a, b, (((1,), (1,)), ((), ())), preferred_element_type=F32)


def _dot_tn(a, b):
    return lax.dot_general(a, b, (((0,), (0,)), ((), ())), preferred_element_type=F32)


def _rms(x, g):
    return x * lax.rsqrt(jnp.mean(x * x, axis=-1, keepdims=True) + RMS_EPS) * g


def _params(sem):
    return pltpu.CompilerParams(dimension_semantics=sem, vmem_limit_bytes=VMEM_LIMIT)


def _store_rows_as_tiles(ref, val):
    n, d = val.shape
    assert d == SUBLANES * LANES and ref.shape == (n * SUBLANES, LANES)
    for s in range(SUBLANES):
        ref[pl.ds(s, n, stride=SUBLANES), :] = val[:, s * LANES:(s + 1) * LANES]


def _load_rows_from_tiles(ref):
    n = ref.shape[0] // SUBLANES
    return jnp.concatenate([ref[pl.ds(s, n, stride=SUBLANES), :] for s in range(SUBLANES)], axis=1)


def _inproj_kernel(x_ref, g_ref, w_ref, w2_ref, b2_ref,
                   q_ref, k_ref, v_ref, gr_ref, la_ref, u_ref, *, kw, vw, cw):
    h = _rms(x_ref[...], g_ref[...]).astype(BF16)
    acc = _dot(h, w_ref[...])
    o = 0
    q_ref[...] = (acc[:, o:o + kw] * (GLA_DK ** -0.5)).astype(BF16); o += kw
    k_ref[...] = acc[:, o:o + kw].astype(BF16); o += kw
    v_ref[...] = acc[:, o:o + vw].astype(BF16); o += vw
    r = acc[:, o:o + vw]; o += vw
    gr_ref[...] = (r * jax.nn.sigmoid(r)).astype(BF16)
    a = acc[:, o:o + cw]; o += cw
    g = acc[:, o:o + cw]; o += cw
    u_ref[...] = a * jax.nn.sigmoid(g)
    z = _dot(acc[:, o:o + LANES].astype(BF16), w2_ref[...]) + b2_ref[...]
    la_ref[...] = (jnp.minimum(z, 0.0) - jnp.log1p(jnp.exp(-jnp.abs(z)))) * (1.0 / GATE_TAU)


def _inproj(x2d, g, w, w2, b2, kw, vw, cw, tm):
    T, D = x2d.shape
    W = w.shape[1]
    row = lambda n: pl.BlockSpec((tm, n), lambda i: (i, 0))
    full = lambda a: pl.BlockSpec(a.shape, lambda i: (0,) * a.ndim)
    return pl.pallas_call(
        functools.partial(_inproj_kernel, kw=kw, vw=vw, cw=cw),
        grid=(T // tm,),
        in_specs=[row(D), full(g), full(w), full(w2), full(b2)],
        out_specs=[row(kw), row(kw), row(vw), row(vw), row(2 * kw), row(cw)],
        out_shape=[jax.ShapeDtypeStruct((T, kw), BF16), jax.ShapeDtypeStruct((T, kw), BF16),
                   jax.ShapeDtypeStruct((T, vw), BF16), jax.ShapeDtypeStruct((T, vw), BF16),
                   jax.ShapeDtypeStruct((T, 2 * kw), F32), jax.ShapeDtypeStruct((T, cw), F32)],
        compiler_params=_params(("parallel",)),
        name="inproj",
    )(x2d, g, w, w2, b2)


def _gla_levels():
    ms, m = [], GLA_CHUNK // 2
    while m >= GLA_TILE:
        ms.append(m)
        m //= 2
    return ms


def _gla_constants(bwd):
    C, H = GLA_CHUNK, GLA_HEADS
    t = np.arange(C)[:, None]
    s = np.arange(C)[None, :]
    lev_q, lev_mask = [], []
    for m in _gla_levels():
        g = 2 * m
        second = (t % g) >= m
        same = t // g == s // g
        if not bwd:
            qrow = second
            mask = same & ((t % g) >= m) & ((s % g) < m)
        else:
            qrow = ~second
            mask = same & ((t % g) < m) & ((s % g) >= m)
        lev_q.append(np.broadcast_to(qrow, (C, H * GLA_DK)))
        lev_mask.append(np.tile(mask, (1, H)))
    shifts = range(0, GLA_TILE) if not bwd else range(1, GLA_TILE)
    sh_mask = []
    for sh in shifts:
        j = t - sh if not bwd else t + sh
        sh_mask.append(np.tile((s == j) & (t // GLA_TILE == s // GLA_TILE), (1, H)))
    f = lambda xs: np.stack(xs).astype(np.float32)
    return f(lev_q), f(lev_mask), f(sh_mask)


def _gla_head_mask():
    C, H = GLA_CHUNK, GLA_HEADS
    assert GLA_DK == C
    blk = np.arange(H * C) // C
    return (blk[:, None] == blk[None, :]).astype(np.float32)


def _gla_chunk(la, q, k, v, st_ref, c_ref, tri_ref, lq_ref, lm_ref, sm_ref, hm_ref, bwd):
    C, H = GLA_CHUNK, GLA_HEADS
    kw = q.shape[1]
    hm = hm_ref[...]

    def stack(xb):
        return jnp.concatenate([xb] * H, axis=0) * hm

    tri = tri_ref[...]
    hi = la.astype(BF16)
    incl = _dot(tri, hi) + _dot(tri, (la - hi.astype(F32)).astype(BF16))
    yield None
    tot = incl[C - 1:C]
    cum = incl if not bwd else incl - la
    x_q, x_k = (cum, tot - cum) if not bwd else (tot - cum, cum)
    e_q, e_k, a_tot = jnp.exp(x_q), jnp.exp(x_k), jnp.exp(tot)
    c_ref[...] = cum

    a = jnp.zeros((C, H * C), F32)
    for lv, m in enumerate(_gla_levels()):
        g = 2 * m
        edge = m - 1 if not bwd else m
        r = jnp.concatenate([jnp.broadcast_to(c_ref[s0 + edge:s0 + edge + 1, :], (g, kw))
                             for s0 in range(0, C, g)], axis=0)
        d = cum - r
        z = (jnp.where(lq_ref[lv] > 0, q, k) * jnp.exp(jnp.minimum(d, -d))).astype(BF16)
        a = a + _dot_nt(z, stack(z)) * lm_ref[lv]
    ps = []
    for si in range(sm_ref.shape[0]):
        sh = si if not bwd else si + 1
        if sh == 0:
            ps.append((q * k).astype(BF16))
            continue
        amt = sh if not bwd else C - sh
        kr = pltpu.roll(k, amt, 0)
        cr = pltpu.roll(cum, amt, 0)
        d = (cum - cr) if not bwd else (cr - cum)
        ps.append((q * kr * jnp.exp(jnp.minimum(d, 0.0))).astype(BF16))
    sc = _dot(jnp.concatenate(ps, axis=0), hm)
    yield None
    for si in range(sm_ref.shape[0]):
        a = a + sc[si * C:(si + 1) * C] * sm_ref[si]

    vst = jnp.concatenate([v[:, h * GLA_DV:(h + 1) * GLA_DV] for h in range(H)], axis=0)
    st = st_ref[...]
    o4 = _dot(stack(a.astype(BF16)), vst) + _dot_nt(stack((q * e_q).astype(BF16)), st.astype(BF16))
    st_ref[...] = st * a_tot + _dot_tn(vst, stack((k * e_k).astype(BF16)))
    yield jnp.concatenate([o4[h * C:(h + 1) * C] for h in range(H)], axis=1)


def _gla_kernel(q_ref, k_ref, v_ref, la_ref, gr_ref, ng_ref,
                lqf_ref, lmf_ref, smf_ref, lqb_ref, lmb_ref, smb_ref, tri_ref, hm_ref,
                out_ref, of_ref, ob_ref, stf_ref, stb_ref, cf_ref, cb_ref):
    C = GLA_CHUNK
    S = q_ref.shape[0]
    kw = q_ref.shape[1]
    n = S // C
    stf_ref[...] = jnp.zeros_like(stf_ref)
    stb_ref[...] = jnp.zeros_like(stb_ref)

    def body(i, carry):
        rf = pl.ds(pl.multiple_of(i * C, C), C)
        rb = pl.ds(pl.multiple_of((n - 1 - i) * C, C), C)
        fwd = _gla_chunk(
            la_ref[rf, 0:kw], q_ref[rf, :].astype(F32), k_ref[rf, :].astype(F32), v_ref[rf, :],
            stf_ref, cf_ref, tri_ref, lqf_ref, lmf_ref, smf_ref, hm_ref, False)
        bwd = _gla_chunk(
            la_ref[rb, kw:2 * kw], q_ref[rb, :].astype(F32), k_ref[rb, :].astype(F32), v_ref[rb, :],
            stb_ref, cb_ref, tri_ref, lqb_ref, lmb_ref, smb_ref, hm_ref, True)
        for o_f, o_b in zip(fwd, bwd):
            pass
        of_ref[rf, :] = o_f
        ob_ref[rb, :] = o_b
        return carry

    lax.fori_loop(0, n, body, 0)

    rows = 256
    def fin(i, carry):
        r = pl.ds(pl.multiple_of(i * rows, rows), rows)
        o = of_ref[r, :] + ob_ref[r, :]
        ys = []
        for h in range(GLA_HEADS):
            oh = o[:, h * GLA_DV:(h + 1) * GLA_DV]
            ys.append(_rms(oh, ng_ref[...]))
        y = jnp.concatenate(ys, axis=1) * gr_ref[r, :].astype(F32)
        out_ref[r, :] = y.astype(BF16)
        return carry

    lax.fori_loop(0, S // rows, fin, 0)


def _gla(q, k, v, la, gr, ng, B, S):
    kw, vw = q.shape[1], v.shape[1]
    cf = _gla_constants(False)
    cb = _gla_constants(True)
    tri = np.tril(np.ones((GLA_CHUNK, GLA_CHUNK), np.float32))
    consts = [jnp.asarray(c) for c in cf] + [jnp.asarray(c) for c in cb] + \
             [jnp.asarray(tri, BF16), jnp.asarray(_gla_head_mask(), BF16)]
    seq = lambda n: pl.BlockSpec((S, n), lambda b: (b, 0))
    full = lambda a: pl.BlockSpec(a.shape, lambda b: (0,) * a.ndim)
    return pl.pallas_call(
        _gla_kernel,
        grid=(B,),
        in_specs=[seq(kw), seq(kw), seq(vw), seq(2 * kw), seq(vw), full(ng)] + [full(c) for c in consts],
        out_specs=seq(vw),
        out_shape=jax.ShapeDtypeStruct((B * S, vw), BF16),
        scratch_shapes=[pltpu.VMEM((S, vw), F32), pltpu.VMEM((S, vw), F32),
                        pltpu.VMEM((GLA_DV, kw), F32), pltpu.VMEM((GLA_DV, kw), F32),
                        pltpu.VMEM((GLA_CHUNK, kw), F32), pltpu.VMEM((GLA_CHUNK, kw), F32)],
        compiler_params=_params(("parallel",)),
        name="gla",
    )(q, k, v, la, gr, ng, *consts)


def _conv_kernel(u_ref, w_ref, cb_ref, lg_ref, lb_ref, out_ref, *, tm, halo):
    S = u_ref.shape[0]
    j = pl.program_id(1)
    t0 = pl.multiple_of(j * tm, tm)
    top = u_ref[pl.ds(pl.multiple_of(jnp.maximum(t0 - halo, 0), halo), halo), :]
    bot = u_ref[pl.ds(pl.multiple_of(jnp.minimum(t0 + tm, S - halo), halo), halo), :]
    top = jnp.where(j > 0, top, 0.0)
    bot = jnp.where(j < pl.num_programs(1) - 1, bot, 0.0)
    win = jnp.concatenate([top, u_ref[pl.ds(t0, tm), :], bot], axis=0)
    rows = tm + 2 * halo
    pad = (CONV_KERNEL - 1) // 2
    w = w_ref[...]
    acc = jnp.zeros((tm, u_ref.shape[1]), F32)
    for r in range(8):
        offs = [k for k in range(CONV_KERNEL) if (halo - pad + k) % 8 == r]
        if not offs:
            continue
        xr = win if r == 0 else pltpu.roll(win, rows - r, 0)
        for k in offs:
            a0 = (halo - pad + k) - r
            acc = acc + xr[a0:a0 + tm, :] * w[k:k + 1, :]
    y = acc + cb_ref[...]
    mu = jnp.mean(y, axis=-1, keepdims=True)
    yc = y - mu
    var = jnp.mean(yc * yc, axis=-1, keepdims=True)
    z = yc * lax.rsqrt(var + LN_EPS) * lg_ref[...] + lb_ref[...]
    out_ref[...] = (z * jax.nn.sigmoid(z)).astype(BF16)


def _conv(u, w, cb, lg, lb, B, S, tm):
    cw = u.shape[1]
    halo = 16
    full = lambda a: pl.BlockSpec(a.shape, lambda b, j: (0,) * a.ndim)
    nj = S // tm
    return pl.pallas_call(
        functools.partial(_conv_kernel, tm=tm, halo=halo),
        grid=(B, nj),
        in_specs=[pl.BlockSpec((S, cw), lambda b, j: (b, 0)), full(w), full(cb), full(lg), full(lb)],
        out_specs=pl.BlockSpec((tm, cw), lambda b, j: (b * nj + j, 0)),
        out_shape=jax.ShapeDtypeStruct((B * S, cw), BF16),
        compiler_params=_params(("parallel", "parallel")),
        name="conv",
    )(u, w, cb, lg, lb)


def _memkv_kernel(m_ref, g_ref, wk_ref, wv_ref, k_ref, v_ref):
    m = _rms(m_ref[...], g_ref[...]).astype(BF16)
    k_ref[...] = _dot(m, wk_ref[...]).astype(BF16)
    v_ref[...] = _dot(m, wv_ref[...]).astype(BF16)


def _memkv(mem2d, g, wk, wv, M):
    R, D = mem2d.shape
    full = lambda a: pl.BlockSpec(a.shape, lambda i: (0,) * a.ndim)
    blk = pl.BlockSpec((M, D), lambda i: (i, 0))
    return pl.pallas_call(
        _memkv_kernel,
        grid=(R // M,),
        in_specs=[blk, full(g), full(wk), full(wv)],
        out_specs=[blk, blk],
        out_shape=[jax.ShapeDtypeStruct((R, D), BF16)] * 2,
        compiler_params=_params(("parallel",)),
        name="memkv",
    )(mem2d, g, wk, wv)


def _mid_kernel(x_ref, a_ref, b_ref, wout_ref, gx_ref, wq_ref, km_ref, vm_ref, wo_ref,
                gf_ref, rw_ref, rb_ref, x2_ref, h3_ref, idx_ref, gate_ref, *, n_exp):
    ab = jnp.concatenate([a_ref[...], b_ref[...]], axis=1)
    x1 = x_ref[...] + _dot(ab, wout_ref[...])
    q2 = _dot(_rms(x1, gx_ref[...]).astype(BF16), wq_ref[...])
    D = q2.shape[1]
    dh = D // XATTN_HEADS
    outs = []
    for h in range(XATTN_HEADS):
        sl = slice(h * dh, (h + 1) * dh)
        s = _dot_nt(q2[:, sl].astype(BF16), km_ref[:, sl]) * (dh ** -0.5)
        p = jnp.exp(s - jnp.max(s, axis=-1, keepdims=True))
        p = p / jnp.sum(p, axis=-1, keepdims=True)
        outs.append(_dot(p.astype(BF16), vm_ref[:, sl]))
    o = jnp.concatenate(outs, axis=1).astype(BF16)
    x2 = x1 + _dot(o, wo_ref[...])
    x2_ref[...] = x2
    h3 = _rms(x2, gf_ref[...])
    _store_rows_as_tiles(h3_ref, h3)
    lane = lax.broadcasted_iota(jnp.int32, (x2.shape[0], LANES), 1)
    logits = _dot(h3.astype(BF16), rw_ref[...]) + rb_ref[...]
    cur = jnp.where(lane < n_exp, logits, -jnp.inf)
    lane_f = lane.astype(F32)
    vals, idxs = [], []
    for _ in range(TOP_K):
        m = jnp.max(cur, axis=-1, keepdims=True)
        ik = jnp.min(jnp.where(cur == m, lane_f, float(LANES)), axis=-1, keepdims=True)
        vals.append(m)
        idxs.append(ik)
        cur = jnp.where(lane_f == ik, -jnp.inf, cur)
    es = [jnp.exp(vk - vals[0]) for vk in vals]
    den = es[0]
    for ek in es[1:]:
        den = den + ek
    idx_out = jnp.zeros(lane.shape, F32)
    gate_out = jnp.zeros(lane.shape, F32)
    for kk in range(TOP_K):
        idx_out = jnp.where(lane == kk, idxs[kk], idx_out)
        gate_out = jnp.where(lane == kk, es[kk] / den, gate_out)
    idx_ref[...] = idx_out.astype(jnp.int32)
    gate_ref[...] = gate_out


def _mid(x2d, a, b, wout, gx, wq, km, vm, wo, gf, rw, rb, S, M, n_exp, tm):
    T, D = x2d.shape
    half = a.shape[1]
    per_b = S // tm
    row = lambda n: pl.BlockSpec((tm, n), lambda i: (i, 0))
    full = lambda arr: pl.BlockSpec(arr.shape, lambda i: (0,) * arr.ndim)
    memb = pl.BlockSpec((M, D), lambda i: (i // per_b, 0))
    return pl.pallas_call(
        functools.partial(_mid_kernel, n_exp=n_exp),
        grid=(T // tm,),
        in_specs=[row(D), row(half), row(half), full(wout), full(gx), full(wq), memb, memb,
                  full(wo), full(gf), full(rw), full(rb)],
        out_specs=[row(D), pl.BlockSpec((tm * SUBLANES, LANES), lambda i: (i, 0)),
                   row(LANES), row(LANES)],
        out_shape=[jax.ShapeDtypeStruct((T, D), F32),
                   jax.ShapeDtypeStruct((T * SUBLANES, LANES), F32),
                   jax.ShapeDtypeStruct((T, LANES), jnp.int32),
                   jax.ShapeDtypeStruct((T, LANES), F32)],
        compiler_params=_params(("parallel",)),
        name="mid",
    )(x2d, a, b, wout, gx, wq, km, vm, wo, gf, rw, rb)


def _onehot(idx_ref, kk, lane):
    return (lane == idx_ref[:, kk:kk + 1]).astype(F32)


def _rank_kernel(idx_ref, tri_ref, rank_ref, cnt_ref, carry_ref):
    i = pl.program_id(0)

    @pl.when(i == 0)
    def _():
        carry_ref[...] = jnp.zeros_like(carry_ref)

    tm = idx_ref.shape[0]
    lane = lax.broadcasted_iota(jnp.int32, (tm, LANES), 1)
    hots = [_onehot(idx_ref, kk, lane) for kk in range(TOP_K)]
    mask = hots[0]
    for hk in hots[1:]:
        mask = mask + hk
    incl = _dot(tri_ref[...], mask.astype(BF16))
    before = incl - mask + carry_ref[0:1, :]
    out = jnp.zeros((tm, LANES), F32)
    for kk in range(TOP_K):
        rk = jnp.sum(hots[kk] * before, axis=-1, keepdims=True)
        out = jnp.where(lane == kk, rk, out)
    rank_ref[...] = out.astype(jnp.int32)
    carry_ref[...] = carry_ref[...] + incl[tm - 1:tm, :]
    cnt_ref[...] = carry_ref[...]


def _rank(idx, tm):
    T = idx.shape[0]
    tri = jnp.asarray(np.tril(np.ones((tm, tm), np.float32)), BF16)
    row = pl.BlockSpec((tm, LANES), lambda i: (i, 0))
    return pl.pallas_call(
        _rank_kernel,
        grid=(T // tm,),
        in_specs=[row, pl.BlockSpec((tm, tm), lambda i: (0, 0))],
        out_specs=[row, pl.BlockSpec((8, LANES), lambda i: (0, 0))],
        out_shape=[jax.ShapeDtypeStruct((T, LANES), jnp.int32),
                   jax.ShapeDtypeStruct((8, LANES), F32)],
        scratch_shapes=[pltpu.VMEM((8, LANES), F32)],
        compiler_params=_params(("arbitrary",)),
        name="rank",
    )(idx, tri)


def _dest_kernel(idx_ref, rank_ref, pstart_ref, dest_ref):
    tm = idx_ref.shape[0]
    lane = lax.broadcasted_iota(jnp.int32, (tm, LANES), 1)
    out = jnp.zeros((tm, LANES), F32)
    for kk in range(TOP_K):
        base = jnp.sum(_onehot(idx_ref, kk, lane) * pstart_ref[0:1, :], axis=-1, keepdims=True)
        out = jnp.where(lane == kk, base, out)
    dest_ref[...] = out.astype(jnp.int32) + rank_ref[...]


def _dest(idx, rank, pstart, tm):
    T = idx.shape[0]
    row = pl.BlockSpec((tm, LANES), lambda i: (i, 0))
    return pl.pallas_call(
        _dest_kernel,
        grid=(T // tm,),
        in_specs=[row, row, pl.BlockSpec((8, LANES), lambda i: (0, 0))],
        out_specs=row,
        out_shape=jax.ShapeDtypeStruct((T, LANES), jnp.int32),
        compiler_params=_params(("parallel",)),
        name="dest",
    )(idx, rank, pstart)


ROW_UNROLL = 8


def _tile(ref, row):
    return ref.at[pl.ds(pl.multiple_of(row * SUBLANES, SUBLANES), SUBLANES)]


def _wait_rows(buf_ref, sem):
    pltpu.make_async_copy(buf_ref, buf_ref, sem).wait()


def _dispatch_kernel(dest_ref, h_ref, xs_in_ref, xs_ref, sem):
    del xs_in_ref
    tm = h_ref.shape[0] // SUBLANES

    def issue(i, carry):
        for u in range(ROW_UNROLL):
            t = i * ROW_UNROLL + u
            for kk in range(TOP_K):
                pltpu.make_async_copy(_tile(h_ref, t), _tile(xs_ref, dest_ref[t * TOP_K + kk]),
                                      sem).start(priority=kk % 2)
        return carry

    lax.fori_loop(0, tm // ROW_UNROLL, issue, 0)
    for kk in range(TOP_K):
        _wait_rows(h_ref, sem)


def _dispatch(dest_flat, h3, n_rows, tm):
    T = h3.shape[0] // SUBLANES
    xs0 = jnp.zeros((n_rows * SUBLANES, LANES), F32)
    return pl.pallas_call(
        _dispatch_kernel,
        grid=(T // tm,),
        in_specs=[pl.BlockSpec((tm * TOP_K,), lambda i: (i,), memory_space=pltpu.SMEM),
                  pl.BlockSpec((tm * SUBLANES, LANES), lambda i: (i, 0)),
                  pl.BlockSpec(memory_space=pl.ANY)],
        out_specs=pl.BlockSpec(memory_space=pl.ANY),
        out_shape=jax.ShapeDtypeStruct((n_rows * SUBLANES, LANES), F32),
        scratch_shapes=[pltpu.SemaphoreType.DMA(())],
        input_output_aliases={2: 0},
        compiler_params=_params(("arbitrary",)),
        name="dispatch",
    )(dest_flat, h3, xs0)


def _experts_kernel(be_ref, nb_ref, xs_ref, wu_ref, bu_ref, wd_ref, bd_ref, ys_ref, wu_bf, wd_bf):
    i = pl.program_id(0)

    @pl.when((i == 0) | (be_ref[i] != be_ref[jnp.maximum(i - 1, 0)]))
    def _():
        wu_bf[...] = wu_ref[0].astype(BF16)
        wd_bf[...] = wd_ref[0].astype(BF16)

    @pl.when(i < nb_ref[0])
    def _():
        F = wd_bf.shape[0]
        x = _load_rows_from_tiles(xs_ref).astype(BF16)
        gu = _dot(x, wu_bf[...]) + bu_ref[0]
        x_glu = jnp.minimum(gu[:, :F], SWIGLU_LIMIT)
        x_lin = jnp.clip(gu[:, F:], -SWIGLU_LIMIT, SWIGLU_LIMIT)
        hid = x_glu * jax.nn.sigmoid(SWIGLU_ALPHA * x_glu) * (x_lin + 1.0)
        _store_rows_as_tiles(ys_ref, _dot(hid.astype(BF16), wd_bf[...]) + bd_ref[0])

    @pl.when(i >= nb_ref[0])
    def _():
        ys_ref[...] = jnp.zeros_like(ys_ref)


def _experts(block_e, n_used, xs, wu, bu, wd, bd):
    E, D, F2 = wu.shape
    F = wd.shape[1]
    nb = xs.shape[0] // (MOE_ROWS * SUBLANES)
    rows = pl.BlockSpec((MOE_ROWS * SUBLANES, LANES), lambda i, be, nu: (i, 0))
    grid_spec = pltpu.PrefetchScalarGridSpec(
        num_scalar_prefetch=2,
        grid=(nb,),
        in_specs=[rows,
                  pl.BlockSpec((1, D, F2), lambda i, be, nu: (be[i], 0, 0)),
                  pl.BlockSpec((1, 1, F2), lambda i, be, nu: (be[i], 0, 0)),
                  pl.BlockSpec((1, F, D), lambda i, be, nu: (be[i], 0, 0)),
                  pl.BlockSpec((1, 1, D), lambda i, be, nu: (be[i], 0, 0))],
        out_specs=rows,
        scratch_shapes=[pltpu.VMEM((D, F2), BF16), pltpu.VMEM((F, D), BF16)],
    )
    return pl.pallas_call(
        _experts_kernel,
        grid_spec=grid_spec,
        out_shape=jax.ShapeDtypeStruct(xs.shape, F32),
        compiler_params=_params(("arbitrary",)),
        name="experts",
    )(block_e, n_used, xs, wu, bu, wd, bd)


def _combine_kernel(dcur_ref, dnext_ref, x2_ref, gate_ref, g_ref, ys_ref, out_ref, buf0, buf1, sem,
                    *, apply_norm):
    j = pl.program_id(0)
    tm = x2_ref.shape[0] // 2

    def start(tbl_ref, base, buf, s):
        def body(i, carry):
            for u in range(ROW_UNROLL):
                t = i * ROW_UNROLL + u
                for kk in range(TOP_K):
                    pltpu.make_async_copy(_tile(ys_ref, tbl_ref[(base + t) * TOP_K + kk]),
                                          _tile(buf.at[kk], t), s).start(priority=kk % 2)
            return carry

        lax.fori_loop(0, tm // ROW_UNROLL, body, 0)

    def finish(buf, s, rows):
        _wait_rows(buf, s)
        x = x2_ref[rows, :]
        for kk in range(TOP_K):
            x = x + _load_rows_from_tiles(buf.at[kk]) * gate_ref[rows, kk:kk + 1]
        out_ref[rows, :] = _rms(x, g_ref[...]) if apply_norm else x

    @pl.when(j == 0)
    def _():
        start(dcur_ref, 0, buf0, sem.at[0])

    start(dcur_ref, tm, buf1, sem.at[1])
    finish(buf0, sem.at[0], pl.ds(0, tm))

    @pl.when(j < pl.num_programs(0) - 1)
    def _():
        start(dnext_ref, 0, buf0, sem.at[0])

    finish(buf1, sem.at[1], pl.ds(tm, tm))


def _combine(dest_flat, x2, gates, g, ys, tm, apply_norm):
    T, D = x2.shape
    steps = T // (2 * tm)
    tbl = lambda f: pl.BlockSpec((2 * tm * TOP_K,), f, memory_space=pltpu.SMEM)
    buf = pltpu.VMEM((TOP_K, tm * SUBLANES, LANES), F32)
    return pl.pallas_call(
        functools.partial(_combine_kernel, apply_norm=apply_norm),
        grid=(steps,),
        in_specs=[tbl(lambda j: (j,)), tbl(lambda j: (jnp.minimum(j + 1, steps - 1),)),
                  pl.BlockSpec((2 * tm, D), lambda j: (j, 0)),
                  pl.BlockSpec((2 * tm, LANES), lambda j: (j, 0)),
                  pl.BlockSpec(g.shape, lambda j: (0, 0)),
                  pl.BlockSpec(memory_space=pl.ANY)],
        out_specs=pl.BlockSpec((2 * tm, D), lambda j: (j, 0)),
        out_shape=jax.ShapeDtypeStruct((T, D), F32),
        scratch_shapes=[buf, buf, pltpu.SemaphoreType.DMA((2,))],
        compiler_params=_params(("arbitrary",)),
        name="combine",
    )(dest_flat, dest_flat, x2, gates, g, ys)


def _row(v):
    return v.reshape(1, -1).astype(F32)


def kernel(x, mem, norm_mix_g, w_in, gate_w2_fwd, gate_b_fwd, gate_w2_bwd, gate_b_bwd,
           gla_norm_g, conv_w, conv_b, conv_ln_g, conv_ln_b, w_out, norm_xattn_g,
           norm_mem_g, xattn_wq, xattn_wk, xattn_wv, xattn_wo, norm_ffn_g, router_w,
           router_b, exp_w_up, exp_b_up, exp_w_down, exp_b_down, final_norm_g):
    B, S, D = x.shape
    M = mem.shape[1]
    T = B * S
    depth = w_in.shape[0]
    kw = GLA_HEADS * GLA_DK
    vw = GLA_HEADS * GLA_DV
    rank_w = gate_w2_fwd.shape[1]
    cw = conv_w.shape[2]
    E = router_w.shape[2]
    assert vw + cw == w_out.shape[1] and 2 * rank_w <= LANES and E <= LANES

    xc = x.reshape(T, D)
    for l in range(depth):
        o_gf = 2 * kw + 2 * vw
        o_glu = o_gf + 2 * rank_w
        wl = w_in[l]
        w_perm = jnp.concatenate(
            [wl[:, :o_gf], wl[:, o_glu:], wl[:, o_gf:o_glu],
             jnp.zeros((D, LANES - 2 * rank_w), wl.dtype)], axis=1).astype(BF16)
        w2 = jnp.zeros((LANES, 2 * kw), F32)
        w2 = w2.at[:rank_w, :kw].set(gate_w2_fwd[l]).at[rank_w:2 * rank_w, kw:].set(gate_w2_bwd[l])
        b2 = jnp.concatenate([gate_b_fwd[l], gate_b_bwd[l]]).reshape(1, -1)

        q, k, v, gr, la, u = _inproj(xc, _row(norm_mix_g[l]), w_perm, w2.astype(BF16), b2,
                                     kw, vw, cw, tm=512)
        a_out = _gla(q, k, v, la, gr, _row(gla_norm_g[l]), B, S)
        b_out = _conv(u, conv_w[l], _row(conv_b[l]), _row(conv_ln_g[l]), _row(conv_ln_b[l]),
                      B, S, tm=512)
        km, vm = _memkv(mem.reshape(B * M, D), _row(norm_mem_g[l]),
                        xattn_wk[l].astype(BF16), xattn_wv[l].astype(BF16), M)

        rw = jnp.zeros((D, LANES), F32).at[:, :E].set(router_w[l]).astype(BF16)
        rb = jnp.zeros((1, LANES), F32).at[0, :E].set(router_b[l])
        x2, h3, idx, gates = _mid(
            xc, a_out, b_out, w_out[l].astype(BF16), _row(norm_xattn_g[l]),
            xattn_wq[l].astype(BF16), km, vm, xattn_wo[l].astype(BF16),
            _row(norm_ffn_g[l]), rw, rb, S, M, E, tm=256)

        rank, cnt = _rank(idx, tm=512)
        counts = cnt[0, :E].astype(jnp.int32)
        padded = ((counts + MOE_ROWS - 1) // MOE_ROWS) * MOE_ROWS
        pends = jnp.cumsum(padded)
        pstart = jnp.zeros((8, LANES), F32).at[0, :E].set((pends - padded).astype(F32))
        dest = _dest(idx, rank, pstart, tm=512)
        dest_flat = dest[:, :TOP_K].reshape(T * TOP_K)
        n_blocks = (T * TOP_K + E * (MOE_ROWS - 1)) // MOE_ROWS
        blk_start = jnp.arange(n_blocks, dtype=jnp.int32) * MOE_ROWS
        block_e = jnp.minimum(jnp.sum(pends[None, :] <= blk_start[:, None], axis=1), E - 1)
        n_used = (pends[E - 1] // MOE_ROWS).reshape(1)

        xs = _dispatch(dest_flat, h3, n_blocks * MOE_ROWS, tm=256)
        ys = _experts(block_e.astype(jnp.int32), n_used.astype(jnp.int32), xs,
                      exp_w_up[l], exp_b_up[l][:, None, :],
                      exp_w_down[l], exp_b_down[l][:, None, :])
        xc = _combine(dest_flat, x2, gates, _row(final_norm_g), ys, tm=128,
                      apply_norm=(l == depth - 1))
    return xc.reshape(B, S, D)
```

```python
import functools

import numpy as np
import jax
import jax.numpy as jnp
from jax import lax
from jax.experimental import pallas as pl
from jax.experimental.pallas import tpu as pltpu

F32 = jnp.float32
BF16 = jnp.bfloat16

GLA_HEADS = 4
GLA_DK = 64
GLA_DV = 128
GLA_CHUNK = 64
GLA_TILE = 4
GATE_TAU = 16.0
CONV_KERNEL = 31
XATTN_HEADS = 4
TOP_K = 4
SWIGLU_ALPHA = 1.702
SWIGLU_LIMIT = 7.0
RMS_EPS = 1e-6
LN_EPS = 1e-5

LANES = 128
SUBLANES = 8
MOE_ROWS = 256
VMEM_LIMIT = 56 * 1024 * 1024


def _dot(a, b):
    return jnp.dot(a, b, preferred_element_type=F32)


def _dot_nt(a, b):
    return lax.dot_general(a, b, (((1,), (1,)), ((), ())), preferred_element_type=F32)


def _dot_tn(a, b):
    return lax.dot_general(a, b, (((0,), (0,)), ((), ())), preferred_element_type=F32)


def _rms(x, g):
    return x * lax.rsqrt(jnp.mean(x * x, axis=-1, keepdims=True) + RMS_EPS) * g


def _params(sem):
    return pltpu.CompilerParams(dimension_semantics=sem, vmem_limit_bytes=VMEM_LIMIT)


def _store_rows_as_tiles(ref, val):
    n, d = val.shape
    assert d == SUBLANES * LANES and ref.shape == (n * SUBLANES, LANES)
    for s in range(SUBLANES):
        ref[pl.ds(s, n, stride=SUBLANES), :] = val[:, s * LANES:(s + 1) * LANES]


def _load_rows_from_tiles(ref):
    n = ref.shape[0] // SUBLANES
    return jnp.concatenate([ref[pl.ds(s, n, stride=SUBLANES), :] for s in range(SUBLANES)], axis=1)


def _inproj_kernel(x_ref, g_ref, w_ref, w2_ref, b2_ref,
                   q_ref, k_ref, v_ref, gr_ref, la_ref, u_ref, *, kw, vw, cw):
    h = _rms(x_ref[...], g_ref[...]).astype(BF16)
    acc = _dot(h, w_ref[...])
    o = 0
    q_ref[...] = (acc[:, o:o + kw] * (GLA_DK ** -0.5)).astype(BF16); o += kw
    k_ref[...] = acc[:, o:o + kw].astype(BF16); o += kw
    v_ref[...] = acc[:, o:o + vw].astype(BF16); o += vw
    r = acc[:, o:o + vw]; o += vw
    gr_ref[...] = (r * jax.nn.sigmoid(r)).astype(BF16)
    a = acc[:, o:o + cw]; o += cw
    g = acc[:, o:o + cw]; o += cw
    u_ref[...] = a * jax.nn.sigmoid(g)
    z = _dot(acc[:, o:o + LANES].astype(BF16), w2_ref[...]) + b2_ref[...]
    la_ref[...] = (jnp.minimum(z, 0.0) - jnp.log1p(jnp.exp(-jnp.abs(z)))) * (1.0 / GATE_TAU)


def _inproj(x2d, g, w, w2, b2, kw, vw, cw, tm):
    T, D = x2d.shape
    W = w.shape[1]
    row = lambda n: pl.BlockSpec((tm, n), lambda i: (i, 0))
    full = lambda a: pl.BlockSpec(a.shape, lambda i: (0,) * a.ndim)
    return pl.pallas_call(
        functools.partial(_inproj_kernel, kw=kw, vw=vw, cw=cw),
        grid=(T // tm,),
        in_specs=[row(D), full(g), full(w), full(w2), full(b2)],
        out_specs=[row(kw), row(kw), row(vw), row(vw), row(2 * kw), row(cw)],
        out_shape=[jax.ShapeDtypeStruct((T, kw), BF16), jax.ShapeDtypeStruct((T, kw), BF16),
                   jax.ShapeDtypeStruct((T, vw), BF16), jax.ShapeDtypeStruct((T, vw), BF16),
                   jax.ShapeDtypeStruct((T, 2 * kw), F32), jax.ShapeDtypeStruct((T, cw), F32)],
        compiler_params=_params(("parallel",)),
        name="inproj",
    )(x2d, g, w, w2, b2)


def _gla_levels():
    ms, m = [], GLA_CHUNK // 2
    while m >= GLA_TILE:
        ms.append(m)
        m //= 2
    return ms


def _gla_constants(bwd):
    C, H = GLA_CHUNK, GLA_HEADS
    t = np.arange(C)[:, None]
    s = np.arange(C)[None, :]
    lev_q, lev_mask = [], []
    for m in _gla_levels():
        g = 2 * m
        second = (t % g) >= m
        same = t // g == s // g
        if not bwd:
            qrow = second
            mask = same & ((t % g) >= m) & ((s % g) < m)
        else:
            qrow = ~second
            mask = same & ((t % g) < m) & ((s % g) >= m)
        lev_q.append(np.broadcast_to(qrow, (C, H * GLA_DK)))
        lev_mask.append(np.tile(mask, (1, H)))
    shifts = range(0, GLA_TILE) if not bwd else range(1, GLA_TILE)
    sh_mask = []
    for sh in shifts:
        j = t - sh if not bwd else t + sh
        sh_mask.append(np.tile((s == j) & (t // GLA_TILE == s // GLA_TILE), (1, H)))
    f = lambda xs: np.stack(xs).astype(np.float32)
    return f(lev_q), f(lev_mask), f(sh_mask)


def _gla_head_mask():
    C, H = GLA_CHUNK, GLA_HEADS
    assert GLA_DK == C
    blk = np.arange(H * C) // C
    return (blk[:, None] == blk[None, :]).astype(np.float32)


def _gla_chunk(la, q, k, v, st_ref, c_ref, tri_ref, lq_ref, lm_ref, sm_ref, hm_ref, bwd):
    C, H = GLA_CHUNK, GLA_HEADS
    kw = q.shape[1]
    hm = hm_ref[...]

    def stack(xb):
        return jnp.concatenate([xb] * H, axis=0) * hm

    tri = tri_ref[...]
    hi = la.astype(BF16)
    incl = _dot(tri, hi) + _dot(tri, (la - hi.astype(F32)).astype(BF16))
    yield None
    tot = incl[C - 1:C]
    cum = incl if not bwd else incl - la
    x_q, x_k = (cum, tot - cum) if not bwd else (tot - cum, cum)
    e_q, e_k, a_tot = jnp.exp(x_q), jnp.exp(x_k), jnp.exp(tot)
    c_ref[...] = cum

    a = jnp.zeros((C, H * C), F32)
    for lv, m in enumerate(_gla_levels()):
        g = 2 * m
        edge = m - 1 if not bwd else m
        r = jnp.concatenate([jnp.broadcast_to(c_ref[s0 + edge:s0 + edge + 1, :], (g, kw))
                             for s0 in range(0, C, g)], axis=0)
        d = cum - r
        z = (jnp.where(lq_ref[lv] > 0, q, k) * jnp.exp(jnp.minimum(d, -d))).astype(BF16)
        a = a + _dot_nt(z, stack(z)) * lm_ref[lv]
    ps = []
    for si in range(sm_ref.shape[0]):
        sh = si if not bwd else si + 1
        if sh == 0:
            ps.append((q * k).astype(BF16))
            continue
        amt = sh if not bwd else C - sh
        kr = pltpu.roll(k, amt, 0)
        cr = pltpu.roll(cum, amt, 0)
        d = (cum - cr) if not bwd else (cr - cum)
        ps.append((q * kr * jnp.exp(jnp.minimum(d, 0.0))).astype(BF16))
    sc = _dot(jnp.concatenate(ps, axis=0), hm)
    yield None
    for si in range(sm_ref.shape[0]):
        a = a + sc[si * C:(si + 1) * C] * sm_ref[si]

    vst = jnp.concatenate([v[:, h * GLA_DV:(h + 1) * GLA_DV] for h in range(H)], axis=0)
    st = st_ref[...]
    o4 = _dot(stack(a.astype(BF16)), vst) + _dot_nt(stack((q * e_q).astype(BF16)), st.astype(BF16))
    st_ref[...] = st * a_tot + _dot_tn(vst, stack((k * e_k).astype(BF16)))
    yield jnp.concatenate([o4[h * C:(h + 1) * C] for h in range(H)], axis=1)


def _gla_kernel(q_ref, k_ref, v_ref, la_ref, gr_ref, ng_ref,
                lqf_ref, lmf_ref, smf_ref, lqb_ref, lmb_ref, smb_ref, tri_ref, hm_ref,
                out_ref, of_ref, ob_ref, stf_ref, stb_ref, cf_ref, cb_ref):
    C = GLA_CHUNK
    S = q_ref.shape[0]
    kw = q_ref.shape[1]
    n = S // C
    stf_ref[...] = jnp.zeros_like(stf_ref)
    stb_ref[...] = jnp.zeros_like(stb_ref)

    def body(i, carry):
        rf = pl.ds(pl.multiple_of(i * C, C), C)
        rb = pl.ds(pl.multiple_of((n - 1 - i) * C, C), C)
        fwd = _gla_chunk(
            la_ref[rf, 0:kw], q_ref[rf, :].astype(F32), k_ref[rf, :].astype(F32), v_ref[rf, :],
            stf_ref, cf_ref, tri_ref, lqf_ref, lmf_ref, smf_ref, hm_ref, False)
        bwd = _gla_chunk(
            la_ref[rb, kw:2 * kw], q_ref[rb, :].astype(F32), k_ref[rb, :].astype(F32), v_ref[rb, :],
            stb_ref, cb_ref, tri_ref, lqb_ref, lmb_ref, smb_ref, hm_ref, True)
        for o_f, o_b in zip(fwd, bwd):
            pass
        of_ref[rf, :] = o_f
        ob_ref[rb, :] = o_b
        return carry

    lax.fori_loop(0, n, body, 0)

    rows = 256
    def fin(i, carry):
        r = pl.ds(pl.multiple_of(i * rows, rows), rows)
        o = of_ref[r, :] + ob_ref[r, :]
        ys = []
        for h in range(GLA_HEADS):
            oh = o[:, h * GLA_DV:(h + 1) * GLA_DV]
            ys.append(_rms(oh, ng_ref[...]))
        y = jnp.concatenate(ys, axis=1) * gr_ref[r, :].astype(F32)
        out_ref[r, :] = y.astype(BF16)
        return carry

    lax.fori_loop(0, S // rows, fin, 0)


def _gla(q, k, v, la, gr, ng, B, S):
    kw, vw = q.shape[1], v.shape[1]
    cf = _gla_constants(False)
    cb = _gla_constants(True)
    tri = np.tril(np.ones((GLA_CHUNK, GLA_CHUNK), np.float32))
    consts = [jnp.asarray(c) for c in cf] + [jnp.asarray(c) for c in cb] + \
             [jnp.asarray(tri, BF16), jnp.asarray(_gla_head_mask(), BF16)]
    seq = lambda n: pl.BlockSpec((S, n), lambda b: (b, 0))
    full = lambda a: pl.BlockSpec(a.shape, lambda b: (0,) * a.ndim)
    return pl.pallas_call(
        _gla_kernel,
        grid=(B,),
        in_specs=[seq(kw), seq(kw), seq(vw), seq(2 * kw), seq(vw), full(ng)] + [full(c) for c in consts],
        out_specs=seq(vw),
        out_shape=jax.ShapeDtypeStruct((B * S, vw), BF16),
        scratch_shapes=[pltpu.VMEM((S, vw), F32), pltpu.VMEM((S, vw), F32),
                        pltpu.VMEM((GLA_DV, kw), F32), pltpu.VMEM((GLA_DV, kw), F32),
                        pltpu.VMEM((GLA_CHUNK, kw), F32), pltpu.VMEM((GLA_CHUNK, kw), F32)],
        compiler_params=_params(("parallel",)),
        name="gla",
    )(q, k, v, la, gr, ng, *consts)


def _conv_kernel(u_ref, w_ref, cb_ref, lg_ref, lb_ref, out_ref, *, tm, halo):
    S = u_ref.shape[0]
    j = pl.program_id(1)
    t0 = pl.multiple_of(j * tm, tm)
    top = u_ref[pl.ds(pl.multiple_of(jnp.maximum(t0 - halo, 0), halo), halo), :]
    bot = u_ref[pl.ds(pl.multiple_of(jnp.minimum(t0 + tm, S - halo), halo), halo), :]
    top = jnp.where(j > 0, top, 0.0)
    bot = jnp.where(j < pl.num_programs(1) - 1, bot, 0.0)
    win = jnp.concatenate([top, u_ref[pl.ds(t0, tm), :], bot], axis=0)
    rows = tm + 2 * halo
    pad = (CONV_KERNEL - 1) // 2
    w = w_ref[...]
    acc = jnp.zeros((tm, u_ref.shape[1]), F32)
    for r in range(8):
        offs = [k for k in range(CONV_KERNEL) if (halo - pad + k) % 8 == r]
        if not offs:
            continue
        xr = win if r == 0 else pltpu.roll(win, rows - r, 0)
        for k in offs:
            a0 = (halo - pad + k) - r
            acc = acc + xr[a0:a0 + tm, :] * w[k:k + 1, :]
    y = acc + cb_ref[...]
    mu = jnp.mean(y, axis=-1, keepdims=True)
    yc = y - mu
    var = jnp.mean(yc * yc, axis=-1, keepdims=True)
    z = yc * lax.rsqrt(var + LN_EPS) * lg_ref[...] + lb_ref[...]
    out_ref[...] = (z * jax.nn.sigmoid(z)).astype(BF16)


def _conv(u, w, cb, lg, lb, B, S, tm):
    cw = u.shape[1]
    halo = 16
    full = lambda a: pl.BlockSpec(a.shape, lambda b, j: (0,) * a.ndim)
    nj = S // tm
    return pl.pallas_call(
        functools.partial(_conv_kernel, tm=tm, halo=halo),
        grid=(B, nj),
        in_specs=[pl.BlockSpec((S, cw), lambda b, j: (b, 0)), full(w), full(cb), full(lg), full(lb)],
        out_specs=pl.BlockSpec((tm, cw), lambda b, j: (b * nj + j, 0)),
        out_shape=jax.ShapeDtypeStruct((B * S, cw), BF16),
        compiler_params=_params(("parallel", "parallel")),
        name="conv",
    )(u, w, cb, lg, lb)


def _memkv_kernel(m_ref, g_ref, wk_ref, wv_ref, k_ref, v_ref):
    m = _rms(m_ref[...], g_ref[...]).astype(BF16)
    k_ref[...] = _dot(m, wk_ref[...]).astype(BF16)
    v_ref[...] = _dot(m, wv_ref[...]).astype(BF16)


def _memkv(mem2d, g, wk, wv, M):
    R, D = mem2d.shape
    full = lambda a: pl.BlockSpec(a.shape, lambda i: (0,) * a.ndim)
    blk = pl.BlockSpec((M, D), lambda i: (i, 0))
    return pl.pallas_call(
        _memkv_kernel,
        grid=(R // M,),
        in_specs=[blk, full(g), full(wk), full(wv)],
        out_specs=[blk, blk],
        out_shape=[jax.ShapeDtypeStruct((R, D), BF16)] * 2,
        compiler_params=_params(("parallel",)),
        name="memkv",
    )(mem2d, g, wk, wv)


def _mid_kernel(x_ref, a_ref, b_ref, wout_ref, gx_ref, wq_ref, km_ref, vm_ref, wo_ref,
                gf_ref, rw_ref, rb_ref, x2_ref, h3_ref, idx_ref, gate_ref, *, n_exp, parts):
    n = x_ref.shape[0] // parts
    gens = [_mid_rows(pl.ds(p * n, n), pl.ds(p * n * SUBLANES, n * SUBLANES),
                      x_ref, a_ref, b_ref, wout_ref, gx_ref, wq_ref, km_ref, vm_ref, wo_ref,
                      gf_ref, rw_ref, rb_ref, x2_ref, h3_ref, idx_ref, gate_ref, n_exp)
            for p in range(parts)]
    for _ in zip(*gens):
        pass


def _mid_rows(rows, tiles, x_ref, a_ref, b_ref, wout_ref, gx_ref, wq_ref, km_ref, vm_ref, wo_ref,
              gf_ref, rw_ref, rb_ref, x2_ref, h3_ref, idx_ref, gate_ref, n_exp):
    ab = jnp.concatenate([a_ref[rows, :], b_ref[rows, :]], axis=1)
    x1 = x_ref[rows, :] + _dot(ab, wout_ref[...])
    yield
    q2 = _dot(_rms(x1, gx_ref[...]).astype(BF16), wq_ref[...])
    yield
    D = q2.shape[1]
    dh = D // XATTN_HEADS
    ss = []
    for h in range(XATTN_HEADS):
        sl = slice(h * dh, (h + 1) * dh)
        ss.append(_dot_nt(q2[:, sl].astype(BF16), km_ref[:, sl]) * (dh ** -0.5))
    yield
    outs = []
    for h in range(XATTN_HEADS):
        sl = slice(h * dh, (h + 1) * dh)
        p = jnp.exp(ss[h] - jnp.max(ss[h], axis=-1, keepdims=True))
        p = p / jnp.sum(p, axis=-1, keepdims=True)
        outs.append(_dot(p.astype(BF16), vm_ref[:, sl]))
    yield
    o = jnp.concatenate(outs, axis=1).astype(BF16)
    x2 = x1 + _dot(o, wo_ref[...])
    yield
    x2_ref[rows, :] = x2
    h3 = _rms(x2, gf_ref[...])
    _store_rows_as_tiles(h3_ref.at[tiles], h3)
    lane = lax.broadcasted_iota(jnp.int32, (x2.shape[0], LANES), 1)
    logits = _dot(h3.astype(BF16), rw_ref[...]) + rb_ref[...]
    yield
    cur = jnp.where(lane < n_exp, logits, -jnp.inf)
    lane_f = lane.astype(F32)
    vals, idxs = [], []
    for _ in range(TOP_K):
        m = jnp.max(cur, axis=-1, keepdims=True)
        ik = jnp.min(jnp.where(cur == m, lane_f, float(LANES)), axis=-1, keepdims=True)
        vals.append(m)
        idxs.append(ik)
        cur = jnp.where(lane_f == ik, -jnp.inf, cur)
    es = [jnp.exp(vk - vals[0]) for vk in vals]
    den = es[0]
    for ek in es[1:]:
        den = den + ek
    idx_out = jnp.zeros(lane.shape, F32)
    gate_out = jnp.zeros(lane.shape, F32)
    for kk in range(TOP_K):
        idx_out = jnp.where(lane == kk, idxs[kk], idx_out)
        gate_out = jnp.where(lane == kk, es[kk] / den, gate_out)
    idx_ref[rows, :] = idx_out.astype(jnp.int32)
    gate_ref[rows, :] = gate_out
    yield


def _mid(x2d, a, b, wout, gx, wq, km, vm, wo, gf, rw, rb, S, M, n_exp, tm):
    T, D = x2d.shape
    half = a.shape[1]
    per_b = S // tm
    row = lambda n: pl.BlockSpec((tm, n), lambda i: (i, 0))
    full = lambda arr: pl.BlockSpec(arr.shape, lambda i: (0,) * arr.ndim)
    memb = pl.BlockSpec((M, D), lambda i: (i // per_b, 0))
    return pl.pallas_call(
        functools.partial(_mid_kernel, n_exp=n_exp, parts=2),
        grid=(T // tm,),
        in_specs=[row(D), row(half), row(half), full(wout), full(gx), full(wq), memb, memb,
                  full(wo), full(gf), full(rw), full(rb)],
        out_specs=[row(D), pl.BlockSpec((tm * SUBLANES, LANES), lambda i: (i, 0)),
                   row(LANES), row(LANES)],
        out_shape=[jax.ShapeDtypeStruct((T, D), F32),
                   jax.ShapeDtypeStruct((T * SUBLANES, LANES), F32),
                   jax.ShapeDtypeStruct((T, LANES), jnp.int32),
                   jax.ShapeDtypeStruct((T, LANES), F32)],
        compiler_params=_params(("parallel",)),
        name="mid",
    )(x2d, a, b, wout, gx, wq, km, vm, wo, gf, rw, rb)


def _onehot(idx_ref, kk, lane):
    return (lane == idx_ref[:, kk:kk + 1]).astype(F32)


def _rank_kernel(idx_ref, tri_ref, rank_ref, cnt_ref, carry_ref):
    i = pl.program_id(0)

    @pl.when(i == 0)
    def _():
        carry_ref[...] = jnp.zeros_like(carry_ref)

    tm = idx_ref.shape[0]
    lane = lax.broadcasted_iota(jnp.int32, (tm, LANES), 1)
    hots = [_onehot(idx_ref, kk, lane) for kk in range(TOP_K)]
    mask = hots[0]
    for hk in hots[1:]:
        mask = mask + hk
    incl = _dot(tri_ref[...], mask.astype(BF16))
    before = incl - mask + carry_ref[0:1, :]
    out = jnp.zeros((tm, LANES), F32)
    for kk in range(TOP_K):
        rk = jnp.sum(hots[kk] * before, axis=-1, keepdims=True)
        out = jnp.where(lane == kk, rk, out)
    rank_ref[...] = out.astype(jnp.int32)
    carry_ref[...] = carry_ref[...] + incl[tm - 1:tm, :]
    cnt_ref[...] = carry_ref[...]


def _rank(idx, tm):
    T = idx.shape[0]
    tri = jnp.asarray(np.tril(np.ones((tm, tm), np.float32)), BF16)
    row = pl.BlockSpec((tm, LANES), lambda i: (i, 0))
    return pl.pallas_call(
        _rank_kernel,
        grid=(T // tm,),
        in_specs=[row, pl.BlockSpec((tm, tm), lambda i: (0, 0))],
        out_specs=[row, pl.BlockSpec((8, LANES), lambda i: (0, 0))],
        out_shape=[jax.ShapeDtypeStruct((T, LANES), jnp.int32),
                   jax.ShapeDtypeStruct((8, LANES), F32)],
        scratch_shapes=[pltpu.VMEM((8, LANES), F32)],
        compiler_params=_params(("arbitrary",)),
        name="rank",
    )(idx, tri)


def _dest_kernel(idx_ref, rank_ref, pstart_ref, dest_ref):
    tm = idx_ref.shape[0]
    lane = lax.broadcasted_iota(jnp.int32, (tm, LANES), 1)
    out = jnp.zeros((tm, LANES), F32)
    for kk in range(TOP_K):
        base = jnp.sum(_onehot(idx_ref, kk, lane) * pstart_ref[0:1, :], axis=-1, keepdims=True)
        out = jnp.where(lane == kk, base, out)
    dest_ref[...] = out.astype(jnp.int32) + rank_ref[...]


def _dest(idx, rank, pstart, tm):
    T = idx.shape[0]
    row = pl.BlockSpec((tm, LANES), lambda i: (i, 0))
    return pl.pallas_call(
        _dest_kernel,
        grid=(T // tm,),
        in_specs=[row, row, pl.BlockSpec((8, LANES), lambda i: (0, 0))],
        out_specs=row,
        out_shape=jax.ShapeDtypeStruct((T, LANES), jnp.int32),
        compiler_params=_params(("parallel",)),
        name="dest",
    )(idx, rank, pstart)


ROW_UNROLL = 8


def _tile(ref, row):
    return ref.at[pl.ds(pl.multiple_of(row * SUBLANES, SUBLANES), SUBLANES)]


def _wait_rows(buf_ref, sem):
    pltpu.make_async_copy(buf_ref, buf_ref, sem).wait()


def _dispatch_kernel(dest_ref, lo_ref, hi_ref, nb_ref, h_ref, xs_ref, zero_ref, sem, zsem):
    tm = h_ref.shape[0] // SUBLANES
    blk = zero_ref.shape[0]

    @pl.when(pl.program_id(0) == 0)
    def _():
        zero_ref[...] = jnp.zeros_like(zero_ref)

        def each_fill(fn):
            def expert(e, carry):
                def row(r, c):
                    fn(pltpu.make_async_copy(zero_ref.at[pl.ds(0, SUBLANES)], _tile(xs_ref, r), zsem))
                    return c

                return lax.fori_loop(lo_ref[e], hi_ref[e], row, carry)

            lax.fori_loop(0, lo_ref.shape[0], expert, 0)

            def block(b, carry):
                fn(pltpu.make_async_copy(
                    zero_ref, xs_ref.at[pl.ds(pl.multiple_of(b * blk, blk), blk)], zsem))
                return carry

            lax.fori_loop(nb_ref[0], xs_ref.shape[0] // blk, block, 0)

        each_fill(lambda cp: cp.start())
        each_fill(lambda cp: cp.wait())

    def issue(i, carry):
        for u in range(ROW_UNROLL):
            t = i * ROW_UNROLL + u
            for kk in range(TOP_K):
                pltpu.make_async_copy(_tile(h_ref, t), _tile(xs_ref, dest_ref[t * TOP_K + kk]),
                                      sem).start(priority=kk % 2)
        return carry

    lax.fori_loop(0, tm // ROW_UNROLL, issue, 0)
    for kk in range(TOP_K):
        _wait_rows(h_ref, sem)


def _dispatch(dest_flat, pad_lo, pad_hi, n_used, h3, n_rows, tm):
    T = h3.shape[0] // SUBLANES
    smem = pl.BlockSpec(memory_space=pltpu.SMEM)
    return pl.pallas_call(
        _dispatch_kernel,
        grid=(T // tm,),
        in_specs=[pl.BlockSpec((tm * TOP_K,), lambda i: (i,), memory_space=pltpu.SMEM),
                  smem, smem, smem,
                  pl.BlockSpec((tm * SUBLANES, LANES), lambda i: (i, 0))],
        out_specs=pl.BlockSpec(memory_space=pl.ANY),
        out_shape=jax.ShapeDtypeStruct((n_rows * SUBLANES, LANES), F32),
        scratch_shapes=[pltpu.VMEM((MOE_ROWS * SUBLANES, LANES), F32),
                        pltpu.SemaphoreType.DMA(()), pltpu.SemaphoreType.DMA(())],
        compiler_params=_params(("arbitrary",)),
        name="dispatch",
    )(dest_flat, pad_lo, pad_hi, n_used, h3)


def _experts_kernel(be_ref, nb_ref, first_ref, next_ref, slot_ref,
                    xs_ref, wu_hbm, bu_ref, wd_hbm, bd_ref, ys_ref,
                    wu_f32, wd_f32, wu_bf, wd_bf, sem):
    i = pl.program_id(0)

    def fetch(e, s):
        return (pltpu.make_async_copy(wu_hbm.at[e], wu_f32.at[s], sem.at[s]),
                pltpu.make_async_copy(wd_hbm.at[e], wd_f32.at[s], sem.at[s]))

    @pl.when(i == 0)
    def _():
        for cp in fetch(be_ref[0], 0):
            cp.start()

    @pl.when(first_ref[i] == 1)
    def _():
        s = slot_ref[i]
        for cp in fetch(be_ref[i], s):
            cp.wait()

        @pl.when(next_ref[i] >= 0)
        def _():
            for cp in fetch(next_ref[i], 1 - s):
                cp.start()

        wu_bf[...] = wu_f32[s].astype(BF16)
        wd_bf[...] = wd_f32[s].astype(BF16)

    @pl.when(i < nb_ref[0])
    def _():
        F = wd_bf.shape[0]
        x = _load_rows_from_tiles(xs_ref).astype(BF16)
        gu = _dot(x, wu_bf[...]) + bu_ref[0]
        x_glu = jnp.minimum(gu[:, :F], SWIGLU_LIMIT)
        x_lin = jnp.clip(gu[:, F:], -SWIGLU_LIMIT, SWIGLU_LIMIT)
        hid = x_glu * jax.nn.sigmoid(SWIGLU_ALPHA * x_glu) * (x_lin + 1.0)
        _store_rows_as_tiles(ys_ref, _dot(hid.astype(BF16), wd_bf[...]) + bd_ref[0])

    @pl.when(i >= nb_ref[0])
    def _():
        ys_ref[...] = jnp.zeros_like(ys_ref)


def _expert_runs(block_e):
    nb = block_e.shape[0]
    ar = jnp.arange(nb, dtype=jnp.int32)
    first = jnp.concatenate([jnp.ones((1,), bool), block_e[1:] != block_e[:-1]])
    starts = jnp.where(first, ar, nb)
    later = lax.cummin(jnp.concatenate([starts[1:], jnp.full((1,), nb, jnp.int32)]), reverse=True)
    nxt = jnp.where(later < nb, block_e[jnp.minimum(later, nb - 1)], -1)
    slot = (jnp.cumsum(first.astype(jnp.int32)) - 1) & 1
    return first.astype(jnp.int32), nxt.astype(jnp.int32), slot.astype(jnp.int32)


def _experts(block_e, n_used, xs, wu, bu, wd, bd):
    E, D, F2 = wu.shape
    F = wd.shape[1]
    nb = xs.shape[0] // (MOE_ROWS * SUBLANES)
    rows = pl.BlockSpec((MOE_ROWS * SUBLANES, LANES), lambda i, *_: (i, 0))
    grid_spec = pltpu.PrefetchScalarGridSpec(
        num_scalar_prefetch=5,
        grid=(nb,),
        in_specs=[rows,
                  pl.BlockSpec(memory_space=pl.ANY),
                  pl.BlockSpec((1, 1, F2), lambda i, be, *_: (be[i], 0, 0)),
                  pl.BlockSpec(memory_space=pl.ANY),
                  pl.BlockSpec((1, 1, D), lambda i, be, *_: (be[i], 0, 0))],
        out_specs=rows,
        scratch_shapes=[pltpu.VMEM((2, D, F2), F32), pltpu.VMEM((2, F, D), F32),
                        pltpu.VMEM((D, F2), BF16), pltpu.VMEM((F, D), BF16),
                        pltpu.SemaphoreType.DMA((2,))],
    )
    return pl.pallas_call(
        _experts_kernel,
        grid_spec=grid_spec,
        out_shape=jax.ShapeDtypeStruct(xs.shape, F32),
        compiler_params=_params(("arbitrary",)),
        name="experts",
    )(block_e, n_used, *_expert_runs(block_e), xs, wu, bu, wd, bd)


def _combine_kernel(dcur_ref, dnext_ref, x2_ref, gate_ref, g_ref, ys_ref, out_ref, buf0, buf1, sem,
                    *, apply_norm):
    j = pl.program_id(0)
    tm = x2_ref.shape[0] // 2

    def start(tbl_ref, base, buf, s):
        def body(i, carry):
            for u in range(ROW_UNROLL):
                t = i * ROW_UNROLL + u
                for kk in range(TOP_K):
                    pltpu.make_async_copy(_tile(ys_ref, tbl_ref[(base + t) * TOP_K + kk]),
                                          _tile(buf.at[kk], t), s).start(priority=kk % 2)
            return carry

        lax.fori_loop(0, tm // ROW_UNROLL, body, 0)

    def finish(buf, s, rows):
        _wait_rows(buf, s)
        x = x2_ref[rows, :]
        for kk in range(TOP_K):
            x = x + _load_rows_from_tiles(buf.at[kk]) * gate_ref[rows, kk:kk + 1]
        out_ref[rows, :] = _rms(x, g_ref[...]) if apply_norm else x

    @pl.when(j == 0)
    def _():
        start(dcur_ref, 0, buf0, sem.at[0])

    start(dcur_ref, tm, buf1, sem.at[1])
    finish(buf0, sem.at[0], pl.ds(0, tm))

    @pl.when(j < pl.num_programs(0) - 1)
    def _():
        start(dnext_ref, 0, buf0, sem.at[0])

    finish(buf1, sem.at[1], pl.ds(tm, tm))


def _combine(dest_flat, x2, gates, g, ys, tm, apply_norm):
    T, D = x2.shape
    steps = T // (2 * tm)
    tbl = lambda f: pl.BlockSpec((2 * tm * TOP_K,), f, memory_space=pltpu.SMEM)
    buf = pltpu.VMEM((TOP_K, tm * SUBLANES, LANES), F32)
    return pl.pallas_call(
        functools.partial(_combine_kernel, apply_norm=apply_norm),
        grid=(steps,),
        in_specs=[tbl(lambda j: (j,)), tbl(lambda j: (jnp.minimum(j + 1, steps - 1),)),
                  pl.BlockSpec((2 * tm, D), lambda j: (j, 0)),
                  pl.BlockSpec((2 * tm, LANES), lambda j: (j, 0)),
                  pl.BlockSpec(g.shape, lambda j: (0, 0)),
                  pl.BlockSpec(memory_space=pl.ANY)],
        out_specs=pl.BlockSpec((2 * tm, D), lambda j: (j, 0)),
        out_shape=jax.ShapeDtypeStruct((T, D), F32),
        scratch_shapes=[buf, buf, pltpu.SemaphoreType.DMA((2,))],
        compiler_params=_params(("arbitrary",)),
        name="combine",
    )(dest_flat, dest_flat, x2, gates, g, ys)


def _row(v):
    return v.reshape(1, -1).astype(F32)


def kernel(x, mem, norm_mix_g, w_in, gate_w2_fwd, gate_b_fwd, gate_w2_bwd, gate_b_bwd,
           gla_norm_g, conv_w, conv_b, conv_ln_g, conv_ln_b, w_out, norm_xattn_g,
           norm_mem_g, xattn_wq, xattn_wk, xattn_wv, xattn_wo, norm_ffn_g, router_w,
           router_b, exp_w_up, exp_b_up, exp_w_down, exp_b_down, final_norm_g):
    B, S, D = x.shape
    M = mem.shape[1]
    T = B * S
    depth = w_in.shape[0]
    kw = GLA_HEADS * GLA_DK
    vw = GLA_HEADS * GLA_DV
    rank_w = gate_w2_fwd.shape[1]
    cw = conv_w.shape[2]
    E = router_w.shape[2]
    assert vw + cw == w_out.shape[1] and 2 * rank_w <= LANES and E <= LANES

    xc = x.reshape(T, D)
    for l in range(depth):
        o_gf = 2 * kw + 2 * vw
        o_glu = o_gf + 2 * rank_w
        wl = w_in[l]
        w_perm = jnp.concatenate(
            [wl[:, :o_gf], wl[:, o_glu:], wl[:, o_gf:o_glu],
             jnp.zeros((D, LANES - 2 * rank_w), wl.dtype)], axis=1).astype(BF16)
        w2 = jnp.zeros((LANES, 2 * kw), F32)
        w2 = w2.at[:rank_w, :kw].set(gate_w2_fwd[l]).at[rank_w:2 * rank_w, kw:].set(gate_w2_bwd[l])
        b2 = jnp.concatenate([gate_b_fwd[l], gate_b_bwd[l]]).reshape(1, -1)

        q, k, v, gr, la, u = _inproj(xc, _row(norm_mix_g[l]), w_perm, w2.astype(BF16), b2,
                                     kw, vw, cw, tm=512)
        a_out = _gla(q, k, v, la, gr, _row(gla_norm_g[l]), B, S)
        b_out = _conv(u, conv_w[l], _row(conv_b[l]), _row(conv_ln_g[l]), _row(conv_ln_b[l]),
                      B, S, tm=512)
        km, vm = _memkv(mem.reshape(B * M, D), _row(norm_mem_g[l]),
                        xattn_wk[l].astype(BF16), xattn_wv[l].astype(BF16), M)

        rw = jnp.zeros((D, LANES), F32).at[:, :E].set(router_w[l]).astype(BF16)
        rb = jnp.zeros((1, LANES), F32).at[0, :E].set(router_b[l])
        x2, h3, idx, gates = _mid(
            xc, a_out, b_out, w_out[l].astype(BF16), _row(norm_xattn_g[l]),
            xattn_wq[l].astype(BF16), km, vm, xattn_wo[l].astype(BF16),
            _row(norm_ffn_g[l]), rw, rb, S, M, E, tm=512)

        rank, cnt = _rank(idx, tm=512)
        counts = cnt[0, :E].astype(jnp.int32)
        padded = ((counts + MOE_ROWS - 1) // MOE_ROWS) * MOE_ROWS
        pends = jnp.cumsum(padded)
        pstart = jnp.zeros((8, LANES), F32).at[0, :E].set((pends - padded).astype(F32))
        dest = _dest(idx, rank, pstart, tm=512)
        dest_flat = dest[:, :TOP_K].reshape(T * TOP_K)
        n_blocks = (T * TOP_K + E * (MOE_ROWS - 1)) // MOE_ROWS
        blk_start = jnp.arange(n_blocks, dtype=jnp.int32) * MOE_ROWS
        block_e = jnp.minimum(jnp.sum(pends[None, :] <= blk_start[:, None], axis=1), E - 1)
        n_used = (pends[E - 1] // MOE_ROWS).reshape(1)

        xs = _dispatch(dest_flat, (pends - padded + counts).astype(jnp.int32),
                       pends.astype(jnp.int32), n_used.astype(jnp.int32), h3,
                       n_blocks * MOE_ROWS, tm=256)
        ys = _experts(block_e.astype(jnp.int32), n_used.astype(jnp.int32), xs,
                      exp_w_up[l], exp_b_up[l][:, None, :],
                      exp_w_down[l], exp_b_down[l][:, None, :])
        xc = _combine(dest_flat, x2, gates, _row(final_norm_g), ys, tm=128,
                      apply_norm=(l == depth - 1))
    return xc.reshape(B, S, D)
```

```python
import functools

import numpy as np
import jax
import jax.numpy as jnp
from jax import lax
from jax.experimental import pallas as pl
from jax.experimental.pallas import tpu as pltpu

F32 = jnp.float32
BF16 = jnp.bfloat16

GLA_HEADS = 4
GLA_DK = 64
GLA_DV = 128
GLA_CHUNK = 64
GLA_TILE = 4
GATE_TAU = 16.0
CONV_KERNEL = 31
XATTN_HEADS = 4
TOP_K = 4
SWIGLU_ALPHA = 1.702
SWIGLU_LIMIT = 7.0
RMS_EPS = 1e-6
LN_EPS = 1e-5

LANES = 128
SUBLANES = 8
MOE_ROWS = 512
MOE_PARTS = 2
VMEM_LIMIT = 56 * 1024 * 1024


def _dot(a, b):
    return jnp.dot(a, b, preferred_element_type=F32)


def _dot_nt(a, b):
    return lax.dot_general(a, b, (((1,), (1,)), ((), ())), preferred_element_type=F32)


def _dot_tn(a, b):
    return lax.dot_general(a, b, (((0,), (0,)), ((), ())), preferred_element_type=F32)


def _rms(x, g):
    return x * lax.rsqrt(jnp.mean(x * x, axis=-1, keepdims=True) + RMS_EPS) * g


def _params(sem):
    return pltpu.CompilerParams(dimension_semantics=sem, vmem_limit_bytes=VMEM_LIMIT)


def _store_rows_as_tiles(ref, val):
    n, d = val.shape
    assert d == SUBLANES * LANES and ref.shape == (n * SUBLANES, LANES)
    for s in range(SUBLANES):
        ref[pl.ds(s, n, stride=SUBLANES), :] = val[:, s * LANES:(s + 1) * LANES]


def _load_rows_from_tiles(ref):
    n = ref.shape[0] // SUBLANES
    return jnp.concatenate([ref[pl.ds(s, n, stride=SUBLANES), :] for s in range(SUBLANES)], axis=1)


def _inproj_kernel(x_ref, g_ref, w_ref, w2_ref, b2_ref,
                   q_ref, k_ref, v_ref, gr_ref, la_ref, u_ref, *, kw, vw, cw, parts):
    n = x_ref.shape[0] // parts
    gens = [_inproj_rows(pl.ds(p * n, n), x_ref, g_ref, w_ref, w2_ref, b2_ref,
                         q_ref, k_ref, v_ref, gr_ref, la_ref, u_ref, kw, vw, cw)
            for p in range(parts)]
    for _ in zip(*gens):
        pass


def _inproj_rows(rows, x_ref, g_ref, w_ref, w2_ref, b2_ref,
                 q_ref, k_ref, v_ref, gr_ref, la_ref, u_ref, kw, vw, cw):
    h = _rms(x_ref[rows, :], g_ref[...]).astype(BF16)
    acc = _dot(h, w_ref[...])
    yield
    o = 0
    q_ref[rows, :] = (acc[:, o:o + kw] * (GLA_DK ** -0.5)).astype(BF16); o += kw
    k_ref[rows, :] = acc[:, o:o + kw].astype(BF16); o += kw
    v_ref[rows, :] = acc[:, o:o + vw].astype(BF16); o += vw
    r = acc[:, o:o + vw]; o += vw
    gr_ref[rows, :] = (r * jax.nn.sigmoid(r)).astype(BF16)
    a = acc[:, o:o + cw]; o += cw
    g = acc[:, o:o + cw]; o += cw
    u_ref[rows, :] = a * jax.nn.sigmoid(g)
    z = _dot(acc[:, o:o + LANES].astype(BF16), w2_ref[...]) + b2_ref[...]
    yield
    la_ref[rows, :] = (jnp.minimum(z, 0.0) - jnp.log1p(jnp.exp(-jnp.abs(z)))) * (1.0 / GATE_TAU)
    yield


def _inproj(x2d, g, w, w2, b2, kw, vw, cw, tm):
    T, D = x2d.shape
    W = w.shape[1]
    row = lambda n: pl.BlockSpec((tm, n), lambda i: (i, 0))
    full = lambda a: pl.BlockSpec(a.shape, lambda i: (0,) * a.ndim)
    return pl.pallas_call(
        functools.partial(_inproj_kernel, kw=kw, vw=vw, cw=cw, parts=2),
        grid=(T // tm,),
        in_specs=[row(D), full(g), full(w), full(w2), full(b2)],
        out_specs=[row(kw), row(kw), row(vw), row(vw), row(2 * kw), row(cw)],
        out_shape=[jax.ShapeDtypeStruct((T, kw), BF16), jax.ShapeDtypeStruct((T, kw), BF16),
                   jax.ShapeDtypeStruct((T, vw), BF16), jax.ShapeDtypeStruct((T, vw), BF16),
                   jax.ShapeDtypeStruct((T, 2 * kw), F32), jax.ShapeDtypeStruct((T, cw), F32)],
        compiler_params=_params(("parallel",)),
        name="inproj",
    )(x2d, g, w, w2, b2)


def _gla_levels():
    ms, m = [], GLA_CHUNK // 2
    while m >= GLA_TILE:
        ms.append(m)
        m //= 2
    return ms


def _gla_constants(bwd):
    C, H = GLA_CHUNK, GLA_HEADS
    t = np.arange(C)[:, None]
    s = np.arange(C)[None, :]
    lev_q, lev_mask = [], []
    for m in _gla_levels():
        g = 2 * m
        second = (t % g) >= m
        same = t // g == s // g
        if not bwd:
            qrow = second
            mask = same & ((t % g) >= m) & ((s % g) < m)
        else:
            qrow = ~second
            mask = same & ((t % g) < m) & ((s % g) >= m)
        lev_q.append(np.broadcast_to(qrow, (C, H * GLA_DK)))
        lev_mask.append(np.tile(mask, (1, H)))
    shifts = range(0, GLA_TILE) if not bwd else range(1, GLA_TILE)
    sh_mask = []
    for sh in shifts:
        j = t - sh if not bwd else t + sh
        sh_mask.append(np.tile((s == j) & (t // GLA_TILE == s // GLA_TILE), (1, H)))
    f = lambda xs: np.stack(xs).astype(np.float32)
    return f(lev_q), f(lev_mask), f(sh_mask)


def _gla_head_mask():
    C, H = GLA_CHUNK, GLA_HEADS
    assert GLA_DK == C
    blk = np.arange(H * C) // C
    return (blk[:, None] == blk[None, :]).astype(np.float32)


def _gla_chunk(la, q, k, v, st_ref, c_ref, tri_ref, lq_ref, lm_ref, sm_ref, hm_ref, bwd):
    C, H = GLA_CHUNK, GLA_HEADS
    kw = q.shape[1]
    hm = hm_ref[...]

    def stack(xb):
        return jnp.concatenate([xb] * H, axis=0) * hm

    tri = tri_ref[...]
    hi = la.astype(BF16)
    incl = _dot(tri, hi) + _dot(tri, (la - hi.astype(F32)).astype(BF16))
    yield None
    tot = incl[C - 1:C]
    cum = incl if not bwd else incl - la
    x_q, x_k = (cum, tot - cum) if not bwd else (tot - cum, cum)
    e_q, e_k, a_tot = jnp.exp(x_q), jnp.exp(x_k), jnp.exp(tot)
    c_ref[...] = cum

    a = jnp.zeros((C, H * C), F32)
    for lv, m in enumerate(_gla_levels()):
        g = 2 * m
        edge = m - 1 if not bwd else m
        r = jnp.concatenate([jnp.broadcast_to(c_ref[s0 + edge:s0 + edge + 1, :], (g, kw))
                             for s0 in range(0, C, g)], axis=0)
        d = cum - r
        z = (jnp.where(lq_ref[lv] > 0, q, k) * jnp.exp(jnp.minimum(d, -d))).astype(BF16)
        a = a + _dot_nt(z, stack(z)) * lm_ref[lv]
    ps = []
    for si in range(sm_ref.shape[0]):
        sh = si if not bwd else si + 1
        if sh == 0:
            ps.append((q * k).astype(BF16))
            continue
        amt = sh if not bwd else C - sh
        kr = pltpu.roll(k, amt, 0)
        cr = pltpu.roll(cum, amt, 0)
        d = (cum - cr) if not bwd else (cr - cum)
        ps.append((q * kr * jnp.exp(jnp.minimum(d, 0.0))).astype(BF16))
    sc = _dot(jnp.concatenate(ps, axis=0), hm)
    yield None
    for si in range(sm_ref.shape[0]):
        a = a + sc[si * C:(si + 1) * C] * sm_ref[si]

    vst = jnp.concatenate([v[:, h * GLA_DV:(h + 1) * GLA_DV] for h in range(H)], axis=0)
    st = st_ref[...]
    o4 = _dot(stack(a.astype(BF16)), vst) + _dot_nt(stack((q * e_q).astype(BF16)), st.astype(BF16))
    st_ref[...] = st * a_tot + _dot_tn(vst, stack((k * e_k).astype(BF16)))
    yield jnp.concatenate([o4[h * C:(h + 1) * C] for h in range(H)], axis=1)


def _gla_kernel(q_ref, k_ref, v_ref, la_ref, gr_ref, ng_ref,
                lqf_ref, lmf_ref, smf_ref, lqb_ref, lmb_ref, smb_ref, tri_ref, hm_ref,
                out_ref, of_ref, ob_ref, stf_ref, stb_ref, cf_ref, cb_ref):
    C = GLA_CHUNK
    S = q_ref.shape[0]
    kw = q_ref.shape[1]
    n = S // C
    stf_ref[...] = jnp.zeros_like(stf_ref)
    stb_ref[...] = jnp.zeros_like(stb_ref)

    def body(i, carry):
        rf = pl.ds(pl.multiple_of(i * C, C), C)
        rb = pl.ds(pl.multiple_of((n - 1 - i) * C, C), C)
        fwd = _gla_chunk(
            la_ref[rf, 0:kw], q_ref[rf, :].astype(F32), k_ref[rf, :].astype(F32), v_ref[rf, :],
            stf_ref, cf_ref, tri_ref, lqf_ref, lmf_ref, smf_ref, hm_ref, False)
        bwd = _gla_chunk(
            la_ref[rb, kw:2 * kw], q_ref[rb, :].astype(F32), k_ref[rb, :].astype(F32), v_ref[rb, :],
            stb_ref, cb_ref, tri_ref, lqb_ref, lmb_ref, smb_ref, hm_ref, True)
        for o_f, o_b in zip(fwd, bwd):
            pass
        of_ref[rf, :] = o_f
        ob_ref[rb, :] = o_b
        return carry

    lax.fori_loop(0, n, body, 0)

    rows = 256
    def fin(i, carry):
        r = pl.ds(pl.multiple_of(i * rows, rows), rows)
        o = of_ref[r, :] + ob_ref[r, :]
        ys = []
        for h in range(GLA_HEADS):
            oh = o[:, h * GLA_DV:(h + 1) * GLA_DV]
            ys.append(_rms(oh, ng_ref[...]))
        y = jnp.concatenate(ys, axis=1) * gr_ref[r, :].astype(F32)
        out_ref[r, :] = y.astype(BF16)
        return carry

    lax.fori_loop(0, S // rows, fin, 0)


def _gla(q, k, v, la, gr, ng, B, S):
    kw, vw = q.shape[1], v.shape[1]
    cf = _gla_constants(False)
    cb = _gla_constants(True)
    tri = np.tril(np.ones((GLA_CHUNK, GLA_CHUNK), np.float32))
    consts = [jnp.asarray(c) for c in cf] + [jnp.asarray(c) for c in cb] + \
             [jnp.asarray(tri, BF16), jnp.asarray(_gla_head_mask(), BF16)]
    seq = lambda n: pl.BlockSpec((S, n), lambda b: (b, 0))
    full = lambda a: pl.BlockSpec(a.shape, lambda b: (0,) * a.ndim)
    return pl.pallas_call(
        _gla_kernel,
        grid=(B,),
        in_specs=[seq(kw), seq(kw), seq(vw), seq(2 * kw), seq(vw), full(ng)] + [full(c) for c in consts],
        out_specs=seq(vw),
        out_shape=jax.ShapeDtypeStruct((B * S, vw), BF16),
        scratch_shapes=[pltpu.VMEM((S, vw), F32), pltpu.VMEM((S, vw), F32),
                        pltpu.VMEM((GLA_DV, kw), F32), pltpu.VMEM((GLA_DV, kw), F32),
                        pltpu.VMEM((GLA_CHUNK, kw), F32), pltpu.VMEM((GLA_CHUNK, kw), F32)],
        compiler_params=_params(("parallel",)),
        name="gla",
    )(q, k, v, la, gr, ng, *consts)


def _conv_kernel(u_ref, w_ref, cb_ref, lg_ref, lb_ref, out_ref, *, tm, halo):
    S = u_ref.shape[0]
    j = pl.program_id(1)
    t0 = pl.multiple_of(j * tm, tm)
    top = u_ref[pl.ds(pl.multiple_of(jnp.maximum(t0 - halo, 0), halo), halo), :]
    bot = u_ref[pl.ds(pl.multiple_of(jnp.minimum(t0 + tm, S - halo), halo), halo), :]
    top = jnp.where(j > 0, top, 0.0)
    bot = jnp.where(j < pl.num_programs(1) - 1, bot, 0.0)
    win = jnp.concatenate([top, u_ref[pl.ds(t0, tm), :], bot], axis=0)
    rows = tm + 2 * halo
    pad = (CONV_KERNEL - 1) // 2
    w = w_ref[...]
    acc = jnp.zeros((tm, u_ref.shape[1]), F32)
    for r in range(8):
        offs = [k for k in range(CONV_KERNEL) if (halo - pad + k) % 8 == r]
        if not offs:
            continue
        xr = win if r == 0 else pltpu.roll(win, rows - r, 0)
        for k in offs:
            a0 = (halo - pad + k) - r
            acc = acc + xr[a0:a0 + tm, :] * w[k:k + 1, :]
    y = acc + cb_ref[...]
    mu = jnp.mean(y, axis=-1, keepdims=True)
    yc = y - mu
    var = jnp.mean(yc * yc, axis=-1, keepdims=True)
    z = yc * lax.rsqrt(var + LN_EPS) * lg_ref[...] + lb_ref[...]
    out_ref[...] = (z * jax.nn.sigmoid(z)).astype(BF16)


def _conv(u, w, cb, lg, lb, B, S, tm):
    cw = u.shape[1]
    halo = 16
    full = lambda a: pl.BlockSpec(a.shape, lambda b, j: (0,) * a.ndim)
    nj = S // tm
    return pl.pallas_call(
        functools.partial(_conv_kernel, tm=tm, halo=halo),
        grid=(B, nj),
        in_specs=[pl.BlockSpec((S, cw), lambda b, j: (b, 0)), full(w), full(cb), full(lg), full(lb)],
        out_specs=pl.BlockSpec((tm, cw), lambda b, j: (b * nj + j, 0)),
        out_shape=jax.ShapeDtypeStruct((B * S, cw), BF16),
        compiler_params=_params(("parallel", "parallel")),
        name="conv",
    )(u, w, cb, lg, lb)


def _memkv_kernel(m_ref, g_ref, wk_ref, wv_ref, k_ref, v_ref):
    m = _rms(m_ref[...], g_ref[...]).astype(BF16)
    k_ref[...] = _dot(m, wk_ref[...]).astype(BF16)
    v_ref[...] = _dot(m, wv_ref[...]).astype(BF16)


def _memkv(mem2d, g, wk, wv, M):
    R, D = mem2d.shape
    full = lambda a: pl.BlockSpec(a.shape, lambda i: (0,) * a.ndim)
    blk = pl.BlockSpec((M, D), lambda i: (i, 0))
    return pl.pallas_call(
        _memkv_kernel,
        grid=(R // M,),
        in_specs=[blk, full(g), full(wk), full(wv)],
        out_specs=[blk, blk],
        out_shape=[jax.ShapeDtypeStruct((R, D), BF16)] * 2,
        compiler_params=_params(("parallel",)),
        name="memkv",
    )(mem2d, g, wk, wv)


def _mid_kernel(x_ref, a_ref, b_ref, wout_ref, gx_ref, wq_ref, km_ref, vm_ref, wo_ref,
                gf_ref, rw_ref, rb_ref, x2_ref, h3_ref, idx_ref, gate_ref, *, n_exp, parts):
    n = x_ref.shape[0] // parts
    gens = [_mid_rows(pl.ds(p * n, n), pl.ds(p * n * SUBLANES, n * SUBLANES),
                      x_ref, a_ref, b_ref, wout_ref, gx_ref, wq_ref, km_ref, vm_ref, wo_ref,
                      gf_ref, rw_ref, rb_ref, x2_ref, h3_ref, idx_ref, gate_ref, n_exp)
            for p in range(parts)]
    for _ in zip(*gens):
        pass


def _mid_rows(rows, tiles, x_ref, a_ref, b_ref, wout_ref, gx_ref, wq_ref, km_ref, vm_ref, wo_ref,
              gf_ref, rw_ref, rb_ref, x2_ref, h3_ref, idx_ref, gate_ref, n_exp):
    ab = jnp.concatenate([a_ref[rows, :], b_ref[rows, :]], axis=1)
    x1 = x_ref[rows, :] + _dot(ab, wout_ref[...])
    yield
    q2 = _dot(_rms(x1, gx_ref[...]).astype(BF16), wq_ref[...])
    yield
    D = q2.shape[1]
    dh = D // XATTN_HEADS
    ss = []
    for h in range(XATTN_HEADS):
        sl = slice(h * dh, (h + 1) * dh)
        ss.append(_dot_nt(q2[:, sl].astype(BF16), km_ref[:, sl]) * (dh ** -0.5))
    yield
    outs = []
    for h in range(XATTN_HEADS):
        sl = slice(h * dh, (h + 1) * dh)
        p = jnp.exp(ss[h] - jnp.max(ss[h], axis=-1, keepdims=True))
        p = p / jnp.sum(p, axis=-1, keepdims=True)
        outs.append(_dot(p.astype(BF16), vm_ref[:, sl]))
    yield
    o = jnp.concatenate(outs, axis=1).astype(BF16)
    x2 = x1 + _dot(o, wo_ref[...])
    yield
    x2_ref[rows, :] = x2
    h3 = _rms(x2, gf_ref[...])
    _store_rows_as_tiles(h3_ref.at[tiles], h3)
    lane = lax.broadcasted_iota(jnp.int32, (x2.shape[0], LANES), 1)
    logits = _dot(h3.astype(BF16), rw_ref[...]) + rb_ref[...]
    yield
    cur = jnp.where(lane < n_exp, logits, -jnp.inf)
    lane_f = lane.astype(F32)
    vals, idxs = [], []
    for _ in range(TOP_K):
        m = jnp.max(cur, axis=-1, keepdims=True)
        ik = jnp.min(jnp.where(cur == m, lane_f, float(LANES)), axis=-1, keepdims=True)
        vals.append(m)
        idxs.append(ik)
        cur = jnp.where(lane_f == ik, -jnp.inf, cur)
    es = [jnp.exp(vk - vals[0]) for vk in vals]
    den = es[0]
    for ek in es[1:]:
        den = den + ek
    idx_out = jnp.zeros(lane.shape, F32)
    gate_out = jnp.zeros(lane.shape, F32)
    for kk in range(TOP_K):
        idx_out = jnp.where(lane == kk, idxs[kk], idx_out)
        gate_out = jnp.where(lane == kk, es[kk] / den, gate_out)
    idx_ref[rows, :] = idx_out.astype(jnp.int32)
    gate_ref[rows, :] = gate_out
    yield


def _mid(x2d, a, b, wout, gx, wq, km, vm, wo, gf, rw, rb, S, M, n_exp, tm):
    T, D = x2d.shape
    half = a.shape[1]
    per_b = S // tm
    row = lambda n: pl.BlockSpec((tm, n), lambda i: (i, 0))
    full = lambda arr: pl.BlockSpec(arr.shape, lambda i: (0,) * arr.ndim)
    memb = pl.BlockSpec((M, D), lambda i: (i // per_b, 0))
    return pl.pallas_call(
        functools.partial(_mid_kernel, n_exp=n_exp, parts=2),
        grid=(T // tm,),
        in_specs=[row(D), row(half), row(half), full(wout), full(gx), full(wq), memb, memb,
                  full(wo), full(gf), full(rw), full(rb)],
        out_specs=[row(D), pl.BlockSpec((tm * SUBLANES, LANES), lambda i: (i, 0)),
                   row(LANES), row(LANES)],
        out_shape=[jax.ShapeDtypeStruct((T, D), F32),
                   jax.ShapeDtypeStruct((T * SUBLANES, LANES), F32),
                   jax.ShapeDtypeStruct((T, LANES), jnp.int32),
                   jax.ShapeDtypeStruct((T, LANES), F32)],
        compiler_params=_params(("parallel",)),
        name="mid",
    )(x2d, a, b, wout, gx, wq, km, vm, wo, gf, rw, rb)


def _onehot(idx_ref, kk, lane):
    return (lane == idx_ref[:, kk:kk + 1]).astype(F32)


def _rank_kernel(idx_ref, tri_ref, rank_ref, cnt_ref, carry_ref):
    i = pl.program_id(0)

    @pl.when(i == 0)
    def _():
        carry_ref[...] = jnp.zeros_like(carry_ref)

    tm = idx_ref.shape[0]
    lane = lax.broadcasted_iota(jnp.int32, (tm, LANES), 1)
    hots = [_onehot(idx_ref, kk, lane) for kk in range(TOP_K)]
    mask = hots[0]
    for hk in hots[1:]:
        mask = mask + hk
    incl = _dot(tri_ref[...], mask.astype(BF16))
    before = incl - mask + carry_ref[0:1, :]
    out = jnp.zeros((tm, LANES), F32)
    for kk in range(TOP_K):
        rk = jnp.sum(hots[kk] * before, axis=-1, keepdims=True)
        out = jnp.where(lane == kk, rk, out)
    rank_ref[...] = out.astype(jnp.int32)
    carry_ref[...] = carry_ref[...] + incl[tm - 1:tm, :]
    cnt_ref[...] = carry_ref[...]


def _rank(idx, tm):
    T = idx.shape[0]
    tri = jnp.asarray(np.tril(np.ones((tm, tm), np.float32)), BF16)
    row = pl.BlockSpec((tm, LANES), lambda i: (i, 0))
    return pl.pallas_call(
        _rank_kernel,
        grid=(T // tm,),
        in_specs=[row, pl.BlockSpec((tm, tm), lambda i: (0, 0))],
        out_specs=[row, pl.BlockSpec((8, LANES), lambda i: (0, 0))],
        out_shape=[jax.ShapeDtypeStruct((T, LANES), jnp.int32),
                   jax.ShapeDtypeStruct((8, LANES), F32)],
        scratch_shapes=[pltpu.VMEM((8, LANES), F32)],
        compiler_params=_params(("arbitrary",)),
        name="rank",
    )(idx, tri)


def _dest_kernel(idx_ref, rank_ref, pstart_ref, dest_ref):
    tm = idx_ref.shape[0]
    lane = lax.broadcasted_iota(jnp.int32, (tm, LANES), 1)
    out = jnp.zeros((tm, LANES), F32)
    for kk in range(TOP_K):
        base = jnp.sum(_onehot(idx_ref, kk, lane) * pstart_ref[0:1, :], axis=-1, keepdims=True)
        out = jnp.where(lane == kk, base, out)
    dest_ref[...] = out.astype(jnp.int32) + rank_ref[...]


def _dest(idx, rank, pstart, tm):
    T = idx.shape[0]
    row = pl.BlockSpec((tm, LANES), lambda i: (i, 0))
    return pl.pallas_call(
        _dest_kernel,
        grid=(T // tm,),
        in_specs=[row, row, pl.BlockSpec((8, LANES), lambda i: (0, 0))],
        out_specs=row,
        out_shape=jax.ShapeDtypeStruct((T, LANES), jnp.int32),
        compiler_params=_params(("parallel",)),
        name="dest",
    )(idx, rank, pstart)


ROW_UNROLL = 8


def _tile(ref, row):
    return ref.at[pl.ds(pl.multiple_of(row * SUBLANES, SUBLANES), SUBLANES)]


def _wait_rows(buf_ref, sem):
    pltpu.make_async_copy(buf_ref, buf_ref, sem).wait()


def _dispatch_kernel(dest_ref, lo_ref, hi_ref, nb_ref, h_ref, xs_ref, zero_ref, sem, zsem):
    tm = h_ref.shape[0] // SUBLANES
    blk = zero_ref.shape[0]

    @pl.when(pl.program_id(0) == 0)
    def _():
        zero_ref[...] = jnp.zeros_like(zero_ref)

        def each_fill(fn):
            def expert(e, carry):
                row = lo_ref[e]
                n = hi_ref[e] - row
                for bit in reversed(range(MOE_ROWS.bit_length() - 1)):
                    size = (1 << bit) * SUBLANES
                    take = (n & (1 << bit)) != 0

                    @pl.when(take)
                    def _(row=row, size=size):
                        dst = xs_ref.at[pl.ds(pl.multiple_of(row * SUBLANES, SUBLANES), size)]
                        fn(pltpu.make_async_copy(zero_ref.at[pl.ds(0, size)], dst, zsem))

                    row = row + jnp.where(take, 1 << bit, 0)
                return carry

            lax.fori_loop(0, lo_ref.shape[0], expert, 0)

            def block(b, carry):
                fn(pltpu.make_async_copy(
                    zero_ref, xs_ref.at[pl.ds(pl.multiple_of(b * blk, blk), blk)], zsem))
                return carry

            lax.fori_loop(nb_ref[0], xs_ref.shape[0] // blk, block, 0)

        each_fill(lambda cp: cp.start())
        each_fill(lambda cp: cp.wait())

    def issue(i, carry):
        for u in range(ROW_UNROLL):
            t = i * ROW_UNROLL + u
            for kk in range(TOP_K):
                pltpu.make_async_copy(_tile(h_ref, t), _tile(xs_ref, dest_ref[t * TOP_K + kk]),
                                      sem).start(priority=kk % 2)
        return carry

    lax.fori_loop(0, tm // ROW_UNROLL, issue, 0)
    for kk in range(TOP_K):
        _wait_rows(h_ref, sem)


def _dispatch(dest_flat, pad_lo, pad_hi, n_used, h3, n_rows, tm):
    T = h3.shape[0] // SUBLANES
    smem = pl.BlockSpec(memory_space=pltpu.SMEM)
    return pl.pallas_call(
        _dispatch_kernel,
        grid=(T // tm,),
        in_specs=[pl.BlockSpec((tm * TOP_K,), lambda i: (i,), memory_space=pltpu.SMEM),
                  smem, smem, smem,
                  pl.BlockSpec((tm * SUBLANES, LANES), lambda i: (i, 0))],
        out_specs=pl.BlockSpec(memory_space=pl.ANY),
        out_shape=jax.ShapeDtypeStruct((n_rows * SUBLANES, LANES), F32),
        scratch_shapes=[pltpu.VMEM((MOE_ROWS * SUBLANES, LANES), F32),
                        pltpu.SemaphoreType.DMA(()), pltpu.SemaphoreType.DMA(())],
        compiler_params=_params(("arbitrary",)),
        name="dispatch",
    )(dest_flat, pad_lo, pad_hi, n_used, h3)


def _experts_kernel(be_ref, nb_ref, first_ref, next_ref, slot_ref,
                    xs_ref, wu_hbm, bu_ref, wd_hbm, bd_ref, ys_ref,
                    wu_f32, wd_f32, wu_bf, wd_bf, sem):
    i = pl.program_id(0)

    def fetch(e, s):
        return (pltpu.make_async_copy(wu_hbm.at[e], wu_f32.at[s], sem.at[s]),
                pltpu.make_async_copy(wd_hbm.at[e], wd_f32.at[s], sem.at[s]))

    @pl.when(i == 0)
    def _():
        for cp in fetch(be_ref[0], 0):
            cp.start()

    @pl.when(first_ref[i] == 1)
    def _():
        s = slot_ref[i]
        for cp in fetch(be_ref[i], s):
            cp.wait()

        @pl.when(next_ref[i] >= 0)
        def _():
            for cp in fetch(next_ref[i], 1 - s):
                cp.start()

        wu_bf[...] = wu_f32[s].astype(BF16)
        wd_bf[...] = wd_f32[s].astype(BF16)

    def mlp(tiles):
        F = wd_bf.shape[0]
        x = _load_rows_from_tiles(xs_ref.at[tiles]).astype(BF16)
        gu = _dot(x, wu_bf[...]) + bu_ref[0]
        yield
        x_glu = jnp.minimum(gu[:, :F], SWIGLU_LIMIT)
        x_lin = jnp.clip(gu[:, F:], -SWIGLU_LIMIT, SWIGLU_LIMIT)
        hid = x_glu * jax.nn.sigmoid(SWIGLU_ALPHA * x_glu) * (x_lin + 1.0)
        y = _dot(hid.astype(BF16), wd_bf[...]) + bd_ref[0]
        yield
        _store_rows_as_tiles(ys_ref.at[tiles], y)
        yield

    @pl.when(i < nb_ref[0])
    def _():
        n = xs_ref.shape[0] // MOE_PARTS
        for _ in zip(*[mlp(pl.ds(p * n, n)) for p in range(MOE_PARTS)]):
            pass

    @pl.when(i >= nb_ref[0])
    def _():
        ys_ref[...] = jnp.zeros_like(ys_ref)


def _expert_runs(block_e):
    nb = block_e.shape[0]
    ar = jnp.arange(nb, dtype=jnp.int32)
    first = jnp.concatenate([jnp.ones((1,), bool), block_e[1:] != block_e[:-1]])
    starts = jnp.where(first, ar, nb)
    later = lax.cummin(jnp.concatenate([starts[1:], jnp.full((1,), nb, jnp.int32)]), reverse=True)
    nxt = jnp.where(later < nb, block_e[jnp.minimum(later, nb - 1)], -1)
    slot = (jnp.cumsum(first.astype(jnp.int32)) - 1) & 1
    return first.astype(jnp.int32), nxt.astype(jnp.int32), slot.astype(jnp.int32)


def _experts(block_e, n_used, xs, wu, bu, wd, bd):
    E, D, F2 = wu.shape
    F = wd.shape[1]
    nb = xs.shape[0] // (MOE_ROWS * SUBLANES)
    rows = pl.BlockSpec((MOE_ROWS * SUBLANES, LANES), lambda i, *_: (i, 0))
    grid_spec = pltpu.PrefetchScalarGridSpec(
        num_scalar_prefetch=5,
        grid=(nb,),
        in_specs=[rows,
                  pl.BlockSpec(memory_space=pl.ANY),
                  pl.BlockSpec((1, 1, F2), lambda i, be, *_: (be[i], 0, 0)),
                  pl.BlockSpec(memory_space=pl.ANY),
                  pl.BlockSpec((1, 1, D), lambda i, be, *_: (be[i], 0, 0))],
        out_specs=rows,
        scratch_shapes=[pltpu.VMEM((2, D, F2), F32), pltpu.VMEM((2, F, D), F32),
                        pltpu.VMEM((D, F2), BF16), pltpu.VMEM((F, D), BF16),
                        pltpu.SemaphoreType.DMA((2,))],
    )
    return pl.pallas_call(
        _experts_kernel,
        grid_spec=grid_spec,
        out_shape=jax.ShapeDtypeStruct(xs.shape, F32),
        compiler_params=_params(("arbitrary",)),
        name="experts",
    )(block_e, n_used, *_expert_runs(block_e), xs, wu, bu, wd, bd)


def _combine_kernel(dcur_ref, dnext_ref, x2_ref, gate_ref, g_ref, ys_ref, out_ref, buf0, buf1, sem,
                    *, apply_norm):
    j = pl.program_id(0)
    tm = x2_ref.shape[0] // 2

    def start(tbl_ref, base, buf, s):
        def body(i, carry):
            for u in range(ROW_UNROLL):
                t = i * ROW_UNROLL + u
                for kk in range(TOP_K):
                    pltpu.make_async_copy(_tile(ys_ref, tbl_ref[(base + t) * TOP_K + kk]),
                                          _tile(buf.at[kk], t), s).start(priority=kk % 2)
            return carry

        lax.fori_loop(0, tm // ROW_UNROLL, body, 0)

    def finish(buf, s, rows):
        _wait_rows(buf, s)
        x = x2_ref[rows, :]
        for kk in range(TOP_K):
            x = x + _load_rows_from_tiles(buf.at[kk]) * gate_ref[rows, kk:kk + 1]
        out_ref[rows, :] = _rms(x, g_ref[...]) if apply_norm else x

    @pl.when(j == 0)
    def _():
        start(dcur_ref, 0, buf0, sem.at[0])

    start(dcur_ref, tm, buf1, sem.at[1])
    finish(buf0, sem.at[0], pl.ds(0, tm))

    @pl.when(j < pl.num_programs(0) - 1)
    def _():
        start(dnext_ref, 0, buf0, sem.at[0])

    finish(buf1, sem.at[1], pl.ds(tm, tm))


def _combine(dest_flat, x2, gates, g, ys, tm, apply_norm):
    T, D = x2.shape
    steps = T // (2 * tm)
    tbl = lambda f: pl.BlockSpec((2 * tm * TOP_K,), f, memory_space=pltpu.SMEM)
    buf = pltpu.VMEM((TOP_K, tm * SUBLANES, LANES), F32)
    return pl.pallas_call(
        functools.partial(_combine_kernel, apply_norm=apply_norm),
        grid=(steps,),
        in_specs=[tbl(lambda j: (j,)), tbl(lambda j: (jnp.minimum(j + 1, steps - 1),)),
                  pl.BlockSpec((2 * tm, D), lambda j: (j, 0)),
                  pl.BlockSpec((2 * tm, LANES), lambda j: (j, 0)),
                  pl.BlockSpec(g.shape, lambda j: (0, 0)),
                  pl.BlockSpec(memory_space=pl.ANY)],
        out_specs=pl.BlockSpec((2 * tm, D), lambda j: (j, 0)),
        out_shape=jax.ShapeDtypeStruct((T, D), F32),
        scratch_shapes=[buf, buf, pltpu.SemaphoreType.DMA((2,))],
        compiler_params=_params(("arbitrary",)),
        name="combine",
    )(dest_flat, dest_flat, x2, gates, g, ys)


def _row(v):
    return v.reshape(1, -1).astype(F32)


def kernel(x, mem, norm_mix_g, w_in, gate_w2_fwd, gate_b_fwd, gate_w2_bwd, gate_b_bwd,
           gla_norm_g, conv_w, conv_b, conv_ln_g, conv_ln_b, w_out, norm_xattn_g,
           norm_mem_g, xattn_wq, xattn_wk, xattn_wv, xattn_wo, norm_ffn_g, router_w,
           router_b, exp_w_up, exp_b_up, exp_w_down, exp_b_down, final_norm_g):
    B, S, D = x.shape
    M = mem.shape[1]
    T = B * S
    depth = w_in.shape[0]
    kw = GLA_HEADS * GLA_DK
    vw = GLA_HEADS * GLA_DV
    rank_w = gate_w2_fwd.shape[1]
    cw = conv_w.shape[2]
    E = router_w.shape[2]
    assert vw + cw == w_out.shape[1] and 2 * rank_w <= LANES and E <= LANES

    xc = x.reshape(T, D)
    for l in range(depth):
        o_gf = 2 * kw + 2 * vw
        o_glu = o_gf + 2 * rank_w
        wl = w_in[l]
        w_perm = jnp.concatenate(
            [wl[:, :o_gf], wl[:, o_glu:], wl[:, o_gf:o_glu],
             jnp.zeros((D, LANES - 2 * rank_w), wl.dtype)], axis=1).astype(BF16)
        w2 = jnp.zeros((LANES, 2 * kw), F32)
        w2 = w2.at[:rank_w, :kw].set(gate_w2_fwd[l]).at[rank_w:2 * rank_w, kw:].set(gate_w2_bwd[l])
        b2 = jnp.concatenate([gate_b_fwd[l], gate_b_bwd[l]]).reshape(1, -1)

        q, k, v, gr, la, u = _inproj(xc, _row(norm_mix_g[l]), w_perm, w2.astype(BF16), b2,
                                     kw, vw, cw, tm=512)
        a_out = _gla(q, k, v, la, gr, _row(gla_norm_g[l]), B, S)
        b_out = _conv(u, conv_w[l], _row(conv_b[l]), _row(conv_ln_g[l]), _row(conv_ln_b[l]),
                      B, S, tm=512)
        km, vm = _memkv(mem.reshape(B * M, D), _row(norm_mem_g[l]),
                        xattn_wk[l].astype(BF16), xattn_wv[l].astype(BF16), M)

        rw = jnp.zeros((D, LANES), F32).at[:, :E].set(router_w[l]).astype(BF16)
        rb = jnp.zeros((1, LANES), F32).at[0, :E].set(router_b[l])
        x2, h3, idx, gates = _mid(
            xc, a_out, b_out, w_out[l].astype(BF16), _row(norm_xattn_g[l]),
            xattn_wq[l].astype(BF16), km, vm, xattn_wo[l].astype(BF16),
            _row(norm_ffn_g[l]), rw, rb, S, M, E, tm=512)

        rank, cnt = _rank(idx, tm=512)
        counts = cnt[0, :E].astype(jnp.int32)
        padded = ((counts + MOE_ROWS - 1) // MOE_ROWS) * MOE_ROWS
        pends = jnp.cumsum(padded)
        pstart = jnp.zeros((8, LANES), F32).at[0, :E].set((pends - padded).astype(F32))
        dest = _dest(idx, rank, pstart, tm=512)
        dest_flat = dest[:, :TOP_K].reshape(T * TOP_K)
        n_blocks = (T * TOP_K + E * (MOE_ROWS - 1)) // MOE_ROWS
        blk_start = jnp.arange(n_blocks, dtype=jnp.int32) * MOE_ROWS
        block_e = jnp.minimum(jnp.sum(pends[None, :] <= blk_start[:, None], axis=1), E - 1)
        n_used = (pends[E - 1] // MOE_ROWS).reshape(1)

        xs = _dispatch(dest_flat, (pends - padded + counts).astype(jnp.int32),
                       pends.astype(jnp.int32), n_used.astype(jnp.int32), h3,
                       n_blocks * MOE_ROWS, tm=256)
        ys = _experts(block_e.astype(jnp.int32), n_used.astype(jnp.int32), xs,
                      exp_w_up[l], exp_b_up[l][:, None, :],
                      exp_w_down[l], exp_b_down[l][:, None, :])
        xc = _combine(dest_flat, x2, gates, _row(final_norm_g), ys, tm=128,
                      apply_norm=(l == depth - 1))
    return xc.reshape(B, S, D)
```

```python
import functools

import numpy as np
import jax
import jax.numpy as jnp
from jax import lax
from jax.experimental import pallas as pl
from jax.experimental.pallas import tpu as pltpu

F32 = jnp.float32
BF16 = jnp.bfloat16

GLA_HEADS = 4
GLA_DK = 64
GLA_DV = 128
GLA_CHUNK = 64
GLA_TILE = 4
GATE_TAU = 16.0
CONV_KERNEL = 31
XATTN_HEADS = 4
TOP_K = 4
SWIGLU_ALPHA = 1.702
SWIGLU_LIMIT = 7.0
RMS_EPS = 1e-6
LN_EPS = 1e-5

LANES = 128
SUBLANES = 8
MOE_ROWS = 512
MOE_PARTS = 2
MOE_TILE = 512
VMEM_LIMIT = 56 * 1024 * 1024


def _dot(a, b):
    return jnp.dot(a, b, preferred_element_type=F32)


def _dot_nt(a, b):
    return lax.dot_general(a, b, (((1,), (1,)), ((), ())), preferred_element_type=F32)


def _dot_tn(a, b):
    return lax.dot_general(a, b, (((0,), (0,)), ((), ())), preferred_element_type=F32)


def _rms(x, g):
    return x * lax.rsqrt(jnp.mean(x * x, axis=-1, keepdims=True) + RMS_EPS) * g


def _params(sem):
    return pltpu.CompilerParams(dimension_semantics=sem, vmem_limit_bytes=VMEM_LIMIT)


def _store_rows_as_tiles(ref, val):
    n, d = val.shape
    assert d == SUBLANES * LANES and ref.shape == (n * SUBLANES, LANES)
    for s in range(SUBLANES):
        ref[pl.ds(s, n, stride=SUBLANES), :] = val[:, s * LANES:(s + 1) * LANES]


def _load_rows_from_tiles(ref):
    n = ref.shape[0] // SUBLANES
    return jnp.concatenate([ref[pl.ds(s, n, stride=SUBLANES), :] for s in range(SUBLANES)], axis=1)


def _inproj_kernel(x_ref, g_ref, w_ref, w2_ref, b2_ref,
                   q_ref, k_ref, v_ref, gr_ref, la_ref, u_ref, *, kw, vw, cw, parts):
    n = x_ref.shape[0] // parts
    gens = [_inproj_rows(pl.ds(p * n, n), x_ref, g_ref, w_ref, w2_ref, b2_ref,
                         q_ref, k_ref, v_ref, gr_ref, la_ref, u_ref, kw, vw, cw)
            for p in range(parts)]
    for _ in zip(*gens):
        pass


def _inproj_rows(rows, x_ref, g_ref, w_ref, w2_ref, b2_ref,
                 q_ref, k_ref, v_ref, gr_ref, la_ref, u_ref, kw, vw, cw):
    h = _rms(x_ref[rows, :], g_ref[...]).astype(BF16)
    acc = _dot(h, w_ref[...])
    yield
    o = 0
    q_ref[rows, :] = (acc[:, o:o + kw] * (GLA_DK ** -0.5)).astype(BF16); o += kw
    k_ref[rows, :] = acc[:, o:o + kw].astype(BF16); o += kw
    v_ref[rows, :] = acc[:, o:o + vw].astype(BF16); o += vw
    r = acc[:, o:o + vw]; o += vw
    gr_ref[rows, :] = (r * jax.nn.sigmoid(r)).astype(BF16)
    a = acc[:, o:o + cw]; o += cw
    g = acc[:, o:o + cw]; o += cw
    u_ref[rows, :] = a * jax.nn.sigmoid(g)
    z = _dot(acc[:, o:o + LANES].astype(BF16), w2_ref[...]) + b2_ref[...]
    yield
    la_ref[rows, :] = (jnp.minimum(z, 0.0) - jnp.log1p(jnp.exp(-jnp.abs(z)))) * (1.0 / GATE_TAU)
    yield


def _inproj(x2d, g, w, w2, b2, kw, vw, cw, tm):
    T, D = x2d.shape
    W = w.shape[1]
    row = lambda n: pl.BlockSpec((tm, n), lambda i: (i, 0))
    full = lambda a: pl.BlockSpec(a.shape, lambda i: (0,) * a.ndim)
    return pl.pallas_call(
        functools.partial(_inproj_kernel, kw=kw, vw=vw, cw=cw, parts=2),
        grid=(T // tm,),
        in_specs=[row(D), full(g), full(w), full(w2), full(b2)],
        out_specs=[row(kw), row(kw), row(vw), row(vw), row(2 * kw), row(cw)],
        out_shape=[jax.ShapeDtypeStruct((T, kw), BF16), jax.ShapeDtypeStruct((T, kw), BF16),
                   jax.ShapeDtypeStruct((T, vw), BF16), jax.ShapeDtypeStruct((T, vw), BF16),
                   jax.ShapeDtypeStruct((T, 2 * kw), F32), jax.ShapeDtypeStruct((T, cw), F32)],
        compiler_params=_params(("parallel",)),
        name="inproj",
    )(x2d, g, w, w2, b2)


def _gla_levels():
    ms, m = [], GLA_CHUNK // 2
    while m >= GLA_TILE:
        ms.append(m)
        m //= 2
    return ms


def _gla_constants(bwd):
    C, H = GLA_CHUNK, GLA_HEADS
    t = np.arange(C)[:, None]
    s = np.arange(C)[None, :]
    lev_q, lev_mask = [], []
    for m in _gla_levels():
        g = 2 * m
        second = (t % g) >= m
        same = t // g == s // g
        if not bwd:
            qrow = second
            mask = same & ((t % g) >= m) & ((s % g) < m)
        else:
            qrow = ~second
            mask = same & ((t % g) < m) & ((s % g) >= m)
        lev_q.append(np.broadcast_to(qrow, (C, H * GLA_DK)))
        lev_mask.append(np.tile(mask, (1, H)))
    shifts = range(0, GLA_TILE) if not bwd else range(1, GLA_TILE)
    sh_mask = []
    for sh in shifts:
        j = t - sh if not bwd else t + sh
        sh_mask.append(np.tile((s == j) & (t // GLA_TILE == s // GLA_TILE), (1, H)))
    f = lambda xs: np.stack(xs).astype(np.float32)
    return f(lev_q), f(lev_mask), f(sh_mask)


def _gla_head_mask():
    C, H = GLA_CHUNK, GLA_HEADS
    assert GLA_DK == C
    blk = np.arange(H * C) // C
    return (blk[:, None] == blk[None, :]).astype(np.float32)


def _gla_chunk(la, q, k, v, st_ref, c_ref, tri_ref, lq_ref, lm_ref, sm_ref, hm_ref, bwd):
    C, H = GLA_CHUNK, GLA_HEADS
    kw = q.shape[1]
    hm = hm_ref[...]

    def stack(xb):
        return jnp.concatenate([xb] * H, axis=0) * hm

    tri = tri_ref[...]
    hi = la.astype(BF16)
    incl = _dot(tri, hi) + _dot(tri, (la - hi.astype(F32)).astype(BF16))
    yield None
    tot = incl[C - 1:C]
    cum = incl if not bwd else incl - la
    x_q, x_k = (cum, tot - cum) if not bwd else (tot - cum, cum)
    e_q, e_k, a_tot = jnp.exp(x_q), jnp.exp(x_k), jnp.exp(tot)
    c_ref[...] = cum

    a = jnp.zeros((C, H * C), F32)
    for lv, m in enumerate(_gla_levels()):
        g = 2 * m
        edge = m - 1 if not bwd else m
        r = jnp.concatenate([jnp.broadcast_to(c_ref[s0 + edge:s0 + edge + 1, :], (g, kw))
                             for s0 in range(0, C, g)], axis=0)
        d = cum - r
        z = (jnp.where(lq_ref[lv] > 0, q, k) * jnp.exp(jnp.minimum(d, -d))).astype(BF16)
        a = a + _dot_nt(z, stack(z)) * lm_ref[lv]
    ps = []
    for si in range(sm_ref.shape[0]):
        sh = si if not bwd else si + 1
        if sh == 0:
            ps.append((q * k).astype(BF16))
            continue
        amt = sh if not bwd else C - sh
        kr = pltpu.roll(k, amt, 0)
        cr = pltpu.roll(cum, amt, 0)
        d = (cum - cr) if not bwd else (cr - cum)
        ps.append((q * kr * jnp.exp(jnp.minimum(d, 0.0))).astype(BF16))
    sc = _dot(jnp.concatenate(ps, axis=0), hm)
    yield None
    for si in range(sm_ref.shape[0]):
        a = a + sc[si * C:(si + 1) * C] * sm_ref[si]

    vst = jnp.concatenate([v[:, h * GLA_DV:(h + 1) * GLA_DV] for h in range(H)], axis=0)
    st = st_ref[...]
    o4 = _dot(stack(a.astype(BF16)), vst) + _dot_nt(stack((q * e_q).astype(BF16)), st.astype(BF16))
    st_ref[...] = st * a_tot + _dot_tn(vst, stack((k * e_k).astype(BF16)))
    yield jnp.concatenate([o4[h * C:(h + 1) * C] for h in range(H)], axis=1)


def _gla_kernel(q_ref, k_ref, v_ref, la_ref, gr_ref, ng_ref,
                lqf_ref, lmf_ref, smf_ref, lqb_ref, lmb_ref, smb_ref, tri_ref, hm_ref,
                out_ref, of_ref, ob_ref, stf_ref, stb_ref, cf_ref, cb_ref):
    C = GLA_CHUNK
    S = q_ref.shape[0]
    kw = q_ref.shape[1]
    n = S // C
    stf_ref[...] = jnp.zeros_like(stf_ref)
    stb_ref[...] = jnp.zeros_like(stb_ref)

    def body(i, carry):
        rf = pl.ds(pl.multiple_of(i * C, C), C)
        rb = pl.ds(pl.multiple_of((n - 1 - i) * C, C), C)
        fwd = _gla_chunk(
            la_ref[rf, 0:kw], q_ref[rf, :].astype(F32), k_ref[rf, :].astype(F32), v_ref[rf, :],
            stf_ref, cf_ref, tri_ref, lqf_ref, lmf_ref, smf_ref, hm_ref, False)
        bwd = _gla_chunk(
            la_ref[rb, kw:2 * kw], q_ref[rb, :].astype(F32), k_ref[rb, :].astype(F32), v_ref[rb, :],
            stb_ref, cb_ref, tri_ref, lqb_ref, lmb_ref, smb_ref, hm_ref, True)
        for o_f, o_b in zip(fwd, bwd):
            pass
        of_ref[rf, :] = o_f
        ob_ref[rb, :] = o_b
        return carry

    lax.fori_loop(0, n, body, 0)

    rows = 256
    def fin(i, carry):
        r = pl.ds(pl.multiple_of(i * rows, rows), rows)
        o = of_ref[r, :] + ob_ref[r, :]
        ys = []
        for h in range(GLA_HEADS):
            oh = o[:, h * GLA_DV:(h + 1) * GLA_DV]
            ys.append(_rms(oh, ng_ref[...]))
        y = jnp.concatenate(ys, axis=1) * gr_ref[r, :].astype(F32)
        out_ref[r, :] = y.astype(BF16)
        return carry

    lax.fori_loop(0, S // rows, fin, 0)


def _gla(q, k, v, la, gr, ng, B, S):
    kw, vw = q.shape[1], v.shape[1]
    cf = _gla_constants(False)
    cb = _gla_constants(True)
    tri = np.tril(np.ones((GLA_CHUNK, GLA_CHUNK), np.float32))
    consts = [jnp.asarray(c) for c in cf] + [jnp.asarray(c) for c in cb] + \
             [jnp.asarray(tri, BF16), jnp.asarray(_gla_head_mask(), BF16)]
    seq = lambda n: pl.BlockSpec((S, n), lambda b: (b, 0))
    full = lambda a: pl.BlockSpec(a.shape, lambda b: (0,) * a.ndim)
    return pl.pallas_call(
        _gla_kernel,
        grid=(B,),
        in_specs=[seq(kw), seq(kw), seq(vw), seq(2 * kw), seq(vw), full(ng)] + [full(c) for c in consts],
        out_specs=seq(vw),
        out_shape=jax.ShapeDtypeStruct((B * S, vw), BF16),
        scratch_shapes=[pltpu.VMEM((S, vw), F32), pltpu.VMEM((S, vw), F32),
                        pltpu.VMEM((GLA_DV, kw), F32), pltpu.VMEM((GLA_DV, kw), F32),
                        pltpu.VMEM((GLA_CHUNK, kw), F32), pltpu.VMEM((GLA_CHUNK, kw), F32)],
        compiler_params=_params(("parallel",)),
        name="gla",
    )(q, k, v, la, gr, ng, *consts)


def _conv_kernel(u_ref, w_ref, cb_ref, lg_ref, lb_ref, out_ref, *, tm, halo):
    S = u_ref.shape[0]
    j = pl.program_id(1)
    t0 = pl.multiple_of(j * tm, tm)
    top = u_ref[pl.ds(pl.multiple_of(jnp.maximum(t0 - halo, 0), halo), halo), :]
    bot = u_ref[pl.ds(pl.multiple_of(jnp.minimum(t0 + tm, S - halo), halo), halo), :]
    top = jnp.where(j > 0, top, 0.0)
    bot = jnp.where(j < pl.num_programs(1) - 1, bot, 0.0)
    win = jnp.concatenate([top, u_ref[pl.ds(t0, tm), :], bot], axis=0)
    rows = tm + 2 * halo
    pad = (CONV_KERNEL - 1) // 2
    w = w_ref[...]
    acc = jnp.zeros((tm, u_ref.shape[1]), F32)
    for r in range(8):
        offs = [k for k in range(CONV_KERNEL) if (halo - pad + k) % 8 == r]
        if not offs:
            continue
        xr = win if r == 0 else pltpu.roll(win, rows - r, 0)
        for k in offs:
            a0 = (halo - pad + k) - r
            acc = acc + xr[a0:a0 + tm, :] * w[k:k + 1, :]
    y = acc + cb_ref[...]
    mu = jnp.mean(y, axis=-1, keepdims=True)
    yc = y - mu
    var = jnp.mean(yc * yc, axis=-1, keepdims=True)
    z = yc * lax.rsqrt(var + LN_EPS) * lg_ref[...] + lb_ref[...]
    out_ref[...] = (z * jax.nn.sigmoid(z)).astype(BF16)


def _conv(u, w, cb, lg, lb, B, S, tm):
    cw = u.shape[1]
    halo = 16
    full = lambda a: pl.BlockSpec(a.shape, lambda b, j: (0,) * a.ndim)
    nj = S // tm
    return pl.pallas_call(
        functools.partial(_conv_kernel, tm=tm, halo=halo),
        grid=(B, nj),
        in_specs=[pl.BlockSpec((S, cw), lambda b, j: (b, 0)), full(w), full(cb), full(lg), full(lb)],
        out_specs=pl.BlockSpec((tm, cw), lambda b, j: (b * nj + j, 0)),
        out_shape=jax.ShapeDtypeStruct((B * S, cw), BF16),
        compiler_params=_params(("parallel", "parallel")),
        name="conv",
    )(u, w, cb, lg, lb)


def _memkv_kernel(m_ref, g_ref, wk_ref, wv_ref, k_ref, v_ref):
    m = _rms(m_ref[...], g_ref[...]).astype(BF16)
    k_ref[...] = _dot(m, wk_ref[...]).astype(BF16)
    v_ref[...] = _dot(m, wv_ref[...]).astype(BF16)


def _memkv(mem2d, g, wk, wv, M):
    R, D = mem2d.shape
    full = lambda a: pl.BlockSpec(a.shape, lambda i: (0,) * a.ndim)
    blk = pl.BlockSpec((M, D), lambda i: (i, 0))
    return pl.pallas_call(
        _memkv_kernel,
        grid=(R // M,),
        in_specs=[blk, full(g), full(wk), full(wv)],
        out_specs=[blk, blk],
        out_shape=[jax.ShapeDtypeStruct((R, D), BF16)] * 2,
        compiler_params=_params(("parallel",)),
        name="memkv",
    )(mem2d, g, wk, wv)


def _mid_kernel(x_ref, a_ref, b_ref, wout_ref, gx_ref, wq_ref, km_ref, vm_ref, wo_ref,
                gf_ref, rw_ref, rb_ref, x2_ref, h3_ref, idx_ref, gate_ref, *, n_exp, parts):
    n = x_ref.shape[0] // parts
    gens = [_mid_rows(pl.ds(p * n, n), pl.ds(p * n * SUBLANES, n * SUBLANES),
                      x_ref, a_ref, b_ref, wout_ref, gx_ref, wq_ref, km_ref, vm_ref, wo_ref,
                      gf_ref, rw_ref, rb_ref, x2_ref, h3_ref, idx_ref, gate_ref, n_exp)
            for p in range(parts)]
    for _ in zip(*gens):
        pass


def _mid_rows(rows, tiles, x_ref, a_ref, b_ref, wout_ref, gx_ref, wq_ref, km_ref, vm_ref, wo_ref,
              gf_ref, rw_ref, rb_ref, x2_ref, h3_ref, idx_ref, gate_ref, n_exp):
    ab = jnp.concatenate([a_ref[rows, :], b_ref[rows, :]], axis=1)
    x1 = x_ref[rows, :] + _dot(ab, wout_ref[...])
    yield
    q2 = _dot(_rms(x1, gx_ref[...]).astype(BF16), wq_ref[...])
    yield
    D = q2.shape[1]
    dh = D // XATTN_HEADS
    ss = []
    for h in range(XATTN_HEADS):
        sl = slice(h * dh, (h + 1) * dh)
        ss.append(_dot_nt(q2[:, sl].astype(BF16), km_ref[:, sl]) * (dh ** -0.5))
    yield
    outs = []
    for h in range(XATTN_HEADS):
        sl = slice(h * dh, (h + 1) * dh)
        p = jnp.exp(ss[h] - jnp.max(ss[h], axis=-1, keepdims=True))
        p = p / jnp.sum(p, axis=-1, keepdims=True)
        outs.append(_dot(p.astype(BF16), vm_ref[:, sl]))
    yield
    o = jnp.concatenate(outs, axis=1).astype(BF16)
    x2 = x1 + _dot(o, wo_ref[...])
    yield
    x2_ref[rows, :] = x2
    h3 = _rms(x2, gf_ref[...])
    _store_rows_as_tiles(h3_ref.at[tiles], h3)
    lane = lax.broadcasted_iota(jnp.int32, (x2.shape[0], LANES), 1)
    logits = _dot(h3.astype(BF16), rw_ref[...]) + rb_ref[...]
    yield
    cur = jnp.where(lane < n_exp, logits, -jnp.inf)
    lane_f = lane.astype(F32)
    vals, idxs = [], []
    for _ in range(TOP_K):
        m = jnp.max(cur, axis=-1, keepdims=True)
        ik = jnp.min(jnp.where(cur == m, lane_f, float(LANES)), axis=-1, keepdims=True)
        vals.append(m)
        idxs.append(ik)
        cur = jnp.where(lane_f == ik, -jnp.inf, cur)
    es = [jnp.exp(vk - vals[0]) for vk in vals]
    den = es[0]
    for ek in es[1:]:
        den = den + ek
    idx_out = jnp.zeros(lane.shape, F32)
    gate_out = jnp.zeros(lane.shape, F32)
    for kk in range(TOP_K):
        idx_out = jnp.where(lane == kk, idxs[kk], idx_out)
        gate_out = jnp.where(lane == kk, es[kk] / den, gate_out)
    idx_ref[rows, :] = idx_out.astype(jnp.int32)
    gate_ref[rows, :] = gate_out
    yield


def _mid(x2d, a, b, wout, gx, wq, km, vm, wo, gf, rw, rb, S, M, n_exp, tm):
    T, D = x2d.shape
    half = a.shape[1]
    per_b = S // tm
    row = lambda n: pl.BlockSpec((tm, n), lambda i: (i, 0))
    full = lambda arr: pl.BlockSpec(arr.shape, lambda i: (0,) * arr.ndim)
    memb = pl.BlockSpec((M, D), lambda i: (i // per_b, 0))
    return pl.pallas_call(
        functools.partial(_mid_kernel, n_exp=n_exp, parts=2),
        grid=(T // tm,),
        in_specs=[row(D), row(half), row(half), full(wout), full(gx), full(wq), memb, memb,
                  full(wo), full(gf), full(rw), full(rb)],
        out_specs=[row(D), pl.BlockSpec((tm * SUBLANES, LANES), lambda i: (i, 0)),
                   row(LANES), row(LANES)],
        out_shape=[jax.ShapeDtypeStruct((T, D), F32),
                   jax.ShapeDtypeStruct((T * SUBLANES, LANES), F32),
                   jax.ShapeDtypeStruct((T, LANES), jnp.int32),
                   jax.ShapeDtypeStruct((T, LANES), F32)],
        compiler_params=_params(("parallel",)),
        name="mid",
    )(x2d, a, b, wout, gx, wq, km, vm, wo, gf, rw, rb)


def _onehot(idx_ref, kk, lane):
    return (lane == idx_ref[:, kk:kk + 1]).astype(F32)


def _place_kernel(idx_ref, tri_ref, upper_ref, pos_ref, cnt_ref):
    tm = idx_ref.shape[0]
    lane = lax.broadcasted_iota(jnp.int32, (tm, LANES), 1)
    hots = [_onehot(idx_ref, kk, lane) for kk in range(TOP_K)]
    mask = hots[0]
    for hk in hots[1:]:
        mask = mask + hk
    incl = _dot(tri_ref[...], mask.astype(BF16))
    cnt = jnp.broadcast_to(incl[tm - 1:tm, :], (SUBLANES, LANES))
    hi = jnp.floor(cnt * (1.0 / 16.0))
    lo = cnt - 16.0 * hi
    start = 16.0 * _dot(hi.astype(BF16), upper_ref[...]) + _dot(lo.astype(BF16), upper_ref[...])
    before = incl - mask + start[0:1, :]
    out = jnp.zeros((tm, LANES), F32)
    for kk in range(TOP_K):
        rk = jnp.sum(hots[kk] * before, axis=-1, keepdims=True)
        out = jnp.where(lane == kk, rk, out)
    pos_ref[...] = out.astype(jnp.int32)
    cnt_ref[...] = cnt


def _place(idx, tm):
    T = idx.shape[0]
    assert tm <= 16 * 256
    tri = jnp.asarray(np.tril(np.ones((tm, tm), np.float32)), BF16)
    upper = jnp.asarray(np.triu(np.ones((LANES, LANES), np.float32), 1), BF16)
    row = pl.BlockSpec((tm, LANES), lambda i: (i, 0))
    return pl.pallas_call(
        _place_kernel,
        grid=(T // tm,),
        in_specs=[row, pl.BlockSpec((tm, tm), lambda i: (0, 0)),
                  pl.BlockSpec((LANES, LANES), lambda i: (0, 0))],
        out_specs=[row, pl.BlockSpec((SUBLANES, LANES), lambda i: (i, 0))],
        out_shape=[jax.ShapeDtypeStruct((T, LANES), jnp.int32),
                   jax.ShapeDtypeStruct((T // tm * SUBLANES, LANES), F32)],
        compiler_params=_params(("parallel",)),
        name="place",
    )(idx, tri, upper)


ROW_UNROLL = 8


def _tile(ref, row):
    return ref.at[pl.ds(pl.multiple_of(row * SUBLANES, SUBLANES), SUBLANES)]


def _wait_rows(buf_ref, sem):
    pltpu.make_async_copy(buf_ref, buf_ref, sem).wait()


RUN_CHUNK = 16


def _rows(ref, row, n):
    return ref.at[pl.ds(pl.multiple_of(row * SUBLANES, SUBLANES), n * SUBLANES)]


def _for_run_pieces(n, fn):
    shift = RUN_CHUNK.bit_length() - 1
    whole = lax.shift_right_logical(n, shift)

    def body(c, carry):
        fn(c * RUN_CHUNK, RUN_CHUNK)
        return carry

    lax.fori_loop(0, whole, body, 0)
    off = whole * RUN_CHUNK
    for bit in reversed(range(shift)):
        take = (n & (1 << bit)) != 0

        @pl.when(take)
        def _(off=off, bit=bit):
            fn(off, 1 << bit)

        off = off + jnp.where(take, 1 << bit, 0)


def _dispatch_kernel(pos_ref, rdst_ref, rn_ref, rloc_ref, lo_ref, hi_ref, nb_ref, h_ref, xs_ref,
                     sb0, sb1, zero_ref, sem, zsem):
    j = pl.program_id(0)
    tm = h_ref.shape[0] // SUBLANES // 2
    slots = rn_ref.shape[0] // 2
    n_exp = lo_ref.shape[0]
    blk = zero_ref.shape[0]

    def sort(half, sb):
        def body(i, carry):
            for u in range(ROW_UNROLL):
                t = half * tm + i * ROW_UNROLL + u
                tile = h_ref[pl.ds(pl.multiple_of(t * SUBLANES, SUBLANES), SUBLANES), :]
                for kk in range(TOP_K):
                    p = pos_ref[t * TOP_K + kk]
                    sb[pl.ds(pl.multiple_of(p, SUBLANES), SUBLANES), :] = tile
            return carry

        lax.fori_loop(0, tm // ROW_UNROLL, body, 0)

    def start_runs(half, sb, s):
        def expert(e, carry):
            r = half * slots + e
            loc, dst = rloc_ref[r], rdst_ref[r]
            _for_run_pieces(rn_ref[r], lambda off, n: pltpu.make_async_copy(
                _rows(sb, loc + off, n), _rows(xs_ref, dst + off, n), s).start())
            return carry

        lax.fori_loop(0, n_exp, expert, 0)

    @pl.when(j == 0)
    def _():
        zero_ref[...] = jnp.zeros_like(zero_ref)

        def each_fill(fn):
            def expert(e, carry):
                row = lo_ref[e]
                n = hi_ref[e] - row
                for bit in reversed(range(MOE_ROWS.bit_length() - 1)):
                    size = (1 << bit) * SUBLANES
                    take = (n & (1 << bit)) != 0

                    @pl.when(take)
                    def _(row=row, size=size):
                        dst = xs_ref.at[pl.ds(pl.multiple_of(row * SUBLANES, SUBLANES), size)]
                        fn(pltpu.make_async_copy(zero_ref.at[pl.ds(0, size)], dst, zsem))

                    row = row + jnp.where(take, 1 << bit, 0)
                return carry

            lax.fori_loop(0, lo_ref.shape[0], expert, 0)

            def block(b, carry):
                fn(pltpu.make_async_copy(
                    zero_ref, xs_ref.at[pl.ds(pl.multiple_of(b * blk, blk), blk)], zsem))
                return carry

            lax.fori_loop(nb_ref[0], xs_ref.shape[0] // blk, block, 0)

        each_fill(lambda cp: cp.start())
        each_fill(lambda cp: cp.wait())

    sort(0, sb0)
    start_runs(0, sb0, sem.at[0])

    @pl.when(j > 0)
    def _():
        _wait_rows(sb1, sem.at[1])

    sort(1, sb1)
    start_runs(1, sb1, sem.at[1])
    _wait_rows(sb0, sem.at[0])

    @pl.when(j == pl.num_programs(0) - 1)
    def _():
        _wait_rows(sb1, sem.at[1])


def _dispatch(pos_flat, run_dst, run_n, run_loc, pad_lo, pad_hi, n_used, h3, n_rows, tm):
    T = h3.shape[0] // SUBLANES
    slots = run_n.shape[0] // (T // tm)
    smem = pl.BlockSpec(memory_space=pltpu.SMEM)
    per_step = lambda n: pl.BlockSpec((n,), lambda j: (j,), memory_space=pltpu.SMEM)
    sorted_rows = pltpu.VMEM((tm * TOP_K * SUBLANES, LANES), F32)
    return pl.pallas_call(
        _dispatch_kernel,
        grid=(T // (2 * tm),),
        in_specs=[per_step(2 * tm * TOP_K), per_step(2 * slots), per_step(2 * slots),
                  per_step(2 * slots), smem, smem, smem,
                  pl.BlockSpec((2 * tm * SUBLANES, LANES), lambda j: (j, 0))],
        out_specs=pl.BlockSpec(memory_space=pl.ANY),
        out_shape=jax.ShapeDtypeStruct((n_rows * SUBLANES, LANES), F32),
        scratch_shapes=[sorted_rows, sorted_rows, pltpu.VMEM((MOE_ROWS * SUBLANES, LANES), F32),
                        pltpu.SemaphoreType.DMA((2,)), pltpu.SemaphoreType.DMA(())],
        compiler_params=_params(("arbitrary",)),
        name="dispatch",
    )(pos_flat, run_dst, run_n, run_loc, pad_lo, pad_hi, n_used, h3)


def _experts_kernel(be_ref, nb_ref, first_ref, next_ref, slot_ref,
                    xs_ref, wu_hbm, bu_ref, wd_hbm, bd_ref, ys_ref,
                    wu_f32, wd_f32, wu_bf, wd_bf, sem):
    i = pl.program_id(0)

    def fetch(e, s):
        return (pltpu.make_async_copy(wu_hbm.at[e], wu_f32.at[s], sem.at[s]),
                pltpu.make_async_copy(wd_hbm.at[e], wd_f32.at[s], sem.at[s]))

    @pl.when(i == 0)
    def _():
        for cp in fetch(be_ref[0], 0):
            cp.start()

    @pl.when(first_ref[i] == 1)
    def _():
        s = slot_ref[i]
        for cp in fetch(be_ref[i], s):
            cp.wait()

        @pl.when(next_ref[i] >= 0)
        def _():
            for cp in fetch(next_ref[i], 1 - s):
                cp.start()

        wu_bf[...] = wu_f32[s].astype(BF16)
        wd_bf[...] = wd_f32[s].astype(BF16)

    def mlp(tiles):
        F = wd_bf.shape[0]
        x = _load_rows_from_tiles(xs_ref.at[tiles]).astype(BF16)
        gu = _dot(x, wu_bf[...]) + bu_ref[0]
        yield
        x_glu = jnp.minimum(gu[:, :F], SWIGLU_LIMIT)
        x_lin = jnp.clip(gu[:, F:], -SWIGLU_LIMIT, SWIGLU_LIMIT)
        hid = x_glu * jax.nn.sigmoid(SWIGLU_ALPHA * x_glu) * (x_lin + 1.0)
        y = _dot(hid.astype(BF16), wd_bf[...]) + bd_ref[0]
        yield
        _store_rows_as_tiles(ys_ref.at[tiles], y)
        yield

    @pl.when(i < nb_ref[0])
    def _():
        n = xs_ref.shape[0] // MOE_PARTS
        for _ in zip(*[mlp(pl.ds(p * n, n)) for p in range(MOE_PARTS)]):
            pass

    @pl.when(i >= nb_ref[0])
    def _():
        ys_ref[...] = jnp.zeros_like(ys_ref)


def _expert_runs(block_e):
    nb = block_e.shape[0]
    ar = jnp.arange(nb, dtype=jnp.int32)
    first = jnp.concatenate([jnp.ones((1,), bool), block_e[1:] != block_e[:-1]])
    starts = jnp.where(first, ar, nb)
    later = lax.cummin(jnp.concatenate([starts[1:], jnp.full((1,), nb, jnp.int32)]), reverse=True)
    nxt = jnp.where(later < nb, block_e[jnp.minimum(later, nb - 1)], -1)
    slot = (jnp.cumsum(first.astype(jnp.int32)) - 1) & 1
    return first.astype(jnp.int32), nxt.astype(jnp.int32), slot.astype(jnp.int32)


def _experts(block_e, n_used, xs, wu, bu, wd, bd):
    E, D, F2 = wu.shape
    F = wd.shape[1]
    nb = xs.shape[0] // (MOE_ROWS * SUBLANES)
    rows = pl.BlockSpec((MOE_ROWS * SUBLANES, LANES), lambda i, *_: (i, 0))
    grid_spec = pltpu.PrefetchScalarGridSpec(
        num_scalar_prefetch=5,
        grid=(nb,),
        in_specs=[rows,
                  pl.BlockSpec(memory_space=pl.ANY),
                  pl.BlockSpec((1, 1, F2), lambda i, be, *_: (be[i], 0, 0)),
                  pl.BlockSpec(memory_space=pl.ANY),
                  pl.BlockSpec((1, 1, D), lambda i, be, *_: (be[i], 0, 0))],
        out_specs=rows,
        scratch_shapes=[pltpu.VMEM((2, D, F2), F32), pltpu.VMEM((2, F, D), F32),
                        pltpu.VMEM((D, F2), BF16), pltpu.VMEM((F, D), BF16),
                        pltpu.SemaphoreType.DMA((2,))],
    )
    return pl.pallas_call(
        _experts_kernel,
        grid_spec=grid_spec,
        out_shape=jax.ShapeDtypeStruct(xs.shape, F32),
        compiler_params=_params(("arbitrary",)),
        name="experts",
    )(block_e, n_used, *_expert_runs(block_e), xs, wu, bu, wd, bd)


def _combine_kernel(pos_ref, gate_ref, src_ref, rn_ref, rloc_ref, nsrc_ref, nrn_ref, nrloc_ref,
                    x2_ref, g_ref, ys_ref, out_ref, yb0, yb1, acc_ref, sem, *, n_exp, apply_norm):
    j = pl.program_id(0)
    tm = x2_ref.shape[0] // 2
    slots = rn_ref.shape[0] // 2

    def start(tbls, half, yb, s):
        src_t, n_t, loc_t = tbls

        def expert(e, carry):
            r = half * slots + e
            loc, src = loc_t[r], src_t[r]
            _for_run_pieces(n_t[r], lambda off, n: pltpu.make_async_copy(
                _rows(ys_ref, src + off, n), _rows(yb, loc + off, n), s).start())
            return carry

        lax.fori_loop(0, n_exp, expert, 0)

    def finish(half, yb, s):
        _wait_rows(yb, s)

        def body(i, carry):
            for u in range(ROW_UNROLL):
                t = i * ROW_UNROLL + u
                a = (half * tm + t) * TOP_K
                acc = None
                for kk in range(TOP_K):
                    p = pos_ref[a + kk]
                    term = yb[pl.ds(pl.multiple_of(p, SUBLANES), SUBLANES), :] \
                        * gate_ref[a + kk]
                    acc = term if acc is None else acc + term
                acc_ref[pl.ds(pl.multiple_of(t * SUBLANES, SUBLANES), SUBLANES), :] = acc
            return carry

        lax.fori_loop(0, tm // ROW_UNROLL, body, 0)
        rows = pl.ds(half * tm, tm)
        x = x2_ref[rows, :] + _load_rows_from_tiles(acc_ref)
        out_ref[rows, :] = _rms(x, g_ref[...]) if apply_norm else x

    cur = (src_ref, rn_ref, rloc_ref)

    @pl.when(j == 0)
    def _():
        start(cur, 0, yb0, sem.at[0])

    start(cur, 1, yb1, sem.at[1])
    finish(0, yb0, sem.at[0])

    @pl.when(j < pl.num_programs(0) - 1)
    def _():
        start((nsrc_ref, nrn_ref, nrloc_ref), 0, yb0, sem.at[0])

    finish(1, yb1, sem.at[1])


def _combine(pos_flat, gate_flat, run_src, run_n, run_loc, x2, g, ys, tm, n_exp, apply_norm):
    T, D = x2.shape
    steps = T // (2 * tm)
    slots = run_n.shape[0] // (T // tm)
    per_step = lambda n: pl.BlockSpec((n,), lambda j: (j,), memory_space=pltpu.SMEM)
    next_step = lambda n: pl.BlockSpec((n,), lambda j: (jnp.minimum(j + 1, steps - 1),),
                                       memory_space=pltpu.SMEM)
    sorted_rows = pltpu.VMEM((tm * TOP_K * SUBLANES, LANES), F32)
    return pl.pallas_call(
        functools.partial(_combine_kernel, n_exp=n_exp, apply_norm=apply_norm),
        grid=(steps,),
        in_specs=[per_step(2 * tm * TOP_K), per_step(2 * tm * TOP_K),
                  per_step(2 * slots), per_step(2 * slots), per_step(2 * slots),
                  next_step(2 * slots), next_step(2 * slots), next_step(2 * slots),
                  pl.BlockSpec((2 * tm, D), lambda j: (j, 0)),
                  pl.BlockSpec(g.shape, lambda j: (0, 0)),
                  pl.BlockSpec(memory_space=pl.ANY)],
        out_specs=pl.BlockSpec((2 * tm, D), lambda j: (j, 0)),
        out_shape=jax.ShapeDtypeStruct((T, D), F32),
        scratch_shapes=[sorted_rows, sorted_rows, pltpu.VMEM((tm * SUBLANES, LANES), F32),
                        pltpu.SemaphoreType.DMA((2,))],
        compiler_params=_params(("arbitrary",)),
        name="combine",
    )(pos_flat, gate_flat, run_src, run_n, run_loc, run_src, run_n, run_loc, x2, g, ys)


def _row(v):
    return v.reshape(1, -1).astype(F32)


def kernel(x, mem, norm_mix_g, w_in, gate_w2_fwd, gate_b_fwd, gate_w2_bwd, gate_b_bwd,
           gla_norm_g, conv_w, conv_b, conv_ln_g, conv_ln_b, w_out, norm_xattn_g,
           norm_mem_g, xattn_wq, xattn_wk, xattn_wv, xattn_wo, norm_ffn_g, router_w,
           router_b, exp_w_up, exp_b_up, exp_w_down, exp_b_down, final_norm_g):
    B, S, D = x.shape
    M = mem.shape[1]
    T = B * S
    depth = w_in.shape[0]
    kw = GLA_HEADS * GLA_DK
    vw = GLA_HEADS * GLA_DV
    rank_w = gate_w2_fwd.shape[1]
    cw = conv_w.shape[2]
    E = router_w.shape[2]
    assert vw + cw == w_out.shape[1] and 2 * rank_w <= LANES and E <= LANES

    xc = x.reshape(T, D)
    for l in range(depth):
        o_gf = 2 * kw + 2 * vw
        o_glu = o_gf + 2 * rank_w
        wl = w_in[l]
        w_perm = jnp.concatenate(
            [wl[:, :o_gf], wl[:, o_glu:], wl[:, o_gf:o_glu],
             jnp.zeros((D, LANES - 2 * rank_w), wl.dtype)], axis=1).astype(BF16)
        w2 = jnp.zeros((LANES, 2 * kw), F32)
        w2 = w2.at[:rank_w, :kw].set(gate_w2_fwd[l]).at[rank_w:2 * rank_w, kw:].set(gate_w2_bwd[l])
        b2 = jnp.concatenate([gate_b_fwd[l], gate_b_bwd[l]]).reshape(1, -1)

        q, k, v, gr, la, u = _inproj(xc, _row(norm_mix_g[l]), w_perm, w2.astype(BF16), b2,
                                     kw, vw, cw, tm=512)
        a_out = _gla(q, k, v, la, gr, _row(gla_norm_g[l]), B, S)
        b_out = _conv(u, conv_w[l], _row(conv_b[l]), _row(conv_ln_g[l]), _row(conv_ln_b[l]),
                      B, S, tm=512)
        km, vm = _memkv(mem.reshape(B * M, D), _row(norm_mem_g[l]),
                        xattn_wk[l].astype(BF16), xattn_wv[l].astype(BF16), M)

        rw = jnp.zeros((D, LANES), F32).at[:, :E].set(router_w[l]).astype(BF16)
        rb = jnp.zeros((1, LANES), F32).at[0, :E].set(router_b[l])
        x2, h3, idx, gates = _mid(
            xc, a_out, b_out, w_out[l].astype(BF16), _row(norm_xattn_g[l]),
            xattn_wq[l].astype(BF16), km, vm, xattn_wo[l].astype(BF16),
            _row(norm_ffn_g[l]), rw, rb, S, M, E, tm=512)

        tm = MOE_TILE
        pos, tile_cnt = _place(idx, tm=tm)
        pos_flat = pos[:, :TOP_K].reshape(T * TOP_K) * SUBLANES
        n_te = tile_cnt.reshape(T // tm, SUBLANES, LANES)[:, 0, :E].astype(jnp.int32)
        counts = jnp.sum(n_te, axis=0)
        padded = ((counts + MOE_ROWS - 1) // MOE_ROWS) * MOE_ROWS
        pends = jnp.cumsum(padded)
        run_row = (pends - padded)[None, :] + jnp.cumsum(n_te, axis=0) - n_te
        run_loc = jnp.cumsum(n_te, axis=1) - n_te
        slots = max(E, LANES // 2)
        runs = [jnp.pad(a, ((0, 0), (0, slots - E))).reshape(-1).astype(jnp.int32)
                for a in (run_row, n_te, run_loc)]
        n_blocks = (T * TOP_K + E * (MOE_ROWS - 1)) // MOE_ROWS
        blk_start = jnp.arange(n_blocks, dtype=jnp.int32) * MOE_ROWS
        block_e = jnp.minimum(jnp.sum(pends[None, :] <= blk_start[:, None], axis=1), E - 1)
        n_used = (pends[E - 1] // MOE_ROWS).reshape(1)

        xs = _dispatch(pos_flat, *runs, (pends - padded + counts).astype(jnp.int32),
                       pends.astype(jnp.int32), n_used.astype(jnp.int32), h3,
                       n_blocks * MOE_ROWS, tm=tm)
        ys = _experts(block_e.astype(jnp.int32), n_used.astype(jnp.int32), xs,
                      exp_w_up[l], exp_b_up[l][:, None, :],
                      exp_w_down[l], exp_b_down[l][:, None, :])
        xc = _combine(pos_flat, gates[:, :TOP_K].reshape(T * TOP_K), *runs, x2,
                      _row(final_norm_g), ys, tm=tm, n_exp=E, apply_norm=(l == depth - 1))
    return xc.reshape(B, S, D)
```

```python
import functools

import numpy as np
import jax
import jax.numpy as jnp
from jax import lax
from jax.experimental import pallas as pl
from jax.experimental.pallas import tpu as pltpu

F32 = jnp.float32
BF16 = jnp.bfloat16

GLA_HEADS = 4
GLA_DK = 64
GLA_DV = 128
GLA_CHUNK = 64
GLA_TILE = 4
GATE_TAU = 16.0
CONV_KERNEL = 31
XATTN_HEADS = 4
TOP_K = 4
SWIGLU_ALPHA = 1.702
SWIGLU_LIMIT = 7.0
RMS_EPS = 1e-6
LN_EPS = 1e-5

LANES = 128
SUBLANES = 8
MOE_ROWS = 512
MOE_PARTS = 2
MOE_TILE = 512
VMEM_LIMIT = 56 * 1024 * 1024


def _dot(a, b):
    return jnp.dot(a, b, preferred_element_type=F32)


def _dot_nt(a, b):
    return lax.dot_general(a, b, (((1,), (1,)), ((), ())), preferred_element_type=F32)


def _dot_tn(a, b):
    return lax.dot_general(a, b, (((0,), (0,)), ((), ())), preferred_element_type=F32)


def _rms(x, g):
    return x * lax.rsqrt(jnp.mean(x * x, axis=-1, keepdims=True) + RMS_EPS) * g


def _params(sem):
    return pltpu.CompilerParams(dimension_semantics=sem, vmem_limit_bytes=VMEM_LIMIT)


def _store_rows_as_tiles(ref, val):
    n, d = val.shape
    assert d == SUBLANES * LANES and ref.shape == (n * SUBLANES, LANES)
    for s in range(SUBLANES):
        ref[pl.ds(s, n, stride=SUBLANES), :] = val[:, s * LANES:(s + 1) * LANES]


def _load_rows_from_tiles(ref):
    n = ref.shape[0] // SUBLANES
    return jnp.concatenate([ref[pl.ds(s, n, stride=SUBLANES), :] for s in range(SUBLANES)], axis=1)


def _inproj_kernel(x_ref, g_ref, w_ref, w2_ref, b2_ref,
                   q_ref, k_ref, v_ref, gr_ref, la_ref, u_ref, *, kw, vw, cw, parts):
    n = x_ref.shape[0] // parts
    gens = [_inproj_rows(pl.ds(p * n, n), x_ref, g_ref, w_ref, w2_ref, b2_ref,
                         q_ref, k_ref, v_ref, gr_ref, la_ref, u_ref, kw, vw, cw)
            for p in range(parts)]
    for _ in zip(*gens):
        pass


def _inproj_rows(rows, x_ref, g_ref, w_ref, w2_ref, b2_ref,
                 q_ref, k_ref, v_ref, gr_ref, la_ref, u_ref, kw, vw, cw):
    h = _rms(x_ref[rows, :], g_ref[...]).astype(BF16)
    acc = _dot(h, w_ref[...])
    yield
    o = 0
    q_ref[rows, :] = (acc[:, o:o + kw] * (GLA_DK ** -0.5)).astype(BF16); o += kw
    k_ref[rows, :] = acc[:, o:o + kw].astype(BF16); o += kw
    v_ref[rows, :] = acc[:, o:o + vw].astype(BF16); o += vw
    r = acc[:, o:o + vw]; o += vw
    gr_ref[rows, :] = (r * jax.nn.sigmoid(r)).astype(BF16)
    a = acc[:, o:o + cw]; o += cw
    g = acc[:, o:o + cw]; o += cw
    u_ref[rows, :] = a * jax.nn.sigmoid(g)
    z = _dot(acc[:, o:o + LANES].astype(BF16), w2_ref[...]) + b2_ref[...]
    yield
    la_ref[rows, :] = (jnp.minimum(z, 0.0) - jnp.log1p(jnp.exp(-jnp.abs(z)))) * (1.0 / GATE_TAU)
    yield


def _inproj(x2d, g, w, w2, b2, kw, vw, cw, tm):
    T, D = x2d.shape
    W = w.shape[1]
    row = lambda n: pl.BlockSpec((tm, n), lambda i: (i, 0))
    full = lambda a: pl.BlockSpec(a.shape, lambda i: (0,) * a.ndim)
    return pl.pallas_call(
        functools.partial(_inproj_kernel, kw=kw, vw=vw, cw=cw, parts=2),
        grid=(T // tm,),
        in_specs=[row(D), full(g), full(w), full(w2), full(b2)],
        out_specs=[row(kw), row(kw), row(vw), row(vw), row(2 * kw), row(cw)],
        out_shape=[jax.ShapeDtypeStruct((T, kw), BF16), jax.ShapeDtypeStruct((T, kw), BF16),
                   jax.ShapeDtypeStruct((T, vw), BF16), jax.ShapeDtypeStruct((T, vw), BF16),
                   jax.ShapeDtypeStruct((T, 2 * kw), F32), jax.ShapeDtypeStruct((T, cw), F32)],
        compiler_params=_params(("parallel",)),
        name="inproj",
    )(x2d, g, w, w2, b2)


def _gla_levels():
    ms, m = [], GLA_CHUNK // 2
    while m >= GLA_TILE:
        ms.append(m)
        m //= 2
    return ms


def _gla_constants(bwd):
    C, H = GLA_CHUNK, GLA_HEADS
    t = np.arange(C)[:, None]
    s = np.arange(C)[None, :]
    lev_q, lev_mask = [], []
    for m in _gla_levels():
        g = 2 * m
        second = (t % g) >= m
        same = t // g == s // g
        if not bwd:
            qrow = second
            mask = same & ((t % g) >= m) & ((s % g) < m)
        else:
            qrow = ~second
            mask = same & ((t % g) < m) & ((s % g) >= m)
        lev_q.append(np.broadcast_to(qrow, (C, H * GLA_DK)))
        lev_mask.append(np.tile(mask, (1, H)))
    shifts = range(0, GLA_TILE) if not bwd else range(1, GLA_TILE)
    sh_mask = []
    for sh in shifts:
        j = t - sh if not bwd else t + sh
        sh_mask.append(np.tile((s == j) & (t // GLA_TILE == s // GLA_TILE), (1, H)))
    f = lambda xs: np.stack(xs).astype(np.float32)
    return f(lev_q), f(lev_mask), f(sh_mask)


def _gla_head_mask():
    C, H = GLA_CHUNK, GLA_HEADS
    assert GLA_DK == C
    blk = np.arange(H * C) // C
    return (blk[:, None] == blk[None, :]).astype(np.float32)


def _gla_chunk(la, q, k, v, st_ref, c_ref, tri_ref, lq_ref, lm_ref, sm_ref, hm_ref, bwd):
    C, H = GLA_CHUNK, GLA_HEADS
    kw = q.shape[1]
    hm = hm_ref[...]

    def stack(xb):
        return jnp.concatenate([xb] * H, axis=0) * hm

    tri = tri_ref[...]
    hi = la.astype(BF16)
    incl = _dot(tri, hi) + _dot(tri, (la - hi.astype(F32)).astype(BF16))
    yield None
    tot = incl[C - 1:C]
    cum = incl if not bwd else incl - la
    x_q, x_k = (cum, tot - cum) if not bwd else (tot - cum, cum)
    e_q, e_k, a_tot = jnp.exp(x_q), jnp.exp(x_k), jnp.exp(tot)
    c_ref[...] = cum

    a = jnp.zeros((C, H * C), F32)
    for lv, m in enumerate(_gla_levels()):
        g = 2 * m
        edge = m - 1 if not bwd else m
        r = jnp.concatenate([jnp.broadcast_to(c_ref[s0 + edge:s0 + edge + 1, :], (g, kw))
                             for s0 in range(0, C, g)], axis=0)
        d = cum - r
        z = (jnp.where(lq_ref[lv] > 0, q, k) * jnp.exp(jnp.minimum(d, -d))).astype(BF16)
        a = a + _dot_nt(z, stack(z)) * lm_ref[lv]
    ps = []
    for si in range(sm_ref.shape[0]):
        sh = si if not bwd else si + 1
        if sh == 0:
            ps.append((q * k).astype(BF16))
            continue
        amt = sh if not bwd else C - sh
        kr = pltpu.roll(k, amt, 0)
        cr = pltpu.roll(cum, amt, 0)
        d = (cum - cr) if not bwd else (cr - cum)
        ps.append((q * kr * jnp.exp(jnp.minimum(d, 0.0))).astype(BF16))
    sc = _dot(jnp.concatenate(ps, axis=0), hm)
    yield None
    for si in range(sm_ref.shape[0]):
        a = a + sc[si * C:(si + 1) * C] * sm_ref[si]

    vst = jnp.concatenate([v[:, h * GLA_DV:(h + 1) * GLA_DV] for h in range(H)], axis=0)
    st = st_ref[...]
    o4 = _dot(stack(a.astype(BF16)), vst) + _dot_nt(stack((q * e_q).astype(BF16)), st.astype(BF16))
    st_ref[...] = st * a_tot + _dot_tn(vst, stack((k * e_k).astype(BF16)))
    yield jnp.concatenate([o4[h * C:(h + 1) * C] for h in range(H)], axis=1)


def _gla_kernel(q_ref, k_ref, v_ref, la_ref, gr_ref, ng_ref,
                lqf_ref, lmf_ref, smf_ref, lqb_ref, lmb_ref, smb_ref, tri_ref, hm_ref,
                out_ref, of_ref, ob_ref, stf_ref, stb_ref, cf_ref, cb_ref):
    C = GLA_CHUNK
    S = q_ref.shape[0]
    kw = q_ref.shape[1]
    n = S // C
    stf_ref[...] = jnp.zeros_like(stf_ref)
    stb_ref[...] = jnp.zeros_like(stb_ref)

    def body(i, carry):
        rf = pl.ds(pl.multiple_of(i * C, C), C)
        rb = pl.ds(pl.multiple_of((n - 1 - i) * C, C), C)
        fwd = _gla_chunk(
            la_ref[rf, 0:kw], q_ref[rf, :].astype(F32), k_ref[rf, :].astype(F32), v_ref[rf, :],
            stf_ref, cf_ref, tri_ref, lqf_ref, lmf_ref, smf_ref, hm_ref, False)
        bwd = _gla_chunk(
            la_ref[rb, kw:2 * kw], q_ref[rb, :].astype(F32), k_ref[rb, :].astype(F32), v_ref[rb, :],
            stb_ref, cb_ref, tri_ref, lqb_ref, lmb_ref, smb_ref, hm_ref, True)
        for o_f, o_b in zip(fwd, bwd):
            pass
        of_ref[rf, :] = o_f
        ob_ref[rb, :] = o_b
        return carry

    lax.fori_loop(0, n, body, 0)

    rows = 256
    def fin(i, carry):
        r = pl.ds(pl.multiple_of(i * rows, rows), rows)
        o = of_ref[r, :] + ob_ref[r, :]
        ys = []
        for h in range(GLA_HEADS):
            oh = o[:, h * GLA_DV:(h + 1) * GLA_DV]
            ys.append(_rms(oh, ng_ref[...]))
        y = jnp.concatenate(ys, axis=1) * gr_ref[r, :].astype(F32)
        out_ref[r, :] = y.astype(BF16)
        return carry

    lax.fori_loop(0, S // rows, fin, 0)


def _gla(q, k, v, la, gr, ng, B, S):
    kw, vw = q.shape[1], v.shape[1]
    cf = _gla_constants(False)
    cb = _gla_constants(True)
    tri = np.tril(np.ones((GLA_CHUNK, GLA_CHUNK), np.float32))
    consts = [jnp.asarray(c) for c in cf] + [jnp.asarray(c) for c in cb] + \
             [jnp.asarray(tri, BF16), jnp.asarray(_gla_head_mask(), BF16)]
    seq = lambda n: pl.BlockSpec((S, n), lambda b: (b, 0))
    full = lambda a: pl.BlockSpec(a.shape, lambda b: (0,) * a.ndim)
    return pl.pallas_call(
        _gla_kernel,
        grid=(B,),
        in_specs=[seq(kw), seq(kw), seq(vw), seq(2 * kw), seq(vw), full(ng)] + [full(c) for c in consts],
        out_specs=seq(vw),
        out_shape=jax.ShapeDtypeStruct((B * S, vw), BF16),
        scratch_shapes=[pltpu.VMEM((S, vw), F32), pltpu.VMEM((S, vw), F32),
                        pltpu.VMEM((GLA_DV, kw), F32), pltpu.VMEM((GLA_DV, kw), F32),
                        pltpu.VMEM((GLA_CHUNK, kw), F32), pltpu.VMEM((GLA_CHUNK, kw), F32)],
        compiler_params=_params(("parallel",)),
        name="gla",
    )(q, k, v, la, gr, ng, *consts)


def _conv_kernel(u_ref, w_ref, cb_ref, lg_ref, lb_ref, out_ref, *, tm, halo):
    S = u_ref.shape[0]
    j = pl.program_id(1)
    t0 = pl.multiple_of(j * tm, tm)
    top = u_ref[pl.ds(pl.multiple_of(jnp.maximum(t0 - halo, 0), halo), halo), :]
    bot = u_ref[pl.ds(pl.multiple_of(jnp.minimum(t0 + tm, S - halo), halo), halo), :]
    top = jnp.where(j > 0, top, 0.0)
    bot = jnp.where(j < pl.num_programs(1) - 1, bot, 0.0)
    win = jnp.concatenate([top, u_ref[pl.ds(t0, tm), :], bot], axis=0)
    rows = tm + 2 * halo
    pad = (CONV_KERNEL - 1) // 2
    w = w_ref[...]
    acc = jnp.zeros((tm, u_ref.shape[1]), F32)
    for r in range(8):
        offs = [k for k in range(CONV_KERNEL) if (halo - pad + k) % 8 == r]
        if not offs:
            continue
        xr = win if r == 0 else pltpu.roll(win, rows - r, 0)
        for k in offs:
            a0 = (halo - pad + k) - r
            acc = acc + xr[a0:a0 + tm, :] * w[k:k + 1, :]
    y = acc + cb_ref[...]
    mu = jnp.mean(y, axis=-1, keepdims=True)
    yc = y - mu
    var = jnp.mean(yc * yc, axis=-1, keepdims=True)
    z = yc * lax.rsqrt(var + LN_EPS) * lg_ref[...] + lb_ref[...]
    out_ref[...] = (z * jax.nn.sigmoid(z)).astype(BF16)


def _conv(u, w, cb, lg, lb, B, S, tm):
    cw = u.shape[1]
    halo = 16
    full = lambda a: pl.BlockSpec(a.shape, lambda b, j: (0,) * a.ndim)
    nj = S // tm
    return pl.pallas_call(
        functools.partial(_conv_kernel, tm=tm, halo=halo),
        grid=(B, nj),
        in_specs=[pl.BlockSpec((S, cw), lambda b, j: (b, 0)), full(w), full(cb), full(lg), full(lb)],
        out_specs=pl.BlockSpec((tm, cw), lambda b, j: (b * nj + j, 0)),
        out_shape=jax.ShapeDtypeStruct((B * S, cw), BF16),
        compiler_params=_params(("parallel", "parallel")),
        name="conv",
    )(u, w, cb, lg, lb)


def _memkv_kernel(m_ref, g_ref, wk_ref, wv_ref, k_ref, v_ref):
    m = _rms(m_ref[...], g_ref[...]).astype(BF16)
    k_ref[...] = _dot(m, wk_ref[...]).astype(BF16)
    v_ref[...] = _dot(m, wv_ref[...]).astype(BF16)


def _memkv(mem2d, g, wk, wv, M):
    R, D = mem2d.shape
    full = lambda a: pl.BlockSpec(a.shape, lambda i: (0,) * a.ndim)
    blk = pl.BlockSpec((M, D), lambda i: (i, 0))
    return pl.pallas_call(
        _memkv_kernel,
        grid=(R // M,),
        in_specs=[blk, full(g), full(wk), full(wv)],
        out_specs=[blk, blk],
        out_shape=[jax.ShapeDtypeStruct((R, D), BF16)] * 2,
        compiler_params=_params(("parallel",)),
        name="memkv",
    )(mem2d, g, wk, wv)


def _mid_kernel(x_ref, a_ref, b_ref, wout_ref, gx_ref, wq_ref, km_ref, vm_ref, wo_ref,
                gf_ref, rw_ref, rb_ref, x2_ref, h3_ref, idx_ref, gate_ref, *, n_exp, parts):
    n = x_ref.shape[0] // parts
    gens = [_mid_rows(pl.ds(p * n, n), pl.ds(p * n * SUBLANES, n * SUBLANES),
                      x_ref, a_ref, b_ref, wout_ref, gx_ref, wq_ref, km_ref, vm_ref, wo_ref,
                      gf_ref, rw_ref, rb_ref, x2_ref, h3_ref, idx_ref, gate_ref, n_exp)
            for p in range(parts)]
    for _ in zip(*gens):
        pass


def _mid_rows(rows, tiles, x_ref, a_ref, b_ref, wout_ref, gx_ref, wq_ref, km_ref, vm_ref, wo_ref,
              gf_ref, rw_ref, rb_ref, x2_ref, h3_ref, idx_ref, gate_ref, n_exp):
    ab = jnp.concatenate([a_ref[rows, :], b_ref[rows, :]], axis=1)
    x1 = x_ref[rows, :] + _dot(ab, wout_ref[...])
    yield
    q2 = _dot(_rms(x1, gx_ref[...]).astype(BF16), wq_ref[...])
    yield
    D = q2.shape[1]
    dh = D // XATTN_HEADS
    ss = []
    for h in range(XATTN_HEADS):
        sl = slice(h * dh, (h + 1) * dh)
        ss.append(_dot_nt(q2[:, sl].astype(BF16), km_ref[:, sl]) * (dh ** -0.5))
    yield
    outs = []
    for h in range(XATTN_HEADS):
        sl = slice(h * dh, (h + 1) * dh)
        p = jnp.exp(ss[h] - jnp.max(ss[h], axis=-1, keepdims=True))
        p = p / jnp.sum(p, axis=-1, keepdims=True)
        outs.append(_dot(p.astype(BF16), vm_ref[:, sl]))
    yield
    o = jnp.concatenate(outs, axis=1).astype(BF16)
    x2 = x1 + _dot(o, wo_ref[...])
    yield
    x2_ref[rows, :] = x2
    h3 = _rms(x2, gf_ref[...])
    _store_rows_as_tiles(h3_ref.at[tiles], h3)
    lane = lax.broadcasted_iota(jnp.int32, (x2.shape[0], LANES), 1)
    logits = _dot(h3.astype(BF16), rw_ref[...]) + rb_ref[...]
    yield
    cur = jnp.where(lane < n_exp, logits, -jnp.inf)
    lane_f = lane.astype(F32)
    vals, idxs = [], []
    for _ in range(TOP_K):
        m = jnp.max(cur, axis=-1, keepdims=True)
        ik = jnp.min(jnp.where(cur == m, lane_f, float(LANES)), axis=-1, keepdims=True)
        vals.append(m)
        idxs.append(ik)
        cur = jnp.where(lane_f == ik, -jnp.inf, cur)
    es = [jnp.exp(vk - vals[0]) for vk in vals]
    den = es[0]
    for ek in es[1:]:
        den = den + ek
    idx_out = jnp.zeros(lane.shape, F32)
    gate_out = jnp.zeros(lane.shape, F32)
    for kk in range(TOP_K):
        idx_out = jnp.where(lane == kk, idxs[kk], idx_out)
        gate_out = jnp.where(lane == kk, es[kk] / den, gate_out)
    idx_ref[rows, :] = idx_out.astype(jnp.int32)
    gate_ref[:, rows] = gate_out.T[0:SUBLANES, :]
    yield


def _mid(x2d, a, b, wout, gx, wq, km, vm, wo, gf, rw, rb, S, M, n_exp, tm):
    T, D = x2d.shape
    half = a.shape[1]
    per_b = S // tm
    row = lambda n: pl.BlockSpec((tm, n), lambda i: (i, 0))
    full = lambda arr: pl.BlockSpec(arr.shape, lambda i: (0,) * arr.ndim)
    memb = pl.BlockSpec((M, D), lambda i: (i // per_b, 0))
    return pl.pallas_call(
        functools.partial(_mid_kernel, n_exp=n_exp, parts=2),
        grid=(T // tm,),
        in_specs=[row(D), row(half), row(half), full(wout), full(gx), full(wq), memb, memb,
                  full(wo), full(gf), full(rw), full(rb)],
        out_specs=[row(D), pl.BlockSpec((tm * SUBLANES, LANES), lambda i: (i, 0)),
                   row(LANES), pl.BlockSpec((SUBLANES, tm), lambda i: (0, i))],
        out_shape=[jax.ShapeDtypeStruct((T, D), F32),
                   jax.ShapeDtypeStruct((T * SUBLANES, LANES), F32),
                   jax.ShapeDtypeStruct((T, LANES), jnp.int32),
                   jax.ShapeDtypeStruct((SUBLANES, T), F32)],
        compiler_params=_params(("parallel",)),
        name="mid",
    )(x2d, a, b, wout, gx, wq, km, vm, wo, gf, rw, rb)


def _onehot(idx_ref, kk, lane):
    return (lane == idx_ref[:, kk:kk + 1]).astype(F32)


def _place_kernel(idx_ref, tri_ref, upper_ref, pos_ref, cnt_ref):
    tm = idx_ref.shape[0]
    lane = lax.broadcasted_iota(jnp.int32, (tm, LANES), 1)
    hots = [_onehot(idx_ref, kk, lane) for kk in range(TOP_K)]
    mask = hots[0]
    for hk in hots[1:]:
        mask = mask + hk
    incl = _dot(tri_ref[...], mask.astype(BF16))
    cnt = jnp.broadcast_to(incl[tm - 1:tm, :], (SUBLANES, LANES))
    hi = jnp.floor(cnt * (1.0 / 16.0))
    lo = cnt - 16.0 * hi
    start = 16.0 * _dot(hi.astype(BF16), upper_ref[...]) + _dot(lo.astype(BF16), upper_ref[...])
    before = incl - mask + start[0:1, :]
    out = jnp.zeros((tm, LANES), F32)
    for kk in range(TOP_K):
        rk = jnp.sum(hots[kk] * before, axis=-1, keepdims=True)
        out = jnp.where(lane == kk, rk, out)
    pos_ref[...] = (out.T[0:SUBLANES, :] * SUBLANES).astype(jnp.int32)
    cnt_ref[...] = cnt


def _place(idx, tm):
    T = idx.shape[0]
    assert tm <= 16 * 256
    tri = jnp.asarray(np.tril(np.ones((tm, tm), np.float32)), BF16)
    upper = jnp.asarray(np.triu(np.ones((LANES, LANES), np.float32), 1), BF16)
    row = pl.BlockSpec((tm, LANES), lambda i: (i, 0))
    return pl.pallas_call(
        _place_kernel,
        grid=(T // tm,),
        in_specs=[row, pl.BlockSpec((tm, tm), lambda i: (0, 0)),
                  pl.BlockSpec((LANES, LANES), lambda i: (0, 0))],
        out_specs=[pl.BlockSpec((SUBLANES, tm), lambda i: (0, i)),
                   pl.BlockSpec((SUBLANES, LANES), lambda i: (i, 0))],
        out_shape=[jax.ShapeDtypeStruct((SUBLANES, T), jnp.int32),
                   jax.ShapeDtypeStruct((T // tm * SUBLANES, LANES), F32)],
        compiler_params=_params(("parallel",)),
        name="place",
    )(idx, tri, upper)


ROW_UNROLL = 8


def _tile(ref, row):
    return ref.at[pl.ds(pl.multiple_of(row * SUBLANES, SUBLANES), SUBLANES)]


def _wait_rows(buf_ref, sem):
    pltpu.make_async_copy(buf_ref, buf_ref, sem).wait()


RUN_CHUNK = 16


def _rows(ref, row, n):
    return ref.at[pl.ds(pl.multiple_of(row * SUBLANES, SUBLANES), n * SUBLANES)]


def _for_run_pieces(n, fn):
    shift = RUN_CHUNK.bit_length() - 1
    whole = lax.shift_right_logical(n, shift)

    def body(c, carry):
        fn(c * RUN_CHUNK, RUN_CHUNK)
        return carry

    lax.fori_loop(0, whole, body, 0)
    off = whole * RUN_CHUNK
    for bit in reversed(range(shift)):
        take = (n & (1 << bit)) != 0

        @pl.when(take)
        def _(off=off, bit=bit):
            fn(off, 1 << bit)

        off = off + jnp.where(take, 1 << bit, 0)


def _dispatch_kernel(pos_ref, rdst_ref, rn_ref, rloc_ref, lo_ref, hi_ref, nb_ref, h_ref, xs_ref,
                     sb0, sb1, zero_ref, sem, zsem):
    j = pl.program_id(0)
    tm = h_ref.shape[0] // SUBLANES // 2
    slots = rn_ref.shape[0] // 2
    n_exp = lo_ref.shape[0]
    blk = zero_ref.shape[0]

    def sort(half, sb):
        def body(i, carry):
            for u in range(ROW_UNROLL):
                t = half * tm + i * ROW_UNROLL + u
                tile = h_ref[pl.ds(pl.multiple_of(t * SUBLANES, SUBLANES), SUBLANES), :]
                for kk in range(TOP_K):
                    p = pos_ref[kk, t]
                    sb[pl.ds(pl.multiple_of(p, SUBLANES), SUBLANES), :] = tile
            return carry

        lax.fori_loop(0, tm // ROW_UNROLL, body, 0)

    def start_runs(half, sb, s):
        def expert(e, carry):
            r = half * slots + e
            loc, dst = rloc_ref[r], rdst_ref[r]
            _for_run_pieces(rn_ref[r], lambda off, n: pltpu.make_async_copy(
                _rows(sb, loc + off, n), _rows(xs_ref, dst + off, n), s).start())
            return carry

        lax.fori_loop(0, n_exp, expert, 0)

    @pl.when(j == 0)
    def _():
        zero_ref[...] = jnp.zeros_like(zero_ref)

        def each_fill(fn):
            def expert(e, carry):
                row = lo_ref[e]
                n = hi_ref[e] - row
                for bit in reversed(range(MOE_ROWS.bit_length() - 1)):
                    size = (1 << bit) * SUBLANES
                    take = (n & (1 << bit)) != 0

                    @pl.when(take)
                    def _(row=row, size=size):
                        dst = xs_ref.at[pl.ds(pl.multiple_of(row * SUBLANES, SUBLANES), size)]
                        fn(pltpu.make_async_copy(zero_ref.at[pl.ds(0, size)], dst, zsem))

                    row = row + jnp.where(take, 1 << bit, 0)
                return carry

            lax.fori_loop(0, lo_ref.shape[0], expert, 0)

            def block(b, carry):
                fn(pltpu.make_async_copy(
                    zero_ref, xs_ref.at[pl.ds(pl.multiple_of(b * blk, blk), blk)], zsem))
                return carry

            lax.fori_loop(nb_ref[0], xs_ref.shape[0] // blk, block, 0)

        each_fill(lambda cp: cp.start())
        each_fill(lambda cp: cp.wait())

    sort(0, sb0)
    start_runs(0, sb0, sem.at[0])

    @pl.when(j > 0)
    def _():
        _wait_rows(sb1, sem.at[1])

    sort(1, sb1)
    start_runs(1, sb1, sem.at[1])
    _wait_rows(sb0, sem.at[0])

    @pl.when(j == pl.num_programs(0) - 1)
    def _():
        _wait_rows(sb1, sem.at[1])


def _dispatch(pos, run_dst, run_n, run_loc, pad_lo, pad_hi, n_used, h3, n_rows, tm):
    T = h3.shape[0] // SUBLANES
    slots = run_n.shape[0] // (T // tm)
    smem = pl.BlockSpec(memory_space=pltpu.SMEM)
    per_step = lambda n: pl.BlockSpec((n,), lambda j: (j,), memory_space=pltpu.SMEM)
    sorted_rows = pltpu.VMEM((tm * TOP_K * SUBLANES, LANES), F32)
    return pl.pallas_call(
        _dispatch_kernel,
        grid=(T // (2 * tm),),
        in_specs=[pl.BlockSpec((SUBLANES, 2 * tm), lambda j: (0, j), memory_space=pltpu.SMEM),
                  per_step(2 * slots), per_step(2 * slots),
                  per_step(2 * slots), smem, smem, smem,
                  pl.BlockSpec((2 * tm * SUBLANES, LANES), lambda j: (j, 0))],
        out_specs=pl.BlockSpec(memory_space=pl.ANY),
        out_shape=jax.ShapeDtypeStruct((n_rows * SUBLANES, LANES), F32),
        scratch_shapes=[sorted_rows, sorted_rows, pltpu.VMEM((MOE_ROWS * SUBLANES, LANES), F32),
                        pltpu.SemaphoreType.DMA((2,)), pltpu.SemaphoreType.DMA(())],
        compiler_params=_params(("arbitrary",)),
        name="dispatch",
    )(pos, run_dst, run_n, run_loc, pad_lo, pad_hi, n_used, h3)


def _experts_kernel(be_ref, live_ref, first_ref, next_ref, slot_ref,
                    xs_ref, wu_hbm, bu_ref, wd_hbm, bd_ref, ys_ref,
                    wu_f32, wd_f32, wu_bf, wd_bf, sem):
    i = pl.program_id(0)

    def fetch(e, s):
        return (pltpu.make_async_copy(wu_hbm.at[e], wu_f32.at[s], sem.at[s]),
                pltpu.make_async_copy(wd_hbm.at[e], wd_f32.at[s], sem.at[s]))

    @pl.when(i == 0)
    def _():
        for cp in fetch(be_ref[0], 0):
            cp.start()

    @pl.when(first_ref[i] == 1)
    def _():
        s = slot_ref[i]
        for cp in fetch(be_ref[i], s):
            cp.wait()

        @pl.when(next_ref[i] >= 0)
        def _():
            for cp in fetch(next_ref[i], 1 - s):
                cp.start()

        wu_bf[...] = wu_f32[s].astype(BF16)
        wd_bf[...] = wd_f32[s].astype(BF16)

    def mlp(tiles):
        F = wd_bf.shape[0]
        x = _load_rows_from_tiles(xs_ref.at[tiles]).astype(BF16)
        gu = _dot(x, wu_bf[...]) + bu_ref[0]
        yield
        x_glu = jnp.minimum(gu[:, :F], SWIGLU_LIMIT)
        x_lin = jnp.clip(gu[:, F:], -SWIGLU_LIMIT, SWIGLU_LIMIT)
        hid = x_glu * jax.nn.sigmoid(SWIGLU_ALPHA * x_glu) * (x_lin + 1.0)
        y = _dot(hid.astype(BF16), wd_bf[...]) + bd_ref[0]
        yield
        _store_rows_as_tiles(ys_ref.at[tiles], y)
        yield

    n = xs_ref.shape[0] // MOE_PARTS
    for live in range(MOE_PARTS + 1):
        @pl.when(live_ref[i] == live)
        def _(live=live):
            for _ in zip(*[mlp(pl.ds(p * n, n)) for p in range(live)]):
                pass
            if live < MOE_PARTS:
                rest = (MOE_PARTS - live) * n
                ys_ref[pl.ds(live * n, rest), :] = jnp.zeros((rest, ys_ref.shape[1]), F32)


def _expert_runs(block_e):
    nb = block_e.shape[0]
    ar = jnp.arange(nb, dtype=jnp.int32)
    first = jnp.concatenate([jnp.ones((1,), bool), block_e[1:] != block_e[:-1]])
    starts = jnp.where(first, ar, nb)
    later = lax.cummin(jnp.concatenate([starts[1:], jnp.full((1,), nb, jnp.int32)]), reverse=True)
    nxt = jnp.where(later < nb, block_e[jnp.minimum(later, nb - 1)], -1)
    slot = (jnp.cumsum(first.astype(jnp.int32)) - 1) & 1
    return first.astype(jnp.int32), nxt.astype(jnp.int32), slot.astype(jnp.int32)


def _experts(block_e, live, xs, wu, bu, wd, bd):
    E, D, F2 = wu.shape
    F = wd.shape[1]
    nb = xs.shape[0] // (MOE_ROWS * SUBLANES)
    rows = pl.BlockSpec((MOE_ROWS * SUBLANES, LANES), lambda i, *_: (i, 0))
    grid_spec = pltpu.PrefetchScalarGridSpec(
        num_scalar_prefetch=5,
        grid=(nb,),
        in_specs=[rows,
                  pl.BlockSpec(memory_space=pl.ANY),
                  pl.BlockSpec((1, 1, F2), lambda i, be, *_: (be[i], 0, 0)),
                  pl.BlockSpec(memory_space=pl.ANY),
                  pl.BlockSpec((1, 1, D), lambda i, be, *_: (be[i], 0, 0))],
        out_specs=rows,
        scratch_shapes=[pltpu.VMEM((2, D, F2), F32), pltpu.VMEM((2, F, D), F32),
                        pltpu.VMEM((D, F2), BF16), pltpu.VMEM((F, D), BF16),
                        pltpu.SemaphoreType.DMA((2,))],
    )
    return pl.pallas_call(
        _experts_kernel,
        grid_spec=grid_spec,
        out_shape=jax.ShapeDtypeStruct(xs.shape, F32),
        compiler_params=_params(("arbitrary",)),
        name="experts",
    )(block_e, live, *_expert_runs(block_e), xs, wu, bu, wd, bd)


def _combine_kernel(pos_ref, gate_ref, src_ref, rn_ref, rloc_ref, nsrc_ref, nrn_ref, nrloc_ref,
                    x2_ref, g_ref, ys_ref, out_ref, yb0, yb1, acc_ref, sem, *, n_exp, apply_norm):
    j = pl.program_id(0)
    tm = x2_ref.shape[0] // 2
    slots = rn_ref.shape[0] // 2

    def start(tbls, half, yb, s):
        src_t, n_t, loc_t = tbls

        def expert(e, carry):
            r = half * slots + e
            loc, src = loc_t[r], src_t[r]
            _for_run_pieces(n_t[r], lambda off, n: pltpu.make_async_copy(
                _rows(ys_ref, src + off, n), _rows(yb, loc + off, n), s).start())
            return carry

        lax.fori_loop(0, n_exp, expert, 0)

    def finish(half, yb, s):
        _wait_rows(yb, s)

        def body(i, carry):
            for u in range(ROW_UNROLL):
                t = i * ROW_UNROLL + u
                a = half * tm + t
                acc = None
                for kk in range(TOP_K):
                    p = pos_ref[kk, a]
                    term = yb[pl.ds(pl.multiple_of(p, SUBLANES), SUBLANES), :] \
                        * gate_ref[kk, a]
                    acc = term if acc is None else acc + term
                acc_ref[pl.ds(pl.multiple_of(t * SUBLANES, SUBLANES), SUBLANES), :] = acc
            return carry

        lax.fori_loop(0, tm // ROW_UNROLL, body, 0)
        rows = pl.ds(half * tm, tm)
        x = x2_ref[rows, :] + _load_rows_from_tiles(acc_ref)
        out_ref[rows, :] = _rms(x, g_ref[...]) if apply_norm else x

    cur = (src_ref, rn_ref, rloc_ref)

    @pl.when(j == 0)
    def _():
        start(cur, 0, yb0, sem.at[0])

    start(cur, 1, yb1, sem.at[1])
    finish(0, yb0, sem.at[0])

    @pl.when(j < pl.num_programs(0) - 1)
    def _():
        start((nsrc_ref, nrn_ref, nrloc_ref), 0, yb0, sem.at[0])

    finish(1, yb1, sem.at[1])


def _combine(pos, gates, run_src, run_n, run_loc, x2, g, ys, tm, n_exp, apply_norm):
    T, D = x2.shape
    steps = T // (2 * tm)
    slots = run_n.shape[0] // (T // tm)
    per_step = lambda n: pl.BlockSpec((n,), lambda j: (j,), memory_space=pltpu.SMEM)
    next_step = lambda n: pl.BlockSpec((n,), lambda j: (jnp.minimum(j + 1, steps - 1),),
                                       memory_space=pltpu.SMEM)
    sorted_rows = pltpu.VMEM((tm * TOP_K * SUBLANES, LANES), F32)
    return pl.pallas_call(
        functools.partial(_combine_kernel, n_exp=n_exp, apply_norm=apply_norm),
        grid=(steps,),
        in_specs=[pl.BlockSpec((SUBLANES, 2 * tm), lambda j: (0, j), memory_space=pltpu.SMEM),
                  pl.BlockSpec((SUBLANES, 2 * tm), lambda j: (0, j), memory_space=pltpu.SMEM),
                  per_step(2 * slots), per_step(2 * slots), per_step(2 * slots),
                  next_step(2 * slots), next_step(2 * slots), next_step(2 * slots),
                  pl.BlockSpec((2 * tm, D), lambda j: (j, 0)),
                  pl.BlockSpec(g.shape, lambda j: (0, 0)),
                  pl.BlockSpec(memory_space=pl.ANY)],
        out_specs=pl.BlockSpec((2 * tm, D), lambda j: (j, 0)),
        out_shape=jax.ShapeDtypeStruct((T, D), F32),
        scratch_shapes=[sorted_rows, sorted_rows, pltpu.VMEM((tm * SUBLANES, LANES), F32),
                        pltpu.SemaphoreType.DMA((2,))],
        compiler_params=_params(("arbitrary",)),
        name="combine",
    )(pos, gates, run_src, run_n, run_loc, run_src, run_n, run_loc, x2, g, ys)


def _row(v):
    return v.reshape(1, -1).astype(F32)


def kernel(x, mem, norm_mix_g, w_in, gate_w2_fwd, gate_b_fwd, gate_w2_bwd, gate_b_bwd,
           gla_norm_g, conv_w, conv_b, conv_ln_g, conv_ln_b, w_out, norm_xattn_g,
           norm_mem_g, xattn_wq, xattn_wk, xattn_wv, xattn_wo, norm_ffn_g, router_w,
           router_b, exp_w_up, exp_b_up, exp_w_down, exp_b_down, final_norm_g):
    B, S, D = x.shape
    M = mem.shape[1]
    T = B * S
    depth = w_in.shape[0]
    kw = GLA_HEADS * GLA_DK
    vw = GLA_HEADS * GLA_DV
    rank_w = gate_w2_fwd.shape[1]
    cw = conv_w.shape[2]
    E = router_w.shape[2]
    assert vw + cw == w_out.shape[1] and 2 * rank_w <= LANES and E <= LANES

    xc = x.reshape(T, D)
    for l in range(depth):
        o_gf = 2 * kw + 2 * vw
        o_glu = o_gf + 2 * rank_w
        wl = w_in[l]
        w_perm = jnp.concatenate(
            [wl[:, :o_gf], wl[:, o_glu:], wl[:, o_gf:o_glu],
             jnp.zeros((D, LANES - 2 * rank_w), wl.dtype)], axis=1).astype(BF16)
        w2 = jnp.zeros((LANES, 2 * kw), F32)
        w2 = w2.at[:rank_w, :kw].set(gate_w2_fwd[l]).at[rank_w:2 * rank_w, kw:].set(gate_w2_bwd[l])
        b2 = jnp.concatenate([gate_b_fwd[l], gate_b_bwd[l]]).reshape(1, -1)

        q, k, v, gr, la, u = _inproj(xc, _row(norm_mix_g[l]), w_perm, w2.astype(BF16), b2,
                                     kw, vw, cw, tm=512)
        a_out = _gla(q, k, v, la, gr, _row(gla_norm_g[l]), B, S)
        b_out = _conv(u, conv_w[l], _row(conv_b[l]), _row(conv_ln_g[l]), _row(conv_ln_b[l]),
                      B, S, tm=512)
        km, vm = _memkv(mem.reshape(B * M, D), _row(norm_mem_g[l]),
                        xattn_wk[l].astype(BF16), xattn_wv[l].astype(BF16), M)

        rw = jnp.zeros((D, LANES), F32).at[:, :E].set(router_w[l]).astype(BF16)
        rb = jnp.zeros((1, LANES), F32).at[0, :E].set(router_b[l])
        x2, h3, idx, gates = _mid(
            xc, a_out, b_out, w_out[l].astype(BF16), _row(norm_xattn_g[l]),
            xattn_wq[l].astype(BF16), km, vm, xattn_wo[l].astype(BF16),
            _row(norm_ffn_g[l]), rw, rb, S, M, E, tm=512)

        tm = MOE_TILE
        pos, tile_cnt = _place(idx, tm=tm)
        n_te = tile_cnt.reshape(T // tm, SUBLANES, LANES)[:, 0, :E].astype(jnp.int32)
        counts = jnp.sum(n_te, axis=0)
        padded = ((counts + MOE_ROWS - 1) // MOE_ROWS) * MOE_ROWS
        pends = jnp.cumsum(padded)
        run_row = (pends - padded)[None, :] + jnp.cumsum(n_te, axis=0) - n_te
        run_loc = jnp.cumsum(n_te, axis=1) - n_te
        slots = max(E, LANES // 2)
        runs = [jnp.pad(a, ((0, 0), (0, slots - E))).reshape(-1).astype(jnp.int32)
                for a in (run_row, n_te, run_loc)]
        n_blocks = (T * TOP_K + E * (MOE_ROWS - 1)) // MOE_ROWS
        blk_start = jnp.arange(n_blocks, dtype=jnp.int32) * MOE_ROWS
        block_e = jnp.minimum(jnp.sum(pends[None, :] <= blk_start[:, None], axis=1), E - 1)
        n_used = (pends[E - 1] // MOE_ROWS).reshape(1)

        xs = _dispatch(pos, *runs, (pends - padded + counts).astype(jnp.int32),
                       pends.astype(jnp.int32), n_used.astype(jnp.int32), h3,
                       n_blocks * MOE_ROWS, tm=tm)
        group = MOE_ROWS // MOE_PARTS
        real = jnp.clip((pends - padded + counts)[block_e] - blk_start, 0, MOE_ROWS)
        live = jnp.where(blk_start < pends[E - 1], (real + group - 1) // group, 0)
        ys = _experts(block_e.astype(jnp.int32), live.astype(jnp.int32), xs,
                      exp_w_up[l], exp_b_up[l][:, None, :],
                      exp_w_down[l], exp_b_down[l][:, None, :])
        xc = _combine(pos, gates, *runs, x2,
                      _row(final_norm_g), ys, tm=tm, n_exp=E, apply_norm=(l == depth - 1))
    return xc.reshape(B, S, D)
```

```python
import functools

import numpy as np
import jax
import jax.numpy as jnp
from jax import lax
from jax.experimental import pallas as pl
from jax.experimental.pallas import tpu as pltpu

F32 = jnp.float32
BF16 = jnp.bfloat16

GLA_HEADS = 4
GLA_DK = 64
GLA_DV = 128
GLA_CHUNK = 64
GLA_TILE = 4
GATE_TAU = 16.0
CONV_KERNEL = 31
XATTN_HEADS = 4
TOP_K = 4
SWIGLU_ALPHA = 1.702
SWIGLU_LIMIT = 7.0
RMS_EPS = 1e-6
LN_EPS = 1e-5

LANES = 128
SUBLANES = 8
MOE_ROWS = 512
MOE_PARTS = 2
MOE_TILE = 512
VMEM_LIMIT = 56 * 1024 * 1024


def _dot(a, b):
    return jnp.dot(a, b, preferred_element_type=F32)


def _dot_nt(a, b):
    return lax.dot_general(a, b, (((1,), (1,)), ((), ())), preferred_element_type=F32)


def _dot_tn(a, b):
    return lax.dot_general(a, b, (((0,), (0,)), ((), ())), preferred_element_type=F32)


def _rms(x, g):
    return x * lax.rsqrt(jnp.mean(x * x, axis=-1, keepdims=True) + RMS_EPS) * g


def _params(sem):
    return pltpu.CompilerParams(dimension_semantics=sem, vmem_limit_bytes=VMEM_LIMIT)


def _store_rows_as_tiles(ref, val):
    n, d = val.shape
    assert d == SUBLANES * LANES and ref.shape == (n * SUBLANES, LANES)
    for s in range(SUBLANES):
        ref[pl.ds(s, n, stride=SUBLANES), :] = val[:, s * LANES:(s + 1) * LANES]


def _load_rows_from_tiles(ref):
    n = ref.shape[0] // SUBLANES
    return jnp.concatenate([ref[pl.ds(s, n, stride=SUBLANES), :] for s in range(SUBLANES)], axis=1)


def _inproj_kernel(x_ref, g_ref, w_ref, w2_ref, b2_ref,
                   q_ref, k_ref, v_ref, gr_ref, la_ref, u_ref, *, kw, vw, cw, parts):
    n = x_ref.shape[0] // parts
    gens = [_inproj_rows(pl.ds(p * n, n), x_ref, g_ref, w_ref, w2_ref, b2_ref,
                         q_ref, k_ref, v_ref, gr_ref, la_ref, u_ref, kw, vw, cw)
            for p in range(parts)]
    for _ in zip(*gens):
        pass


def _inproj_rows(rows, x_ref, g_ref, w_ref, w2_ref, b2_ref,
                 q_ref, k_ref, v_ref, gr_ref, la_ref, u_ref, kw, vw, cw):
    h = _rms(x_ref[rows, :], g_ref[...]).astype(BF16)
    acc = _dot(h, w_ref[...])
    yield
    o = 0
    q_ref[rows, :] = (acc[:, o:o + kw] * (GLA_DK ** -0.5)).astype(BF16); o += kw
    k_ref[rows, :] = acc[:, o:o + kw].astype(BF16); o += kw
    v_ref[rows, :] = acc[:, o:o + vw].astype(BF16); o += vw
    r = acc[:, o:o + vw]; o += vw
    gr_ref[rows, :] = (r * jax.nn.sigmoid(r)).astype(BF16)
    a = acc[:, o:o + cw]; o += cw
    g = acc[:, o:o + cw]; o += cw
    u_ref[rows, :] = a * jax.nn.sigmoid(g)
    z = _dot(acc[:, o:o + LANES].astype(BF16), w2_ref[...]) + b2_ref[...]
    yield
    la_ref[rows, :] = (jnp.minimum(z, 0.0) - jnp.log1p(jnp.exp(-jnp.abs(z)))) * (1.0 / GATE_TAU)
    yield


def _inproj(x2d, g, w, w2, b2, kw, vw, cw, tm):
    T, D = x2d.shape
    W = w.shape[1]
    row = lambda n: pl.BlockSpec((tm, n), lambda i: (i, 0))
    full = lambda a: pl.BlockSpec(a.shape, lambda i: (0,) * a.ndim)
    return pl.pallas_call(
        functools.partial(_inproj_kernel, kw=kw, vw=vw, cw=cw, parts=2),
        grid=(T // tm,),
        in_specs=[row(D), full(g), full(w), full(w2), full(b2)],
        out_specs=[row(kw), row(kw), row(vw), row(vw), row(2 * kw), row(cw)],
        out_shape=[jax.ShapeDtypeStruct((T, kw), BF16), jax.ShapeDtypeStruct((T, kw), BF16),
                   jax.ShapeDtypeStruct((T, vw), BF16), jax.ShapeDtypeStruct((T, vw), BF16),
                   jax.ShapeDtypeStruct((T, 2 * kw), F32), jax.ShapeDtypeStruct((T, cw), F32)],
        compiler_params=_params(("parallel",)),
        name="inproj",
    )(x2d, g, w, w2, b2)


def _gla_levels():
    ms, m = [], GLA_CHUNK // 2
    while m >= GLA_TILE:
        ms.append(m)
        m //= 2
    return ms


def _gla_constants(bwd):
    C, H = GLA_CHUNK, GLA_HEADS
    t = np.arange(C)[:, None]
    s = np.arange(C)[None, :]
    lev_q, lev_mask = [], []
    for m in _gla_levels():
        g = 2 * m
        second = (t % g) >= m
        same = t // g == s // g
        if not bwd:
            qrow = second
            mask = same & ((t % g) >= m) & ((s % g) < m)
        else:
            qrow = ~second
            mask = same & ((t % g) < m) & ((s % g) >= m)
        lev_q.append(np.broadcast_to(qrow, (C, H * GLA_DK)))
        lev_mask.append(np.tile(mask, (1, H)))
    shifts = range(0, GLA_TILE) if not bwd else range(1, GLA_TILE)
    sh_mask = []
    for sh in shifts:
        j = t - sh if not bwd else t + sh
        sh_mask.append(np.tile((s == j) & (t // GLA_TILE == s // GLA_TILE), (1, H)))
    f = lambda xs: np.stack(xs).astype(np.float32)
    return f(lev_q), f(lev_mask), f(sh_mask)


def _gla_head_mask():
    C, H = GLA_CHUNK, GLA_HEADS
    assert GLA_DK == C
    blk = np.arange(H * C) // C
    return (blk[:, None] == blk[None, :]).astype(np.float32)


def _gla_chunk(la, q, k, v, st_ref, c_ref, tri_ref, lq_ref, lm_ref, sm_ref, hm_ref, bwd):
    C, H = GLA_CHUNK, GLA_HEADS
    kw = q.shape[1]
    hm = hm_ref[...]

    def stack(xb):
        return jnp.concatenate([xb] * H, axis=0) * hm

    tri = tri_ref[...]
    hi = la.astype(BF16)
    incl = _dot(tri, hi) + _dot(tri, (la - hi.astype(F32)).astype(BF16))
    yield None
    tot = incl[C - 1:C]
    cum = incl if not bwd else incl - la
    x_q, x_k = (cum, tot - cum) if not bwd else (tot - cum, cum)
    e_q, e_k, a_tot = jnp.exp(x_q), jnp.exp(x_k), jnp.exp(tot)
    c_ref[...] = cum

    a = jnp.zeros((C, H * C), F32)
    for lv, m in enumerate(_gla_levels()):
        g = 2 * m
        edge = m - 1 if not bwd else m
        r = jnp.concatenate([jnp.broadcast_to(c_ref[s0 + edge:s0 + edge + 1, :], (g, kw))
                             for s0 in range(0, C, g)], axis=0)
        d = cum - r
        z = (jnp.where(lq_ref[lv] > 0, q, k) * jnp.exp(jnp.minimum(d, -d))).astype(BF16)
        a = a + _dot_nt(z, stack(z)) * lm_ref[lv]
    ps = []
    for si in range(sm_ref.shape[0]):
        sh = si if not bwd else si + 1
        if sh == 0:
            ps.append((q * k).astype(BF16))
            continue
        amt = sh if not bwd else C - sh
        kr = pltpu.roll(k, amt, 0)
        cr = pltpu.roll(cum, amt, 0)
        d = (cum - cr) if not bwd else (cr - cum)
        ps.append((q * kr * jnp.exp(jnp.minimum(d, 0.0))).astype(BF16))
    sc = _dot(jnp.concatenate(ps, axis=0), hm)
    yield None
    for si in range(sm_ref.shape[0]):
        a = a + sc[si * C:(si + 1) * C] * sm_ref[si]

    vst = jnp.concatenate([v[:, h * GLA_DV:(h + 1) * GLA_DV] for h in range(H)], axis=0)
    st = st_ref[...]
    o4 = _dot(stack(a.astype(BF16)), vst) + _dot_nt(stack((q * e_q).astype(BF16)), st.astype(BF16))
    st_ref[...] = st * a_tot + _dot_tn(vst, stack((k * e_k).astype(BF16)))
    yield jnp.concatenate([o4[h * C:(h + 1) * C] for h in range(H)], axis=1)


def _gla_kernel(q_ref, k_ref, v_ref, la_ref, gr_ref, ng_ref,
                lqf_ref, lmf_ref, smf_ref, lqb_ref, lmb_ref, smb_ref, tri_ref, hm_ref,
                out_ref, of_ref, ob_ref, stf_ref, stb_ref, cf_ref, cb_ref):
    C = GLA_CHUNK
    S = q_ref.shape[0]
    kw = q_ref.shape[1]
    n = S // C
    stf_ref[...] = jnp.zeros_like(stf_ref)
    stb_ref[...] = jnp.zeros_like(stb_ref)

    def body(i, carry):
        rf = pl.ds(pl.multiple_of(i * C, C), C)
        rb = pl.ds(pl.multiple_of((n - 1 - i) * C, C), C)
        fwd = _gla_chunk(
            la_ref[rf, 0:kw], q_ref[rf, :].astype(F32), k_ref[rf, :].astype(F32), v_ref[rf, :],
            stf_ref, cf_ref, tri_ref, lqf_ref, lmf_ref, smf_ref, hm_ref, False)
        bwd = _gla_chunk(
            la_ref[rb, kw:2 * kw], q_ref[rb, :].astype(F32), k_ref[rb, :].astype(F32), v_ref[rb, :],
            stb_ref, cb_ref, tri_ref, lqb_ref, lmb_ref, smb_ref, hm_ref, True)
        for o_f, o_b in zip(fwd, bwd):
            pass
        of_ref[rf, :] = o_f
        ob_ref[rb, :] = o_b
        return carry

    lax.fori_loop(0, n, body, 0)

    rows = 256
    def fin(i, carry):
        r = pl.ds(pl.multiple_of(i * rows, rows), rows)
        o = of_ref[r, :] + ob_ref[r, :]
        ys = []
        for h in range(GLA_HEADS):
            oh = o[:, h * GLA_DV:(h + 1) * GLA_DV]
            ys.append(_rms(oh, ng_ref[...]))
        y = jnp.concatenate(ys, axis=1) * gr_ref[r, :].astype(F32)
        out_ref[r, :] = y.astype(BF16)
        return carry

    lax.fori_loop(0, S // rows, fin, 0)


def _gla(q, k, v, la, gr, ng, B, S):
    kw, vw = q.shape[1], v.shape[1]
    cf = _gla_constants(False)
    cb = _gla_constants(True)
    tri = np.tril(np.ones((GLA_CHUNK, GLA_CHUNK), np.float32))
    consts = [jnp.asarray(c) for c in cf] + [jnp.asarray(c) for c in cb] + \
             [jnp.asarray(tri, BF16), jnp.asarray(_gla_head_mask(), BF16)]
    seq = lambda n: pl.BlockSpec((S, n), lambda b: (b, 0))
    full = lambda a: pl.BlockSpec(a.shape, lambda b: (0,) * a.ndim)
    return pl.pallas_call(
        _gla_kernel,
        grid=(B,),
        in_specs=[seq(kw), seq(kw), seq(vw), seq(2 * kw), seq(vw), full(ng)] + [full(c) for c in consts],
        out_specs=seq(vw),
        out_shape=jax.ShapeDtypeStruct((B * S, vw), BF16),
        scratch_shapes=[pltpu.VMEM((S, vw), F32), pltpu.VMEM((S, vw), F32),
                        pltpu.VMEM((GLA_DV, kw), F32), pltpu.VMEM((GLA_DV, kw), F32),
                        pltpu.VMEM((GLA_CHUNK, kw), F32), pltpu.VMEM((GLA_CHUNK, kw), F32)],
        compiler_params=_params(("parallel",)),
        name="gla",
    )(q, k, v, la, gr, ng, *consts)


def _conv_kernel(u_ref, w_ref, cb_ref, lg_ref, lb_ref, out_ref, *, tm, halo):
    S = u_ref.shape[0]
    j = pl.program_id(1)
    t0 = pl.multiple_of(j * tm, tm)
    top = u_ref[pl.ds(pl.multiple_of(jnp.maximum(t0 - halo, 0), halo), halo), :]
    bot = u_ref[pl.ds(pl.multiple_of(jnp.minimum(t0 + tm, S - halo), halo), halo), :]
    top = jnp.where(j > 0, top, 0.0)
    bot = jnp.where(j < pl.num_programs(1) - 1, bot, 0.0)
    win = jnp.concatenate([top, u_ref[pl.ds(t0, tm), :], bot], axis=0)
    rows = tm + 2 * halo
    pad = (CONV_KERNEL - 1) // 2
    w = w_ref[...]
    acc = jnp.zeros((tm, u_ref.shape[1]), F32)
    for r in range(8):
        offs = [k for k in range(CONV_KERNEL) if (halo - pad + k) % 8 == r]
        if not offs:
            continue
        xr = win if r == 0 else pltpu.roll(win, rows - r, 0)
        for k in offs:
            a0 = (halo - pad + k) - r
            acc = acc + xr[a0:a0 + tm, :] * w[k:k + 1, :]
    y = acc + cb_ref[...]
    mu = jnp.mean(y, axis=-1, keepdims=True)
    yc = y - mu
    var = jnp.mean(yc * yc, axis=-1, keepdims=True)
    z = yc * lax.rsqrt(var + LN_EPS) * lg_ref[...] + lb_ref[...]
    out_ref[...] = (z * jax.nn.sigmoid(z)).astype(BF16)


def _conv(u, w, cb, lg, lb, B, S, tm):
    cw = u.shape[1]
    halo = 16
    full = lambda a: pl.BlockSpec(a.shape, lambda b, j: (0,) * a.ndim)
    nj = S // tm
    return pl.pallas_call(
        functools.partial(_conv_kernel, tm=tm, halo=halo),
        grid=(B, nj),
        in_specs=[pl.BlockSpec((S, cw), lambda b, j: (b, 0)), full(w), full(cb), full(lg), full(lb)],
        out_specs=pl.BlockSpec((tm, cw), lambda b, j: (b * nj + j, 0)),
        out_shape=jax.ShapeDtypeStruct((B * S, cw), BF16),
        compiler_params=_params(("parallel", "parallel")),
        name="conv",
    )(u, w, cb, lg, lb)


def _memkv_kernel(m_ref, g_ref, wk_ref, wv_ref, k_ref, v_ref):
    m = _rms(m_ref[...], g_ref[...]).astype(BF16)
    k_ref[...] = _dot(m, wk_ref[...]).astype(BF16)
    v_ref[...] = _dot(m, wv_ref[...]).astype(BF16)


def _memkv(mem2d, g, wk, wv, M):
    R, D = mem2d.shape
    full = lambda a: pl.BlockSpec(a.shape, lambda i: (0,) * a.ndim)
    blk = pl.BlockSpec((M, D), lambda i: (i, 0))
    return pl.pallas_call(
        _memkv_kernel,
        grid=(R // M,),
        in_specs=[blk, full(g), full(wk), full(wv)],
        out_specs=[blk, blk],
        out_shape=[jax.ShapeDtypeStruct((R, D), BF16)] * 2,
        compiler_params=_params(("parallel",)),
        name="memkv",
    )(mem2d, g, wk, wv)


def _mid_kernel(x_ref, a_ref, b_ref, wout_ref, gx_ref, wq_ref, km_ref, vm_ref, wo_ref,
                gf_ref, rw_ref, rb_ref, x2_ref, h3_ref, idx_ref, gate_ref, *, n_exp, parts):
    n = x_ref.shape[0] // parts
    gens = [_mid_rows(pl.ds(p * n, n), pl.ds(p * n * SUBLANES, n * SUBLANES),
                      x_ref, a_ref, b_ref, wout_ref, gx_ref, wq_ref, km_ref, vm_ref, wo_ref,
                      gf_ref, rw_ref, rb_ref, x2_ref, h3_ref, idx_ref, gate_ref, n_exp)
            for p in range(parts)]
    for _ in zip(*gens):
        pass


def _mid_rows(rows, tiles, x_ref, a_ref, b_ref, wout_ref, gx_ref, wq_ref, km_ref, vm_ref, wo_ref,
              gf_ref, rw_ref, rb_ref, x2_ref, h3_ref, idx_ref, gate_ref, n_exp):
    ab = jnp.concatenate([a_ref[rows, :], b_ref[rows, :]], axis=1)
    x1 = x_ref[rows, :] + _dot(ab, wout_ref[...])
    yield
    q2 = _dot(_rms(x1, gx_ref[...]).astype(BF16), wq_ref[...])
    yield
    D = q2.shape[1]
    dh = D // XATTN_HEADS
    ss = []
    for h in range(XATTN_HEADS):
        sl = slice(h * dh, (h + 1) * dh)
        ss.append(_dot_nt(q2[:, sl].astype(BF16), km_ref[:, sl]) * (dh ** -0.5))
    yield
    outs = []
    for h in range(XATTN_HEADS):
        sl = slice(h * dh, (h + 1) * dh)
        p = jnp.exp(ss[h] - jnp.max(ss[h], axis=-1, keepdims=True))
        p = p / jnp.sum(p, axis=-1, keepdims=True)
        outs.append(_dot(p.astype(BF16), vm_ref[:, sl]))
    yield
    o = jnp.concatenate(outs, axis=1).astype(BF16)
    x2 = x1 + _dot(o, wo_ref[...])
    yield
    x2_ref[rows, :] = x2
    h3 = _rms(x2, gf_ref[...])
    _store_rows_as_tiles(h3_ref.at[tiles], h3)
    lane = lax.broadcasted_iota(jnp.int32, (x2.shape[0], LANES), 1)
    logits = _dot(h3.astype(BF16), rw_ref[...]) + rb_ref[...]
    yield
    cur = jnp.where(lane < n_exp, logits, -jnp.inf)
    lane_f = lane.astype(F32)
    vals, idxs = [], []
    for _ in range(TOP_K):
        m = jnp.max(cur, axis=-1, keepdims=True)
        ik = jnp.min(jnp.where(cur == m, lane_f, float(LANES)), axis=-1, keepdims=True)
        vals.append(m)
        idxs.append(ik)
        cur = jnp.where(lane_f == ik, -jnp.inf, cur)
    es = [jnp.exp(vk - vals[0]) for vk in vals]
    den = es[0]
    for ek in es[1:]:
        den = den + ek
    idx_out = jnp.zeros(lane.shape, F32)
    gate_out = jnp.zeros(lane.shape, F32)
    for kk in range(TOP_K):
        idx_out = jnp.where(lane == kk, idxs[kk], idx_out)
        gate_out = jnp.where(lane == kk, es[kk] / den, gate_out)
    idx_ref[rows, :] = idx_out.astype(jnp.int32)
    gate_ref[:, rows] = gate_out.T[0:SUBLANES, :]
    yield


def _mid(x2d, a, b, wout, gx, wq, km, vm, wo, gf, rw, rb, S, M, n_exp, tm):
    T, D = x2d.shape
    half = a.shape[1]
    per_b = S // tm
    row = lambda n: pl.BlockSpec((tm, n), lambda i: (i, 0))
    full = lambda arr: pl.BlockSpec(arr.shape, lambda i: (0,) * arr.ndim)
    memb = pl.BlockSpec((M, D), lambda i: (i // per_b, 0))
    return pl.pallas_call(
        functools.partial(_mid_kernel, n_exp=n_exp, parts=2),
        grid=(T // tm,),
        in_specs=[row(D), row(half), row(half), full(wout), full(gx), full(wq), memb, memb,
                  full(wo), full(gf), full(rw), full(rb)],
        out_specs=[row(D), pl.BlockSpec((tm * SUBLANES, LANES), lambda i: (i, 0)),
                   row(LANES), pl.BlockSpec((SUBLANES, tm), lambda i: (i // 2, i % 2))],
        out_shape=[jax.ShapeDtypeStruct((T, D), F32),
                   jax.ShapeDtypeStruct((T * SUBLANES, LANES), F32),
                   jax.ShapeDtypeStruct((T, LANES), jnp.int32),
                   jax.ShapeDtypeStruct((T // (2 * tm) * SUBLANES, 2 * tm), F32)],
        compiler_params=_params(("parallel",)),
        name="mid",
    )(x2d, a, b, wout, gx, wq, km, vm, wo, gf, rw, rb)


def _onehot(idx_ref, kk, lane):
    return (lane == idx_ref[:, kk:kk + 1]).astype(F32)


def _place_kernel(idx_ref, tri_ref, upper_ref, pos_ref, cnt_ref):
    tm = idx_ref.shape[0]
    lane = lax.broadcasted_iota(jnp.int32, (tm, LANES), 1)
    hots = [_onehot(idx_ref, kk, lane) for kk in range(TOP_K)]
    mask = hots[0]
    for hk in hots[1:]:
        mask = mask + hk
    incl = _dot(tri_ref[...], mask.astype(BF16))
    cnt = jnp.broadcast_to(incl[tm - 1:tm, :], (SUBLANES, LANES))
    hi = jnp.floor(cnt * (1.0 / 16.0))
    lo = cnt - 16.0 * hi
    start = 16.0 * _dot(hi.astype(BF16), upper_ref[...]) + _dot(lo.astype(BF16), upper_ref[...])
    before = incl - mask + start[0:1, :]
    out = jnp.zeros((tm, LANES), F32)
    for kk in range(TOP_K):
        rk = jnp.sum(hots[kk] * before, axis=-1, keepdims=True)
        out = jnp.where(lane == kk, rk, out)
    pos_ref[...] = (out.T[0:SUBLANES, :] * SUBLANES).astype(jnp.int32)
    cnt_ref[...] = cnt


def _place(idx, tm):
    T = idx.shape[0]
    assert tm <= 16 * 256
    tri = jnp.asarray(np.tril(np.ones((tm, tm), np.float32)), BF16)
    upper = jnp.asarray(np.triu(np.ones((LANES, LANES), np.float32), 1), BF16)
    row = pl.BlockSpec((tm, LANES), lambda i: (i, 0))
    return pl.pallas_call(
        _place_kernel,
        grid=(T // tm,),
        in_specs=[row, pl.BlockSpec((tm, tm), lambda i: (0, 0)),
                  pl.BlockSpec((LANES, LANES), lambda i: (0, 0))],
        out_specs=[pl.BlockSpec((SUBLANES, tm), lambda i: (i // 2, i % 2)),
                   pl.BlockSpec((SUBLANES, LANES), lambda i: (i, 0))],
        out_shape=[jax.ShapeDtypeStruct((T // (2 * tm) * SUBLANES, 2 * tm), jnp.int32),
                   jax.ShapeDtypeStruct((T // tm * SUBLANES, LANES), F32)],
        compiler_params=_params(("parallel",)),
        name="place",
    )(idx, tri, upper)


ROW_UNROLL = 8


def _tile(ref, row):
    return ref.at[pl.ds(pl.multiple_of(row * SUBLANES, SUBLANES), SUBLANES)]


def _wait_rows(buf_ref, sem):
    pltpu.make_async_copy(buf_ref, buf_ref, sem).wait()


RUN_CHUNK = 16


def _rows(ref, row, n):
    return ref.at[pl.ds(pl.multiple_of(row * SUBLANES, SUBLANES), n * SUBLANES)]


def _for_run_pieces(n, fn):
    shift = RUN_CHUNK.bit_length() - 1
    whole = lax.shift_right_logical(n, shift)

    def body(c, carry):
        fn(c * RUN_CHUNK, RUN_CHUNK)
        return carry

    lax.fori_loop(0, whole, body, 0)
    off = whole * RUN_CHUNK
    for bit in reversed(range(shift)):
        take = (n & (1 << bit)) != 0

        @pl.when(take)
        def _(off=off, bit=bit):
            fn(off, 1 << bit)

        off = off + jnp.where(take, 1 << bit, 0)


def _dispatch_kernel(pos_ref, rdst_ref, rn_ref, rloc_ref, lo_ref, hi_ref, nb_ref, h_ref, xs_ref,
                     sb0, sb1, zero_ref, sem, zsem):
    j = pl.program_id(0)
    tm = h_ref.shape[0] // SUBLANES // 2
    slots = rn_ref.shape[0] // 2
    n_exp = lo_ref.shape[0]
    blk = zero_ref.shape[0]

    def sort(half, sb):
        def body(i, carry):
            for u in range(ROW_UNROLL):
                t = half * tm + i * ROW_UNROLL + u
                tile = h_ref[pl.ds(pl.multiple_of(t * SUBLANES, SUBLANES), SUBLANES), :]
                for kk in range(TOP_K):
                    p = pos_ref[kk * 2 * tm + t]
                    sb[pl.ds(pl.multiple_of(p, SUBLANES), SUBLANES), :] = tile
            return carry

        lax.fori_loop(0, tm // ROW_UNROLL, body, 0)

    def start_runs(half, sb, s):
        def expert(e, carry):
            r = half * slots + e
            loc, dst = rloc_ref[r], rdst_ref[r]
            _for_run_pieces(rn_ref[r], lambda off, n: pltpu.make_async_copy(
                _rows(sb, loc + off, n), _rows(xs_ref, dst + off, n), s).start())
            return carry

        lax.fori_loop(0, n_exp, expert, 0)

    @pl.when(j == 0)
    def _():
        zero_ref[...] = jnp.zeros_like(zero_ref)

        def each_fill(fn):
            def expert(e, carry):
                row = lo_ref[e]
                n = hi_ref[e] - row
                for bit in reversed(range(MOE_ROWS.bit_length() - 1)):
                    size = (1 << bit) * SUBLANES
                    take = (n & (1 << bit)) != 0

                    @pl.when(take)
                    def _(row=row, size=size):
                        dst = xs_ref.at[pl.ds(pl.multiple_of(row * SUBLANES, SUBLANES), size)]
                        fn(pltpu.make_async_copy(zero_ref.at[pl.ds(0, size)], dst, zsem))

                    row = row + jnp.where(take, 1 << bit, 0)
                return carry

            lax.fori_loop(0, lo_ref.shape[0], expert, 0)

            def block(b, carry):
                fn(pltpu.make_async_copy(
                    zero_ref, xs_ref.at[pl.ds(pl.multiple_of(b * blk, blk), blk)], zsem))
                return carry

            lax.fori_loop(nb_ref[0], xs_ref.shape[0] // blk, block, 0)

        each_fill(lambda cp: cp.start())
        each_fill(lambda cp: cp.wait())

    sort(0, sb0)
    start_runs(0, sb0, sem.at[0])

    @pl.when(j > 0)
    def _():
        _wait_rows(sb1, sem.at[1])

    sort(1, sb1)
    start_runs(1, sb1, sem.at[1])
    _wait_rows(sb0, sem.at[0])

    @pl.when(j == pl.num_programs(0) - 1)
    def _():
        _wait_rows(sb1, sem.at[1])


def _dispatch(pos, run_dst, run_n, run_loc, pad_lo, pad_hi, n_used, h3, n_rows, tm):
    T = h3.shape[0] // SUBLANES
    slots = run_n.shape[0] // (T // tm)
    smem = pl.BlockSpec(memory_space=pltpu.SMEM)
    per_step = lambda n: pl.BlockSpec((n,), lambda j: (j,), memory_space=pltpu.SMEM)
    sorted_rows = pltpu.VMEM((tm * TOP_K * SUBLANES, LANES), F32)
    return pl.pallas_call(
        _dispatch_kernel,
        grid=(T // (2 * tm),),
        in_specs=[per_step(SUBLANES * 2 * tm),
                  per_step(2 * slots), per_step(2 * slots),
                  per_step(2 * slots), smem, smem, smem,
                  pl.BlockSpec((2 * tm * SUBLANES, LANES), lambda j: (j, 0))],
        out_specs=pl.BlockSpec(memory_space=pl.ANY),
        out_shape=jax.ShapeDtypeStruct((n_rows * SUBLANES, LANES), F32),
        scratch_shapes=[sorted_rows, sorted_rows, pltpu.VMEM((MOE_ROWS * SUBLANES, LANES), F32),
                        pltpu.SemaphoreType.DMA((2,)), pltpu.SemaphoreType.DMA(())],
        compiler_params=_params(("arbitrary",)),
        name="dispatch",
    )(pos, run_dst, run_n, run_loc, pad_lo, pad_hi, n_used, h3)


def _experts_kernel(be_ref, live_ref, first_ref, next_ref, slot_ref,
                    xs_ref, wu_hbm, bu_ref, wd_hbm, bd_ref, ys_ref,
                    wu_f32, wd_f32, wu_bf, wd_bf, sem):
    i = pl.program_id(0)

    def fetch(e, s):
        return (pltpu.make_async_copy(wu_hbm.at[e], wu_f32.at[s], sem.at[s]),
                pltpu.make_async_copy(wd_hbm.at[e], wd_f32.at[s], sem.at[s]))

    @pl.when(i == 0)
    def _():
        for cp in fetch(be_ref[0], 0):
            cp.start()

    @pl.when(first_ref[i] == 1)
    def _():
        s = slot_ref[i]
        for cp in fetch(be_ref[i], s):
            cp.wait()

        @pl.when(next_ref[i] >= 0)
        def _():
            for cp in fetch(next_ref[i], 1 - s):
                cp.start()

        wu_bf[...] = wu_f32[s].astype(BF16)
        wd_bf[...] = wd_f32[s].astype(BF16)

    def mlp(tiles):
        F = wd_bf.shape[0]
        x = _load_rows_from_tiles(xs_ref.at[tiles]).astype(BF16)
        gu = _dot(x, wu_bf[...]) + bu_ref[0]
        yield
        x_glu = jnp.minimum(gu[:, :F], SWIGLU_LIMIT)
        x_lin = jnp.clip(gu[:, F:], -SWIGLU_LIMIT, SWIGLU_LIMIT)
        hid = x_glu * jax.nn.sigmoid(SWIGLU_ALPHA * x_glu) * (x_lin + 1.0)
        y = _dot(hid.astype(BF16), wd_bf[...]) + bd_ref[0]
        yield
        _store_rows_as_tiles(ys_ref.at[tiles], y)
        yield

    n = xs_ref.shape[0] // MOE_PARTS
    for live in range(MOE_PARTS + 1):
        @pl.when(live_ref[i] == live)
        def _(live=live):
            for _ in zip(*[mlp(pl.ds(p * n, n)) for p in range(live)]):
                pass
            if live < MOE_PARTS:
                rest = (MOE_PARTS - live) * n
                ys_ref[pl.ds(live * n, rest), :] = jnp.zeros((rest, ys_ref.shape[1]), F32)


def _expert_runs(block_e):
    nb = block_e.shape[0]
    ar = jnp.arange(nb, dtype=jnp.int32)
    first = jnp.concatenate([jnp.ones((1,), bool), block_e[1:] != block_e[:-1]])
    starts = jnp.where(first, ar, nb)
    later = lax.cummin(jnp.concatenate([starts[1:], jnp.full((1,), nb, jnp.int32)]), reverse=True)
    nxt = jnp.where(later < nb, block_e[jnp.minimum(later, nb - 1)], -1)
    slot = (jnp.cumsum(first.astype(jnp.int32)) - 1) & 1
    return first.astype(jnp.int32), nxt.astype(jnp.int32), slot.astype(jnp.int32)


def _experts(block_e, live, xs, wu, bu, wd, bd):
    E, D, F2 = wu.shape
    F = wd.shape[1]
    nb = xs.shape[0] // (MOE_ROWS * SUBLANES)
    rows = pl.BlockSpec((MOE_ROWS * SUBLANES, LANES), lambda i, *_: (i, 0))
    grid_spec = pltpu.PrefetchScalarGridSpec(
        num_scalar_prefetch=5,
        grid=(nb,),
        in_specs=[rows,
                  pl.BlockSpec(memory_space=pl.ANY),
                  pl.BlockSpec((1, 1, F2), lambda i, be, *_: (be[i], 0, 0)),
                  pl.BlockSpec(memory_space=pl.ANY),
                  pl.BlockSpec((1, 1, D), lambda i, be, *_: (be[i], 0, 0))],
        out_specs=rows,
        scratch_shapes=[pltpu.VMEM((2, D, F2), F32), pltpu.VMEM((2, F, D), F32),
                        pltpu.VMEM((D, F2), BF16), pltpu.VMEM((F, D), BF16),
                        pltpu.SemaphoreType.DMA((2,))],
    )
    return pl.pallas_call(
        _experts_kernel,
        grid_spec=grid_spec,
        out_shape=jax.ShapeDtypeStruct(xs.shape, F32),
        compiler_params=_params(("arbitrary",)),
        name="experts",
    )(block_e, live, *_expert_runs(block_e), xs, wu, bu, wd, bd)


def _combine_kernel(pos_ref, gate_ref, src_ref, rn_ref, rloc_ref, nsrc_ref, nrn_ref, nrloc_ref,
                    x2_ref, g_ref, ys_ref, out_ref, yb0, yb1, acc_ref, sem, *, n_exp, apply_norm):
    j = pl.program_id(0)
    tm = x2_ref.shape[0] // 2
    slots = rn_ref.shape[0] // 2

    def start(tbls, half, yb, s):
        src_t, n_t, loc_t = tbls

        def expert(e, carry):
            r = half * slots + e
            loc, src = loc_t[r], src_t[r]
            _for_run_pieces(n_t[r], lambda off, n: pltpu.make_async_copy(
                _rows(ys_ref, src + off, n), _rows(yb, loc + off, n), s).start())
            return carry

        lax.fori_loop(0, n_exp, expert, 0)

    def finish(half, yb, s):
        _wait_rows(yb, s)

        def body(i, carry):
            for u in range(ROW_UNROLL):
                t = i * ROW_UNROLL + u
                a = half * tm + t
                acc = None
                for kk in range(TOP_K):
                    p = pos_ref[kk * 2 * tm + a]
                    term = yb[pl.ds(pl.multiple_of(p, SUBLANES), SUBLANES), :] \
                        * gate_ref[kk * 2 * tm + a]
                    acc = term if acc is None else acc + term
                acc_ref[pl.ds(pl.multiple_of(t * SUBLANES, SUBLANES), SUBLANES), :] = acc
            return carry

        lax.fori_loop(0, tm // ROW_UNROLL, body, 0)
        rows = pl.ds(half * tm, tm)
        x = x2_ref[rows, :] + _load_rows_from_tiles(acc_ref)
        out_ref[rows, :] = _rms(x, g_ref[...]) if apply_norm else x

    cur = (src_ref, rn_ref, rloc_ref)

    @pl.when(j == 0)
    def _():
        start(cur, 0, yb0, sem.at[0])

    start(cur, 1, yb1, sem.at[1])
    finish(0, yb0, sem.at[0])

    @pl.when(j < pl.num_programs(0) - 1)
    def _():
        start((nsrc_ref, nrn_ref, nrloc_ref), 0, yb0, sem.at[0])

    finish(1, yb1, sem.at[1])


def _combine(pos, gates, run_src, run_n, run_loc, x2, g, ys, tm, n_exp, apply_norm):
    T, D = x2.shape
    steps = T // (2 * tm)
    slots = run_n.shape[0] // (T // tm)
    per_step = lambda n: pl.BlockSpec((n,), lambda j: (j,), memory_space=pltpu.SMEM)
    next_step = lambda n: pl.BlockSpec((n,), lambda j: (jnp.minimum(j + 1, steps - 1),),
                                       memory_space=pltpu.SMEM)
    sorted_rows = pltpu.VMEM((tm * TOP_K * SUBLANES, LANES), F32)
    return pl.pallas_call(
        functools.partial(_combine_kernel, n_exp=n_exp, apply_norm=apply_norm),
        grid=(steps,),
        in_specs=[per_step(SUBLANES * 2 * tm),
                  per_step(SUBLANES * 2 * tm),
                  per_step(2 * slots), per_step(2 * slots), per_step(2 * slots),
                  next_step(2 * slots), next_step(2 * slots), next_step(2 * slots),
                  pl.BlockSpec((2 * tm, D), lambda j: (j, 0)),
                  pl.BlockSpec(g.shape, lambda j: (0, 0)),
                  pl.BlockSpec(memory_space=pl.ANY)],
        out_specs=pl.BlockSpec((2 * tm, D), lambda j: (j, 0)),
        out_shape=jax.ShapeDtypeStruct((T, D), F32),
        scratch_shapes=[sorted_rows, sorted_rows, pltpu.VMEM((tm * SUBLANES, LANES), F32),
                        pltpu.SemaphoreType.DMA((2,))],
        compiler_params=_params(("arbitrary",)),
        name="combine",
    )(pos, gates, run_src, run_n, run_loc, run_src, run_n, run_loc, x2, g, ys)


def _row(v):
    return v.reshape(1, -1).astype(F32)


def kernel(x, mem, norm_mix_g, w_in, gate_w2_fwd, gate_b_fwd, gate_w2_bwd, gate_b_bwd,
           gla_norm_g, conv_w, conv_b, conv_ln_g, conv_ln_b, w_out, norm_xattn_g,
           norm_mem_g, xattn_wq, xattn_wk, xattn_wv, xattn_wo, norm_ffn_g, router_w,
           router_b, exp_w_up, exp_b_up, exp_w_down, exp_b_down, final_norm_g):
    B, S, D = x.shape
    M = mem.shape[1]
    T = B * S
    depth = w_in.shape[0]
    kw = GLA_HEADS * GLA_DK
    vw = GLA_HEADS * GLA_DV
    rank_w = gate_w2_fwd.shape[1]
    cw = conv_w.shape[2]
    E = router_w.shape[2]
    assert vw + cw == w_out.shape[1] and 2 * rank_w <= LANES and E <= LANES

    xc = x.reshape(T, D)
    for l in range(depth):
        o_gf = 2 * kw + 2 * vw
        o_glu = o_gf + 2 * rank_w
        wl = w_in[l]
        w_perm = jnp.concatenate(
            [wl[:, :o_gf], wl[:, o_glu:], wl[:, o_gf:o_glu],
             jnp.zeros((D, LANES - 2 * rank_w), wl.dtype)], axis=1).astype(BF16)
        w2 = jnp.zeros((LANES, 2 * kw), F32)
        w2 = w2.at[:rank_w, :kw].set(gate_w2_fwd[l]).at[rank_w:2 * rank_w, kw:].set(gate_w2_bwd[l])
        b2 = jnp.concatenate([gate_b_fwd[l], gate_b_bwd[l]]).reshape(1, -1)

        q, k, v, gr, la, u = _inproj(xc, _row(norm_mix_g[l]), w_perm, w2.astype(BF16), b2,
                                     kw, vw, cw, tm=512)
        a_out = _gla(q, k, v, la, gr, _row(gla_norm_g[l]), B, S)
        b_out = _conv(u, conv_w[l], _row(conv_b[l]), _row(conv_ln_g[l]), _row(conv_ln_b[l]),
                      B, S, tm=512)
        km, vm = _memkv(mem.reshape(B * M, D), _row(norm_mem_g[l]),
                        xattn_wk[l].astype(BF16), xattn_wv[l].astype(BF16), M)

        rw = jnp.zeros((D, LANES), F32).at[:, :E].set(router_w[l]).astype(BF16)
        rb = jnp.zeros((1, LANES), F32).at[0, :E].set(router_b[l])
        x2, h3, idx, gates = _mid(
            xc, a_out, b_out, w_out[l].astype(BF16), _row(norm_xattn_g[l]),
            xattn_wq[l].astype(BF16), km, vm, xattn_wo[l].astype(BF16),
            _row(norm_ffn_g[l]), rw, rb, S, M, E, tm=MOE_TILE)

        tm = MOE_TILE
        pos, tile_cnt = _place(idx, tm=tm)
        n_te = tile_cnt.reshape(T // tm, SUBLANES, LANES)[:, 0, :E].astype(jnp.int32)
        counts = jnp.sum(n_te, axis=0)
        padded = ((counts + MOE_ROWS - 1) // MOE_ROWS) * MOE_ROWS
        pends = jnp.cumsum(padded)
        run_row = (pends - padded)[None, :] + jnp.cumsum(n_te, axis=0) - n_te
        run_loc = jnp.cumsum(n_te, axis=1) - n_te
        slots = max(E, LANES // 2)
        runs = [jnp.pad(a, ((0, 0), (0, slots - E))).reshape(-1).astype(jnp.int32)
                for a in (run_row, n_te, run_loc)]
        n_blocks = (T * TOP_K + E * (MOE_ROWS - 1)) // MOE_ROWS
        blk_start = jnp.arange(n_blocks, dtype=jnp.int32) * MOE_ROWS
        block_e = jnp.minimum(jnp.sum(pends[None, :] <= blk_start[:, None], axis=1), E - 1)
        n_used = (pends[E - 1] // MOE_ROWS).reshape(1)

        pos, gates = pos.reshape(-1), gates.reshape(-1)
        xs = _dispatch(pos, *runs, (pends - padded + counts).astype(jnp.int32),
                       pends.astype(jnp.int32), n_used.astype(jnp.int32), h3,
                       n_blocks * MOE_ROWS, tm=tm)
        group = MOE_ROWS // MOE_PARTS
        real = jnp.clip((pends - padded + counts)[block_e] - blk_start, 0, MOE_ROWS)
        live = jnp.where(blk_start < pends[E - 1], (real + group - 1) // group, 0)
        ys = _experts(block_e.astype(jnp.int32), live.astype(jnp.int32), xs,
                      exp_w_up[l], exp_b_up[l][:, None, :],
                      exp_w_down[l], exp_b_down[l][:, None, :])
        xc = _combine(pos, gates, *runs, x2,
                      _row(final_norm_g), ys, tm=tm, n_exp=E, apply_norm=(l == depth - 1))
    return xc.reshape(B, S, D)
```

```python
import functools

import numpy as np
import jax
import jax.numpy as jnp
from jax import lax
from jax.experimental import pallas as pl
from jax.experimental.pallas import tpu as pltpu

F32 = jnp.float32
BF16 = jnp.bfloat16

GLA_HEADS = 4
GLA_DK = 64
GLA_DV = 128
GLA_CHUNK = 64
GLA_TILE = 4
GLA_UNROLL = 4
GATE_TAU = 16.0
CONV_KERNEL = 31
XATTN_HEADS = 4
TOP_K = 4
SWIGLU_ALPHA = 1.702
SWIGLU_LIMIT = 7.0
RMS_EPS = 1e-6
LN_EPS = 1e-5

LANES = 128
SUBLANES = 8
MOE_ROWS = 512
MOE_PARTS = 2
MOE_TILE = 512
VMEM_LIMIT = 56 * 1024 * 1024


def _dot(a, b):
    return jnp.dot(a, b, preferred_element_type=F32)


def _dot_nt(a, b):
    return lax.dot_general(a, b, (((1,), (1,)), ((), ())), preferred_element_type=F32)


def _dot_tn(a, b):
    return lax.dot_general(a, b, (((0,), (0,)), ((), ())), preferred_element_type=F32)


def _rms(x, g):
    return x * lax.rsqrt(jnp.mean(x * x, axis=-1, keepdims=True) + RMS_EPS) * g


def _params(sem):
    return pltpu.CompilerParams(dimension_semantics=sem, vmem_limit_bytes=VMEM_LIMIT)


def _store_rows_as_tiles(ref, val):
    n, d = val.shape
    assert d == SUBLANES * LANES and ref.shape == (n * SUBLANES, LANES)
    for s in range(SUBLANES):
        ref[pl.ds(s, n, stride=SUBLANES), :] = val[:, s * LANES:(s + 1) * LANES]


def _load_rows_from_tiles(ref):
    n = ref.shape[0] // SUBLANES
    return jnp.concatenate([ref[pl.ds(s, n, stride=SUBLANES), :] for s in range(SUBLANES)], axis=1)


def _inproj_kernel(x_ref, g_ref, w_ref, w2_ref, b2_ref,
                   q_ref, k_ref, v_ref, gr_ref, la_ref, u_ref, *, kw, vw, cw, parts):
    n = x_ref.shape[0] // parts
    gens = [_inproj_rows(pl.ds(p * n, n), x_ref, g_ref, w_ref, w2_ref, b2_ref,
                         q_ref, k_ref, v_ref, gr_ref, la_ref, u_ref, kw, vw, cw)
            for p in range(parts)]
    for _ in zip(*gens):
        pass


def _inproj_rows(rows, x_ref, g_ref, w_ref, w2_ref, b2_ref,
                 q_ref, k_ref, v_ref, gr_ref, la_ref, u_ref, kw, vw, cw):
    h = _rms(x_ref[rows, :], g_ref[...]).astype(BF16)
    acc = _dot(h, w_ref[...])
    yield
    o = 0
    q_ref[rows, :] = (acc[:, o:o + kw] * (GLA_DK ** -0.5)).astype(BF16); o += kw
    k_ref[rows, :] = acc[:, o:o + kw].astype(BF16); o += kw
    v_ref[rows, :] = acc[:, o:o + vw].astype(BF16); o += vw
    r = acc[:, o:o + vw]; o += vw
    gr_ref[rows, :] = (r * jax.nn.sigmoid(r)).astype(BF16)
    a = acc[:, o:o + cw]; o += cw
    g = acc[:, o:o + cw]; o += cw
    u_ref[rows, :] = a * jax.nn.sigmoid(g)
    z = _dot(acc[:, o:o + LANES].astype(BF16), w2_ref[...]) + b2_ref[...]
    yield
    la_ref[rows, :] = (jnp.minimum(z, 0.0) - jnp.log1p(jnp.exp(-jnp.abs(z)))) * (1.0 / GATE_TAU)
    yield


def _inproj(x2d, g, w, w2, b2, kw, vw, cw, tm):
    T, D = x2d.shape
    W = w.shape[1]
    row = lambda n: pl.BlockSpec((tm, n), lambda i: (i, 0))
    full = lambda a: pl.BlockSpec(a.shape, lambda i: (0,) * a.ndim)
    return pl.pallas_call(
        functools.partial(_inproj_kernel, kw=kw, vw=vw, cw=cw, parts=2),
        grid=(T // tm,),
        in_specs=[row(D), full(g), full(w), full(w2), full(b2)],
        out_specs=[row(kw), row(kw), row(vw), row(vw), row(2 * kw), row(cw)],
        out_shape=[jax.ShapeDtypeStruct((T, kw), BF16), jax.ShapeDtypeStruct((T, kw), BF16),
                   jax.ShapeDtypeStruct((T, vw), BF16), jax.ShapeDtypeStruct((T, vw), BF16),
                   jax.ShapeDtypeStruct((T, 2 * kw), F32), jax.ShapeDtypeStruct((T, cw), F32)],
        compiler_params=_params(("parallel",)),
        name="inproj",
    )(x2d, g, w, w2, b2)


def _gla_levels():
    ms, m = [], GLA_CHUNK // 2
    while m >= GLA_TILE:
        ms.append(m)
        m //= 2
    return ms


def _gla_constants(bwd):
    C, H = GLA_CHUNK, GLA_HEADS
    t = np.arange(C)[:, None]
    s = np.arange(C)[None, :]
    lev_q, lev_mask = [], []
    for m in _gla_levels():
        g = 2 * m
        second = (t % g) >= m
        same = t // g == s // g
        if not bwd:
            qrow = second
            mask = same & ((t % g) >= m) & ((s % g) < m)
        else:
            qrow = ~second
            mask = same & ((t % g) < m) & ((s % g) >= m)
        lev_q.append(np.broadcast_to(qrow, (C, H * GLA_DK)))
        lev_mask.append(np.tile(mask, (1, H)))
    shifts = range(0, GLA_TILE) if not bwd else range(1, GLA_TILE)
    sh_mask = []
    for sh in shifts:
        j = t - sh if not bwd else t + sh
        sh_mask.append(np.tile((s == j) & (t // GLA_TILE == s // GLA_TILE), (1, H)))
    f = lambda xs: np.stack(xs).astype(np.float32)
    return f(lev_q), f(lev_mask), f(sh_mask)


def _gla_head_mask():
    C, H = GLA_CHUNK, GLA_HEADS
    assert GLA_DK == C
    blk = np.arange(H * C) // C
    return (blk[:, None] == blk[None, :]).astype(np.float32)


def _gla_chunk(la, q, k, v, st_ref, c_ref, tri_ref, lq_ref, lm_ref, sm_ref, hm_ref, bwd):
    C, H = GLA_CHUNK, GLA_HEADS
    kw = q.shape[1]
    hm = hm_ref[...]

    def stack(xb):
        return jnp.concatenate([xb] * H, axis=0) * hm

    tri = tri_ref[...]
    hi = la.astype(BF16)
    incl = _dot(tri, hi) + _dot(tri, (la - hi.astype(F32)).astype(BF16))
    yield None
    tot = incl[C - 1:C]
    cum = incl if not bwd else incl - la
    x_q, x_k = (cum, tot - cum) if not bwd else (tot - cum, cum)
    e_q, e_k, a_tot = jnp.exp(x_q), jnp.exp(x_k), jnp.exp(tot)
    c_ref[...] = cum

    a = jnp.zeros((C, H * C), F32)
    for lv, m in enumerate(_gla_levels()):
        g = 2 * m
        edge = m - 1 if not bwd else m
        r = jnp.concatenate([jnp.broadcast_to(c_ref[s0 + edge:s0 + edge + 1, :], (g, kw))
                             for s0 in range(0, C, g)], axis=0)
        d = cum - r
        z = (jnp.where(lq_ref[lv] > 0, q, k) * jnp.exp(jnp.minimum(d, -d))).astype(BF16)
        a = a + _dot_nt(z, stack(z)) * lm_ref[lv]
    ps = []
    for si in range(sm_ref.shape[0]):
        sh = si if not bwd else si + 1
        if sh == 0:
            ps.append((q * k).astype(BF16))
            continue
        amt = sh if not bwd else C - sh
        kr = pltpu.roll(k, amt, 0)
        cr = pltpu.roll(cum, amt, 0)
        d = (cum - cr) if not bwd else (cr - cum)
        ps.append((q * kr * jnp.exp(jnp.minimum(d, 0.0))).astype(BF16))
    sc = _dot(jnp.concatenate(ps, axis=0), hm)
    yield None
    for si in range(sm_ref.shape[0]):
        a = a + sc[si * C:(si + 1) * C] * sm_ref[si]

    vst = jnp.concatenate([v[:, h * GLA_DV:(h + 1) * GLA_DV] for h in range(H)], axis=0)
    st = st_ref[...]
    o4 = _dot(stack(a.astype(BF16)), vst) + _dot_nt(stack((q * e_q).astype(BF16)), st.astype(BF16))
    st_ref[...] = st * a_tot + _dot_tn(vst, stack((k * e_k).astype(BF16)))
    yield jnp.concatenate([o4[h * C:(h + 1) * C] for h in range(H)], axis=1)


def _gla_kernel(q_ref, k_ref, v_ref, la_ref, gr_ref, ng_ref,
                lqf_ref, lmf_ref, smf_ref, lqb_ref, lmb_ref, smb_ref, tri_ref, hm_ref,
                out_ref, of_ref, ob_ref, stf_ref, stb_ref, cf_ref, cb_ref):
    C = GLA_CHUNK
    S = q_ref.shape[0]
    kw = q_ref.shape[1]
    n = S // C
    stf_ref[...] = jnp.zeros_like(stf_ref)
    stb_ref[...] = jnp.zeros_like(stb_ref)

    def body(i, carry):
        streams, outs = [], []
        for u in range(GLA_UNROLL):
            rf = pl.ds(pl.multiple_of((i * GLA_UNROLL + u) * C, C), C)
            rb = pl.ds(pl.multiple_of((n - 1 - i * GLA_UNROLL - u) * C, C), C)
            streams.append(_gla_chunk(
                la_ref[rf, 0:kw], q_ref[rf, :].astype(F32), k_ref[rf, :].astype(F32),
                v_ref[rf, :], stf_ref, cf_ref.at[u], tri_ref, lqf_ref, lmf_ref, smf_ref, hm_ref,
                False))
            streams.append(_gla_chunk(
                la_ref[rb, kw:2 * kw], q_ref[rb, :].astype(F32), k_ref[rb, :].astype(F32),
                v_ref[rb, :], stb_ref, cb_ref.at[u], tri_ref, lqb_ref, lmb_ref, smb_ref, hm_ref,
                True))
            outs += [(of_ref, rf), (ob_ref, rb)]
        for results in zip(*streams):
            pass
        for (ref, rows), o in zip(outs, results):
            ref[rows, :] = o
        return carry

    lax.fori_loop(0, n // GLA_UNROLL, body, 0)

    rows = 256
    def fin(i, carry):
        r = pl.ds(pl.multiple_of(i * rows, rows), rows)
        o = of_ref[r, :] + ob_ref[r, :]
        ys = []
        for h in range(GLA_HEADS):
            oh = o[:, h * GLA_DV:(h + 1) * GLA_DV]
            ys.append(_rms(oh, ng_ref[...]))
        y = jnp.concatenate(ys, axis=1) * gr_ref[r, :].astype(F32)
        out_ref[r, :] = y.astype(BF16)
        return carry

    lax.fori_loop(0, S // rows, fin, 0)


def _gla(q, k, v, la, gr, ng, B, S):
    kw, vw = q.shape[1], v.shape[1]
    cf = _gla_constants(False)
    cb = _gla_constants(True)
    tri = np.tril(np.ones((GLA_CHUNK, GLA_CHUNK), np.float32))
    consts = [jnp.asarray(c) for c in cf] + [jnp.asarray(c) for c in cb] + \
             [jnp.asarray(tri, BF16), jnp.asarray(_gla_head_mask(), BF16)]
    seq = lambda n: pl.BlockSpec((S, n), lambda b: (b, 0))
    full = lambda a: pl.BlockSpec(a.shape, lambda b: (0,) * a.ndim)
    return pl.pallas_call(
        _gla_kernel,
        grid=(B,),
        in_specs=[seq(kw), seq(kw), seq(vw), seq(2 * kw), seq(vw), full(ng)] + [full(c) for c in consts],
        out_specs=seq(vw),
        out_shape=jax.ShapeDtypeStruct((B * S, vw), BF16),
        scratch_shapes=[pltpu.VMEM((S, vw), F32), pltpu.VMEM((S, vw), F32),
                        pltpu.VMEM((GLA_DV, kw), F32), pltpu.VMEM((GLA_DV, kw), F32),
                        pltpu.VMEM((GLA_UNROLL, GLA_CHUNK, kw), F32),
                        pltpu.VMEM((GLA_UNROLL, GLA_CHUNK, kw), F32)],
        compiler_params=_params(("parallel",)),
        name="gla",
    )(q, k, v, la, gr, ng, *consts)


def _conv_kernel(u_ref, w_ref, cb_ref, lg_ref, lb_ref, out_ref, *, tm, halo):
    S = u_ref.shape[0]
    j = pl.program_id(1)
    t0 = pl.multiple_of(j * tm, tm)
    top = u_ref[pl.ds(pl.multiple_of(jnp.maximum(t0 - halo, 0), halo), halo), :]
    bot = u_ref[pl.ds(pl.multiple_of(jnp.minimum(t0 + tm, S - halo), halo), halo), :]
    top = jnp.where(j > 0, top, 0.0)
    bot = jnp.where(j < pl.num_programs(1) - 1, bot, 0.0)
    win = jnp.concatenate([top, u_ref[pl.ds(t0, tm), :], bot], axis=0)
    rows = tm + 2 * halo
    pad = (CONV_KERNEL - 1) // 2
    w = w_ref[...]
    acc = jnp.zeros((tm, u_ref.shape[1]), F32)
    for r in range(8):
        offs = [k for k in range(CONV_KERNEL) if (halo - pad + k) % 8 == r]
        if not offs:
            continue
        xr = win if r == 0 else pltpu.roll(win, rows - r, 0)
        for k in offs:
            a0 = (halo - pad + k) - r
            acc = acc + xr[a0:a0 + tm, :] * w[k:k + 1, :]
    y = acc + cb_ref[...]
    mu = jnp.mean(y, axis=-1, keepdims=True)
    yc = y - mu
    var = jnp.mean(yc * yc, axis=-1, keepdims=True)
    z = yc * lax.rsqrt(var + LN_EPS) * lg_ref[...] + lb_ref[...]
    out_ref[...] = (z * jax.nn.sigmoid(z)).astype(BF16)


def _conv(u, w, cb, lg, lb, B, S, tm):
    cw = u.shape[1]
    halo = 16
    full = lambda a: pl.BlockSpec(a.shape, lambda b, j: (0,) * a.ndim)
    nj = S // tm
    return pl.pallas_call(
        functools.partial(_conv_kernel, tm=tm, halo=halo),
        grid=(B, nj),
        in_specs=[pl.BlockSpec((S, cw), lambda b, j: (b, 0)), full(w), full(cb), full(lg), full(lb)],
        out_specs=pl.BlockSpec((tm, cw), lambda b, j: (b * nj + j, 0)),
        out_shape=jax.ShapeDtypeStruct((B * S, cw), BF16),
        compiler_params=_params(("parallel", "parallel")),
        name="conv",
    )(u, w, cb, lg, lb)


def _memkv_kernel(m_ref, g_ref, wk_ref, wv_ref, k_ref, v_ref):
    m = _rms(m_ref[...], g_ref[...]).astype(BF16)
    k_ref[...] = _dot(m, wk_ref[...]).astype(BF16)
    v_ref[...] = _dot(m, wv_ref[...]).astype(BF16)


def _memkv(mem2d, g, wk, wv, M):
    R, D = mem2d.shape
    full = lambda a: pl.BlockSpec(a.shape, lambda i: (0,) * a.ndim)
    blk = pl.BlockSpec((M, D), lambda i: (i, 0))
    return pl.pallas_call(
        _memkv_kernel,
        grid=(R // M,),
        in_specs=[blk, full(g), full(wk), full(wv)],
        out_specs=[blk, blk],
        out_shape=[jax.ShapeDtypeStruct((R, D), BF16)] * 2,
        compiler_params=_params(("parallel",)),
        name="memkv",
    )(mem2d, g, wk, wv)


def _mid_kernel(x_ref, a_ref, b_ref, wout_ref, gx_ref, wq_ref, km_ref, vm_ref, wo_ref,
                gf_ref, rw_ref, rb_ref, x2_ref, h3_ref, idx_ref, gate_ref, *, n_exp, parts):
    n = x_ref.shape[0] // parts
    gens = [_mid_rows(pl.ds(p * n, n), pl.ds(p * n * SUBLANES, n * SUBLANES),
                      x_ref, a_ref, b_ref, wout_ref, gx_ref, wq_ref, km_ref, vm_ref, wo_ref,
                      gf_ref, rw_ref, rb_ref, x2_ref, h3_ref, idx_ref, gate_ref, n_exp)
            for p in range(parts)]
    for _ in zip(*gens):
        pass


def _mid_rows(rows, tiles, x_ref, a_ref, b_ref, wout_ref, gx_ref, wq_ref, km_ref, vm_ref, wo_ref,
              gf_ref, rw_ref, rb_ref, x2_ref, h3_ref, idx_ref, gate_ref, n_exp):
    ab = jnp.concatenate([a_ref[rows, :], b_ref[rows, :]], axis=1)
    x1 = x_ref[rows, :] + _dot(ab, wout_ref[...])
    yield
    q2 = _dot(_rms(x1, gx_ref[...]).astype(BF16), wq_ref[...])
    yield
    D = q2.shape[1]
    dh = D // XATTN_HEADS
    ss = []
    for h in range(XATTN_HEADS):
        sl = slice(h * dh, (h + 1) * dh)
        ss.append(_dot_nt(q2[:, sl].astype(BF16), km_ref[:, sl]) * (dh ** -0.5))
    yield
    outs = []
    for h in range(XATTN_HEADS):
        sl = slice(h * dh, (h + 1) * dh)
        p = jnp.exp(ss[h] - jnp.max(ss[h], axis=-1, keepdims=True))
        p = p / jnp.sum(p, axis=-1, keepdims=True)
        outs.append(_dot(p.astype(BF16), vm_ref[:, sl]))
    yield
    o = jnp.concatenate(outs, axis=1).astype(BF16)
    x2 = x1 + _dot(o, wo_ref[...])
    yield
    x2_ref[rows, :] = x2
    h3 = _rms(x2, gf_ref[...])
    _store_rows_as_tiles(h3_ref.at[tiles], h3)
    lane = lax.broadcasted_iota(jnp.int32, (x2.shape[0], LANES), 1)
    logits = _dot(h3.astype(BF16), rw_ref[...]) + rb_ref[...]
    yield
    cur = jnp.where(lane < n_exp, logits, -jnp.inf)
    lane_f = lane.astype(F32)
    vals, idxs = [], []
    for _ in range(TOP_K):
        m = jnp.max(cur, axis=-1, keepdims=True)
        ik = jnp.min(jnp.where(cur == m, lane_f, float(LANES)), axis=-1, keepdims=True)
        vals.append(m)
        idxs.append(ik)
        cur = jnp.where(lane_f == ik, -jnp.inf, cur)
    es = [jnp.exp(vk - vals[0]) for vk in vals]
    den = es[0]
    for ek in es[1:]:
        den = den + ek
    idx_out = jnp.zeros(lane.shape, F32)
    gate_out = jnp.zeros(lane.shape, F32)
    for kk in range(TOP_K):
        idx_out = jnp.where(lane == kk, idxs[kk], idx_out)
        gate_out = jnp.where(lane == kk, es[kk] / den, gate_out)
    idx_ref[rows, :] = idx_out.astype(jnp.int32)
    gate_ref[:, rows] = gate_out.T[0:SUBLANES, :]
    yield


def _mid(x2d, a, b, wout, gx, wq, km, vm, wo, gf, rw, rb, S, M, n_exp, tm):
    T, D = x2d.shape
    half = a.shape[1]
    per_b = S // tm
    row = lambda n: pl.BlockSpec((tm, n), lambda i: (i, 0))
    full = lambda arr: pl.BlockSpec(arr.shape, lambda i: (0,) * arr.ndim)
    memb = pl.BlockSpec((M, D), lambda i: (i // per_b, 0))
    return pl.pallas_call(
        functools.partial(_mid_kernel, n_exp=n_exp, parts=2),
        grid=(T // tm,),
        in_specs=[row(D), row(half), row(half), full(wout), full(gx), full(wq), memb, memb,
                  full(wo), full(gf), full(rw), full(rb)],
        out_specs=[row(D), pl.BlockSpec((tm * SUBLANES, LANES), lambda i: (i, 0)),
                   row(LANES), pl.BlockSpec((SUBLANES, tm), lambda i: (i // 2, i % 2))],
        out_shape=[jax.ShapeDtypeStruct((T, D), F32),
                   jax.ShapeDtypeStruct((T * SUBLANES, LANES), F32),
                   jax.ShapeDtypeStruct((T, LANES), jnp.int32),
                   jax.ShapeDtypeStruct((T // (2 * tm) * SUBLANES, 2 * tm), F32)],
        compiler_params=_params(("parallel",)),
        name="mid",
    )(x2d, a, b, wout, gx, wq, km, vm, wo, gf, rw, rb)


def _onehot(idx_ref, kk, lane):
    return (lane == idx_ref[:, kk:kk + 1]).astype(F32)


def _place_kernel(idx_ref, tri_ref, upper_ref, pos_ref, cnt_ref):
    tm = idx_ref.shape[0]
    lane = lax.broadcasted_iota(jnp.int32, (tm, LANES), 1)
    hots = [_onehot(idx_ref, kk, lane) for kk in range(TOP_K)]
    mask = hots[0]
    for hk in hots[1:]:
        mask = mask + hk
    incl = _dot(tri_ref[...], mask.astype(BF16))
    cnt = jnp.broadcast_to(incl[tm - 1:tm, :], (SUBLANES, LANES))
    hi = jnp.floor(cnt * (1.0 / 16.0))
    lo = cnt - 16.0 * hi
    start = 16.0 * _dot(hi.astype(BF16), upper_ref[...]) + _dot(lo.astype(BF16), upper_ref[...])
    before = incl - mask + start[0:1, :]
    out = jnp.zeros((tm, LANES), F32)
    for kk in range(TOP_K):
        rk = jnp.sum(hots[kk] * before, axis=-1, keepdims=True)
        out = jnp.where(lane == kk, rk, out)
    pos_ref[...] = (out.T[0:SUBLANES, :] * SUBLANES).astype(jnp.int32)
    cnt_ref[...] = cnt


def _place(idx, tm):
    T = idx.shape[0]
    assert tm <= 16 * 256
    tri = jnp.asarray(np.tril(np.ones((tm, tm), np.float32)), BF16)
    upper = jnp.asarray(np.triu(np.ones((LANES, LANES), np.float32), 1), BF16)
    row = pl.BlockSpec((tm, LANES), lambda i: (i, 0))
    return pl.pallas_call(
        _place_kernel,
        grid=(T // tm,),
        in_specs=[row, pl.BlockSpec((tm, tm), lambda i: (0, 0)),
                  pl.BlockSpec((LANES, LANES), lambda i: (0, 0))],
        out_specs=[pl.BlockSpec((SUBLANES, tm), lambda i: (i // 2, i % 2)),
                   pl.BlockSpec((SUBLANES, LANES), lambda i: (i, 0))],
        out_shape=[jax.ShapeDtypeStruct((T // (2 * tm) * SUBLANES, 2 * tm), jnp.int32),
                   jax.ShapeDtypeStruct((T // tm * SUBLANES, LANES), F32)],
        compiler_params=_params(("parallel",)),
        name="place",
    )(idx, tri, upper)


ROW_UNROLL = 8


def _tile(ref, row):
    return ref.at[pl.ds(pl.multiple_of(row * SUBLANES, SUBLANES), SUBLANES)]


def _wait_rows(buf_ref, sem):
    pltpu.make_async_copy(buf_ref, buf_ref, sem).wait()


RUN_CHUNK = 16


def _rows(ref, row, n):
    return ref.at[pl.ds(pl.multiple_of(row * SUBLANES, SUBLANES), n * SUBLANES)]


def _for_run_pieces(n, fn):
    shift = RUN_CHUNK.bit_length() - 1
    whole = lax.shift_right_logical(n, shift)

    def body(c, carry):
        fn(c * RUN_CHUNK, RUN_CHUNK)
        return carry

    lax.fori_loop(0, whole, body, 0)
    off = whole * RUN_CHUNK
    for bit in reversed(range(shift)):
        take = (n & (1 << bit)) != 0

        @pl.when(take)
        def _(off=off, bit=bit):
            fn(off, 1 << bit)

        off = off + jnp.where(take, 1 << bit, 0)


def _dispatch_kernel(pos_ref, rdst_ref, rn_ref, rloc_ref, lo_ref, hi_ref, nb_ref, h_ref, xs_ref,
                     sb0, sb1, zero_ref, sem, zsem):
    j = pl.program_id(0)
    tm = h_ref.shape[0] // SUBLANES // 2
    slots = rn_ref.shape[0] // 2
    n_exp = lo_ref.shape[0]
    blk = zero_ref.shape[0]

    def sort(half, sb):
        def body(i, carry):
            for u in range(ROW_UNROLL):
                t = half * tm + i * ROW_UNROLL + u
                tile = h_ref[pl.ds(pl.multiple_of(t * SUBLANES, SUBLANES), SUBLANES), :]
                for kk in range(TOP_K):
                    p = pos_ref[kk * 2 * tm + t]
                    sb[pl.ds(pl.multiple_of(p, SUBLANES), SUBLANES), :] = tile
            return carry

        lax.fori_loop(0, tm // ROW_UNROLL, body, 0)

    def start_runs(half, sb, s):
        def expert(e, carry):
            r = half * slots + e
            loc, dst = rloc_ref[r], rdst_ref[r]
            _for_run_pieces(rn_ref[r], lambda off, n: pltpu.make_async_copy(
                _rows(sb, loc + off, n), _rows(xs_ref, dst + off, n), s).start())
            return carry

        lax.fori_loop(0, n_exp, expert, 0)

    @pl.when(j == 0)
    def _():
        zero_ref[...] = jnp.zeros_like(zero_ref)

        def each_fill(fn):
            def expert(e, carry):
                row = lo_ref[e]
                n = hi_ref[e] - row
                for bit in reversed(range(MOE_ROWS.bit_length() - 1)):
                    size = (1 << bit) * SUBLANES
                    take = (n & (1 << bit)) != 0

                    @pl.when(take)
                    def _(row=row, size=size):
                        dst = xs_ref.at[pl.ds(pl.multiple_of(row * SUBLANES, SUBLANES), size)]
                        fn(pltpu.make_async_copy(zero_ref.at[pl.ds(0, size)], dst, zsem))

                    row = row + jnp.where(take, 1 << bit, 0)
                return carry

            lax.fori_loop(0, lo_ref.shape[0], expert, 0)

            def block(b, carry):
                fn(pltpu.make_async_copy(
                    zero_ref, xs_ref.at[pl.ds(pl.multiple_of(b * blk, blk), blk)], zsem))
                return carry

            lax.fori_loop(nb_ref[0], xs_ref.shape[0] // blk, block, 0)

        each_fill(lambda cp: cp.start())
        each_fill(lambda cp: cp.wait())

    sort(0, sb0)
    start_runs(0, sb0, sem.at[0])

    @pl.when(j > 0)
    def _():
        _wait_rows(sb1, sem.at[1])

    sort(1, sb1)
    start_runs(1, sb1, sem.at[1])
    _wait_rows(sb0, sem.at[0])

    @pl.when(j == pl.num_programs(0) - 1)
    def _():
        _wait_rows(sb1, sem.at[1])


def _dispatch(pos, run_dst, run_n, run_loc, pad_lo, pad_hi, n_used, h3, n_rows, tm):
    T = h3.shape[0] // SUBLANES
    slots = run_n.shape[0] // (T // tm)
    smem = pl.BlockSpec(memory_space=pltpu.SMEM)
    per_step = lambda n: pl.BlockSpec((n,), lambda j: (j,), memory_space=pltpu.SMEM)
    sorted_rows = pltpu.VMEM((tm * TOP_K * SUBLANES, LANES), F32)
    return pl.pallas_call(
        _dispatch_kernel,
        grid=(T // (2 * tm),),
        in_specs=[per_step(SUBLANES * 2 * tm),
                  per_step(2 * slots), per_step(2 * slots),
                  per_step(2 * slots), smem, smem, smem,
                  pl.BlockSpec((2 * tm * SUBLANES, LANES), lambda j: (j, 0))],
        out_specs=pl.BlockSpec(memory_space=pl.ANY),
        out_shape=jax.ShapeDtypeStruct((n_rows * SUBLANES, LANES), F32),
        scratch_shapes=[sorted_rows, sorted_rows, pltpu.VMEM((MOE_ROWS * SUBLANES, LANES), F32),
                        pltpu.SemaphoreType.DMA((2,)), pltpu.SemaphoreType.DMA(())],
        compiler_params=_params(("arbitrary",)),
        name="dispatch",
    )(pos, run_dst, run_n, run_loc, pad_lo, pad_hi, n_used, h3)


def _experts_kernel(be_ref, live_ref, first_ref, next_ref, slot_ref,
                    xs_ref, wu_hbm, bu_ref, wd_hbm, bd_ref, ys_ref,
                    wu_f32, wd_f32, wu_bf, wd_bf, sem):
    i = pl.program_id(0)

    def fetch(e, s):
        return (pltpu.make_async_copy(wu_hbm.at[e], wu_f32.at[s], sem.at[s]),
                pltpu.make_async_copy(wd_hbm.at[e], wd_f32.at[s], sem.at[s]))

    @pl.when(i == 0)
    def _():
        for cp in fetch(be_ref[0], 0):
            cp.start()

    @pl.when(first_ref[i] == 1)
    def _():
        s = slot_ref[i]
        for cp in fetch(be_ref[i], s):
            cp.wait()

        @pl.when(next_ref[i] >= 0)
        def _():
            for cp in fetch(next_ref[i], 1 - s):
                cp.start()

        wu_bf[...] = wu_f32[s].astype(BF16)
        wd_bf[...] = wd_f32[s].astype(BF16)

    def mlp(tiles):
        F = wd_bf.shape[0]
        x = _load_rows_from_tiles(xs_ref.at[tiles]).astype(BF16)
        gu = _dot(x, wu_bf[...]) + bu_ref[0]
        yield
        x_glu = jnp.minimum(gu[:, :F], SWIGLU_LIMIT)
        x_lin = jnp.clip(gu[:, F:], -SWIGLU_LIMIT, SWIGLU_LIMIT)
        hid = x_glu * jax.nn.sigmoid(SWIGLU_ALPHA * x_glu) * (x_lin + 1.0)
        y = _dot(hid.astype(BF16), wd_bf[...]) + bd_ref[0]
        yield
        _store_rows_as_tiles(ys_ref.at[tiles], y)
        yield

    n = xs_ref.shape[0] // MOE_PARTS
    for live in range(MOE_PARTS + 1):
        @pl.when(live_ref[i] == live)
        def _(live=live):
            for _ in zip(*[mlp(pl.ds(p * n, n)) for p in range(live)]):
                pass
            if live < MOE_PARTS:
                rest = (MOE_PARTS - live) * n
                ys_ref[pl.ds(live * n, rest), :] = jnp.zeros((rest, ys_ref.shape[1]), F32)


def _expert_runs(block_e):
    nb = block_e.shape[0]
    ar = jnp.arange(nb, dtype=jnp.int32)
    first = jnp.concatenate([jnp.ones((1,), bool), block_e[1:] != block_e[:-1]])
    starts = jnp.where(first, ar, nb)
    later = lax.cummin(jnp.concatenate([starts[1:], jnp.full((1,), nb, jnp.int32)]), reverse=True)
    nxt = jnp.where(later < nb, block_e[jnp.minimum(later, nb - 1)], -1)
    slot = (jnp.cumsum(first.astype(jnp.int32)) - 1) & 1
    return first.astype(jnp.int32), nxt.astype(jnp.int32), slot.astype(jnp.int32)


def _experts(block_e, live, xs, wu, bu, wd, bd):
    E, D, F2 = wu.shape
    F = wd.shape[1]
    nb = xs.shape[0] // (MOE_ROWS * SUBLANES)
    rows = pl.BlockSpec((MOE_ROWS * SUBLANES, LANES), lambda i, *_: (i, 0))
    grid_spec = pltpu.PrefetchScalarGridSpec(
        num_scalar_prefetch=5,
        grid=(nb,),
        in_specs=[rows,
                  pl.BlockSpec(memory_space=pl.ANY),
                  pl.BlockSpec((1, 1, F2), lambda i, be, *_: (be[i], 0, 0)),
                  pl.BlockSpec(memory_space=pl.ANY),
                  pl.BlockSpec((1, 1, D), lambda i, be, *_: (be[i], 0, 0))],
        out_specs=rows,
        scratch_shapes=[pltpu.VMEM((2, D, F2), F32), pltpu.VMEM((2, F, D), F32),
                        pltpu.VMEM((D, F2), BF16), pltpu.VMEM((F, D), BF16),
                        pltpu.SemaphoreType.DMA((2,))],
    )
    return pl.pallas_call(
        _experts_kernel,
        grid_spec=grid_spec,
        out_shape=jax.ShapeDtypeStruct(xs.shape, F32),
        compiler_params=_params(("arbitrary",)),
        name="experts",
    )(block_e, live, *_expert_runs(block_e), xs, wu, bu, wd, bd)


def _combine_kernel(pos_ref, gate_ref, src_ref, rn_ref, rloc_ref, nsrc_ref, nrn_ref, nrloc_ref,
                    x2_ref, g_ref, ys_ref, out_ref, yb0, yb1, acc_ref, sem, *, n_exp, apply_norm):
    j = pl.program_id(0)
    tm = x2_ref.shape[0] // 2
    slots = rn_ref.shape[0] // 2

    def start(tbls, half, yb, s):
        src_t, n_t, loc_t = tbls

        def expert(e, carry):
            r = half * slots + e
            loc, src = loc_t[r], src_t[r]
            _for_run_pieces(n_t[r], lambda off, n: pltpu.make_async_copy(
                _rows(ys_ref, src + off, n), _rows(yb, loc + off, n), s).start())
            return carry

        lax.fori_loop(0, n_exp, expert, 0)

    def finish(half, yb, s):
        _wait_rows(yb, s)

        def body(i, carry):
            for u in range(ROW_UNROLL):
                t = i * ROW_UNROLL + u
                a = half * tm + t
                acc = None
                for kk in range(TOP_K):
                    p = pos_ref[kk * 2 * tm + a]
                    term = yb[pl.ds(pl.multiple_of(p, SUBLANES), SUBLANES), :] \
                        * gate_ref[kk * 2 * tm + a]
                    acc = term if acc is None else acc + term
                acc_ref[pl.ds(pl.multiple_of(t * SUBLANES, SUBLANES), SUBLANES), :] = acc
            return carry

        lax.fori_loop(0, tm // ROW_UNROLL, body, 0)
        rows = pl.ds(half * tm, tm)
        x = x2_ref[rows, :] + _load_rows_from_tiles(acc_ref)
        out_ref[rows, :] = _rms(x, g_ref[...]) if apply_norm else x

    cur = (src_ref, rn_ref, rloc_ref)

    @pl.when(j == 0)
    def _():
        start(cur, 0, yb0, sem.at[0])

    start(cur, 1, yb1, sem.at[1])
    finish(0, yb0, sem.at[0])

    @pl.when(j < pl.num_programs(0) - 1)
    def _():
        start((nsrc_ref, nrn_ref, nrloc_ref), 0, yb0, sem.at[0])

    finish(1, yb1, sem.at[1])


def _combine(pos, gates, run_src, run_n, run_loc, x2, g, ys, tm, n_exp, apply_norm):
    T, D = x2.shape
    steps = T // (2 * tm)
    slots = run_n.shape[0] // (T // tm)
    per_step = lambda n: pl.BlockSpec((n,), lambda j: (j,), memory_space=pltpu.SMEM)
    next_step = lambda n: pl.BlockSpec((n,), lambda j: (jnp.minimum(j + 1, steps - 1),),
                                       memory_space=pltpu.SMEM)
    sorted_rows = pltpu.VMEM((tm * TOP_K * SUBLANES, LANES), F32)
    return pl.pallas_call(
        functools.partial(_combine_kernel, n_exp=n_exp, apply_norm=apply_norm),
        grid=(steps,),
        in_specs=[per_step(SUBLANES * 2 * tm),
                  per_step(SUBLANES * 2 * tm),
                  per_step(2 * slots), per_step(2 * slots), per_step(2 * slots),
                  next_step(2 * slots), next_step(2 * slots), next_step(2 * slots),
                  pl.BlockSpec((2 * tm, D), lambda j: (j, 0)),
                  pl.BlockSpec(g.shape, lambda j: (0, 0)),
                  pl.BlockSpec(memory_space=pl.ANY)],
        out_specs=pl.BlockSpec((2 * tm, D), lambda j: (j, 0)),
        out_shape=jax.ShapeDtypeStruct((T, D), F32),
        scratch_shapes=[sorted_rows, sorted_rows, pltpu.VMEM((tm * SUBLANES, LANES), F32),
                        pltpu.SemaphoreType.DMA((2,))],
        compiler_params=_params(("arbitrary",)),
        name="combine",
    )(pos, gates, run_src, run_n, run_loc, run_src, run_n, run_loc, x2, g, ys)


def _row(v):
    return v.reshape(1, -1).astype(F32)


def kernel(x, mem, norm_mix_g, w_in, gate_w2_fwd, gate_b_fwd, gate_w2_bwd, gate_b_bwd,
           gla_norm_g, conv_w, conv_b, conv_ln_g, conv_ln_b, w_out, norm_xattn_g,
           norm_mem_g, xattn_wq, xattn_wk, xattn_wv, xattn_wo, norm_ffn_g, router_w,
           router_b, exp_w_up, exp_b_up, exp_w_down, exp_b_down, final_norm_g):
    B, S, D = x.shape
    M = mem.shape[1]
    T = B * S
    depth = w_in.shape[0]
    kw = GLA_HEADS * GLA_DK
    vw = GLA_HEADS * GLA_DV
    rank_w = gate_w2_fwd.shape[1]
    cw = conv_w.shape[2]
    E = router_w.shape[2]
    assert vw + cw == w_out.shape[1] and 2 * rank_w <= LANES and E <= LANES

    xc = x.reshape(T, D)
    for l in range(depth):
        o_gf = 2 * kw + 2 * vw
        o_glu = o_gf + 2 * rank_w
        wl = w_in[l]
        w_perm = jnp.concatenate(
            [wl[:, :o_gf], wl[:, o_glu:], wl[:, o_gf:o_glu],
             jnp.zeros((D, LANES - 2 * rank_w), wl.dtype)], axis=1).astype(BF16)
        w2 = jnp.zeros((LANES, 2 * kw), F32)
        w2 = w2.at[:rank_w, :kw].set(gate_w2_fwd[l]).at[rank_w:2 * rank_w, kw:].set(gate_w2_bwd[l])
        b2 = jnp.concatenate([gate_b_fwd[l], gate_b_bwd[l]]).reshape(1, -1)

        q, k, v, gr, la, u = _inproj(xc, _row(norm_mix_g[l]), w_perm, w2.astype(BF16), b2,
                                     kw, vw, cw, tm=512)
        a_out = _gla(q, k, v, la, gr, _row(gla_norm_g[l]), B, S)
        b_out = _conv(u, conv_w[l], _row(conv_b[l]), _row(conv_ln_g[l]), _row(conv_ln_b[l]),
                      B, S, tm=512)
        km, vm = _memkv(mem.reshape(B * M, D), _row(norm_mem_g[l]),
                        xattn_wk[l].astype(BF16), xattn_wv[l].astype(BF16), M)

        rw = jnp.zeros((D, LANES), F32).at[:, :E].set(router_w[l]).astype(BF16)
        rb = jnp.zeros((1, LANES), F32).at[0, :E].set(router_b[l])
        x2, h3, idx, gates = _mid(
            xc, a_out, b_out, w_out[l].astype(BF16), _row(norm_xattn_g[l]),
            xattn_wq[l].astype(BF16), km, vm, xattn_wo[l].astype(BF16),
            _row(norm_ffn_g[l]), rw, rb, S, M, E, tm=MOE_TILE)

        tm = MOE_TILE
        pos, tile_cnt = _place(idx, tm=tm)
        n_te = tile_cnt.reshape(T // tm, SUBLANES, LANES)[:, 0, :E].astype(jnp.int32)
        counts = jnp.sum(n_te, axis=0)
        padded = ((counts + MOE_ROWS - 1) // MOE_ROWS) * MOE_ROWS
        pends = jnp.cumsum(padded)
        run_row = (pends - padded)[None, :] + jnp.cumsum(n_te, axis=0) - n_te
        run_loc = jnp.cumsum(n_te, axis=1) - n_te
        slots = max(E, LANES // 2)
        runs = [jnp.pad(a, ((0, 0), (0, slots - E))).reshape(-1).astype(jnp.int32)
                for a in (run_row, n_te, run_loc)]
        n_blocks = (T * TOP_K + E * (MOE_ROWS - 1)) // MOE_ROWS
        blk_start = jnp.arange(n_blocks, dtype=jnp.int32) * MOE_ROWS
        block_e = jnp.minimum(jnp.sum(pends[None, :] <= blk_start[:, None], axis=1), E - 1)
        n_used = (pends[E - 1] // MOE_ROWS).reshape(1)

        pos, gates = pos.reshape(-1), gates.reshape(-1)
        xs = _dispatch(pos, *runs, (pends - padded + counts).astype(jnp.int32),
                       pends.astype(jnp.int32), n_used.astype(jnp.int32), h3,
                       n_blocks * MOE_ROWS, tm=tm)
        group = MOE_ROWS // MOE_PARTS
        real = jnp.clip((pends - padded + counts)[block_e] - blk_start, 0, MOE_ROWS)
        live = jnp.where(blk_start < pends[E - 1], (real + group - 1) // group, 0)
        ys = _experts(block_e.astype(jnp.int32), live.astype(jnp.int32), xs,
                      exp_w_up[l], exp_b_up[l][:, None, :],
                      exp_w_down[l], exp_b_down[l][:, None, :])
        xc = _combine(pos, gates, *runs, x2,
                      _row(final_norm_g), ys, tm=tm, n_exp=E, apply_norm=(l == depth - 1))
    return xc.reshape(B, S, D)
```

```python
import functools

import numpy as np
import jax
import jax.numpy as jnp
from jax import lax
from jax.experimental import pallas as pl
from jax.experimental.pallas import tpu as pltpu

F32 = jnp.float32
BF16 = jnp.bfloat16

GLA_HEADS = 4
GLA_DK = 64
GLA_DV = 128
GLA_CHUNK = 64
GLA_TILE = 4
GLA_UNROLL = 4
GATE_TAU = 16.0
CONV_KERNEL = 31
XATTN_HEADS = 4
TOP_K = 4
SWIGLU_ALPHA = 1.702
SWIGLU_LIMIT = 7.0
RMS_EPS = 1e-6
LN_EPS = 1e-5

LANES = 128
SUBLANES = 8
MOE_ROWS = 512
MOE_PARTS = 2
MOE_TILE = 512
VMEM_LIMIT = 56 * 1024 * 1024


def _dot(a, b):
    return jnp.dot(a, b, preferred_element_type=F32)


def _dot_nt(a, b):
    return lax.dot_general(a, b, (((1,), (1,)), ((), ())), preferred_element_type=F32)


def _dot_tn(a, b):
    return lax.dot_general(a, b, (((0,), (0,)), ((), ())), preferred_element_type=F32)


def _rms(x, g):
    return x * lax.rsqrt(jnp.mean(x * x, axis=-1, keepdims=True) + RMS_EPS) * g


def _params(sem):
    return pltpu.CompilerParams(dimension_semantics=sem, vmem_limit_bytes=VMEM_LIMIT)


def _store_rows_as_tiles(ref, val):
    n, d = val.shape
    assert d == SUBLANES * LANES and ref.shape == (n * SUBLANES, LANES)
    for s in range(SUBLANES):
        ref[pl.ds(s, n, stride=SUBLANES), :] = val[:, s * LANES:(s + 1) * LANES]


def _load_rows_from_tiles(ref):
    n = ref.shape[0] // SUBLANES
    return jnp.concatenate([ref[pl.ds(s, n, stride=SUBLANES), :] for s in range(SUBLANES)], axis=1)


def _inproj_kernel(x_ref, g_ref, w_ref, w2_ref, b2_ref,
                   q_ref, k_ref, v_ref, gr_ref, la_ref, u_ref, *, kw, vw, cw, parts):
    n = x_ref.shape[0] // parts
    gens = [_inproj_rows(pl.ds(p * n, n), x_ref, g_ref, w_ref, w2_ref, b2_ref,
                         q_ref, k_ref, v_ref, gr_ref, la_ref, u_ref, kw, vw, cw)
            for p in range(parts)]
    for _ in zip(*gens):
        pass


def _inproj_rows(rows, x_ref, g_ref, w_ref, w2_ref, b2_ref,
                 q_ref, k_ref, v_ref, gr_ref, la_ref, u_ref, kw, vw, cw):
    h = _rms(x_ref[rows, :], g_ref[...]).astype(BF16)
    acc = _dot(h, w_ref[...])
    yield
    o = 0
    q_ref[rows, :] = (acc[:, o:o + kw] * (GLA_DK ** -0.5)).astype(BF16); o += kw
    k_ref[rows, :] = acc[:, o:o + kw].astype(BF16); o += kw
    v_ref[rows, :] = acc[:, o:o + vw].astype(BF16); o += vw
    r = acc[:, o:o + vw]; o += vw
    gr_ref[rows, :] = (r * jax.nn.sigmoid(r)).astype(BF16)
    a = acc[:, o:o + cw]; o += cw
    g = acc[:, o:o + cw]; o += cw
    u_ref[rows, :] = a * jax.nn.sigmoid(g)
    z = _dot(acc[:, o:o + LANES].astype(BF16), w2_ref[...]) + b2_ref[...]
    yield
    la_ref[rows, :] = (jnp.minimum(z, 0.0) - jnp.log1p(jnp.exp(-jnp.abs(z)))) * (1.0 / GATE_TAU)
    yield


def _inproj(x2d, g, w, w2, b2, kw, vw, cw, tm):
    T, D = x2d.shape
    row = lambda n: pl.BlockSpec((tm, n), lambda i: (i, 0))
    full = lambda a: pl.BlockSpec(a.shape, lambda i: (0,) * a.ndim)
    return pl.pallas_call(
        functools.partial(_inproj_kernel, kw=kw, vw=vw, cw=cw, parts=2),
        grid=(T // tm,),
        in_specs=[row(D), full(g), full(w), full(w2), full(b2)],
        out_specs=[row(kw), row(kw), row(vw), row(vw), row(2 * kw), row(cw)],
        out_shape=[jax.ShapeDtypeStruct((T, kw), BF16), jax.ShapeDtypeStruct((T, kw), BF16),
                   jax.ShapeDtypeStruct((T, vw), BF16), jax.ShapeDtypeStruct((T, vw), BF16),
                   jax.ShapeDtypeStruct((T, 2 * kw), F32), jax.ShapeDtypeStruct((T, cw), F32)],
        compiler_params=_params(("parallel",)),
        name="inproj",
    )(x2d, g, w, w2, b2)


def _gla_levels():
    ms, m = [], GLA_CHUNK // 2
    while m >= GLA_TILE:
        ms.append(m)
        m //= 2
    return ms


def _gla_constants(bwd):
    C, H = GLA_CHUNK, GLA_HEADS
    t = np.arange(C)[:, None]
    s = np.arange(C)[None, :]
    lev_q, lev_mask = [], []
    for m in _gla_levels():
        g = 2 * m
        second = (t % g) >= m
        same = t // g == s // g
        if not bwd:
            qrow = second
            mask = same & ((t % g) >= m) & ((s % g) < m)
        else:
            qrow = ~second
            mask = same & ((t % g) < m) & ((s % g) >= m)
        lev_q.append(np.broadcast_to(qrow, (C, H * GLA_DK)))
        lev_mask.append(np.tile(mask, (1, H)))
    shifts = range(0, GLA_TILE) if not bwd else range(1, GLA_TILE)
    sh_mask = []
    for sh in shifts:
        j = t - sh if not bwd else t + sh
        sh_mask.append(np.tile((s == j) & (t // GLA_TILE == s // GLA_TILE), (1, H)))
    f = lambda xs: np.stack(xs).astype(np.float32)
    return f(lev_q), f(lev_mask), f(sh_mask)


def _gla_head_mask():
    C, H = GLA_CHUNK, GLA_HEADS
    assert GLA_DK == C
    blk = np.arange(H * C) // C
    return (blk[:, None] == blk[None, :]).astype(np.float32)


def _gla_chunk(la, q, k, v, st_ref, c_ref, tri_ref, lq_ref, lm_ref, sm_ref, hm_ref, bwd):
    C, H = GLA_CHUNK, GLA_HEADS
    kw = q.shape[1]
    hm = hm_ref[...]

    def stack(xb):
        return jnp.concatenate([xb] * H, axis=0) * hm

    tri = tri_ref[...]
    hi = la.astype(BF16)
    incl = _dot(tri, hi) + _dot(tri, (la - hi.astype(F32)).astype(BF16))
    yield None
    tot = incl[C - 1:C]
    cum = incl if not bwd else incl - la
    x_q, x_k = (cum, tot - cum) if not bwd else (tot - cum, cum)
    e_q, e_k, a_tot = jnp.exp(x_q), jnp.exp(x_k), jnp.exp(tot)
    c_ref[...] = cum

    a = jnp.zeros((C, H * C), F32)
    for lv, m in enumerate(_gla_levels()):
        g = 2 * m
        edge = m - 1 if not bwd else m
        r = jnp.concatenate([jnp.broadcast_to(c_ref[s0 + edge:s0 + edge + 1, :], (g, kw))
                             for s0 in range(0, C, g)], axis=0)
        d = cum - r
        z = (jnp.where(lq_ref[lv] > 0, q, k) * jnp.exp(jnp.minimum(d, -d))).astype(BF16)
        a = a + _dot_nt(z, stack(z)) * lm_ref[lv]
    ps = []
    for si in range(sm_ref.shape[0]):
        sh = si if not bwd else si + 1
        if sh == 0:
            ps.append((q * k).astype(BF16))
            continue
        amt = sh if not bwd else C - sh
        kr = pltpu.roll(k, amt, 0)
        cr = pltpu.roll(cum, amt, 0)
        d = (cum - cr) if not bwd else (cr - cum)
        ps.append((q * kr * jnp.exp(jnp.minimum(d, 0.0))).astype(BF16))
    sc = _dot(jnp.concatenate(ps, axis=0), hm)
    yield None
    for si in range(sm_ref.shape[0]):
        a = a + sc[si * C:(si + 1) * C] * sm_ref[si]

    vst = jnp.concatenate([v[:, h * GLA_DV:(h + 1) * GLA_DV] for h in range(H)], axis=0)
    st = st_ref[...]
    o4 = _dot(stack(a.astype(BF16)), vst) + _dot_nt(stack((q * e_q).astype(BF16)), st.astype(BF16))
    st_ref[...] = st * a_tot + _dot_tn(vst, stack((k * e_k).astype(BF16)))
    yield jnp.concatenate([o4[h * C:(h + 1) * C] for h in range(H)], axis=1)


def _gla_kernel(q_ref, k_ref, v_ref, la_ref, gr_ref, ng_ref,
                lqf_ref, lmf_ref, smf_ref, lqb_ref, lmb_ref, smb_ref, tri_ref, hm_ref,
                out_ref, of_ref, ob_ref, stf_ref, stb_ref, cf_ref, cb_ref):
    C = GLA_CHUNK
    S = q_ref.shape[0]
    kw = q_ref.shape[1]
    n = S // C
    stf_ref[...] = jnp.zeros_like(stf_ref)
    stb_ref[...] = jnp.zeros_like(stb_ref)

    def body(i, carry):
        streams, outs = [], []
        for u in range(GLA_UNROLL):
            rf = pl.ds(pl.multiple_of((i * GLA_UNROLL + u) * C, C), C)
            rb = pl.ds(pl.multiple_of((n - 1 - i * GLA_UNROLL - u) * C, C), C)
            streams.append(_gla_chunk(
                la_ref[rf, 0:kw], q_ref[rf, :].astype(F32), k_ref[rf, :].astype(F32),
                v_ref[rf, :], stf_ref, cf_ref.at[u], tri_ref, lqf_ref, lmf_ref, smf_ref, hm_ref,
                False))
            streams.append(_gla_chunk(
                la_ref[rb, kw:2 * kw], q_ref[rb, :].astype(F32), k_ref[rb, :].astype(F32),
                v_ref[rb, :], stb_ref, cb_ref.at[u], tri_ref, lqb_ref, lmb_ref, smb_ref, hm_ref,
                True))
            outs += [(of_ref, rf), (ob_ref, rb)]
        for results in zip(*streams):
            pass
        for (ref, rows), o in zip(outs, results):
            ref[rows, :] = o
        return carry

    lax.fori_loop(0, n // GLA_UNROLL, body, 0)

    rows = 256
    def fin(i, carry):
        r = pl.ds(pl.multiple_of(i * rows, rows), rows)
        o = of_ref[r, :] + ob_ref[r, :]
        ys = []
        for h in range(GLA_HEADS):
            oh = o[:, h * GLA_DV:(h + 1) * GLA_DV]
            ys.append(_rms(oh, ng_ref[...]))
        y = jnp.concatenate(ys, axis=1) * gr_ref[r, :].astype(F32)
        out_ref[r, :] = y.astype(BF16)
        return carry

    lax.fori_loop(0, S // rows, fin, 0)


def _gla(q, k, v, la, gr, ng, B, S):
    kw, vw = q.shape[1], v.shape[1]
    cf = _gla_constants(False)
    cb = _gla_constants(True)
    tri = np.tril(np.ones((GLA_CHUNK, GLA_CHUNK), np.float32))
    consts = [jnp.asarray(c) for c in cf] + [jnp.asarray(c) for c in cb] + \
             [jnp.asarray(tri, BF16), jnp.asarray(_gla_head_mask(), BF16)]
    seq = lambda n: pl.BlockSpec((S, n), lambda b: (b, 0))
    full = lambda a: pl.BlockSpec(a.shape, lambda b: (0,) * a.ndim)
    return pl.pallas_call(
        _gla_kernel,
        grid=(B,),
        in_specs=[seq(kw), seq(kw), seq(vw), seq(2 * kw), seq(vw), full(ng)] + [full(c) for c in consts],
        out_specs=seq(vw),
        out_shape=jax.ShapeDtypeStruct((B * S, vw), BF16),
        scratch_shapes=[pltpu.VMEM((S, vw), F32), pltpu.VMEM((S, vw), F32),
                        pltpu.VMEM((GLA_DV, kw), F32), pltpu.VMEM((GLA_DV, kw), F32),
                        pltpu.VMEM((GLA_UNROLL, GLA_CHUNK, kw), F32),
                        pltpu.VMEM((GLA_UNROLL, GLA_CHUNK, kw), F32)],
        compiler_params=_params(("parallel",)),
        name="gla",
    )(q, k, v, la, gr, ng, *consts)


def _conv_kernel(u_ref, w_ref, cb_ref, lg_ref, lb_ref, out_ref, *, tm, halo):
    S = u_ref.shape[0]
    j = pl.program_id(1)
    t0 = pl.multiple_of(j * tm, tm)
    top = u_ref[pl.ds(pl.multiple_of(jnp.maximum(t0 - halo, 0), halo), halo), :]
    bot = u_ref[pl.ds(pl.multiple_of(jnp.minimum(t0 + tm, S - halo), halo), halo), :]
    top = jnp.where(j > 0, top, 0.0)
    bot = jnp.where(j < pl.num_programs(1) - 1, bot, 0.0)
    win = jnp.concatenate([top, u_ref[pl.ds(t0, tm), :], bot], axis=0)
    rows = tm + 2 * halo
    pad = (CONV_KERNEL - 1) // 2
    w = w_ref[...]
    acc = jnp.zeros((tm, u_ref.shape[1]), F32)
    for r in range(8):
        offs = [k for k in range(CONV_KERNEL) if (halo - pad + k) % 8 == r]
        if not offs:
            continue
        xr = win if r == 0 else pltpu.roll(win, rows - r, 0)
        for k in offs:
            a0 = (halo - pad + k) - r
            acc = acc + xr[a0:a0 + tm, :] * w[k:k + 1, :]
    y = acc + cb_ref[...]
    mu = jnp.mean(y, axis=-1, keepdims=True)
    yc = y - mu
    var = jnp.mean(yc * yc, axis=-1, keepdims=True)
    z = yc * lax.rsqrt(var + LN_EPS) * lg_ref[...] + lb_ref[...]
    out_ref[...] = (z * jax.nn.sigmoid(z)).astype(BF16)


def _conv(u, w, cb, lg, lb, B, S, tm):
    cw = u.shape[1]
    halo = 16
    full = lambda a: pl.BlockSpec(a.shape, lambda b, j: (0,) * a.ndim)
    nj = S // tm
    return pl.pallas_call(
        functools.partial(_conv_kernel, tm=tm, halo=halo),
        grid=(B, nj),
        in_specs=[pl.BlockSpec((S, cw), lambda b, j: (b, 0)), full(w), full(cb), full(lg), full(lb)],
        out_specs=pl.BlockSpec((tm, cw), lambda b, j: (b * nj + j, 0)),
        out_shape=jax.ShapeDtypeStruct((B * S, cw), BF16),
        compiler_params=_params(("parallel", "parallel")),
        name="conv",
    )(u, w, cb, lg, lb)


def _memkv_kernel(m_ref, g_ref, wk_ref, wv_ref, k_ref, v_ref):
    m = _rms(m_ref[...], g_ref[...]).astype(BF16)
    k_ref[...] = _dot(m, wk_ref[...]).astype(BF16)
    v_ref[...] = _dot(m, wv_ref[...]).astype(BF16)


def _memkv(mem2d, g, wk, wv, M):
    R, D = mem2d.shape
    full = lambda a: pl.BlockSpec(a.shape, lambda i: (0,) * a.ndim)
    blk = pl.BlockSpec((M, D), lambda i: (i, 0))
    return pl.pallas_call(
        _memkv_kernel,
        grid=(R // M,),
        in_specs=[blk, full(g), full(wk), full(wv)],
        out_specs=[blk, blk],
        out_shape=[jax.ShapeDtypeStruct((R, D), BF16)] * 2,
        compiler_params=_params(("parallel",)),
        name="memkv",
    )(mem2d, g, wk, wv)


def _mid_kernel(x_ref, a_ref, b_ref, wout_ref, gx_ref, wq_ref, km_ref, vm_ref, wo_ref,
                gf_ref, rw_ref, rb_ref, x2_ref, h3_ref, idx_ref, gate_ref, *, n_exp, parts):
    n = x_ref.shape[0] // parts
    gens = [_mid_rows(pl.ds(p * n, n), pl.ds(p * n * SUBLANES, n * SUBLANES),
                      x_ref, a_ref, b_ref, wout_ref, gx_ref, wq_ref, km_ref, vm_ref, wo_ref,
                      gf_ref, rw_ref, rb_ref, x2_ref, h3_ref, idx_ref, gate_ref, n_exp)
            for p in range(parts)]
    for _ in zip(*gens):
        pass


def _mid_rows(rows, tiles, x_ref, a_ref, b_ref, wout_ref, gx_ref, wq_ref, km_ref, vm_ref, wo_ref,
              gf_ref, rw_ref, rb_ref, x2_ref, h3_ref, idx_ref, gate_ref, n_exp):
    ab = jnp.concatenate([a_ref[rows, :], b_ref[rows, :]], axis=1)
    x1 = x_ref[rows, :] + _dot(ab, wout_ref[...])
    yield
    q2 = _dot(_rms(x1, gx_ref[...]).astype(BF16), wq_ref[...])
    yield
    D = q2.shape[1]
    dh = D // XATTN_HEADS
    ss = []
    for h in range(XATTN_HEADS):
        sl = slice(h * dh, (h + 1) * dh)
        ss.append(_dot_nt(q2[:, sl].astype(BF16), km_ref[:, sl]) * (dh ** -0.5))
    yield
    outs = []
    for h in range(XATTN_HEADS):
        sl = slice(h * dh, (h + 1) * dh)
        p = jnp.exp(ss[h] - jnp.max(ss[h], axis=-1, keepdims=True))
        p = p / jnp.sum(p, axis=-1, keepdims=True)
        outs.append(_dot(p.astype(BF16), vm_ref[:, sl]))
    yield
    o = jnp.concatenate(outs, axis=1).astype(BF16)
    x2 = x1 + _dot(o, wo_ref[...])
    yield
    x2_ref[rows, :] = x2
    h3 = _rms(x2, gf_ref[...])
    _store_rows_as_tiles(h3_ref.at[tiles], h3)
    lane = lax.broadcasted_iota(jnp.int32, (x2.shape[0], LANES), 1)
    logits = _dot(h3.astype(BF16), rw_ref[...]) + rb_ref[...]
    yield
    cur = jnp.where(lane < n_exp, logits, -jnp.inf)
    lane_f = lane.astype(F32)
    vals, idxs = [], []
    for _ in range(TOP_K):
        m = jnp.max(cur, axis=-1, keepdims=True)
        ik = jnp.min(jnp.where(cur == m, lane_f, float(LANES)), axis=-1, keepdims=True)
        vals.append(m)
        idxs.append(ik)
        cur = jnp.where(lane_f == ik, -jnp.inf, cur)
    es = [jnp.exp(vk - vals[0]) for vk in vals]
    den = es[0]
    for ek in es[1:]:
        den = den + ek
    idx_out = jnp.zeros(lane.shape, F32)
    gate_out = jnp.zeros(lane.shape, F32)
    for kk in range(TOP_K):
        idx_out = jnp.where(lane == kk, idxs[kk], idx_out)
        gate_out = jnp.where(lane == kk, es[kk] / den, gate_out)
    idx_ref[rows, :] = idx_out.astype(jnp.int32)
    gate_ref[:, rows] = gate_out.T[0:SUBLANES, :]
    yield


def _mid(x2d, a, b, wout, gx, wq, km, vm, wo, gf, rw, rb, S, M, n_exp, tm):
    T, D = x2d.shape
    half = a.shape[1]
    per_b = S // tm
    row = lambda n: pl.BlockSpec((tm, n), lambda i: (i, 0))
    full = lambda arr: pl.BlockSpec(arr.shape, lambda i: (0,) * arr.ndim)
    memb = pl.BlockSpec((M, D), lambda i: (i // per_b, 0))
    return pl.pallas_call(
        functools.partial(_mid_kernel, n_exp=n_exp, parts=2),
        grid=(T // tm,),
        in_specs=[row(D), row(half), row(half), full(wout), full(gx), full(wq), memb, memb,
                  full(wo), full(gf), full(rw), full(rb)],
        out_specs=[row(D), pl.BlockSpec((tm * SUBLANES, LANES), lambda i: (i, 0)),
                   row(LANES), pl.BlockSpec((SUBLANES, tm), lambda i: (i // 2, i % 2))],
        out_shape=[jax.ShapeDtypeStruct((T, D), F32),
                   jax.ShapeDtypeStruct((T * SUBLANES, LANES), F32),
                   jax.ShapeDtypeStruct((T, LANES), jnp.int32),
                   jax.ShapeDtypeStruct((T // (2 * tm) * SUBLANES, 2 * tm), F32)],
        compiler_params=_params(("parallel",)),
        name="mid",
    )(x2d, a, b, wout, gx, wq, km, vm, wo, gf, rw, rb)


def _onehot(idx_ref, kk, lane):
    return (lane == idx_ref[:, kk:kk + 1]).astype(F32)


def _place_kernel(idx_ref, tri_ref, upper_ref, pos_ref, cnt_ref):
    tm = idx_ref.shape[0]
    lane = lax.broadcasted_iota(jnp.int32, (tm, LANES), 1)
    hots = [_onehot(idx_ref, kk, lane) for kk in range(TOP_K)]
    mask = hots[0]
    for hk in hots[1:]:
        mask = mask + hk
    incl = _dot(tri_ref[...], mask.astype(BF16))
    cnt = jnp.broadcast_to(incl[tm - 1:tm, :], (SUBLANES, LANES))
    hi = jnp.floor(cnt * (1.0 / 16.0))
    lo = cnt - 16.0 * hi
    start = 16.0 * _dot(hi.astype(BF16), upper_ref[...]) + _dot(lo.astype(BF16), upper_ref[...])
    before = incl - mask + start[0:1, :]
    out = jnp.zeros((tm, LANES), F32)
    for kk in range(TOP_K):
        rk = jnp.sum(hots[kk] * before, axis=-1, keepdims=True)
        out = jnp.where(lane == kk, rk, out)
    pos_ref[...] = (out.T[0:SUBLANES, :] * SUBLANES).astype(jnp.int32)
    cnt_ref[...] = cnt


def _place(idx, tm):
    T = idx.shape[0]
    assert tm <= 16 * 256
    tri = jnp.asarray(np.tril(np.ones((tm, tm), np.float32)), BF16)
    upper = jnp.asarray(np.triu(np.ones((LANES, LANES), np.float32), 1), BF16)
    row = pl.BlockSpec((tm, LANES), lambda i: (i, 0))
    return pl.pallas_call(
        _place_kernel,
        grid=(T // tm,),
        in_specs=[row, pl.BlockSpec((tm, tm), lambda i: (0, 0)),
                  pl.BlockSpec((LANES, LANES), lambda i: (0, 0))],
        out_specs=[pl.BlockSpec((SUBLANES, tm), lambda i: (i // 2, i % 2)),
                   pl.BlockSpec((SUBLANES, LANES), lambda i: (i, 0))],
        out_shape=[jax.ShapeDtypeStruct((T // (2 * tm) * SUBLANES, 2 * tm), jnp.int32),
                   jax.ShapeDtypeStruct((T // tm * SUBLANES, LANES), F32)],
        compiler_params=_params(("parallel",)),
        name="place",
    )(idx, tri, upper)


ROW_UNROLL = 8


def _wait_rows(buf_ref, sem):
    pltpu.make_async_copy(buf_ref, buf_ref, sem).wait()


RUN_CHUNK = 32


def _rows(ref, row, n):
    return ref.at[pl.ds(pl.multiple_of(row * SUBLANES, SUBLANES), n * SUBLANES)]


def _for_run_pieces(n, fn):
    shift = RUN_CHUNK.bit_length() - 1
    whole = lax.shift_right_logical(n, shift)

    def body(c, carry):
        fn(c * RUN_CHUNK, RUN_CHUNK)
        return carry

    lax.fori_loop(0, whole, body, 0)
    off = whole * RUN_CHUNK
    for bit in reversed(range(shift)):
        take = (n & (1 << bit)) != 0

        @pl.when(take)
        def _(off=off, bit=bit):
            fn(off, 1 << bit)

        off = off + jnp.where(take, 1 << bit, 0)


def _dispatch_kernel(pos_ref, rdst_ref, rn_ref, rloc_ref, lo_ref, hi_ref, nb_ref, h_ref, xs_ref,
                     sb0, sb1, zero_ref, sem, zsem):
    j = pl.program_id(0)
    tm = h_ref.shape[0] // SUBLANES // 2
    slots = rn_ref.shape[0] // 2
    n_exp = lo_ref.shape[0]
    blk = zero_ref.shape[0]

    def sort(half, sb):
        def body(i, carry):
            for u in range(ROW_UNROLL):
                t = half * tm + i * ROW_UNROLL + u
                tile = h_ref[pl.ds(pl.multiple_of(t * SUBLANES, SUBLANES), SUBLANES), :]
                for kk in range(TOP_K):
                    p = pos_ref[kk * 2 * tm + t]
                    sb[pl.ds(pl.multiple_of(p, SUBLANES), SUBLANES), :] = tile
            return carry

        lax.fori_loop(0, tm // ROW_UNROLL, body, 0)

    def start_runs(half, sb, s):
        def expert(e, carry):
            r = half * slots + e
            loc, dst = rloc_ref[r], rdst_ref[r]
            _for_run_pieces(rn_ref[r], lambda off, n: pltpu.make_async_copy(
                _rows(sb, loc + off, n), _rows(xs_ref, dst + off, n), s).start())
            return carry

        lax.fori_loop(0, n_exp, expert, 0)

    @pl.when(j == 0)
    def _():
        zero_ref[...] = jnp.zeros_like(zero_ref)

        def each_fill(fn):
            def expert(e, carry):
                row = lo_ref[e]
                n = hi_ref[e] - row
                for bit in reversed(range(MOE_ROWS.bit_length() - 1)):
                    size = (1 << bit) * SUBLANES
                    take = (n & (1 << bit)) != 0

                    @pl.when(take)
                    def _(row=row, size=size):
                        dst = xs_ref.at[pl.ds(pl.multiple_of(row * SUBLANES, SUBLANES), size)]
                        fn(pltpu.make_async_copy(zero_ref.at[pl.ds(0, size)], dst, zsem))

                    row = row + jnp.where(take, 1 << bit, 0)
                return carry

            lax.fori_loop(0, lo_ref.shape[0], expert, 0)

            def block(b, carry):
                fn(pltpu.make_async_copy(
                    zero_ref, xs_ref.at[pl.ds(pl.multiple_of(b * blk, blk), blk)], zsem))
                return carry

            lax.fori_loop(nb_ref[0], xs_ref.shape[0] // blk, block, 0)

        each_fill(lambda cp: cp.start())
        each_fill(lambda cp: cp.wait())

    sort(0, sb0)
    start_runs(0, sb0, sem.at[0])

    @pl.when(j > 0)
    def _():
        _wait_rows(sb1, sem.at[1])

    sort(1, sb1)
    start_runs(1, sb1, sem.at[1])
    _wait_rows(sb0, sem.at[0])

    @pl.when(j == pl.num_programs(0) - 1)
    def _():
        _wait_rows(sb1, sem.at[1])


def _dispatch(pos, run_dst, run_n, run_loc, pad_lo, pad_hi, n_used, h3, n_rows, tm):
    T = h3.shape[0] // SUBLANES
    slots = run_n.shape[0] // (T // tm)
    smem = pl.BlockSpec(memory_space=pltpu.SMEM)
    per_step = lambda n: pl.BlockSpec((n,), lambda j: (j,), memory_space=pltpu.SMEM)
    sorted_rows = pltpu.VMEM((tm * TOP_K * SUBLANES, LANES), F32)
    return pl.pallas_call(
        _dispatch_kernel,
        grid=(T // (2 * tm),),
        in_specs=[per_step(SUBLANES * 2 * tm),
                  per_step(2 * slots), per_step(2 * slots),
                  per_step(2 * slots), smem, smem, smem,
                  pl.BlockSpec((2 * tm * SUBLANES, LANES), lambda j: (j, 0))],
        out_specs=pl.BlockSpec(memory_space=pl.ANY),
        out_shape=jax.ShapeDtypeStruct((n_rows * SUBLANES, LANES), F32),
        scratch_shapes=[sorted_rows, sorted_rows, pltpu.VMEM((MOE_ROWS * SUBLANES, LANES), F32),
                        pltpu.SemaphoreType.DMA((2,)), pltpu.SemaphoreType.DMA(())],
        compiler_params=_params(("arbitrary",)),
        name="dispatch",
    )(pos, run_dst, run_n, run_loc, pad_lo, pad_hi, n_used, h3)


def _experts_kernel(be_ref, live_ref, first_ref, next_ref, slot_ref,
                    xs_ref, wu_hbm, bu_ref, wd_hbm, bd_ref, ys_ref,
                    wu_f32, wd_f32, wu_bf, wd_bf, sem):
    i = pl.program_id(0)

    def fetch(e, s):
        return (pltpu.make_async_copy(wu_hbm.at[e], wu_f32.at[s], sem.at[s]),
                pltpu.make_async_copy(wd_hbm.at[e], wd_f32.at[s], sem.at[s]))

    @pl.when(i == 0)
    def _():
        for cp in fetch(be_ref[0], 0):
            cp.start()

    @pl.when(first_ref[i] == 1)
    def _():
        s = slot_ref[i]
        for cp in fetch(be_ref[i], s):
            cp.wait()

        @pl.when(next_ref[i] >= 0)
        def _():
            for cp in fetch(next_ref[i], 1 - s):
                cp.start()

        wu_bf[...] = wu_f32[s].astype(BF16)
        wd_bf[...] = wd_f32[s].astype(BF16)

    def mlp(tiles):
        F = wd_bf.shape[0]
        x = _load_rows_from_tiles(xs_ref.at[tiles]).astype(BF16)
        gu = _dot(x, wu_bf[...]) + bu_ref[0]
        yield
        x_glu = jnp.minimum(gu[:, :F], SWIGLU_LIMIT)
        x_lin = jnp.clip(gu[:, F:], -SWIGLU_LIMIT, SWIGLU_LIMIT)
        hid = x_glu * jax.nn.sigmoid(SWIGLU_ALPHA * x_glu) * (x_lin + 1.0)
        y = _dot(hid.astype(BF16), wd_bf[...]) + bd_ref[0]
        yield
        _store_rows_as_tiles(ys_ref.at[tiles], y)
        yield

    n = xs_ref.shape[0] // MOE_PARTS
    for live in range(MOE_PARTS + 1):
        @pl.when(live_ref[i] == live)
        def _(live=live):
            for _ in zip(*[mlp(pl.ds(p * n, n)) for p in range(live)]):
                pass
            if live < MOE_PARTS:
                rest = (MOE_PARTS - live) * n
                ys_ref[pl.ds(live * n, rest), :] = jnp.zeros((rest, ys_ref.shape[1]), F32)


def _expert_runs(block_e):
    nb = block_e.shape[0]
    ar = jnp.arange(nb, dtype=jnp.int32)
    first = jnp.concatenate([jnp.ones((1,), bool), block_e[1:] != block_e[:-1]])
    starts = jnp.where(first, ar, nb)
    later = lax.cummin(jnp.concatenate([starts[1:], jnp.full((1,), nb, jnp.int32)]), reverse=True)
    nxt = jnp.where(later < nb, block_e[jnp.minimum(later, nb - 1)], -1)
    slot = (jnp.cumsum(first.astype(jnp.int32)) - 1) & 1
    return first.astype(jnp.int32), nxt.astype(jnp.int32), slot.astype(jnp.int32)


def _experts(block_e, live, xs, wu, bu, wd, bd):
    E, D, F2 = wu.shape
    F = wd.shape[1]
    nb = xs.shape[0] // (MOE_ROWS * SUBLANES)
    rows = pl.BlockSpec((MOE_ROWS * SUBLANES, LANES), lambda i, *_: (i, 0))
    grid_spec = pltpu.PrefetchScalarGridSpec(
        num_scalar_prefetch=5,
        grid=(nb,),
        in_specs=[rows,
                  pl.BlockSpec(memory_space=pl.ANY),
                  pl.BlockSpec((1, 1, F2), lambda i, be, *_: (be[i], 0, 0)),
                  pl.BlockSpec(memory_space=pl.ANY),
                  pl.BlockSpec((1, 1, D), lambda i, be, *_: (be[i], 0, 0))],
        out_specs=rows,
        scratch_shapes=[pltpu.VMEM((2, D, F2), F32), pltpu.VMEM((2, F, D), F32),
                        pltpu.VMEM((D, F2), BF16), pltpu.VMEM((F, D), BF16),
                        pltpu.SemaphoreType.DMA((2,))],
    )
    return pl.pallas_call(
        _experts_kernel,
        grid_spec=grid_spec,
        out_shape=jax.ShapeDtypeStruct(xs.shape, F32),
        compiler_params=_params(("arbitrary",)),
        name="experts",
    )(block_e, live, *_expert_runs(block_e), xs, wu, bu, wd, bd)


def _combine_kernel(pos_ref, gate_ref, src_ref, rn_ref, rloc_ref, nsrc_ref, nrn_ref, nrloc_ref,
                    x2_ref, g_ref, ys_ref, out_ref, yb0, yb1, acc_ref, sem, *, n_exp, apply_norm):
    j = pl.program_id(0)
    tm = x2_ref.shape[0] // 2
    slots = rn_ref.shape[0] // 2

    def start(tbls, half, yb, s):
        src_t, n_t, loc_t = tbls

        def expert(e, carry):
            r = half * slots + e
            loc, src = loc_t[r], src_t[r]
            _for_run_pieces(n_t[r], lambda off, n: pltpu.make_async_copy(
                _rows(ys_ref, src + off, n), _rows(yb, loc + off, n), s).start())
            return carry

        lax.fori_loop(0, n_exp, expert, 0)

    def finish(half, yb, s):
        _wait_rows(yb, s)

        def body(i, carry):
            for u in range(ROW_UNROLL):
                t = i * ROW_UNROLL + u
                a = half * tm + t
                acc = None
                for kk in range(TOP_K):
                    p = pos_ref[kk * 2 * tm + a]
                    term = yb[pl.ds(pl.multiple_of(p, SUBLANES), SUBLANES), :] \
                        * gate_ref[kk * 2 * tm + a]
                    acc = term if acc is None else acc + term
                acc_ref[pl.ds(pl.multiple_of(t * SUBLANES, SUBLANES), SUBLANES), :] = acc
            return carry

        lax.fori_loop(0, tm // ROW_UNROLL, body, 0)
        rows = pl.ds(half * tm, tm)
        x = x2_ref[rows, :] + _load_rows_from_tiles(acc_ref)
        out_ref[rows, :] = _rms(x, g_ref[...]) if apply_norm else x

    cur = (src_ref, rn_ref, rloc_ref)

    @pl.when(j == 0)
    def _():
        start(cur, 0, yb0, sem.at[0])

    start(cur, 1, yb1, sem.at[1])
    finish(0, yb0, sem.at[0])

    @pl.when(j < pl.num_programs(0) - 1)
    def _():
        start((nsrc_ref, nrn_ref, nrloc_ref), 0, yb0, sem.at[0])

    finish(1, yb1, sem.at[1])


def _combine(pos, gates, run_src, run_n, run_loc, x2, g, ys, tm, n_exp, apply_norm):
    T, D = x2.shape
    steps = T // (2 * tm)
    slots = run_n.shape[0] // (T // tm)
    per_step = lambda n: pl.BlockSpec((n,), lambda j: (j,), memory_space=pltpu.SMEM)
    next_step = lambda n: pl.BlockSpec((n,), lambda j: (jnp.minimum(j + 1, steps - 1),),
                                       memory_space=pltpu.SMEM)
    sorted_rows = pltpu.VMEM((tm * TOP_K * SUBLANES, LANES), F32)
    return pl.pallas_call(
        functools.partial(_combine_kernel, n_exp=n_exp, apply_norm=apply_norm),
        grid=(steps,),
        in_specs=[per_step(SUBLANES * 2 * tm),
                  per_step(SUBLANES * 2 * tm),
                  per_step(2 * slots), per_step(2 * slots), per_step(2 * slots),
                  next_step(2 * slots), next_step(2 * slots), next_step(2 * slots),
                  pl.BlockSpec((2 * tm, D), lambda j: (j, 0)),
                  pl.BlockSpec(g.shape, lambda j: (0, 0)),
                  pl.BlockSpec(memory_space=pl.ANY)],
        out_specs=pl.BlockSpec((2 * tm, D), lambda j: (j, 0)),
        out_shape=jax.ShapeDtypeStruct((T, D), F32),
        scratch_shapes=[sorted_rows, sorted_rows, pltpu.VMEM((tm * SUBLANES, LANES), F32),
                        pltpu.SemaphoreType.DMA((2,))],
        compiler_params=_params(("arbitrary",)),
        name="combine",
    )(pos, gates, run_src, run_n, run_loc, run_src, run_n, run_loc, x2, g, ys)


def _row(v):
    return v.reshape(1, -1).astype(F32)


def kernel(x, mem, norm_mix_g, w_in, gate_w2_fwd, gate_b_fwd, gate_w2_bwd, gate_b_bwd,
           gla_norm_g, conv_w, conv_b, conv_ln_g, conv_ln_b, w_out, norm_xattn_g,
           norm_mem_g, xattn_wq, xattn_wk, xattn_wv, xattn_wo, norm_ffn_g, router_w,
           router_b, exp_w_up, exp_b_up, exp_w_down, exp_b_down, final_norm_g):
    B, S, D = x.shape
    M = mem.shape[1]
    T = B * S
    depth = w_in.shape[0]
    kw = GLA_HEADS * GLA_DK
    vw = GLA_HEADS * GLA_DV
    rank_w = gate_w2_fwd.shape[1]
    cw = conv_w.shape[2]
    E = router_w.shape[2]
    assert vw + cw == w_out.shape[1] and 2 * rank_w <= LANES and E <= LANES

    xc = x.reshape(T, D)
    for l in range(depth):
        o_gf = 2 * kw + 2 * vw
        o_glu = o_gf + 2 * rank_w
        wl = w_in[l]
        w_perm = jnp.concatenate(
            [wl[:, :o_gf], wl[:, o_glu:], wl[:, o_gf:o_glu],
             jnp.zeros((D, LANES - 2 * rank_w), wl.dtype)], axis=1).astype(BF16)
        w2 = jnp.zeros((LANES, 2 * kw), F32)
        w2 = w2.at[:rank_w, :kw].set(gate_w2_fwd[l]).at[rank_w:2 * rank_w, kw:].set(gate_w2_bwd[l])
        b2 = jnp.concatenate([gate_b_fwd[l], gate_b_bwd[l]]).reshape(1, -1)

        q, k, v, gr, la, u = _inproj(xc, _row(norm_mix_g[l]), w_perm, w2.astype(BF16), b2,
                                     kw, vw, cw, tm=512)
        a_out = _gla(q, k, v, la, gr, _row(gla_norm_g[l]), B, S)
        b_out = _conv(u, conv_w[l], _row(conv_b[l]), _row(conv_ln_g[l]), _row(conv_ln_b[l]),
                      B, S, tm=512)
        km, vm = _memkv(mem.reshape(B * M, D), _row(norm_mem_g[l]),
                        xattn_wk[l].astype(BF16), xattn_wv[l].astype(BF16), M)

        rw = jnp.zeros((D, LANES), F32).at[:, :E].set(router_w[l]).astype(BF16)
        rb = jnp.zeros((1, LANES), F32).at[0, :E].set(router_b[l])
        x2, h3, idx, gates = _mid(
            xc, a_out, b_out, w_out[l].astype(BF16), _row(norm_xattn_g[l]),
            xattn_wq[l].astype(BF16), km, vm, xattn_wo[l].astype(BF16),
            _row(norm_ffn_g[l]), rw, rb, S, M, E, tm=MOE_TILE)

        tm = MOE_TILE
        pos, tile_cnt = _place(idx, tm=tm)
        n_te = tile_cnt.reshape(T // tm, SUBLANES, LANES)[:, 0, :E].astype(jnp.int32)
        counts = jnp.sum(n_te, axis=0)
        padded = ((counts + MOE_ROWS - 1) // MOE_ROWS) * MOE_ROWS
        pends = jnp.cumsum(padded)
        run_row = (pends - padded)[None, :] + jnp.cumsum(n_te, axis=0) - n_te
        run_loc = jnp.cumsum(n_te, axis=1) - n_te
        slots = max(E, LANES // 2)
        runs = [jnp.pad(a, ((0, 0), (0, slots - E))).reshape(-1).astype(jnp.int32)
                for a in (run_row, n_te, run_loc)]
        n_blocks = (T * TOP_K + E * (MOE_ROWS - 1)) // MOE_ROWS
        blk_start = jnp.arange(n_blocks, dtype=jnp.int32) * MOE_ROWS
        block_e = jnp.minimum(jnp.sum(pends[None, :] <= blk_start[:, None], axis=1), E - 1)
        n_used = (pends[E - 1] // MOE_ROWS).reshape(1)

        pos, gates = pos.reshape(-1), gates.reshape(-1)
        xs = _dispatch(pos, *runs, (pends - padded + counts).astype(jnp.int32),
                       pends.astype(jnp.int32), n_used.astype(jnp.int32), h3,
                       n_blocks * MOE_ROWS, tm=tm)
        group = MOE_ROWS // MOE_PARTS
        real = jnp.clip((pends - padded + counts)[block_e] - blk_start, 0, MOE_ROWS)
        live = jnp.where(blk_start < pends[E - 1], (real + group - 1) // group, 0)
        ys = _experts(block_e.astype(jnp.int32), live.astype(jnp.int32), xs,
                      exp_w_up[l], exp_b_up[l][:, None, :],
                      exp_w_down[l], exp_b_down[l][:, None, :])
        xc = _combine(pos, gates, *runs, x2,
                      _row(final_norm_g), ys, tm=tm, n_exp=E, apply_norm=(l == depth - 1))
    return xc.reshape(B, S, D)
```

```python
import functools

import numpy as np
import jax
import jax.numpy as jnp
from jax import lax
from jax.experimental import pallas as pl
from jax.experimental.pallas import tpu as pltpu

F32 = jnp.float32
BF16 = jnp.bfloat16

GLA_HEADS = 4
GLA_DK = 64
GLA_DV = 128
GLA_CHUNK = 64
GLA_TILE = 4
GLA_UNROLL = 4
GATE_TAU = 16.0
CONV_KERNEL = 31
XATTN_HEADS = 4
TOP_K = 4
SWIGLU_ALPHA = 1.702
SWIGLU_LIMIT = 7.0
RMS_EPS = 1e-6
LN_EPS = 1e-5

LANES = 128
SUBLANES = 8
MOE_ROWS = 512
MOE_PARTS = 2
MOE_TILE = 512
VMEM_LIMIT = 56 * 1024 * 1024


def _dot(a, b):
    return jnp.dot(a, b, preferred_element_type=F32)


def _dot_nt(a, b):
    return lax.dot_general(a, b, (((1,), (1,)), ((), ())), preferred_element_type=F32)


def _dot_tn(a, b):
    return lax.dot_general(a, b, (((0,), (0,)), ((), ())), preferred_element_type=F32)


def _rms(x, g):
    return x * lax.rsqrt(jnp.mean(x * x, axis=-1, keepdims=True) + RMS_EPS) * g


def _params(sem):
    return pltpu.CompilerParams(dimension_semantics=sem, vmem_limit_bytes=VMEM_LIMIT)


def _store_rows_as_tiles(ref, val):
    n, d = val.shape
    assert d == SUBLANES * LANES and ref.shape == (n * SUBLANES, LANES)
    for s in range(SUBLANES):
        ref[pl.ds(s, n, stride=SUBLANES), :] = val[:, s * LANES:(s + 1) * LANES]


def _load_rows_from_tiles(ref):
    n = ref.shape[0] // SUBLANES
    return jnp.concatenate([ref[pl.ds(s, n, stride=SUBLANES), :] for s in range(SUBLANES)], axis=1)


def _inproj_kernel(x_ref, g_ref, w_qkvr_ref, w_glu_ref, w_gate_ref, w2_ref, b2_ref,
                   q_ref, k_ref, v_ref, gr_ref, la_ref, u_ref, *, kw, vw, cw, parts):
    n = x_ref.shape[0] // parts
    w_ref = (w_qkvr_ref, w_glu_ref, w_gate_ref)
    gens = [_inproj_rows(pl.ds(p * n, n), x_ref, g_ref, w_ref, w2_ref, b2_ref,
                         q_ref, k_ref, v_ref, gr_ref, la_ref, u_ref, kw, vw, cw)
            for p in range(parts)]
    for _ in zip(*gens):
        pass


def _inproj_rows(rows, x_ref, g_ref, w_ref, w2_ref, b2_ref,
                 q_ref, k_ref, v_ref, gr_ref, la_ref, u_ref, kw, vw, cw):
    w_qkvr_ref, w_glu_ref, w_gate_ref = w_ref
    h = _rms(x_ref[rows, :], g_ref[...]).astype(BF16)
    acc = _dot(h, w_qkvr_ref[...])
    glu = _dot(h, w_glu_ref[...])
    ranks = _dot(h, w_gate_ref[...])
    yield
    o = 0
    q_ref[rows, :] = (acc[:, o:o + kw] * (GLA_DK ** -0.5)).astype(BF16); o += kw
    k_ref[rows, :] = acc[:, o:o + kw].astype(BF16); o += kw
    v_ref[rows, :] = acc[:, o:o + vw].astype(BF16); o += vw
    r = acc[:, o:o + vw]
    gr_ref[rows, :] = (r * jax.nn.sigmoid(r)).astype(BF16)
    u_ref[rows, :] = glu[:, :cw] * jax.nn.sigmoid(glu[:, cw:])
    z = _dot(ranks.astype(BF16), w2_ref[...]) + b2_ref[...]
    yield
    la_ref[rows, :] = (jnp.minimum(z, 0.0) - jnp.log1p(jnp.exp(-jnp.abs(z)))) * (1.0 / GATE_TAU)
    yield


def _inproj(x2d, g, ws, w2, b2, kw, vw, cw, tm):
    T, D = x2d.shape
    row = lambda n: pl.BlockSpec((tm, n), lambda i: (i, 0))
    full = lambda a: pl.BlockSpec(a.shape, lambda i: (0,) * a.ndim)
    return pl.pallas_call(
        functools.partial(_inproj_kernel, kw=kw, vw=vw, cw=cw, parts=2),
        grid=(T // tm,),
        in_specs=[row(D), full(g)] + [full(w) for w in ws] + [full(w2), full(b2)],
        out_specs=[row(kw), row(kw), row(vw), row(vw), row(2 * kw), row(cw)],
        out_shape=[jax.ShapeDtypeStruct((T, kw), BF16), jax.ShapeDtypeStruct((T, kw), BF16),
                   jax.ShapeDtypeStruct((T, vw), BF16), jax.ShapeDtypeStruct((T, vw), BF16),
                   jax.ShapeDtypeStruct((T, 2 * kw), F32), jax.ShapeDtypeStruct((T, cw), F32)],
        compiler_params=_params(("parallel",)),
        name="inproj",
    )(x2d, g, *ws, w2, b2)


def _gla_levels():
    ms, m = [], GLA_CHUNK // 2
    while m >= GLA_TILE:
        ms.append(m)
        m //= 2
    return ms


def _gla_constants(bwd):
    C, H = GLA_CHUNK, GLA_HEADS
    t = np.arange(C)[:, None]
    s = np.arange(C)[None, :]
    lev_q, lev_mask = [], []
    for m in _gla_levels():
        g = 2 * m
        second = (t % g) >= m
        same = t // g == s // g
        if not bwd:
            qrow = second
            mask = same & ((t % g) >= m) & ((s % g) < m)
        else:
            qrow = ~second
            mask = same & ((t % g) < m) & ((s % g) >= m)
        lev_q.append(np.broadcast_to(qrow, (C, H * GLA_DK)))
        lev_mask.append(np.tile(mask, (1, H)))
    shifts = range(0, GLA_TILE) if not bwd else range(1, GLA_TILE)
    sh_mask = []
    for sh in shifts:
        j = t - sh if not bwd else t + sh
        sh_mask.append(np.tile((s == j) & (t // GLA_TILE == s // GLA_TILE), (1, H)))
    f = lambda xs: np.stack(xs).astype(np.float32)
    return f(lev_q), f(lev_mask), f(sh_mask)


def _gla_head_mask():
    C, H = GLA_CHUNK, GLA_HEADS
    assert GLA_DK == C
    blk = np.arange(H * C) // C
    return (blk[:, None] == blk[None, :]).astype(np.float32)


def _gla_chunk(la, q, k, v, st_ref, c_ref, tri_ref, lq_ref, lm_ref, sm_ref, hm_ref, bwd):
    C, H = GLA_CHUNK, GLA_HEADS
    kw = q.shape[1]
    hm = hm_ref[...]

    def stack(xb):
        return jnp.concatenate([xb] * H, axis=0) * hm

    tri = tri_ref[...]
    hi = la.astype(BF16)
    incl = _dot(tri, hi) + _dot(tri, (la - hi.astype(F32)).astype(BF16))
    yield None
    tot = incl[C - 1:C]
    cum = incl if not bwd else incl - la
    x_q, x_k = (cum, tot - cum) if not bwd else (tot - cum, cum)
    e_q, e_k, a_tot = jnp.exp(x_q), jnp.exp(x_k), jnp.exp(tot)
    c_ref[...] = cum

    a = jnp.zeros((C, H * C), F32)
    for lv, m in enumerate(_gla_levels()):
        g = 2 * m
        edge = m - 1 if not bwd else m
        r = jnp.concatenate([jnp.broadcast_to(c_ref[s0 + edge:s0 + edge + 1, :], (g, kw))
                             for s0 in range(0, C, g)], axis=0)
        d = cum - r
        z = (jnp.where(lq_ref[lv] > 0, q, k) * jnp.exp(jnp.minimum(d, -d))).astype(BF16)
        a = a + _dot_nt(z, stack(z)) * lm_ref[lv]
    ps = []
    for si in range(sm_ref.shape[0]):
        sh = si if not bwd else si + 1
        if sh == 0:
            ps.append((q * k).astype(BF16))
            continue
        amt = sh if not bwd else C - sh
        kr = pltpu.roll(k, amt, 0)
        cr = pltpu.roll(cum, amt, 0)
        d = (cum - cr) if not bwd else (cr - cum)
        ps.append((q * kr * jnp.exp(jnp.minimum(d, 0.0))).astype(BF16))
    sc = _dot(jnp.concatenate(ps, axis=0), hm)
    yield None
    for si in range(sm_ref.shape[0]):
        a = a + sc[si * C:(si + 1) * C] * sm_ref[si]

    vst = jnp.concatenate([v[:, h * GLA_DV:(h + 1) * GLA_DV] for h in range(H)], axis=0)
    st = st_ref[...]
    o4 = _dot(stack(a.astype(BF16)), vst) + _dot_nt(stack((q * e_q).astype(BF16)), st.astype(BF16))
    st_ref[...] = st * a_tot + _dot_tn(vst, stack((k * e_k).astype(BF16)))
    yield jnp.concatenate([o4[h * C:(h + 1) * C] for h in range(H)], axis=1)


def _gla_kernel(q_ref, k_ref, v_ref, la_ref, gr_ref, ng_ref,
                lqf_ref, lmf_ref, smf_ref, lqb_ref, lmb_ref, smb_ref, tri_ref, hm_ref,
                out_ref, of_ref, ob_ref, stf_ref, stb_ref, cf_ref, cb_ref):
    C = GLA_CHUNK
    S = q_ref.shape[0]
    kw = q_ref.shape[1]
    n = S // C
    stf_ref[...] = jnp.zeros_like(stf_ref)
    stb_ref[...] = jnp.zeros_like(stb_ref)

    def body(i, carry):
        streams, outs = [], []
        for u in range(GLA_UNROLL):
            rf = pl.ds(pl.multiple_of((i * GLA_UNROLL + u) * C, C), C)
            rb = pl.ds(pl.multiple_of((n - 1 - i * GLA_UNROLL - u) * C, C), C)
            streams.append(_gla_chunk(
                la_ref[rf, 0:kw], q_ref[rf, :].astype(F32), k_ref[rf, :].astype(F32),
                v_ref[rf, :], stf_ref, cf_ref.at[u], tri_ref, lqf_ref, lmf_ref, smf_ref, hm_ref,
                False))
            streams.append(_gla_chunk(
                la_ref[rb, kw:2 * kw], q_ref[rb, :].astype(F32), k_ref[rb, :].astype(F32),
                v_ref[rb, :], stb_ref, cb_ref.at[u], tri_ref, lqb_ref, lmb_ref, smb_ref, hm_ref,
                True))
            outs += [(of_ref, rf), (ob_ref, rb)]
        for results in zip(*streams):
            pass
        for (ref, rows), o in zip(outs, results):
            ref[rows, :] = o
        return carry

    lax.fori_loop(0, n // GLA_UNROLL, body, 0)

    rows = 256
    def fin(i, carry):
        r = pl.ds(pl.multiple_of(i * rows, rows), rows)
        o = of_ref[r, :] + ob_ref[r, :]
        ys = []
        for h in range(GLA_HEADS):
            oh = o[:, h * GLA_DV:(h + 1) * GLA_DV]
            ys.append(_rms(oh, ng_ref[...]))
        y = jnp.concatenate(ys, axis=1) * gr_ref[r, :].astype(F32)
        out_ref[r, :] = y.astype(BF16)
        return carry

    lax.fori_loop(0, S // rows, fin, 0)


def _gla(q, k, v, la, gr, ng, B, S):
    kw, vw = q.shape[1], v.shape[1]
    cf = _gla_constants(False)
    cb = _gla_constants(True)
    tri = np.tril(np.ones((GLA_CHUNK, GLA_CHUNK), np.float32))
    consts = [jnp.asarray(c) for c in cf] + [jnp.asarray(c) for c in cb] + \
             [jnp.asarray(tri, BF16), jnp.asarray(_gla_head_mask(), BF16)]
    seq = lambda n: pl.BlockSpec((S, n), lambda b: (b, 0))
    full = lambda a: pl.BlockSpec(a.shape, lambda b: (0,) * a.ndim)
    return pl.pallas_call(
        _gla_kernel,
        grid=(B,),
        in_specs=[seq(kw), seq(kw), seq(vw), seq(2 * kw), seq(vw), full(ng)] + [full(c) for c in consts],
        out_specs=seq(vw),
        out_shape=jax.ShapeDtypeStruct((B * S, vw), BF16),
        scratch_shapes=[pltpu.VMEM((S, vw), F32), pltpu.VMEM((S, vw), F32),
                        pltpu.VMEM((GLA_DV, kw), F32), pltpu.VMEM((GLA_DV, kw), F32),
                        pltpu.VMEM((GLA_UNROLL, GLA_CHUNK, kw), F32),
                        pltpu.VMEM((GLA_UNROLL, GLA_CHUNK, kw), F32)],
        compiler_params=_params(("parallel",)),
        name="gla",
    )(q, k, v, la, gr, ng, *consts)


def _conv_kernel(u_ref, w_ref, cb_ref, lg_ref, lb_ref, out_ref, *, tm, halo):
    S = u_ref.shape[0]
    j = pl.program_id(1)
    t0 = pl.multiple_of(j * tm, tm)
    top = u_ref[pl.ds(pl.multiple_of(jnp.maximum(t0 - halo, 0), halo), halo), :]
    bot = u_ref[pl.ds(pl.multiple_of(jnp.minimum(t0 + tm, S - halo), halo), halo), :]
    top = jnp.where(j > 0, top, 0.0)
    bot = jnp.where(j < pl.num_programs(1) - 1, bot, 0.0)
    win = jnp.concatenate([top, u_ref[pl.ds(t0, tm), :], bot], axis=0)
    rows = tm + 2 * halo
    pad = (CONV_KERNEL - 1) // 2
    w = w_ref[...]
    acc = jnp.zeros((tm, u_ref.shape[1]), F32)
    for r in range(8):
        offs = [k for k in range(CONV_KERNEL) if (halo - pad + k) % 8 == r]
        if not offs:
            continue
        xr = win if r == 0 else pltpu.roll(win, rows - r, 0)
        for k in offs:
            a0 = (halo - pad + k) - r
            acc = acc + xr[a0:a0 + tm, :] * w[k:k + 1, :]
    y = acc + cb_ref[...]
    mu = jnp.mean(y, axis=-1, keepdims=True)
    yc = y - mu
    var = jnp.mean(yc * yc, axis=-1, keepdims=True)
    z = yc * lax.rsqrt(var + LN_EPS) * lg_ref[...] + lb_ref[...]
    out_ref[...] = (z * jax.nn.sigmoid(z)).astype(BF16)


def _conv(u, w, cb, lg, lb, B, S, tm):
    cw = u.shape[1]
    halo = 16
    full = lambda a: pl.BlockSpec(a.shape, lambda b, j: (0,) * a.ndim)
    nj = S // tm
    return pl.pallas_call(
        functools.partial(_conv_kernel, tm=tm, halo=halo),
        grid=(B, nj),
        in_specs=[pl.BlockSpec((S, cw), lambda b, j: (b, 0)), full(w), full(cb), full(lg), full(lb)],
        out_specs=pl.BlockSpec((tm, cw), lambda b, j: (b * nj + j, 0)),
        out_shape=jax.ShapeDtypeStruct((B * S, cw), BF16),
        compiler_params=_params(("parallel", "parallel")),
        name="conv",
    )(u, w, cb, lg, lb)


def _memkv_kernel(m_ref, g_ref, wk_ref, wv_ref, k_ref, v_ref):
    m = _rms(m_ref[...], g_ref[...]).astype(BF16)
    k_ref[...] = _dot(m, wk_ref[...]).astype(BF16)
    v_ref[...] = _dot(m, wv_ref[...]).astype(BF16)


def _memkv(mem2d, g, wk, wv, M):
    R, D = mem2d.shape
    full = lambda a: pl.BlockSpec(a.shape, lambda i: (0,) * a.ndim)
    blk = pl.BlockSpec((M, D), lambda i: (i, 0))
    return pl.pallas_call(
        _memkv_kernel,
        grid=(R // M,),
        in_specs=[blk, full(g), full(wk), full(wv)],
        out_specs=[blk, blk],
        out_shape=[jax.ShapeDtypeStruct((R, D), BF16)] * 2,
        compiler_params=_params(("parallel",)),
        name="memkv",
    )(mem2d, g, wk, wv)


def _mid_kernel(x_ref, a_ref, b_ref, wout_ref, gx_ref, wq_ref, km_ref, vm_ref, wo_ref,
                gf_ref, rw_ref, rb_ref, x2_ref, h3_ref, idx_ref, gate_ref, *, n_exp, parts):
    n = x_ref.shape[0] // parts
    gens = [_mid_rows(pl.ds(p * n, n), pl.ds(p * n * SUBLANES, n * SUBLANES),
                      x_ref, a_ref, b_ref, wout_ref, gx_ref, wq_ref, km_ref, vm_ref, wo_ref,
                      gf_ref, rw_ref, rb_ref, x2_ref, h3_ref, idx_ref, gate_ref, n_exp)
            for p in range(parts)]
    for _ in zip(*gens):
        pass


def _mid_rows(rows, tiles, x_ref, a_ref, b_ref, wout_ref, gx_ref, wq_ref, km_ref, vm_ref, wo_ref,
              gf_ref, rw_ref, rb_ref, x2_ref, h3_ref, idx_ref, gate_ref, n_exp):
    ab = jnp.concatenate([a_ref[rows, :], b_ref[rows, :]], axis=1)
    x1 = x_ref[rows, :] + _dot(ab, wout_ref[...])
    yield
    q2 = _dot(_rms(x1, gx_ref[...]).astype(BF16), wq_ref[...])
    yield
    D = q2.shape[1]
    dh = D // XATTN_HEADS
    ss = []
    for h in range(XATTN_HEADS):
        sl = slice(h * dh, (h + 1) * dh)
        ss.append(_dot_nt(q2[:, sl].astype(BF16), km_ref[:, sl]) * (dh ** -0.5))
    yield
    outs = []
    for h in range(XATTN_HEADS):
        sl = slice(h * dh, (h + 1) * dh)
        p = jnp.exp(ss[h] - jnp.max(ss[h], axis=-1, keepdims=True))
        p = p / jnp.sum(p, axis=-1, keepdims=True)
        outs.append(_dot(p.astype(BF16), vm_ref[:, sl]))
    yield
    o = jnp.concatenate(outs, axis=1).astype(BF16)
    x2 = x1 + _dot(o, wo_ref[...])
    yield
    x2_ref[rows, :] = x2
    h3 = _rms(x2, gf_ref[...])
    _store_rows_as_tiles(h3_ref.at[tiles], h3)
    lane = lax.broadcasted_iota(jnp.int32, (x2.shape[0], LANES), 1)
    logits = _dot(h3.astype(BF16), rw_ref[...]) + rb_ref[...]
    yield
    cur = jnp.where(lane < n_exp, logits, -jnp.inf)
    lane_f = lane.astype(F32)
    vals, idxs = [], []
    for _ in range(TOP_K):
        m = jnp.max(cur, axis=-1, keepdims=True)
        ik = jnp.min(jnp.where(cur == m, lane_f, float(LANES)), axis=-1, keepdims=True)
        vals.append(m)
        idxs.append(ik)
        cur = jnp.where(lane_f == ik, -jnp.inf, cur)
    es = [jnp.exp(vk - vals[0]) for vk in vals]
    den = es[0]
    for ek in es[1:]:
        den = den + ek
    idx_out = jnp.zeros(lane.shape, F32)
    gate_out = jnp.zeros(lane.shape, F32)
    for kk in range(TOP_K):
        idx_out = jnp.where(lane == kk, idxs[kk], idx_out)
        gate_out = jnp.where(lane == kk, es[kk] / den, gate_out)
    idx_ref[rows, :] = idx_out.astype(jnp.int32)
    gate_ref[:, rows] = gate_out.T[0:SUBLANES, :]
    yield


def _mid(x2d, a, b, wout, gx, wq, km, vm, wo, gf, rw, rb, S, M, n_exp, tm):
    T, D = x2d.shape
    half = a.shape[1]
    per_b = S // tm
    row = lambda n: pl.BlockSpec((tm, n), lambda i: (i, 0))
    full = lambda arr: pl.BlockSpec(arr.shape, lambda i: (0,) * arr.ndim)
    memb = pl.BlockSpec((M, D), lambda i: (i // per_b, 0))
    return pl.pallas_call(
        functools.partial(_mid_kernel, n_exp=n_exp, parts=2),
        grid=(T // tm,),
        in_specs=[row(D), row(half), row(half), full(wout), full(gx), full(wq), memb, memb,
                  full(wo), full(gf), full(rw), full(rb)],
        out_specs=[row(D), pl.BlockSpec((tm * SUBLANES, LANES), lambda i: (i, 0)),
                   row(LANES), pl.BlockSpec((SUBLANES, tm), lambda i: (i // 2, i % 2))],
        out_shape=[jax.ShapeDtypeStruct((T, D), F32),
                   jax.ShapeDtypeStruct((T * SUBLANES, LANES), F32),
                   jax.ShapeDtypeStruct((T, LANES), jnp.int32),
                   jax.ShapeDtypeStruct((T // (2 * tm) * SUBLANES, 2 * tm), F32)],
        compiler_params=_params(("parallel",)),
        name="mid",
    )(x2d, a, b, wout, gx, wq, km, vm, wo, gf, rw, rb)


def _place_kernel(idx_ref, tri_ref, upper_ref, pos_ref, cnt_ref):
    tm = idx_ref.shape[0] // 2
    for _ in zip(*[_place_tile(half, tm, idx_ref, tri_ref, upper_ref, pos_ref, cnt_ref)
                   for half in range(2)]):
        pass


def _place_tile(half, tm, idx_ref, tri_ref, upper_ref, pos_ref, cnt_ref):
    rows = pl.ds(half * tm, tm)
    lane = lax.broadcasted_iota(jnp.int32, (tm, LANES), 1)
    hots = [(lane == idx_ref[rows, kk:kk + 1]).astype(F32) for kk in range(TOP_K)]
    mask = hots[0]
    for hk in hots[1:]:
        mask = mask + hk
    incl = _dot(tri_ref[...], mask.astype(BF16))
    yield
    cnt = jnp.broadcast_to(incl[tm - 1:tm, :], (SUBLANES, LANES))
    hi = jnp.floor(cnt * (1.0 / 16.0))
    lo = cnt - 16.0 * hi
    start = 16.0 * _dot(hi.astype(BF16), upper_ref[...]) + _dot(lo.astype(BF16), upper_ref[...])
    yield
    before = incl - mask + start[0:1, :]
    out = jnp.zeros((tm, LANES), F32)
    for kk in range(TOP_K):
        rk = jnp.sum(hots[kk] * before, axis=-1, keepdims=True)
        out = jnp.where(lane == kk, rk, out)
    pos_ref[:, rows] = (out.T[0:SUBLANES, :] * SUBLANES).astype(jnp.int32)
    cnt_ref[pl.ds(half * SUBLANES, SUBLANES), :] = cnt
    yield


def _place(idx, tm):
    T = idx.shape[0]
    assert tm <= 16 * 256
    tri = jnp.asarray(np.tril(np.ones((tm, tm), np.float32)), BF16)
    upper = jnp.asarray(np.triu(np.ones((LANES, LANES), np.float32), 1), BF16)
    return pl.pallas_call(
        _place_kernel,
        grid=(T // (2 * tm),),
        in_specs=[pl.BlockSpec((2 * tm, LANES), lambda i: (i, 0)),
                  pl.BlockSpec((tm, tm), lambda i: (0, 0)),
                  pl.BlockSpec((LANES, LANES), lambda i: (0, 0))],
        out_specs=[pl.BlockSpec((SUBLANES, 2 * tm), lambda i: (i, 0)),
                   pl.BlockSpec((2 * SUBLANES, LANES), lambda i: (i, 0))],
        out_shape=[jax.ShapeDtypeStruct((T // (2 * tm) * SUBLANES, 2 * tm), jnp.int32),
                   jax.ShapeDtypeStruct((T // tm * SUBLANES, LANES), F32)],
        compiler_params=_params(("parallel",)),
        name="place",
    )(idx, tri, upper)


ROW_UNROLL = 8


def _wait_rows(buf_ref, sem):
    pltpu.make_async_copy(buf_ref, buf_ref, sem).wait()


RUN_CHUNK = 32


def _rows(ref, row, n):
    return ref.at[pl.ds(pl.multiple_of(row * SUBLANES, SUBLANES), n * SUBLANES)]


def _for_run_pieces(n, fn):
    shift = RUN_CHUNK.bit_length() - 1
    whole = lax.shift_right_logical(n, shift)

    def body(c, carry):
        fn(c * RUN_CHUNK, RUN_CHUNK)
        return carry

    lax.fori_loop(0, whole, body, 0)
    off = whole * RUN_CHUNK
    for bit in reversed(range(shift)):
        take = (n & (1 << bit)) != 0

        @pl.when(take)
        def _(off=off, bit=bit):
            fn(off, 1 << bit)

        off = off + jnp.where(take, 1 << bit, 0)


def _dispatch_kernel(pos_ref, rdst_ref, rn_ref, rloc_ref, lo_ref, hi_ref, nb_ref, h_ref, xs_ref,
                     sb0, sb1, zero_ref, sem, zsem):
    j = pl.program_id(0)
    tm = h_ref.shape[0] // SUBLANES // 2
    slots = rn_ref.shape[0] // 2
    n_exp = lo_ref.shape[0]
    blk = zero_ref.shape[0]

    def sort(half, sb):
        def body(i, carry):
            for u in range(ROW_UNROLL):
                t = half * tm + i * ROW_UNROLL + u
                tile = h_ref[pl.ds(pl.multiple_of(t * SUBLANES, SUBLANES), SUBLANES), :]
                for kk in range(TOP_K):
                    p = pos_ref[kk * 2 * tm + t]
                    sb[pl.ds(pl.multiple_of(p, SUBLANES), SUBLANES), :] = tile
            return carry

        lax.fori_loop(0, tm // ROW_UNROLL, body, 0)

    def start_runs(half, sb, s):
        def expert(e, carry):
            r = half * slots + e
            loc, dst = rloc_ref[r], rdst_ref[r]
            _for_run_pieces(rn_ref[r], lambda off, n: pltpu.make_async_copy(
                _rows(sb, loc + off, n), _rows(xs_ref, dst + off, n), s).start())
            return carry

        lax.fori_loop(0, n_exp, expert, 0)

    @pl.when(j == 0)
    def _():
        zero_ref[...] = jnp.zeros_like(zero_ref)

        def each_fill(fn):
            def expert(e, carry):
                row = lo_ref[e]
                n = hi_ref[e] - row
                for bit in reversed(range(MOE_ROWS.bit_length() - 1)):
                    size = (1 << bit) * SUBLANES
                    take = (n & (1 << bit)) != 0

                    @pl.when(take)
                    def _(row=row, size=size):
                        dst = xs_ref.at[pl.ds(pl.multiple_of(row * SUBLANES, SUBLANES), size)]
                        fn(pltpu.make_async_copy(zero_ref.at[pl.ds(0, size)], dst, zsem))

                    row = row + jnp.where(take, 1 << bit, 0)
                return carry

            lax.fori_loop(0, lo_ref.shape[0], expert, 0)

            def block(b, carry):
                fn(pltpu.make_async_copy(
                    zero_ref, xs_ref.at[pl.ds(pl.multiple_of(b * blk, blk), blk)], zsem))
                return carry

            lax.fori_loop(nb_ref[0], xs_ref.shape[0] // blk, block, 0)

        each_fill(lambda cp: cp.start())
        each_fill(lambda cp: cp.wait())

    sort(0, sb0)
    start_runs(0, sb0, sem.at[0])

    @pl.when(j > 0)
    def _():
        _wait_rows(sb1, sem.at[1])

    sort(1, sb1)
    start_runs(1, sb1, sem.at[1])
    _wait_rows(sb0, sem.at[0])

    @pl.when(j == pl.num_programs(0) - 1)
    def _():
        _wait_rows(sb1, sem.at[1])


def _dispatch(pos, run_dst, run_n, run_loc, pad_lo, pad_hi, n_used, h3, n_rows, tm):
    T = h3.shape[0] // SUBLANES
    slots = run_n.shape[0] // (T // tm)
    smem = pl.BlockSpec(memory_space=pltpu.SMEM)
    per_step = lambda n: pl.BlockSpec((n,), lambda j: (j,), memory_space=pltpu.SMEM)
    sorted_rows = pltpu.VMEM((tm * TOP_K * SUBLANES, LANES), F32)
    return pl.pallas_call(
        _dispatch_kernel,
        grid=(T // (2 * tm),),
        in_specs=[per_step(SUBLANES * 2 * tm),
                  per_step(2 * slots), per_step(2 * slots),
                  per_step(2 * slots), smem, smem, smem,
                  pl.BlockSpec((2 * tm * SUBLANES, LANES), lambda j: (j, 0))],
        out_specs=pl.BlockSpec(memory_space=pl.ANY),
        out_shape=jax.ShapeDtypeStruct((n_rows * SUBLANES, LANES), F32),
        scratch_shapes=[sorted_rows, sorted_rows, pltpu.VMEM((MOE_ROWS * SUBLANES, LANES), F32),
                        pltpu.SemaphoreType.DMA((2,)), pltpu.SemaphoreType.DMA(())],
        compiler_params=_params(("arbitrary",)),
        name="dispatch",
    )(pos, run_dst, run_n, run_loc, pad_lo, pad_hi, n_used, h3)


def _experts_kernel(be_ref, live_ref, first_ref, next_ref, slot_ref,
                    xs_ref, wu_hbm, bu_ref, wd_hbm, bd_ref, ys_ref,
                    wu_f32, wd_f32, wu_bf, wd_bf, sem):
    i = pl.program_id(0)

    def fetch(e, s):
        return (pltpu.make_async_copy(wu_hbm.at[e], wu_f32.at[s], sem.at[s]),
                pltpu.make_async_copy(wd_hbm.at[e], wd_f32.at[s], sem.at[s]))

    @pl.when(i == 0)
    def _():
        for cp in fetch(be_ref[0], 0):
            cp.start()

    @pl.when(first_ref[i] == 1)
    def _():
        s = slot_ref[i]
        for cp in fetch(be_ref[i], s):
            cp.wait()

        @pl.when(next_ref[i] >= 0)
        def _():
            for cp in fetch(next_ref[i], 1 - s):
                cp.start()

        wu_bf[...] = wu_f32[s].astype(BF16)
        wd_bf[...] = wd_f32[s].astype(BF16)

    def mlp(tiles):
        F = wd_bf.shape[0]
        x = _load_rows_from_tiles(xs_ref.at[tiles]).astype(BF16)
        gu = _dot(x, wu_bf[...]) + bu_ref[0]
        yield
        x_glu = jnp.minimum(gu[:, :F], SWIGLU_LIMIT)
        x_lin = jnp.clip(gu[:, F:], -SWIGLU_LIMIT, SWIGLU_LIMIT)
        hid = x_glu * jax.nn.sigmoid(SWIGLU_ALPHA * x_glu) * (x_lin + 1.0)
        y = _dot(hid.astype(BF16), wd_bf[...]) + bd_ref[0]
        yield
        _store_rows_as_tiles(ys_ref.at[tiles], y)
        yield

    n = xs_ref.shape[0] // MOE_PARTS
    for live in range(MOE_PARTS + 1):
        @pl.when(live_ref[i] == live)
        def _(live=live):
            for _ in zip(*[mlp(pl.ds(p * n, n)) for p in range(live)]):
                pass
            if live < MOE_PARTS:
                rest = (MOE_PARTS - live) * n
                ys_ref[pl.ds(live * n, rest), :] = jnp.zeros((rest, ys_ref.shape[1]), F32)


def _expert_runs(block_e):
    nb = block_e.shape[0]
    ar = jnp.arange(nb, dtype=jnp.int32)
    first = jnp.concatenate([jnp.ones((1,), bool), block_e[1:] != block_e[:-1]])
    starts = jnp.where(first, ar, nb)
    later = lax.cummin(jnp.concatenate([starts[1:], jnp.full((1,), nb, jnp.int32)]), reverse=True)
    nxt = jnp.where(later < nb, block_e[jnp.minimum(later, nb - 1)], -1)
    slot = (jnp.cumsum(first.astype(jnp.int32)) - 1) & 1
    return first.astype(jnp.int32), nxt.astype(jnp.int32), slot.astype(jnp.int32)


def _experts(block_e, live, xs, wu, bu, wd, bd):
    E, D, F2 = wu.shape
    F = wd.shape[1]
    nb = xs.shape[0] // (MOE_ROWS * SUBLANES)
    rows = pl.BlockSpec((MOE_ROWS * SUBLANES, LANES), lambda i, *_: (i, 0))
    grid_spec = pltpu.PrefetchScalarGridSpec(
        num_scalar_prefetch=5,
        grid=(nb,),
        in_specs=[rows,
                  pl.BlockSpec(memory_space=pl.ANY),
                  pl.BlockSpec((1, 1, F2), lambda i, be, *_: (be[i], 0, 0)),
                  pl.BlockSpec(memory_space=pl.ANY),
                  pl.BlockSpec((1, 1, D), lambda i, be, *_: (be[i], 0, 0))],
        out_specs=rows,
        scratch_shapes=[pltpu.VMEM((2, D, F2), F32), pltpu.VMEM((2, F, D), F32),
                        pltpu.VMEM((D, F2), BF16), pltpu.VMEM((F, D), BF16),
                        pltpu.SemaphoreType.DMA((2,))],
    )
    return pl.pallas_call(
        _experts_kernel,
        grid_spec=grid_spec,
        out_shape=jax.ShapeDtypeStruct(xs.shape, F32),
        compiler_params=_params(("arbitrary",)),
        name="experts",
    )(block_e, live, *_expert_runs(block_e), xs, wu, bu, wd, bd)


def _combine_kernel(pos_ref, gate_ref, src_ref, rn_ref, rloc_ref, nsrc_ref, nrn_ref, nrloc_ref,
                    x2_ref, g_ref, ys_ref, out_ref, yb0, yb1, acc_ref, sem, *, n_exp, apply_norm):
    j = pl.program_id(0)
    tm = x2_ref.shape[0] // 2
    slots = rn_ref.shape[0] // 2

    def start(tbls, half, yb, s):
        src_t, n_t, loc_t = tbls

        def expert(e, carry):
            r = half * slots + e
            loc, src = loc_t[r], src_t[r]
            _for_run_pieces(n_t[r], lambda off, n: pltpu.make_async_copy(
                _rows(ys_ref, src + off, n), _rows(yb, loc + off, n), s).start())
            return carry

        lax.fori_loop(0, n_exp, expert, 0)

    def finish(half, yb, s):
        _wait_rows(yb, s)

        def body(i, carry):
            for u in range(ROW_UNROLL):
                t = i * ROW_UNROLL + u
                a = half * tm + t
                acc = None
                for kk in range(TOP_K):
                    p = pos_ref[kk * 2 * tm + a]
                    term = yb[pl.ds(pl.multiple_of(p, SUBLANES), SUBLANES), :] \
                        * gate_ref[kk * 2 * tm + a]
                    acc = term if acc is None else acc + term
                acc_ref[pl.ds(pl.multiple_of(t * SUBLANES, SUBLANES), SUBLANES), :] = acc
            return carry

        lax.fori_loop(0, tm // ROW_UNROLL, body, 0)
        rows = pl.ds(half * tm, tm)
        x = x2_ref[rows, :] + _load_rows_from_tiles(acc_ref)
        out_ref[rows, :] = _rms(x, g_ref[...]) if apply_norm else x

    cur = (src_ref, rn_ref, rloc_ref)

    @pl.when(j == 0)
    def _():
        start(cur, 0, yb0, sem.at[0])

    start(cur, 1, yb1, sem.at[1])
    finish(0, yb0, sem.at[0])

    @pl.when(j < pl.num_programs(0) - 1)
    def _():
        start((nsrc_ref, nrn_ref, nrloc_ref), 0, yb0, sem.at[0])

    finish(1, yb1, sem.at[1])


def _combine(pos, gates, run_src, run_n, run_loc, x2, g, ys, tm, n_exp, apply_norm):
    T, D = x2.shape
    steps = T // (2 * tm)
    slots = run_n.shape[0] // (T // tm)
    per_step = lambda n: pl.BlockSpec((n,), lambda j: (j,), memory_space=pltpu.SMEM)
    next_step = lambda n: pl.BlockSpec((n,), lambda j: (jnp.minimum(j + 1, steps - 1),),
                                       memory_space=pltpu.SMEM)
    sorted_rows = pltpu.VMEM((tm * TOP_K * SUBLANES, LANES), F32)
    return pl.pallas_call(
        functools.partial(_combine_kernel, n_exp=n_exp, apply_norm=apply_norm),
        grid=(steps,),
        in_specs=[per_step(SUBLANES * 2 * tm),
                  per_step(SUBLANES * 2 * tm),
                  per_step(2 * slots), per_step(2 * slots), per_step(2 * slots),
                  next_step(2 * slots), next_step(2 * slots), next_step(2 * slots),
                  pl.BlockSpec((2 * tm, D), lambda j: (j, 0)),
                  pl.BlockSpec(g.shape, lambda j: (0, 0)),
                  pl.BlockSpec(memory_space=pl.ANY)],
        out_specs=pl.BlockSpec((2 * tm, D), lambda j: (j, 0)),
        out_shape=jax.ShapeDtypeStruct((T, D), F32),
        scratch_shapes=[sorted_rows, sorted_rows, pltpu.VMEM((tm * SUBLANES, LANES), F32),
                        pltpu.SemaphoreType.DMA((2,))],
        compiler_params=_params(("arbitrary",)),
        name="combine",
    )(pos, gates, run_src, run_n, run_loc, run_src, run_n, run_loc, x2, g, ys)


def _row(v):
    return v.reshape(1, -1).astype(F32)


def kernel(x, mem, norm_mix_g, w_in, gate_w2_fwd, gate_b_fwd, gate_w2_bwd, gate_b_bwd,
           gla_norm_g, conv_w, conv_b, conv_ln_g, conv_ln_b, w_out, norm_xattn_g,
           norm_mem_g, xattn_wq, xattn_wk, xattn_wv, xattn_wo, norm_ffn_g, router_w,
           router_b, exp_w_up, exp_b_up, exp_w_down, exp_b_down, final_norm_g):
    B, S, D = x.shape
    M = mem.shape[1]
    T = B * S
    depth = w_in.shape[0]
    kw = GLA_HEADS * GLA_DK
    vw = GLA_HEADS * GLA_DV
    rank_w = gate_w2_fwd.shape[1]
    cw = conv_w.shape[2]
    E = router_w.shape[2]
    assert vw + cw == w_out.shape[1] and 2 * rank_w <= LANES and E <= LANES

    xc = x.reshape(T, D)
    for l in range(depth):
        o_gf = 2 * kw + 2 * vw
        o_glu = o_gf + 2 * rank_w
        wl = w_in[l]
        w_gate = jnp.pad(wl[:, o_gf:o_glu], ((0, 0), (0, LANES - 2 * rank_w)))
        ws = [wl[:, :o_gf].astype(BF16), wl[:, o_glu:].astype(BF16), w_gate.astype(BF16)]
        w2 = jnp.zeros((LANES, 2 * kw), F32)
        w2 = w2.at[:rank_w, :kw].set(gate_w2_fwd[l]).at[rank_w:2 * rank_w, kw:].set(gate_w2_bwd[l])
        b2 = jnp.concatenate([gate_b_fwd[l], gate_b_bwd[l]]).reshape(1, -1)

        q, k, v, gr, la, u = _inproj(xc, _row(norm_mix_g[l]), ws, w2.astype(BF16), b2,
                                     kw, vw, cw, tm=512)
        a_out = _gla(q, k, v, la, gr, _row(gla_norm_g[l]), B, S)
        b_out = _conv(u, conv_w[l], _row(conv_b[l]), _row(conv_ln_g[l]), _row(conv_ln_b[l]),
                      B, S, tm=512)
        km, vm = _memkv(mem.reshape(B * M, D), _row(norm_mem_g[l]),
                        xattn_wk[l].astype(BF16), xattn_wv[l].astype(BF16), M)

        rw = jnp.zeros((D, LANES), F32).at[:, :E].set(router_w[l]).astype(BF16)
        rb = jnp.zeros((1, LANES), F32).at[0, :E].set(router_b[l])
        x2, h3, idx, gates = _mid(
            xc, a_out, b_out, w_out[l].astype(BF16), _row(norm_xattn_g[l]),
            xattn_wq[l].astype(BF16), km, vm, xattn_wo[l].astype(BF16),
            _row(norm_ffn_g[l]), rw, rb, S, M, E, tm=MOE_TILE)

        tm = MOE_TILE
        pos, tile_cnt = _place(idx, tm=tm)
        n_te = tile_cnt.reshape(T // tm, SUBLANES, LANES)[:, 0, :E].astype(jnp.int32)
        counts = jnp.sum(n_te, axis=0)
        padded = ((counts + MOE_ROWS - 1) // MOE_ROWS) * MOE_ROWS
        pends = jnp.cumsum(padded)
        run_row = (pends - padded)[None, :] + jnp.cumsum(n_te, axis=0) - n_te
        run_loc = jnp.cumsum(n_te, axis=1) - n_te
        slots = max(E, LANES // 2)
        runs = [jnp.pad(a, ((0, 0), (0, slots - E))).reshape(-1).astype(jnp.int32)
                for a in (run_row, n_te, run_loc)]
        n_blocks = (T * TOP_K + E * (MOE_ROWS - 1)) // MOE_ROWS
        blk_start = jnp.arange(n_blocks, dtype=jnp.int32) * MOE_ROWS
        block_e = jnp.minimum(jnp.sum(pends[None, :] <= blk_start[:, None], axis=1), E - 1)
        n_used = (pends[E - 1] // MOE_ROWS).reshape(1)

        pos, gates = pos.reshape(-1), gates.reshape(-1)
        xs = _dispatch(pos, *runs, (pends - padded + counts).astype(jnp.int32),
                       pends.astype(jnp.int32), n_used.astype(jnp.int32), h3,
                       n_blocks * MOE_ROWS, tm=tm)
        group = MOE_ROWS // MOE_PARTS
        real = jnp.clip((pends - padded + counts)[block_e] - blk_start, 0, MOE_ROWS)
        live = jnp.where(blk_start < pends[E - 1], (real + group - 1) // group, 0)
        ys = _experts(block_e.astype(jnp.int32), live.astype(jnp.int32), xs,
                      exp_w_up[l], exp_b_up[l][:, None, :],
                      exp_w_down[l], exp_b_down[l][:, None, :])
        xc = _combine(pos, gates, *runs, x2,
                      _row(final_norm_g), ys, tm=tm, n_exp=E, apply_norm=(l == depth - 1))
    return xc.reshape(B, S, D)
```

```python
import functools

import numpy as np
import jax
import jax.numpy as jnp
from jax import lax
from jax.experimental import pallas as pl
from jax.experimental.pallas import tpu as pltpu

F32 = jnp.float32
BF16 = jnp.bfloat16

GLA_HEADS = 4
GLA_DK = 64
GLA_DV = 128
GLA_CHUNK = 64
GLA_TILE = 4
GLA_UNROLL = 4
GATE_TAU = 16.0
CONV_KERNEL = 31
XATTN_HEADS = 4
TOP_K = 4
SWIGLU_ALPHA = 1.702
SWIGLU_LIMIT = 7.0
RMS_EPS = 1e-6
LN_EPS = 1e-5

LANES = 128
SUBLANES = 8
MOE_ROWS = 512
MOE_PARTS = 2
MOE_TILE = 512
VMEM_LIMIT = 56 * 1024 * 1024


def _dot(a, b):
    return jnp.dot(a, b, preferred_element_type=F32)


def _dot_nt(a, b):
    return lax.dot_general(a, b, (((1,), (1,)), ((), ())), preferred_element_type=F32)


def _dot_tn(a, b):
    return lax.dot_general(a, b, (((0,), (0,)), ((), ())), preferred_element_type=F32)


def _rms(x, g):
    return x * lax.rsqrt(jnp.mean(x * x, axis=-1, keepdims=True) + RMS_EPS) * g


def _params(sem):
    return pltpu.CompilerParams(dimension_semantics=sem, vmem_limit_bytes=VMEM_LIMIT)


def _store_rows_as_tiles(ref, val):
    n, d = val.shape
    assert d == SUBLANES * LANES and ref.shape == (n * SUBLANES, LANES)
    for s in range(SUBLANES):
        ref[pl.ds(s, n, stride=SUBLANES), :] = val[:, s * LANES:(s + 1) * LANES]


def _load_rows_from_tiles(ref):
    n = ref.shape[0] // SUBLANES
    return jnp.concatenate([ref[pl.ds(s, n, stride=SUBLANES), :] for s in range(SUBLANES)], axis=1)


def _inproj_kernel(x_ref, g_ref, w_qkvr_ref, w_glu_ref, w_gate_ref, w2_ref, b2_ref,
                   q_ref, k_ref, v_ref, gr_ref, la_ref, u_ref, *, kw, vw, cw, parts):
    n = x_ref.shape[0] // parts
    w_ref = (w_qkvr_ref, w_glu_ref, w_gate_ref)
    gens = [_inproj_rows(pl.ds(p * n, n), x_ref, g_ref, w_ref, w2_ref, b2_ref,
                         q_ref, k_ref, v_ref, gr_ref, la_ref, u_ref, kw, vw, cw)
            for p in range(parts)]
    for _ in zip(*gens):
        pass


def _inproj_rows(rows, x_ref, g_ref, w_ref, w2_ref, b2_ref,
                 q_ref, k_ref, v_ref, gr_ref, la_ref, u_ref, kw, vw, cw):
    w_qkvr_ref, w_glu_ref, w_gate_ref = w_ref
    h = _rms(x_ref[rows, :], g_ref[...]).astype(BF16)
    acc = _dot(h, w_qkvr_ref[...])
    glu = _dot(h, w_glu_ref[...])
    ranks = _dot(h, w_gate_ref[...])
    yield
    o = 0
    q_ref[rows, :] = (acc[:, o:o + kw] * (GLA_DK ** -0.5)).astype(BF16); o += kw
    k_ref[rows, :] = acc[:, o:o + kw].astype(BF16); o += kw
    v_ref[rows, :] = acc[:, o:o + vw].astype(BF16); o += vw
    r = acc[:, o:o + vw]
    gr_ref[rows, :] = (r * jax.nn.sigmoid(r)).astype(BF16)
    u_ref[rows, :] = glu[:, :cw] * jax.nn.sigmoid(glu[:, cw:])
    z = _dot(ranks.astype(BF16), w2_ref[...]) + b2_ref[...]
    yield
    la_ref[rows, :] = (jnp.minimum(z, 0.0) - jnp.log1p(jnp.exp(-jnp.abs(z)))) * (1.0 / GATE_TAU)
    yield


def _inproj(x2d, g, ws, w2, b2, kw, vw, cw, tm):
    T, D = x2d.shape
    row = lambda n: pl.BlockSpec((tm, n), lambda i: (i, 0))
    full = lambda a: pl.BlockSpec(a.shape, lambda i: (0,) * a.ndim)
    return pl.pallas_call(
        functools.partial(_inproj_kernel, kw=kw, vw=vw, cw=cw, parts=2),
        grid=(T // tm,),
        in_specs=[row(D), full(g)] + [full(w) for w in ws] + [full(w2), full(b2)],
        out_specs=[row(kw), row(kw), row(vw), row(vw), row(2 * kw), row(cw)],
        out_shape=[jax.ShapeDtypeStruct((T, kw), BF16), jax.ShapeDtypeStruct((T, kw), BF16),
                   jax.ShapeDtypeStruct((T, vw), BF16), jax.ShapeDtypeStruct((T, vw), BF16),
                   jax.ShapeDtypeStruct((T, 2 * kw), F32), jax.ShapeDtypeStruct((T, cw), F32)],
        compiler_params=_params(("parallel",)),
        name="inproj",
    )(x2d, g, *ws, w2, b2)


def _gla_levels():
    ms, m = [], GLA_CHUNK // 2
    while m >= GLA_TILE:
        ms.append(m)
        m //= 2
    return ms


def _gla_constants(bwd):
    C, H = GLA_CHUNK, GLA_HEADS
    t = np.arange(C)[:, None]
    s = np.arange(C)[None, :]
    lev_q, lev_mask = [], []
    for m in _gla_levels():
        g = 2 * m
        second = (t % g) >= m
        same = t // g == s // g
        if not bwd:
            qrow = second
            mask = same & ((t % g) >= m) & ((s % g) < m)
        else:
            qrow = ~second
            mask = same & ((t % g) < m) & ((s % g) >= m)
        lev_q.append(np.broadcast_to(qrow, (C, H * GLA_DK)))
        lev_mask.append(np.tile(mask, (1, H)))
    shifts = range(0, GLA_TILE) if not bwd else range(1, GLA_TILE)
    sh_mask = []
    for sh in shifts:
        j = t - sh if not bwd else t + sh
        sh_mask.append(np.tile((s == j) & (t // GLA_TILE == s // GLA_TILE), (1, H)))
    f = lambda xs: np.stack(xs).astype(np.float32)
    return f(lev_q), f(lev_mask), f(sh_mask)


def _gla_head_mask():
    C, H = GLA_CHUNK, GLA_HEADS
    assert GLA_DK == C
    blk = np.arange(H * C) // C
    return (blk[:, None] == blk[None, :]).astype(np.float32)


def _gla_chunk(la, q, k, v, st_ref, c_ref, tri_ref, lq_ref, lm_ref, sm_ref, hm_ref, bwd):
    C, H = GLA_CHUNK, GLA_HEADS
    kw = q.shape[1]
    hm = hm_ref[...]

    def stack(xb):
        return jnp.concatenate([xb] * H, axis=0) * hm

    tri = tri_ref[...]
    hi = la.astype(BF16)
    incl = _dot(tri, hi) + _dot(tri, (la - hi.astype(F32)).astype(BF16))
    yield None
    tot = incl[C - 1:C]
    cum = incl if not bwd else incl - la
    x_q, x_k = (cum, tot - cum) if not bwd else (tot - cum, cum)
    e_q, e_k, a_tot = jnp.exp(x_q), jnp.exp(x_k), jnp.exp(tot)
    c_ref[...] = cum

    a = jnp.zeros((C, H * C), F32)
    for lv, m in enumerate(_gla_levels()):
        g = 2 * m
        edge = m - 1 if not bwd else m
        r = jnp.concatenate([jnp.broadcast_to(c_ref[s0 + edge:s0 + edge + 1, :], (g, kw))
                             for s0 in range(0, C, g)], axis=0)
        d = cum - r
        z = (jnp.where(lq_ref[lv] > 0, q, k) * jnp.exp(jnp.minimum(d, -d))).astype(BF16)
        a = a + _dot_nt(z, stack(z)) * lm_ref[lv]
    ps = []
    for si in range(sm_ref.shape[0]):
        sh = si if not bwd else si + 1
        if sh == 0:
            ps.append((q * k).astype(BF16))
            continue
        amt = sh if not bwd else C - sh
        kr = pltpu.roll(k, amt, 0)
        cr = pltpu.roll(cum, amt, 0)
        d = (cum - cr) if not bwd else (cr - cum)
        ps.append((q * kr * jnp.exp(jnp.minimum(d, 0.0))).astype(BF16))
    sc = _dot(jnp.concatenate(ps, axis=0), hm)
    yield None
    for si in range(sm_ref.shape[0]):
        a = a + sc[si * C:(si + 1) * C] * sm_ref[si]

    vst = jnp.concatenate([v[:, h * GLA_DV:(h + 1) * GLA_DV] for h in range(H)], axis=0)
    st = st_ref[...]
    o4 = _dot(stack(a.astype(BF16)), vst) + _dot_nt(stack((q * e_q).astype(BF16)), st.astype(BF16))
    st_ref[...] = st * a_tot + _dot_tn(vst, stack((k * e_k).astype(BF16)))
    yield jnp.concatenate([o4[h * C:(h + 1) * C] for h in range(H)], axis=1)


def _gla_kernel(q_ref, k_ref, v_ref, la_ref, gr_ref, ng_ref,
                lqf_ref, lmf_ref, smf_ref, lqb_ref, lmb_ref, smb_ref, tri_ref, hm_ref,
                out_ref, of_ref, ob_ref, stf_ref, stb_ref, cf_ref, cb_ref):
    C = GLA_CHUNK
    S = q_ref.shape[0]
    kw = q_ref.shape[1]
    n = S // C
    stf_ref[...] = jnp.zeros_like(stf_ref)
    stb_ref[...] = jnp.zeros_like(stb_ref)

    def body(i, carry):
        streams, outs = [], []
        for u in range(GLA_UNROLL):
            rf = pl.ds(pl.multiple_of((i * GLA_UNROLL + u) * C, C), C)
            rb = pl.ds(pl.multiple_of((n - 1 - i * GLA_UNROLL - u) * C, C), C)
            streams.append(_gla_chunk(
                la_ref[rf, 0:kw], q_ref[rf, :].astype(F32), k_ref[rf, :].astype(F32),
                v_ref[rf, :], stf_ref, cf_ref.at[u], tri_ref, lqf_ref, lmf_ref, smf_ref, hm_ref,
                False))
            streams.append(_gla_chunk(
                la_ref[rb, kw:2 * kw], q_ref[rb, :].astype(F32), k_ref[rb, :].astype(F32),
                v_ref[rb, :], stb_ref, cb_ref.at[u], tri_ref, lqb_ref, lmb_ref, smb_ref, hm_ref,
                True))
            outs += [(of_ref, rf), (ob_ref, rb)]
        for results in zip(*streams):
            pass
        for (ref, rows), o in zip(outs, results):
            ref[rows, :] = o
        return carry

    lax.fori_loop(0, n // GLA_UNROLL, body, 0)

    rows = 256
    def fin(i, carry):
        r = pl.ds(pl.multiple_of(i * rows, rows), rows)
        o = of_ref[r, :] + ob_ref[r, :]
        ys = []
        for h in range(GLA_HEADS):
            oh = o[:, h * GLA_DV:(h + 1) * GLA_DV]
            ys.append(_rms(oh, ng_ref[...]))
        y = jnp.concatenate(ys, axis=1) * gr_ref[r, :].astype(F32)
        out_ref[r, :] = y.astype(BF16)
        return carry

    lax.fori_loop(0, S // rows, fin, 0)


def _gla(q, k, v, la, gr, ng, B, S):
    kw, vw = q.shape[1], v.shape[1]
    cf = _gla_constants(False)
    cb = _gla_constants(True)
    tri = np.tril(np.ones((GLA_CHUNK, GLA_CHUNK), np.float32))
    consts = [jnp.asarray(c) for c in cf] + [jnp.asarray(c) for c in cb] + \
             [jnp.asarray(tri, BF16), jnp.asarray(_gla_head_mask(), BF16)]
    seq = lambda n: pl.BlockSpec((S, n), lambda b: (b, 0))
    full = lambda a: pl.BlockSpec(a.shape, lambda b: (0,) * a.ndim)
    return pl.pallas_call(
        _gla_kernel,
        grid=(B,),
        in_specs=[seq(kw), seq(kw), seq(vw), seq(2 * kw), seq(vw), full(ng)] + [full(c) for c in consts],
        out_specs=seq(vw),
        out_shape=jax.ShapeDtypeStruct((B * S, vw), BF16),
        scratch_shapes=[pltpu.VMEM((S, vw), F32), pltpu.VMEM((S, vw), F32),
                        pltpu.VMEM((GLA_DV, kw), F32), pltpu.VMEM((GLA_DV, kw), F32),
                        pltpu.VMEM((GLA_UNROLL, GLA_CHUNK, kw), F32),
                        pltpu.VMEM((GLA_UNROLL, GLA_CHUNK, kw), F32)],
        compiler_params=_params(("parallel",)),
        name="gla",
    )(q, k, v, la, gr, ng, *consts)


def _conv_kernel(u_ref, w_ref, cb_ref, lg_ref, lb_ref, out_ref, *, tm, halo):
    S = u_ref.shape[0]
    j = pl.program_id(1)
    t0 = pl.multiple_of(j * tm, tm)
    top = u_ref[pl.ds(pl.multiple_of(jnp.maximum(t0 - halo, 0), halo), halo), :]
    bot = u_ref[pl.ds(pl.multiple_of(jnp.minimum(t0 + tm, S - halo), halo), halo), :]
    top = jnp.where(j > 0, top, 0.0)
    bot = jnp.where(j < pl.num_programs(1) - 1, bot, 0.0)
    win = jnp.concatenate([top, u_ref[pl.ds(t0, tm), :], bot], axis=0)
    rows = tm + 2 * halo
    pad = (CONV_KERNEL - 1) // 2
    w = w_ref[...]
    acc = jnp.zeros((tm, u_ref.shape[1]), F32)
    for r in range(8):
        offs = [k for k in range(CONV_KERNEL) if (halo - pad + k) % 8 == r]
        if not offs:
            continue
        xr = win if r == 0 else pltpu.roll(win, rows - r, 0)
        for k in offs:
            a0 = (halo - pad + k) - r
            acc = acc + xr[a0:a0 + tm, :] * w[k:k + 1, :]
    y = acc + cb_ref[...]
    mu = jnp.mean(y, axis=-1, keepdims=True)
    yc = y - mu
    var = jnp.mean(yc * yc, axis=-1, keepdims=True)
    z = yc * lax.rsqrt(var + LN_EPS) * lg_ref[...] + lb_ref[...]
    out_ref[...] = (z * jax.nn.sigmoid(z)).astype(BF16)


def _conv(u, w, cb, lg, lb, B, S, tm):
    cw = u.shape[1]
    halo = 16
    full = lambda a: pl.BlockSpec(a.shape, lambda b, j: (0,) * a.ndim)
    nj = S // tm
    return pl.pallas_call(
        functools.partial(_conv_kernel, tm=tm, halo=halo),
        grid=(B, nj),
        in_specs=[pl.BlockSpec((S, cw), lambda b, j: (b, 0)), full(w), full(cb), full(lg), full(lb)],
        out_specs=pl.BlockSpec((tm, cw), lambda b, j: (b * nj + j, 0)),
        out_shape=jax.ShapeDtypeStruct((B * S, cw), BF16),
        compiler_params=_params(("parallel", "parallel")),
        name="conv",
    )(u, w, cb, lg, lb)


def _memkv_kernel(m_ref, g_ref, wk_ref, wv_ref, k_ref, v_ref):
    m = _rms(m_ref[...], g_ref[...]).astype(BF16)
    k_ref[...] = _dot(m, wk_ref[...]).astype(BF16)
    v_ref[...] = _dot(m, wv_ref[...]).astype(BF16)


def _memkv(mem2d, g, wk, wv, M):
    R, D = mem2d.shape
    full = lambda a: pl.BlockSpec(a.shape, lambda i: (0,) * a.ndim)
    blk = pl.BlockSpec((M, D), lambda i: (i, 0))
    return pl.pallas_call(
        _memkv_kernel,
        grid=(R // M,),
        in_specs=[blk, full(g), full(wk), full(wv)],
        out_specs=[blk, blk],
        out_shape=[jax.ShapeDtypeStruct((R, D), BF16)] * 2,
        compiler_params=_params(("parallel",)),
        name="memkv",
    )(mem2d, g, wk, wv)


def _mid_kernel(x_ref, a_ref, b_ref, wout_ref, gx_ref, wq_ref, km_ref, vm_ref, wo_ref,
                gf_ref, rw_ref, rb_ref, x2_ref, h3_ref, idx_ref, gate_ref, *, n_exp, parts):
    n = x_ref.shape[0] // parts
    gens = [_mid_rows(pl.ds(p * n, n), pl.ds(p * n * SUBLANES, n * SUBLANES),
                      x_ref, a_ref, b_ref, wout_ref, gx_ref, wq_ref, km_ref, vm_ref, wo_ref,
                      gf_ref, rw_ref, rb_ref, x2_ref, h3_ref, idx_ref, gate_ref, n_exp)
            for p in range(parts)]
    for _ in zip(*gens):
        pass


def _mid_rows(rows, tiles, x_ref, a_ref, b_ref, wout_ref, gx_ref, wq_ref, km_ref, vm_ref, wo_ref,
              gf_ref, rw_ref, rb_ref, x2_ref, h3_ref, idx_ref, gate_ref, n_exp):
    ab = jnp.concatenate([a_ref[rows, :], b_ref[rows, :]], axis=1)
    x1 = x_ref[rows, :] + _dot(ab, wout_ref[...])
    yield
    q2 = _dot(_rms(x1, gx_ref[...]).astype(BF16), wq_ref[...])
    yield
    D = q2.shape[1]
    dh = D // XATTN_HEADS
    ss = []
    for h in range(XATTN_HEADS):
        sl = slice(h * dh, (h + 1) * dh)
        ss.append(_dot_nt(q2[:, sl].astype(BF16), km_ref[:, sl]) * (dh ** -0.5))
    yield
    outs = []
    for h in range(XATTN_HEADS):
        sl = slice(h * dh, (h + 1) * dh)
        p = jnp.exp(ss[h] - jnp.max(ss[h], axis=-1, keepdims=True))
        p = p / jnp.sum(p, axis=-1, keepdims=True)
        outs.append(_dot(p.astype(BF16), vm_ref[:, sl]))
    yield
    o = jnp.concatenate(outs, axis=1).astype(BF16)
    x2 = x1 + _dot(o, wo_ref[...])
    yield
    x2_ref[rows, :] = x2
    h3 = _rms(x2, gf_ref[...])
    _store_rows_as_tiles(h3_ref.at[tiles], h3)
    lane = lax.broadcasted_iota(jnp.int32, (x2.shape[0], LANES), 1)
    logits = _dot(h3.astype(BF16), rw_ref[...]) + rb_ref[...]
    yield
    cur = jnp.where(lane < n_exp, logits, -jnp.inf)
    lane_f = lane.astype(F32)
    vals, idxs = [], []
    for _ in range(TOP_K):
        m = jnp.max(cur, axis=-1, keepdims=True)
        ik = jnp.min(jnp.where(cur == m, lane_f, float(LANES)), axis=-1, keepdims=True)
        vals.append(m)
        idxs.append(ik)
        cur = jnp.where(lane_f == ik, -jnp.inf, cur)
    es = [jnp.exp(vk - vals[0]) for vk in vals]
    den = es[0]
    for ek in es[1:]:
        den = den + ek
    idx_out = jnp.zeros(lane.shape, F32)
    gate_out = jnp.zeros(lane.shape, F32)
    for kk in range(TOP_K):
        idx_out = jnp.where(lane == kk, idxs[kk], idx_out)
        gate_out = jnp.where(lane == kk, es[kk] / den, gate_out)
    idx_ref[rows, :] = idx_out.astype(jnp.int32)
    gate_ref[:, rows] = gate_out.T[0:SUBLANES, :]
    yield


def _mid(x2d, a, b, wout, gx, wq, km, vm, wo, gf, rw, rb, S, M, n_exp, tm):
    T, D = x2d.shape
    half = a.shape[1]
    per_b = S // tm
    row = lambda n: pl.BlockSpec((tm, n), lambda i: (i, 0))
    full = lambda arr: pl.BlockSpec(arr.shape, lambda i: (0,) * arr.ndim)
    memb = pl.BlockSpec((M, D), lambda i: (i // per_b, 0))
    return pl.pallas_call(
        functools.partial(_mid_kernel, n_exp=n_exp, parts=2),
        grid=(T // tm,),
        in_specs=[row(D), row(half), row(half), full(wout), full(gx), full(wq), memb, memb,
                  full(wo), full(gf), full(rw), full(rb)],
        out_specs=[row(D), pl.BlockSpec((tm * SUBLANES, LANES), lambda i: (i, 0)),
                   row(LANES), pl.BlockSpec((SUBLANES, tm), lambda i: (i // 2, i % 2))],
        out_shape=[jax.ShapeDtypeStruct((T, D), F32),
                   jax.ShapeDtypeStruct((T * SUBLANES, LANES), F32),
                   jax.ShapeDtypeStruct((T, LANES), jnp.int32),
                   jax.ShapeDtypeStruct((T // (2 * tm) * SUBLANES, 2 * tm), F32)],
        compiler_params=_params(("parallel",)),
        name="mid",
    )(x2d, a, b, wout, gx, wq, km, vm, wo, gf, rw, rb)


def _place_kernel(idx_ref, tri_ref, upper_ref, pos_ref, cnt_ref):
    tm = idx_ref.shape[0] // 2
    for _ in zip(*[_place_tile(half, tm, idx_ref, tri_ref, upper_ref, pos_ref, cnt_ref)
                   for half in range(2)]):
        pass


def _place_tile(half, tm, idx_ref, tri_ref, upper_ref, pos_ref, cnt_ref):
    rows = pl.ds(half * tm, tm)
    lane = lax.broadcasted_iota(jnp.int32, (tm, LANES), 1)
    hots = [(lane == idx_ref[rows, kk:kk + 1]).astype(F32) for kk in range(TOP_K)]
    mask = hots[0]
    for hk in hots[1:]:
        mask = mask + hk
    incl = _dot(tri_ref[...], mask.astype(BF16))
    yield
    cnt = jnp.broadcast_to(incl[tm - 1:tm, :], (SUBLANES, LANES))
    hi = jnp.floor(cnt * (1.0 / 16.0))
    lo = cnt - 16.0 * hi
    start = 16.0 * _dot(hi.astype(BF16), upper_ref[...]) + _dot(lo.astype(BF16), upper_ref[...])
    yield
    before = incl - mask + start[0:1, :]
    out = jnp.zeros((tm, LANES), F32)
    for kk in range(TOP_K):
        rk = jnp.sum(hots[kk] * before, axis=-1, keepdims=True)
        out = jnp.where(lane == kk, rk, out)
    pos_ref[:, rows] = (out.T[0:SUBLANES, :] * SUBLANES).astype(jnp.int32)
    cnt_ref[pl.ds(half * SUBLANES, SUBLANES), :] = cnt
    yield


def _place(idx, tm):
    T = idx.shape[0]
    assert tm <= 16 * 256
    tri = jnp.asarray(np.tril(np.ones((tm, tm), np.float32)), BF16)
    upper = jnp.asarray(np.triu(np.ones((LANES, LANES), np.float32), 1), BF16)
    return pl.pallas_call(
        _place_kernel,
        grid=(T // (2 * tm),),
        in_specs=[pl.BlockSpec((2 * tm, LANES), lambda i: (i, 0)),
                  pl.BlockSpec((tm, tm), lambda i: (0, 0)),
                  pl.BlockSpec((LANES, LANES), lambda i: (0, 0))],
        out_specs=[pl.BlockSpec((SUBLANES, 2 * tm), lambda i: (i, 0)),
                   pl.BlockSpec((2 * SUBLANES, LANES), lambda i: (i, 0))],
        out_shape=[jax.ShapeDtypeStruct((T // (2 * tm) * SUBLANES, 2 * tm), jnp.int32),
                   jax.ShapeDtypeStruct((T // tm * SUBLANES, LANES), F32)],
        compiler_params=_params(("parallel",)),
        name="place",
    )(idx, tri, upper)


ROW_UNROLL = 8


def _wait_rows(buf_ref, sem):
    pltpu.make_async_copy(buf_ref, buf_ref, sem).wait()


RUN_CHUNK = 32


def _rows(ref, row, n):
    return ref.at[pl.ds(pl.multiple_of(row * SUBLANES, SUBLANES), n * SUBLANES)]


def _for_run_pieces(n, fn):
    shift = RUN_CHUNK.bit_length() - 1
    whole = lax.shift_right_logical(n, shift)

    def body(c, carry):
        fn(c * RUN_CHUNK, RUN_CHUNK)
        return carry

    lax.fori_loop(0, whole, body, 0)
    off = whole * RUN_CHUNK
    for bit in reversed(range(shift)):
        take = (n & (1 << bit)) != 0

        @pl.when(take)
        def _(off=off, bit=bit):
            fn(off, 1 << bit)

        off = off + jnp.where(take, 1 << bit, 0)


def _dispatch_kernel(pos_ref, rdst_ref, rn_ref, rloc_ref, lo_ref, hi_ref, nb_ref, h_ref, xs_ref,
                     sb0, sb1, zero_ref, sem, zsem):
    j = pl.program_id(0)
    tm = h_ref.shape[0] // SUBLANES // 2
    slots = rn_ref.shape[0] // 2
    n_exp = lo_ref.shape[0]
    blk = zero_ref.shape[0]

    def sort(half, sb):
        def body(i, carry):
            for u in range(ROW_UNROLL):
                t = half * tm + i * ROW_UNROLL + u
                tile = h_ref[pl.ds(pl.multiple_of(t * SUBLANES, SUBLANES), SUBLANES), :]
                for kk in range(TOP_K):
                    p = pos_ref[kk * 2 * tm + t]
                    sb[pl.ds(pl.multiple_of(p, SUBLANES), SUBLANES), :] = tile
            return carry

        lax.fori_loop(0, tm // ROW_UNROLL, body, 0)

    def start_runs(half, sb, s):
        def expert(e, carry):
            r = half * slots + e
            loc, dst = rloc_ref[r], rdst_ref[r]
            _for_run_pieces(rn_ref[r], lambda off, n: pltpu.make_async_copy(
                _rows(sb, loc + off, n), _rows(xs_ref, dst + off, n), s).start())
            return carry

        lax.fori_loop(0, n_exp, expert, 0)

    @pl.when(j == 0)
    def _():
        zero_ref[...] = jnp.zeros_like(zero_ref)

        def each_fill(fn):
            def expert(e, carry):
                row = lo_ref[e]
                n = hi_ref[e] - row
                for bit in reversed(range(MOE_ROWS.bit_length() - 1)):
                    size = (1 << bit) * SUBLANES
                    take = (n & (1 << bit)) != 0

                    @pl.when(take)
                    def _(row=row, size=size):
                        dst = xs_ref.at[pl.ds(pl.multiple_of(row * SUBLANES, SUBLANES), size)]
                        fn(pltpu.make_async_copy(zero_ref.at[pl.ds(0, size)], dst, zsem))

                    row = row + jnp.where(take, 1 << bit, 0)
                return carry

            lax.fori_loop(0, lo_ref.shape[0], expert, 0)

            def block(b, carry):
                fn(pltpu.make_async_copy(
                    zero_ref, xs_ref.at[pl.ds(pl.multiple_of(b * blk, blk), blk)], zsem))
                return carry

            lax.fori_loop(nb_ref[0], xs_ref.shape[0] // blk, block, 0)

        each_fill(lambda cp: cp.start())
        each_fill(lambda cp: cp.wait())

    for half, sb in ((0, sb0), (1, sb1)):
        @pl.when(j > 0)
        def _(half=half, sb=sb):
            _wait_rows(sb, sem.at[half])

        sort(half, sb)
        start_runs(half, sb, sem.at[half])

    @pl.when(j == pl.num_programs(0) - 1)
    def _():
        _wait_rows(sb0, sem.at[0])
        _wait_rows(sb1, sem.at[1])


def _dispatch(pos, run_dst, run_n, run_loc, pad_lo, pad_hi, n_used, h3, n_rows, tm):
    T = h3.shape[0] // SUBLANES
    slots = run_n.shape[0] // (T // tm)
    smem = pl.BlockSpec(memory_space=pltpu.SMEM)
    per_step = lambda n: pl.BlockSpec((n,), lambda j: (j,), memory_space=pltpu.SMEM)
    sorted_rows = pltpu.VMEM((tm * TOP_K * SUBLANES, LANES), F32)
    return pl.pallas_call(
        _dispatch_kernel,
        grid=(T // (2 * tm),),
        in_specs=[per_step(SUBLANES * 2 * tm),
                  per_step(2 * slots), per_step(2 * slots),
                  per_step(2 * slots), smem, smem, smem,
                  pl.BlockSpec((2 * tm * SUBLANES, LANES), lambda j: (j, 0))],
        out_specs=pl.BlockSpec(memory_space=pl.ANY),
        out_shape=jax.ShapeDtypeStruct((n_rows * SUBLANES, LANES), F32),
        scratch_shapes=[sorted_rows, sorted_rows, pltpu.VMEM((MOE_ROWS * SUBLANES, LANES), F32),
                        pltpu.SemaphoreType.DMA((2,)), pltpu.SemaphoreType.DMA(())],
        compiler_params=_params(("arbitrary",)),
        name="dispatch",
    )(pos, run_dst, run_n, run_loc, pad_lo, pad_hi, n_used, h3)


def _experts_kernel(be_ref, live_ref, first_ref, next_ref, slot_ref,
                    xs_ref, wu_hbm, bu_ref, wd_hbm, bd_ref, ys_ref,
                    wu_f32, wd_f32, wu_bf, wd_bf, sem):
    i = pl.program_id(0)

    def fetch(e, s):
        return (pltpu.make_async_copy(wu_hbm.at[e], wu_f32.at[s], sem.at[s]),
                pltpu.make_async_copy(wd_hbm.at[e], wd_f32.at[s], sem.at[s]))

    @pl.when(i == 0)
    def _():
        for cp in fetch(be_ref[0], 0):
            cp.start()

    @pl.when(first_ref[i] == 1)
    def _():
        s = slot_ref[i]
        for cp in fetch(be_ref[i], s):
            cp.wait()

        @pl.when(next_ref[i] >= 0)
        def _():
            for cp in fetch(next_ref[i], 1 - s):
                cp.start()

        wu_bf[...] = wu_f32[s].astype(BF16)
        wd_bf[...] = wd_f32[s].astype(BF16)

    def mlp(tiles):
        F = wd_bf.shape[0]
        x = _load_rows_from_tiles(xs_ref.at[tiles]).astype(BF16)
        gu = _dot(x, wu_bf[...]) + bu_ref[0]
        yield
        x_glu = jnp.minimum(gu[:, :F], SWIGLU_LIMIT)
        x_lin = jnp.clip(gu[:, F:], -SWIGLU_LIMIT, SWIGLU_LIMIT)
        hid = x_glu * jax.nn.sigmoid(SWIGLU_ALPHA * x_glu) * (x_lin + 1.0)
        y = _dot(hid.astype(BF16), wd_bf[...]) + bd_ref[0]
        yield
        _store_rows_as_tiles(ys_ref.at[tiles], y)
        yield

    n = xs_ref.shape[0] // MOE_PARTS
    for live in range(MOE_PARTS + 1):
        @pl.when(live_ref[i] == live)
        def _(live=live):
            for _ in zip(*[mlp(pl.ds(p * n, n)) for p in range(live)]):
                pass
            if live < MOE_PARTS:
                rest = (MOE_PARTS - live) * n
                ys_ref[pl.ds(live * n, rest), :] = jnp.zeros((rest, ys_ref.shape[1]), F32)


def _expert_runs(block_e):
    nb = block_e.shape[0]
    ar = jnp.arange(nb, dtype=jnp.int32)
    first = jnp.concatenate([jnp.ones((1,), bool), block_e[1:] != block_e[:-1]])
    starts = jnp.where(first, ar, nb)
    later = lax.cummin(jnp.concatenate([starts[1:], jnp.full((1,), nb, jnp.int32)]), reverse=True)
    nxt = jnp.where(later < nb, block_e[jnp.minimum(later, nb - 1)], -1)
    slot = (jnp.cumsum(first.astype(jnp.int32)) - 1) & 1
    return first.astype(jnp.int32), nxt.astype(jnp.int32), slot.astype(jnp.int32)


def _experts(block_e, live, xs, wu, bu, wd, bd):
    E, D, F2 = wu.shape
    F = wd.shape[1]
    nb = xs.shape[0] // (MOE_ROWS * SUBLANES)
    rows = pl.BlockSpec((MOE_ROWS * SUBLANES, LANES), lambda i, *_: (i, 0))
    grid_spec = pltpu.PrefetchScalarGridSpec(
        num_scalar_prefetch=5,
        grid=(nb,),
        in_specs=[rows,
                  pl.BlockSpec(memory_space=pl.ANY),
                  pl.BlockSpec((1, 1, F2), lambda i, be, *_: (be[i], 0, 0)),
                  pl.BlockSpec(memory_space=pl.ANY),
                  pl.BlockSpec((1, 1, D), lambda i, be, *_: (be[i], 0, 0))],
        out_specs=rows,
        scratch_shapes=[pltpu.VMEM((2, D, F2), F32), pltpu.VMEM((2, F, D), F32),
                        pltpu.VMEM((D, F2), BF16), pltpu.VMEM((F, D), BF16),
                        pltpu.SemaphoreType.DMA((2,))],
    )
    return pl.pallas_call(
        _experts_kernel,
        grid_spec=grid_spec,
        out_shape=jax.ShapeDtypeStruct(xs.shape, F32),
        compiler_params=_params(("arbitrary",)),
        name="experts",
    )(block_e, live, *_expert_runs(block_e), xs, wu, bu, wd, bd)


def _combine_kernel(pos_ref, gate_ref, src_ref, rn_ref, rloc_ref, nsrc_ref, nrn_ref, nrloc_ref,
                    x2_ref, g_ref, ys_ref, out_ref, yb0, yb1, acc_ref, sem, *, n_exp, apply_norm):
    j = pl.program_id(0)
    tm = x2_ref.shape[0] // 2
    slots = rn_ref.shape[0] // 2

    def start(tbls, half, yb, s):
        src_t, n_t, loc_t = tbls

        def expert(e, carry):
            r = half * slots + e
            loc, src = loc_t[r], src_t[r]
            _for_run_pieces(n_t[r], lambda off, n: pltpu.make_async_copy(
                _rows(ys_ref, src + off, n), _rows(yb, loc + off, n), s).start())
            return carry

        lax.fori_loop(0, n_exp, expert, 0)

    def finish(half, yb, s):
        _wait_rows(yb, s)

        def body(i, carry):
            for u in range(ROW_UNROLL):
                t = i * ROW_UNROLL + u
                a = half * tm + t
                acc = None
                for kk in range(TOP_K):
                    p = pos_ref[kk * 2 * tm + a]
                    term = yb[pl.ds(pl.multiple_of(p, SUBLANES), SUBLANES), :] \
                        * gate_ref[kk * 2 * tm + a]
                    acc = term if acc is None else acc + term
                acc_ref[pl.ds(pl.multiple_of(t * SUBLANES, SUBLANES), SUBLANES), :] = acc
            return carry

        lax.fori_loop(0, tm // ROW_UNROLL, body, 0)
        rows = pl.ds(half * tm, tm)
        x = x2_ref[rows, :] + _load_rows_from_tiles(acc_ref)
        out_ref[rows, :] = _rms(x, g_ref[...]) if apply_norm else x

    cur = (src_ref, rn_ref, rloc_ref)

    @pl.when(j == 0)
    def _():
        start(cur, 0, yb0, sem.at[0])

    start(cur, 1, yb1, sem.at[1])
    finish(0, yb0, sem.at[0])

    @pl.when(j < pl.num_programs(0) - 1)
    def _():
        start((nsrc_ref, nrn_ref, nrloc_ref), 0, yb0, sem.at[0])

    finish(1, yb1, sem.at[1])


def _combine(pos, gates, run_src, run_n, run_loc, x2, g, ys, tm, n_exp, apply_norm):
    T, D = x2.shape
    steps = T // (2 * tm)
    slots = run_n.shape[0] // (T // tm)
    per_step = lambda n: pl.BlockSpec((n,), lambda j: (j,), memory_space=pltpu.SMEM)
    next_step = lambda n: pl.BlockSpec((n,), lambda j: (jnp.minimum(j + 1, steps - 1),),
                                       memory_space=pltpu.SMEM)
    sorted_rows = pltpu.VMEM((tm * TOP_K * SUBLANES, LANES), F32)
    return pl.pallas_call(
        functools.partial(_combine_kernel, n_exp=n_exp, apply_norm=apply_norm),
        grid=(steps,),
        in_specs=[per_step(SUBLANES * 2 * tm),
                  per_step(SUBLANES * 2 * tm),
                  per_step(2 * slots), per_step(2 * slots), per_step(2 * slots),
                  next_step(2 * slots), next_step(2 * slots), next_step(2 * slots),
                  pl.BlockSpec((2 * tm, D), lambda j: (j, 0)),
                  pl.BlockSpec(g.shape, lambda j: (0, 0)),
                  pl.BlockSpec(memory_space=pl.ANY)],
        out_specs=pl.BlockSpec((2 * tm, D), lambda j: (j, 0)),
        out_shape=jax.ShapeDtypeStruct((T, D), F32),
        scratch_shapes=[sorted_rows, sorted_rows, pltpu.VMEM((tm * SUBLANES, LANES), F32),
                        pltpu.SemaphoreType.DMA((2,))],
        compiler_params=_params(("arbitrary",)),
        name="combine",
    )(pos, gates, run_src, run_n, run_loc, run_src, run_n, run_loc, x2, g, ys)


def _row(v):
    return v.reshape(1, -1).astype(F32)


def kernel(x, mem, norm_mix_g, w_in, gate_w2_fwd, gate_b_fwd, gate_w2_bwd, gate_b_bwd,
           gla_norm_g, conv_w, conv_b, conv_ln_g, conv_ln_b, w_out, norm_xattn_g,
           norm_mem_g, xattn_wq, xattn_wk, xattn_wv, xattn_wo, norm_ffn_g, router_w,
           router_b, exp_w_up, exp_b_up, exp_w_down, exp_b_down, final_norm_g):
    B, S, D = x.shape
    M = mem.shape[1]
    T = B * S
    depth = w_in.shape[0]
    kw = GLA_HEADS * GLA_DK
    vw = GLA_HEADS * GLA_DV
    rank_w = gate_w2_fwd.shape[1]
    cw = conv_w.shape[2]
    E = router_w.shape[2]
    assert vw + cw == w_out.shape[1] and 2 * rank_w <= LANES and E <= LANES

    xc = x.reshape(T, D)
    for l in range(depth):
        o_gf = 2 * kw + 2 * vw
        o_glu = o_gf + 2 * rank_w
        wl = w_in[l]
        w_gate = jnp.pad(wl[:, o_gf:o_glu], ((0, 0), (0, LANES - 2 * rank_w)))
        ws = [wl[:, :o_gf].astype(BF16), wl[:, o_glu:].astype(BF16), w_gate.astype(BF16)]
        w2 = jnp.zeros((LANES, 2 * kw), F32)
        w2 = w2.at[:rank_w, :kw].set(gate_w2_fwd[l]).at[rank_w:2 * rank_w, kw:].set(gate_w2_bwd[l])
        b2 = jnp.concatenate([gate_b_fwd[l], gate_b_bwd[l]]).reshape(1, -1)

        q, k, v, gr, la, u = _inproj(xc, _row(norm_mix_g[l]), ws, w2.astype(BF16), b2,
                                     kw, vw, cw, tm=512)
        a_out = _gla(q, k, v, la, gr, _row(gla_norm_g[l]), B, S)
        b_out = _conv(u, conv_w[l], _row(conv_b[l]), _row(conv_ln_g[l]), _row(conv_ln_b[l]),
                      B, S, tm=512)
        km, vm = _memkv(mem.reshape(B * M, D), _row(norm_mem_g[l]),
                        xattn_wk[l].astype(BF16), xattn_wv[l].astype(BF16), M)

        rw = jnp.zeros((D, LANES), F32).at[:, :E].set(router_w[l]).astype(BF16)
        rb = jnp.zeros((1, LANES), F32).at[0, :E].set(router_b[l])
        x2, h3, idx, gates = _mid(
            xc, a_out, b_out, w_out[l].astype(BF16), _row(norm_xattn_g[l]),
            xattn_wq[l].astype(BF16), km, vm, xattn_wo[l].astype(BF16),
            _row(norm_ffn_g[l]), rw, rb, S, M, E, tm=MOE_TILE)

        tm = MOE_TILE
        pos, tile_cnt = _place(idx, tm=tm)
        n_te = tile_cnt.reshape(T // tm, SUBLANES, LANES)[:, 0, :E].astype(jnp.int32)
        counts = jnp.sum(n_te, axis=0)
        padded = ((counts + MOE_ROWS - 1) // MOE_ROWS) * MOE_ROWS
        pends = jnp.cumsum(padded)
        run_row = (pends - padded)[None, :] + jnp.cumsum(n_te, axis=0) - n_te
        run_loc = jnp.cumsum(n_te, axis=1) - n_te
        slots = max(E, LANES // 2)
        runs = [jnp.pad(a, ((0, 0), (0, slots - E))).reshape(-1).astype(jnp.int32)
                for a in (run_row, n_te, run_loc)]
        n_blocks = (T * TOP_K + E * (MOE_ROWS - 1)) // MOE_ROWS
        blk_start = jnp.arange(n_blocks, dtype=jnp.int32) * MOE_ROWS
        block_e = jnp.minimum(jnp.sum(pends[None, :] <= blk_start[:, None], axis=1), E - 1)
        n_used = (pends[E - 1] // MOE_ROWS).reshape(1)

        pos, gates = pos.reshape(-1), gates.reshape(-1)
        xs = _dispatch(pos, *runs, (pends - padded + counts).astype(jnp.int32),
                       pends.astype(jnp.int32), n_used.astype(jnp.int32), h3,
                       n_blocks * MOE_ROWS, tm=tm)
        group = MOE_ROWS // MOE_PARTS
        real = jnp.clip((pends - padded + counts)[block_e] - blk_start, 0, MOE_ROWS)
        live = jnp.where(blk_start < pends[E - 1], (real + group - 1) // group, 0)
        ys = _experts(block_e.astype(jnp.int32), live.astype(jnp.int32), xs,
                      exp_w_up[l], exp_b_up[l][:, None, :],
                      exp_w_down[l], exp_b_down[l][:, None, :])
        xc = _combine(pos, gates, *runs, x2,
                      _row(final_norm_g), ys, tm=tm, n_exp=E, apply_norm=(l == depth - 1))
    return xc.reshape(B, S, D)
```

```python
import functools

import numpy as np
import jax
import jax.numpy as jnp
from jax import lax
from jax.experimental import pallas as pl
from jax.experimental.pallas import tpu as pltpu

F32 = jnp.float32
BF16 = jnp.bfloat16

GLA_HEADS = 4
GLA_DK = 64
GLA_DV = 128
GLA_CHUNK = 64
GLA_TILE = 4
GLA_UNROLL = 4
GATE_TAU = 16.0
LOG2_E = 1.4426950408889634
CONV_KERNEL = 31
XATTN_HEADS = 4
TOP_K = 4
SWIGLU_ALPHA = 1.702
SWIGLU_LIMIT = 7.0
RMS_EPS = 1e-6
LN_EPS = 1e-5

LANES = 128
SUBLANES = 8
MOE_ROWS = 512
MOE_PARTS = 2
MOE_TILE = 512
VMEM_LIMIT = 56 * 1024 * 1024


def _dot(a, b):
    return jnp.dot(a, b, preferred_element_type=F32)


def _dot_nt(a, b):
    return lax.dot_general(a, b, (((1,), (1,)), ((), ())), preferred_element_type=F32)


def _dot_tn(a, b):
    return lax.dot_general(a, b, (((0,), (0,)), ((), ())), preferred_element_type=F32)


def _rms(x, g):
    return x * lax.rsqrt(jnp.mean(x * x, axis=-1, keepdims=True) + RMS_EPS) * g


def _params(sem):
    return pltpu.CompilerParams(dimension_semantics=sem, vmem_limit_bytes=VMEM_LIMIT)


def _store_rows_as_tiles(ref, val):
    n, d = val.shape
    assert d == SUBLANES * LANES and ref.shape == (n * SUBLANES, LANES)
    for s in range(SUBLANES):
        ref[pl.ds(s, n, stride=SUBLANES), :] = val[:, s * LANES:(s + 1) * LANES]


def _load_rows_from_tiles(ref):
    n = ref.shape[0] // SUBLANES
    return jnp.concatenate([ref[pl.ds(s, n, stride=SUBLANES), :] for s in range(SUBLANES)], axis=1)


def _inproj_kernel(x_ref, g_ref, w_qkvr_ref, w_glu_ref, w_gate_ref, w2_ref, b2_ref,
                   q_ref, k_ref, v_ref, gr_ref, la_ref, u_ref, *, kw, vw, cw, parts):
    n = x_ref.shape[0] // parts
    w_ref = (w_qkvr_ref, w_glu_ref, w_gate_ref)
    gens = [_inproj_rows(pl.ds(p * n, n), x_ref, g_ref, w_ref, w2_ref, b2_ref,
                         q_ref, k_ref, v_ref, gr_ref, la_ref, u_ref, kw, vw, cw)
            for p in range(parts)]
    for _ in zip(*gens):
        pass


def _inproj_rows(rows, x_ref, g_ref, w_ref, w2_ref, b2_ref,
                 q_ref, k_ref, v_ref, gr_ref, la_ref, u_ref, kw, vw, cw):
    w_qkvr_ref, w_glu_ref, w_gate_ref = w_ref
    h = _rms(x_ref[rows, :], g_ref[...]).astype(BF16)
    acc = _dot(h, w_qkvr_ref[...])
    glu = _dot(h, w_glu_ref[...])
    ranks = _dot(h, w_gate_ref[...])
    yield
    o = 0
    q_ref[rows, :] = (acc[:, o:o + kw] * (GLA_DK ** -0.5)).astype(BF16); o += kw
    k_ref[rows, :] = acc[:, o:o + kw].astype(BF16); o += kw
    v_ref[rows, :] = acc[:, o:o + vw].astype(BF16); o += vw
    r = acc[:, o:o + vw]
    gr_ref[rows, :] = (r * jax.nn.sigmoid(r)).astype(BF16)
    u_ref[rows, :] = glu[:, :cw] * jax.nn.sigmoid(glu[:, cw:])
    z = _dot(ranks.astype(BF16), w2_ref[...]) + b2_ref[...]
    yield
    la_ref[rows, :] = (jnp.minimum(z, 0.0) - jnp.log1p(jnp.exp(-jnp.abs(z)))) * (LOG2_E / GATE_TAU)
    yield


def _inproj(x2d, g, ws, w2, b2, kw, vw, cw, tm):
    T, D = x2d.shape
    row = lambda n: pl.BlockSpec((tm, n), lambda i: (i, 0))
    full = lambda a: pl.BlockSpec(a.shape, lambda i: (0,) * a.ndim)
    return pl.pallas_call(
        functools.partial(_inproj_kernel, kw=kw, vw=vw, cw=cw, parts=2),
        grid=(T // tm,),
        in_specs=[row(D), full(g)] + [full(w) for w in ws] + [full(w2), full(b2)],
        out_specs=[row(kw), row(kw), row(vw), row(vw), row(2 * kw), row(cw)],
        out_shape=[jax.ShapeDtypeStruct((T, kw), BF16), jax.ShapeDtypeStruct((T, kw), BF16),
                   jax.ShapeDtypeStruct((T, vw), BF16), jax.ShapeDtypeStruct((T, vw), BF16),
                   jax.ShapeDtypeStruct((T, 2 * kw), F32), jax.ShapeDtypeStruct((T, cw), F32)],
        compiler_params=_params(("parallel",)),
        name="inproj",
    )(x2d, g, *ws, w2, b2)


def _gla_levels():
    ms, m = [], GLA_CHUNK // 2
    while m >= GLA_TILE:
        ms.append(m)
        m //= 2
    return ms


def _gla_constants(bwd):
    C, H = GLA_CHUNK, GLA_HEADS
    t = np.arange(C)[:, None]
    s = np.arange(C)[None, :]
    lev_q, lev_mask = [], []
    for m in _gla_levels():
        g = 2 * m
        second = (t % g) >= m
        same = t // g == s // g
        if not bwd:
            qrow = second
            mask = same & ((t % g) >= m) & ((s % g) < m)
        else:
            qrow = ~second
            mask = same & ((t % g) < m) & ((s % g) >= m)
        lev_q.append(np.broadcast_to(qrow, (C, H * GLA_DK)))
        lev_mask.append(np.tile(mask, (1, H)))
    shifts = range(0, GLA_TILE) if not bwd else range(1, GLA_TILE)
    sh_mask = []
    for sh in shifts:
        j = t - sh if not bwd else t + sh
        sh_mask.append(np.tile((s == j) & (t // GLA_TILE == s // GLA_TILE), (1, H)))
    f = lambda xs: np.stack(xs).astype(np.float32)
    return f(lev_q), f(lev_mask), f(sh_mask)


def _gla_head_mask():
    C, H = GLA_CHUNK, GLA_HEADS
    assert GLA_DK == C
    blk = np.arange(H * C) // C
    return (blk[:, None] == blk[None, :]).astype(np.float32)


def _gla_chunk(la, q, k, v, st_ref, c_ref, tri_ref, lq_ref, lm_ref, sm_ref, hm_ref, bwd):
    C, H = GLA_CHUNK, GLA_HEADS
    kw = q.shape[1]
    hm = hm_ref[...]

    def stack(xb):
        return jnp.concatenate([xb] * H, axis=0) * hm

    tri = tri_ref[...]
    hi = la.astype(BF16)
    incl = _dot(tri, hi) + _dot(tri, (la - hi.astype(F32)).astype(BF16))
    yield None
    tot = incl[C - 1:C]
    cum = incl if not bwd else incl - la
    x_q, x_k = (cum, tot - cum) if not bwd else (tot - cum, cum)
    e_q, e_k, a_tot = jnp.exp2(x_q), jnp.exp2(x_k), jnp.exp2(tot)
    c_ref[...] = cum

    a = jnp.zeros((C, H * C), F32)
    for lv, m in enumerate(_gla_levels()):
        g = 2 * m
        edge = m - 1 if not bwd else m
        r = jnp.concatenate([jnp.broadcast_to(c_ref[s0 + edge:s0 + edge + 1, :], (g, kw))
                             for s0 in range(0, C, g)], axis=0)
        d = cum - r
        z = (jnp.where(lq_ref[lv] > 0, q, k) * jnp.exp2(jnp.minimum(d, -d))).astype(BF16)
        a = a + _dot_nt(z, stack(z)) * lm_ref[lv]
    ps = []
    for si in range(sm_ref.shape[0]):
        sh = si if not bwd else si + 1
        if sh == 0:
            ps.append((q * k).astype(BF16))
            continue
        amt = sh if not bwd else C - sh
        kr = pltpu.roll(k, amt, 0)
        cr = pltpu.roll(cum, amt, 0)
        d = (cum - cr) if not bwd else (cr - cum)
        ps.append((q * kr * jnp.exp2(jnp.minimum(d, 0.0))).astype(BF16))
    sc = _dot(jnp.concatenate(ps, axis=0), hm)
    yield None
    for si in range(sm_ref.shape[0]):
        a = a + sc[si * C:(si + 1) * C] * sm_ref[si]

    vst = jnp.concatenate([v[:, h * GLA_DV:(h + 1) * GLA_DV] for h in range(H)], axis=0)
    st = st_ref[...]
    o4 = _dot(stack(a.astype(BF16)), vst) + _dot_nt(stack((q * e_q).astype(BF16)), st.astype(BF16))
    st_ref[...] = st * a_tot + _dot_tn(vst, stack((k * e_k).astype(BF16)))
    yield jnp.concatenate([o4[h * C:(h + 1) * C] for h in range(H)], axis=1)


def _gla_kernel(q_ref, k_ref, v_ref, la_ref, gr_ref, ng_ref,
                lqf_ref, lmf_ref, smf_ref, lqb_ref, lmb_ref, smb_ref, tri_ref, hm_ref,
                out_ref, of_ref, ob_ref, stf_ref, stb_ref, cf_ref, cb_ref):
    C = GLA_CHUNK
    S = q_ref.shape[0]
    kw = q_ref.shape[1]
    n = S // C
    stf_ref[...] = jnp.zeros_like(stf_ref)
    stb_ref[...] = jnp.zeros_like(stb_ref)

    def body(i, carry):
        streams, outs = [], []
        for u in range(GLA_UNROLL):
            rf = pl.ds(pl.multiple_of((i * GLA_UNROLL + u) * C, C), C)
            rb = pl.ds(pl.multiple_of((n - 1 - i * GLA_UNROLL - u) * C, C), C)
            streams.append(_gla_chunk(
                la_ref[rf, 0:kw], q_ref[rf, :].astype(F32), k_ref[rf, :].astype(F32),
                v_ref[rf, :], stf_ref, cf_ref.at[u], tri_ref, lqf_ref, lmf_ref, smf_ref, hm_ref,
                False))
            streams.append(_gla_chunk(
                la_ref[rb, kw:2 * kw], q_ref[rb, :].astype(F32), k_ref[rb, :].astype(F32),
                v_ref[rb, :], stb_ref, cb_ref.at[u], tri_ref, lqb_ref, lmb_ref, smb_ref, hm_ref,
                True))
            outs += [(of_ref, rf), (ob_ref, rb)]
        for results in zip(*streams):
            pass
        for (ref, rows), o in zip(outs, results):
            ref[rows, :] = o
        return carry

    lax.fori_loop(0, n // GLA_UNROLL, body, 0)

    rows = 256
    def fin(i, carry):
        r = pl.ds(pl.multiple_of(i * rows, rows), rows)
        o = of_ref[r, :] + ob_ref[r, :]
        ys = []
        for h in range(GLA_HEADS):
            oh = o[:, h * GLA_DV:(h + 1) * GLA_DV]
            ys.append(_rms(oh, ng_ref[...]))
        y = jnp.concatenate(ys, axis=1) * gr_ref[r, :].astype(F32)
        out_ref[r, :] = y.astype(BF16)
        return carry

    lax.fori_loop(0, S // rows, fin, 0)


def _gla(q, k, v, la, gr, ng, B, S):
    kw, vw = q.shape[1], v.shape[1]
    cf = _gla_constants(False)
    cb = _gla_constants(True)
    tri = np.tril(np.ones((GLA_CHUNK, GLA_CHUNK), np.float32))
    consts = [jnp.asarray(c) for c in cf] + [jnp.asarray(c) for c in cb] + \
             [jnp.asarray(tri, BF16), jnp.asarray(_gla_head_mask(), BF16)]
    seq = lambda n: pl.BlockSpec((S, n), lambda b: (b, 0))
    full = lambda a: pl.BlockSpec(a.shape, lambda b: (0,) * a.ndim)
    return pl.pallas_call(
        _gla_kernel,
        grid=(B,),
        in_specs=[seq(kw), seq(kw), seq(vw), seq(2 * kw), seq(vw), full(ng)] + [full(c) for c in consts],
        out_specs=seq(vw),
        out_shape=jax.ShapeDtypeStruct((B * S, vw), BF16),
        scratch_shapes=[pltpu.VMEM((S, vw), F32), pltpu.VMEM((S, vw), F32),
                        pltpu.VMEM((GLA_DV, kw), F32), pltpu.VMEM((GLA_DV, kw), F32),
                        pltpu.VMEM((GLA_UNROLL, GLA_CHUNK, kw), F32),
                        pltpu.VMEM((GLA_UNROLL, GLA_CHUNK, kw), F32)],
        compiler_params=_params(("parallel",)),
        name="gla",
    )(q, k, v, la, gr, ng, *consts)


def _conv_kernel(u_ref, w_ref, cb_ref, lg_ref, lb_ref, out_ref, *, tm, halo):
    S = u_ref.shape[0]
    j = pl.program_id(1)
    t0 = pl.multiple_of(j * tm, tm)
    top = u_ref[pl.ds(pl.multiple_of(jnp.maximum(t0 - halo, 0), halo), halo), :]
    bot = u_ref[pl.ds(pl.multiple_of(jnp.minimum(t0 + tm, S - halo), halo), halo), :]
    top = jnp.where(j > 0, top, 0.0)
    bot = jnp.where(j < pl.num_programs(1) - 1, bot, 0.0)
    win = jnp.concatenate([top, u_ref[pl.ds(t0, tm), :], bot], axis=0)
    rows = tm + 2 * halo
    pad = (CONV_KERNEL - 1) // 2
    w = w_ref[...]
    acc = jnp.zeros((tm, u_ref.shape[1]), F32)
    for r in range(8):
        offs = [k for k in range(CONV_KERNEL) if (halo - pad + k) % 8 == r]
        if not offs:
            continue
        xr = win if r == 0 else pltpu.roll(win, rows - r, 0)
        for k in offs:
            a0 = (halo - pad + k) - r
            acc = acc + xr[a0:a0 + tm, :] * w[k:k + 1, :]
    y = acc + cb_ref[...]
    mu = jnp.mean(y, axis=-1, keepdims=True)
    yc = y - mu
    var = jnp.mean(yc * yc, axis=-1, keepdims=True)
    z = yc * lax.rsqrt(var + LN_EPS) * lg_ref[...] + lb_ref[...]
    out_ref[...] = (z * jax.nn.sigmoid(z)).astype(BF16)


def _conv(u, w, cb, lg, lb, B, S, tm):
    cw = u.shape[1]
    halo = 16
    full = lambda a: pl.BlockSpec(a.shape, lambda b, j: (0,) * a.ndim)
    nj = S // tm
    return pl.pallas_call(
        functools.partial(_conv_kernel, tm=tm, halo=halo),
        grid=(B, nj),
        in_specs=[pl.BlockSpec((S, cw), lambda b, j: (b, 0)), full(w), full(cb), full(lg), full(lb)],
        out_specs=pl.BlockSpec((tm, cw), lambda b, j: (b * nj + j, 0)),
        out_shape=jax.ShapeDtypeStruct((B * S, cw), BF16),
        compiler_params=_params(("parallel", "parallel")),
        name="conv",
    )(u, w, cb, lg, lb)


def _memkv_kernel(m_ref, g_ref, wk_ref, wv_ref, k_ref, v_ref):
    m = _rms(m_ref[...], g_ref[...]).astype(BF16)
    k_ref[...] = _dot(m, wk_ref[...]).astype(BF16)
    v_ref[...] = _dot(m, wv_ref[...]).astype(BF16)


def _memkv(mem2d, g, wk, wv, M):
    R, D = mem2d.shape
    full = lambda a: pl.BlockSpec(a.shape, lambda i: (0,) * a.ndim)
    blk = pl.BlockSpec((M, D), lambda i: (i, 0))
    return pl.pallas_call(
        _memkv_kernel,
        grid=(R // M,),
        in_specs=[blk, full(g), full(wk), full(wv)],
        out_specs=[blk, blk],
        out_shape=[jax.ShapeDtypeStruct((R, D), BF16)] * 2,
        compiler_params=_params(("parallel",)),
        name="memkv",
    )(mem2d, g, wk, wv)


def _mid_kernel(x_ref, a_ref, b_ref, wout_ref, gx_ref, wq_ref, km_ref, vm_ref, wo_ref,
                gf_ref, rw_ref, rb_ref, x2_ref, h3_ref, idx_ref, gate_ref, *, n_exp, parts):
    n = x_ref.shape[0] // parts
    gens = [_mid_rows(pl.ds(p * n, n), pl.ds(p * n * SUBLANES, n * SUBLANES),
                      x_ref, a_ref, b_ref, wout_ref, gx_ref, wq_ref, km_ref, vm_ref, wo_ref,
                      gf_ref, rw_ref, rb_ref, x2_ref, h3_ref, idx_ref, gate_ref, n_exp)
            for p in range(parts)]
    for _ in zip(*gens):
        pass


def _mid_rows(rows, tiles, x_ref, a_ref, b_ref, wout_ref, gx_ref, wq_ref, km_ref, vm_ref, wo_ref,
              gf_ref, rw_ref, rb_ref, x2_ref, h3_ref, idx_ref, gate_ref, n_exp):
    ab = jnp.concatenate([a_ref[rows, :], b_ref[rows, :]], axis=1)
    x1 = x_ref[rows, :] + _dot(ab, wout_ref[...])
    yield
    q2 = _dot(_rms(x1, gx_ref[...]).astype(BF16), wq_ref[...])
    yield
    D = q2.shape[1]
    dh = D // XATTN_HEADS
    ss = []
    for h in range(XATTN_HEADS):
        sl = slice(h * dh, (h + 1) * dh)
        ss.append(_dot_nt(q2[:, sl].astype(BF16), km_ref[:, sl]) * (dh ** -0.5 * LOG2_E))
    yield
    outs = []
    for h in range(XATTN_HEADS):
        sl = slice(h * dh, (h + 1) * dh)
        p = jnp.exp2(ss[h] - jnp.max(ss[h], axis=-1, keepdims=True))
        p = p / jnp.sum(p, axis=-1, keepdims=True)
        outs.append(_dot(p.astype(BF16), vm_ref[:, sl]))
    yield
    o = jnp.concatenate(outs, axis=1).astype(BF16)
    x2 = x1 + _dot(o, wo_ref[...])
    yield
    x2_ref[rows, :] = x2
    h3 = _rms(x2, gf_ref[...])
    _store_rows_as_tiles(h3_ref.at[tiles], h3)
    lane = lax.broadcasted_iota(jnp.int32, (x2.shape[0], LANES), 1)
    logits = _dot(h3.astype(BF16), rw_ref[...]) + rb_ref[...]
    yield
    cur = jnp.where(lane < n_exp, logits, -jnp.inf)
    lane_f = lane.astype(F32)
    vals, idxs = [], []
    for _ in range(TOP_K):
        m = jnp.max(cur, axis=-1, keepdims=True)
        ik = jnp.min(jnp.where(cur == m, lane_f, float(LANES)), axis=-1, keepdims=True)
        vals.append(m)
        idxs.append(ik)
        cur = jnp.where(lane_f == ik, -jnp.inf, cur)
    es = [jnp.exp(vk - vals[0]) for vk in vals]
    den = es[0]
    for ek in es[1:]:
        den = den + ek
    idx_out = jnp.zeros(lane.shape, F32)
    gate_out = jnp.zeros(lane.shape, F32)
    for kk in range(TOP_K):
        idx_out = jnp.where(lane == kk, idxs[kk], idx_out)
        gate_out = jnp.where(lane == kk, es[kk] / den, gate_out)
    idx_ref[rows, :] = idx_out.astype(jnp.int32)
    gate_ref[:, rows] = gate_out.T[0:SUBLANES, :]
    yield


def _mid(x2d, a, b, wout, gx, wq, km, vm, wo, gf, rw, rb, S, M, n_exp, tm):
    T, D = x2d.shape
    half = a.shape[1]
    per_b = S // tm
    row = lambda n: pl.BlockSpec((tm, n), lambda i: (i, 0))
    full = lambda arr: pl.BlockSpec(arr.shape, lambda i: (0,) * arr.ndim)
    memb = pl.BlockSpec((M, D), lambda i: (i // per_b, 0))
    return pl.pallas_call(
        functools.partial(_mid_kernel, n_exp=n_exp, parts=2),
        grid=(T // tm,),
        in_specs=[row(D), row(half), row(half), full(wout), full(gx), full(wq), memb, memb,
                  full(wo), full(gf), full(rw), full(rb)],
        out_specs=[row(D), pl.BlockSpec((tm * SUBLANES, LANES), lambda i: (i, 0)),
                   row(LANES), pl.BlockSpec((SUBLANES, tm), lambda i: (i // 2, i % 2))],
        out_shape=[jax.ShapeDtypeStruct((T, D), F32),
                   jax.ShapeDtypeStruct((T * SUBLANES, LANES), F32),
                   jax.ShapeDtypeStruct((T, LANES), jnp.int32),
                   jax.ShapeDtypeStruct((T // (2 * tm) * SUBLANES, 2 * tm), F32)],
        compiler_params=_params(("parallel",)),
        name="mid",
    )(x2d, a, b, wout, gx, wq, km, vm, wo, gf, rw, rb)


def _place_kernel(idx_ref, tri_ref, upper_ref, pos_ref, cnt_ref):
    tm = idx_ref.shape[0] // 2
    for _ in zip(*[_place_tile(half, tm, idx_ref, tri_ref, upper_ref, pos_ref, cnt_ref)
                   for half in range(2)]):
        pass


def _place_tile(half, tm, idx_ref, tri_ref, upper_ref, pos_ref, cnt_ref):
    rows = pl.ds(half * tm, tm)
    lane = lax.broadcasted_iota(jnp.int32, (tm, LANES), 1)
    hots = [(lane == idx_ref[rows, kk:kk + 1]).astype(F32) for kk in range(TOP_K)]
    mask = hots[0]
    for hk in hots[1:]:
        mask = mask + hk
    incl = _dot(tri_ref[...], mask.astype(BF16))
    yield
    cnt = jnp.broadcast_to(incl[tm - 1:tm, :], (SUBLANES, LANES))
    hi = jnp.floor(cnt * (1.0 / 16.0))
    lo = cnt - 16.0 * hi
    start = 16.0 * _dot(hi.astype(BF16), upper_ref[...]) + _dot(lo.astype(BF16), upper_ref[...])
    yield
    before = incl - mask + start[0:1, :]
    out = jnp.zeros((tm, LANES), F32)
    for kk in range(TOP_K):
        rk = jnp.sum(hots[kk] * before, axis=-1, keepdims=True)
        out = jnp.where(lane == kk, rk, out)
    pos_ref[:, rows] = (out.T[0:SUBLANES, :] * SUBLANES).astype(jnp.int32)
    cnt_ref[pl.ds(half * SUBLANES, SUBLANES), :] = cnt
    yield


def _place(idx, tm):
    T = idx.shape[0]
    assert tm <= 16 * 256
    tri = jnp.asarray(np.tril(np.ones((tm, tm), np.float32)), BF16)
    upper = jnp.asarray(np.triu(np.ones((LANES, LANES), np.float32), 1), BF16)
    return pl.pallas_call(
        _place_kernel,
        grid=(T // (2 * tm),),
        in_specs=[pl.BlockSpec((2 * tm, LANES), lambda i: (i, 0)),
                  pl.BlockSpec((tm, tm), lambda i: (0, 0)),
                  pl.BlockSpec((LANES, LANES), lambda i: (0, 0))],
        out_specs=[pl.BlockSpec((SUBLANES, 2 * tm), lambda i: (i, 0)),
                   pl.BlockSpec((2 * SUBLANES, LANES), lambda i: (i, 0))],
        out_shape=[jax.ShapeDtypeStruct((T // (2 * tm) * SUBLANES, 2 * tm), jnp.int32),
                   jax.ShapeDtypeStruct((T // tm * SUBLANES, LANES), F32)],
        compiler_params=_params(("parallel",)),
        name="place",
    )(idx, tri, upper)


ROW_UNROLL = 8


def _wait_rows(buf_ref, sem):
    pltpu.make_async_copy(buf_ref, buf_ref, sem).wait()


RUN_CHUNK = 32


def _rows(ref, row, n):
    return ref.at[pl.ds(pl.multiple_of(row * SUBLANES, SUBLANES), n * SUBLANES)]


def _for_run_pieces(n, fn):
    shift = RUN_CHUNK.bit_length() - 1
    whole = lax.shift_right_logical(n, shift)

    def body(c, carry):
        fn(c * RUN_CHUNK, RUN_CHUNK)
        return carry

    lax.fori_loop(0, whole, body, 0)
    off = whole * RUN_CHUNK
    for bit in reversed(range(shift)):
        take = (n & (1 << bit)) != 0

        @pl.when(take)
        def _(off=off, bit=bit):
            fn(off, 1 << bit)

        off = off + jnp.where(take, 1 << bit, 0)


def _dispatch_kernel(pos_ref, rdst_ref, rn_ref, rloc_ref, lo_ref, hi_ref, nb_ref, h_ref, xs_ref,
                     sb0, sb1, zero_ref, sem, zsem):
    j = pl.program_id(0)
    tm = h_ref.shape[0] // SUBLANES // 2
    slots = rn_ref.shape[0] // 2
    n_exp = lo_ref.shape[0]
    blk = zero_ref.shape[0]

    def sort(half, sb):
        def body(i, carry):
            for u in range(ROW_UNROLL):
                t = half * tm + i * ROW_UNROLL + u
                tile = h_ref[pl.ds(pl.multiple_of(t * SUBLANES, SUBLANES), SUBLANES), :]
                for kk in range(TOP_K):
                    p = pos_ref[kk * 2 * tm + t]
                    sb[pl.ds(pl.multiple_of(p, SUBLANES), SUBLANES), :] = tile
            return carry

        lax.fori_loop(0, tm // ROW_UNROLL, body, 0)

    def start_runs(half, sb, s):
        def expert(e, carry):
            r = half * slots + e
            loc, dst = rloc_ref[r], rdst_ref[r]
            _for_run_pieces(rn_ref[r], lambda off, n: pltpu.make_async_copy(
                _rows(sb, loc + off, n), _rows(xs_ref, dst + off, n), s).start())
            return carry

        lax.fori_loop(0, n_exp, expert, 0)

    @pl.when(j == 0)
    def _():
        zero_ref[...] = jnp.zeros_like(zero_ref)

        def each_fill(fn):
            def expert(e, carry):
                row = lo_ref[e]
                n = hi_ref[e] - row
                for bit in reversed(range(MOE_ROWS.bit_length() - 1)):
                    size = (1 << bit) * SUBLANES
                    take = (n & (1 << bit)) != 0

                    @pl.when(take)
                    def _(row=row, size=size):
                        dst = xs_ref.at[pl.ds(pl.multiple_of(row * SUBLANES, SUBLANES), size)]
                        fn(pltpu.make_async_copy(zero_ref.at[pl.ds(0, size)], dst, zsem))

                    row = row + jnp.where(take, 1 << bit, 0)
                return carry

            lax.fori_loop(0, lo_ref.shape[0], expert, 0)

            def block(b, carry):
                fn(pltpu.make_async_copy(
                    zero_ref, xs_ref.at[pl.ds(pl.multiple_of(b * blk, blk), blk)], zsem))
                return carry

            lax.fori_loop(nb_ref[0], xs_ref.shape[0] // blk, block, 0)

        each_fill(lambda cp: cp.start())
        each_fill(lambda cp: cp.wait())

    for half, sb in ((0, sb0), (1, sb1)):
        @pl.when(j > 0)
        def _(half=half, sb=sb):
            _wait_rows(sb, sem.at[half])

        sort(half, sb)
        start_runs(half, sb, sem.at[half])

    @pl.when(j == pl.num_programs(0) - 1)
    def _():
        _wait_rows(sb0, sem.at[0])
        _wait_rows(sb1, sem.at[1])


def _dispatch(pos, run_dst, run_n, run_loc, pad_lo, pad_hi, n_used, h3, n_rows, tm):
    T = h3.shape[0] // SUBLANES
    slots = run_n.shape[0] // (T // tm)
    smem = pl.BlockSpec(memory_space=pltpu.SMEM)
    per_step = lambda n: pl.BlockSpec((n,), lambda j: (j,), memory_space=pltpu.SMEM)
    sorted_rows = pltpu.VMEM((tm * TOP_K * SUBLANES, LANES), F32)
    return pl.pallas_call(
        _dispatch_kernel,
        grid=(T // (2 * tm),),
        in_specs=[per_step(SUBLANES * 2 * tm),
                  per_step(2 * slots), per_step(2 * slots),
                  per_step(2 * slots), smem, smem, smem,
                  pl.BlockSpec((2 * tm * SUBLANES, LANES), lambda j: (j, 0))],
        out_specs=pl.BlockSpec(memory_space=pl.ANY),
        out_shape=jax.ShapeDtypeStruct((n_rows * SUBLANES, LANES), F32),
        scratch_shapes=[sorted_rows, sorted_rows, pltpu.VMEM((MOE_ROWS * SUBLANES, LANES), F32),
                        pltpu.SemaphoreType.DMA((2,)), pltpu.SemaphoreType.DMA(())],
        compiler_params=_params(("arbitrary",)),
        name="dispatch",
    )(pos, run_dst, run_n, run_loc, pad_lo, pad_hi, n_used, h3)


def _experts_kernel(be_ref, live_ref, first_ref, next_ref, slot_ref,
                    xs_ref, wu_hbm, bu_ref, wd_hbm, bd_ref, ys_ref,
                    wu_f32, wd_f32, wu_bf, wd_bf, sem):
    i = pl.program_id(0)

    def fetch(e, s):
        return (pltpu.make_async_copy(wu_hbm.at[e], wu_f32.at[s], sem.at[s]),
                pltpu.make_async_copy(wd_hbm.at[e], wd_f32.at[s], sem.at[s]))

    @pl.when(i == 0)
    def _():
        for cp in fetch(be_ref[0], 0):
            cp.start()

    @pl.when(first_ref[i] == 1)
    def _():
        s = slot_ref[i]
        for cp in fetch(be_ref[i], s):
            cp.wait()

        @pl.when(next_ref[i] >= 0)
        def _():
            for cp in fetch(next_ref[i], 1 - s):
                cp.start()

        wu_bf[...] = wu_f32[s].astype(BF16)
        wd_bf[...] = wd_f32[s].astype(BF16)

    def mlp(tiles):
        F = wd_bf.shape[0]
        x = _load_rows_from_tiles(xs_ref.at[tiles]).astype(BF16)
        gu = _dot(x, wu_bf[...]) + bu_ref[0]
        yield
        x_glu = jnp.minimum(gu[:, :F], SWIGLU_LIMIT)
        x_lin = jnp.clip(gu[:, F:], -SWIGLU_LIMIT, SWIGLU_LIMIT)
        hid = x_glu * jax.nn.sigmoid(SWIGLU_ALPHA * x_glu) * (x_lin + 1.0)
        y = _dot(hid.astype(BF16), wd_bf[...]) + bd_ref[0]
        yield
        _store_rows_as_tiles(ys_ref.at[tiles], y)
        yield

    n = xs_ref.shape[0] // MOE_PARTS
    for live in range(MOE_PARTS + 1):
        @pl.when(live_ref[i] == live)
        def _(live=live):
            for _ in zip(*[mlp(pl.ds(p * n, n)) for p in range(live)]):
                pass
            if live < MOE_PARTS:
                rest = (MOE_PARTS - live) * n
                ys_ref[pl.ds(live * n, rest), :] = jnp.zeros((rest, ys_ref.shape[1]), F32)


def _expert_runs(block_e):
    nb = block_e.shape[0]
    ar = jnp.arange(nb, dtype=jnp.int32)
    first = jnp.concatenate([jnp.ones((1,), bool), block_e[1:] != block_e[:-1]])
    starts = jnp.where(first, ar, nb)
    later = lax.cummin(jnp.concatenate([starts[1:], jnp.full((1,), nb, jnp.int32)]), reverse=True)
    nxt = jnp.where(later < nb, block_e[jnp.minimum(later, nb - 1)], -1)
    slot = (jnp.cumsum(first.astype(jnp.int32)) - 1) & 1
    return first.astype(jnp.int32), nxt.astype(jnp.int32), slot.astype(jnp.int32)


def _experts(block_e, live, xs, wu, bu, wd, bd):
    E, D, F2 = wu.shape
    F = wd.shape[1]
    nb = xs.shape[0] // (MOE_ROWS * SUBLANES)
    rows = pl.BlockSpec((MOE_ROWS * SUBLANES, LANES), lambda i, *_: (i, 0))
    grid_spec = pltpu.PrefetchScalarGridSpec(
        num_scalar_prefetch=5,
        grid=(nb,),
        in_specs=[rows,
                  pl.BlockSpec(memory_space=pl.ANY),
                  pl.BlockSpec((1, 1, F2), lambda i, be, *_: (be[i], 0, 0)),
                  pl.BlockSpec(memory_space=pl.ANY),
                  pl.BlockSpec((1, 1, D), lambda i, be, *_: (be[i], 0, 0))],
        out_specs=rows,
        scratch_shapes=[pltpu.VMEM((2, D, F2), F32), pltpu.VMEM((2, F, D), F32),
                        pltpu.VMEM((D, F2), BF16), pltpu.VMEM((F, D), BF16),
                        pltpu.SemaphoreType.DMA((2,))],
    )
    return pl.pallas_call(
        _experts_kernel,
        grid_spec=grid_spec,
        out_shape=jax.ShapeDtypeStruct(xs.shape, F32),
        compiler_params=_params(("arbitrary",)),
        name="experts",
    )(block_e, live, *_expert_runs(block_e), xs, wu, bu, wd, bd)


def _combine_kernel(pos_ref, gate_ref, src_ref, rn_ref, rloc_ref, nsrc_ref, nrn_ref, nrloc_ref,
                    x2_ref, g_ref, ys_ref, out_ref, yb0, yb1, acc_ref, sem, *, n_exp, apply_norm):
    j = pl.program_id(0)
    tm = x2_ref.shape[0] // 2
    slots = rn_ref.shape[0] // 2

    def start(tbls, half, yb, s):
        src_t, n_t, loc_t = tbls

        def expert(e, carry):
            r = half * slots + e
            loc, src = loc_t[r], src_t[r]
            _for_run_pieces(n_t[r], lambda off, n: pltpu.make_async_copy(
                _rows(ys_ref, src + off, n), _rows(yb, loc + off, n), s).start())
            return carry

        lax.fori_loop(0, n_exp, expert, 0)

    def finish(half, yb, s):
        _wait_rows(yb, s)

        def body(i, carry):
            for u in range(ROW_UNROLL):
                t = i * ROW_UNROLL + u
                a = half * tm + t
                acc = None
                for kk in range(TOP_K):
                    p = pos_ref[kk * 2 * tm + a]
                    term = yb[pl.ds(pl.multiple_of(p, SUBLANES), SUBLANES), :] \
                        * gate_ref[kk * 2 * tm + a]
                    acc = term if acc is None else acc + term
                acc_ref[pl.ds(pl.multiple_of(t * SUBLANES, SUBLANES), SUBLANES), :] = acc
            return carry

        lax.fori_loop(0, tm // ROW_UNROLL, body, 0)
        rows = pl.ds(half * tm, tm)
        x = x2_ref[rows, :] + _load_rows_from_tiles(acc_ref)
        out_ref[rows, :] = _rms(x, g_ref[...]) if apply_norm else x

    cur = (src_ref, rn_ref, rloc_ref)

    @pl.when(j == 0)
    def _():
        start(cur, 0, yb0, sem.at[0])

    start(cur, 1, yb1, sem.at[1])
    finish(0, yb0, sem.at[0])

    @pl.when(j < pl.num_programs(0) - 1)
    def _():
        start((nsrc_ref, nrn_ref, nrloc_ref), 0, yb0, sem.at[0])

    finish(1, yb1, sem.at[1])


def _combine(pos, gates, run_src, run_n, run_loc, x2, g, ys, tm, n_exp, apply_norm):
    T, D = x2.shape
    steps = T // (2 * tm)
    slots = run_n.shape[0] // (T // tm)
    per_step = lambda n: pl.BlockSpec((n,), lambda j: (j,), memory_space=pltpu.SMEM)
    next_step = lambda n: pl.BlockSpec((n,), lambda j: (jnp.minimum(j + 1, steps - 1),),
                                       memory_space=pltpu.SMEM)
    sorted_rows = pltpu.VMEM((tm * TOP_K * SUBLANES, LANES), F32)
    return pl.pallas_call(
        functools.partial(_combine_kernel, n_exp=n_exp, apply_norm=apply_norm),
        grid=(steps,),
        in_specs=[per_step(SUBLANES * 2 * tm),
                  per_step(SUBLANES * 2 * tm),
                  per_step(2 * slots), per_step(2 * slots), per_step(2 * slots),
                  next_step(2 * slots), next_step(2 * slots), next_step(2 * slots),
                  pl.BlockSpec((2 * tm, D), lambda j: (j, 0)),
                  pl.BlockSpec(g.shape, lambda j: (0, 0)),
                  pl.BlockSpec(memory_space=pl.ANY)],
        out_specs=pl.BlockSpec((2 * tm, D), lambda j: (j, 0)),
        out_shape=jax.ShapeDtypeStruct((T, D), F32),
        scratch_shapes=[sorted_rows, sorted_rows, pltpu.VMEM((tm * SUBLANES, LANES), F32),
                        pltpu.SemaphoreType.DMA((2,))],
        compiler_params=_params(("arbitrary",)),
        name="combine",
    )(pos, gates, run_src, run_n, run_loc, run_src, run_n, run_loc, x2, g, ys)


def _row(v):
    return v.reshape(1, -1).astype(F32)


def kernel(x, mem, norm_mix_g, w_in, gate_w2_fwd, gate_b_fwd, gate_w2_bwd, gate_b_bwd,
           gla_norm_g, conv_w, conv_b, conv_ln_g, conv_ln_b, w_out, norm_xattn_g,
           norm_mem_g, xattn_wq, xattn_wk, xattn_wv, xattn_wo, norm_ffn_g, router_w,
           router_b, exp_w_up, exp_b_up, exp_w_down, exp_b_down, final_norm_g):
    B, S, D = x.shape
    M = mem.shape[1]
    T = B * S
    depth = w_in.shape[0]
    kw = GLA_HEADS * GLA_DK
    vw = GLA_HEADS * GLA_DV
    rank_w = gate_w2_fwd.shape[1]
    cw = conv_w.shape[2]
    E = router_w.shape[2]
    assert vw + cw == w_out.shape[1] and 2 * rank_w <= LANES and E <= LANES

    xc = x.reshape(T, D)
    for l in range(depth):
        o_gf = 2 * kw + 2 * vw
        o_glu = o_gf + 2 * rank_w
        wl = w_in[l]
        w_gate = jnp.pad(wl[:, o_gf:o_glu], ((0, 0), (0, LANES - 2 * rank_w)))
        ws = [wl[:, :o_gf].astype(BF16), wl[:, o_glu:].astype(BF16), w_gate.astype(BF16)]
        w2 = jnp.zeros((LANES, 2 * kw), F32)
        w2 = w2.at[:rank_w, :kw].set(gate_w2_fwd[l]).at[rank_w:2 * rank_w, kw:].set(gate_w2_bwd[l])
        b2 = jnp.concatenate([gate_b_fwd[l], gate_b_bwd[l]]).reshape(1, -1)

        q, k, v, gr, la, u = _inproj(xc, _row(norm_mix_g[l]), ws, w2.astype(BF16), b2,
                                     kw, vw, cw, tm=512)
        a_out = _gla(q, k, v, la, gr, _row(gla_norm_g[l]), B, S)
        b_out = _conv(u, conv_w[l], _row(conv_b[l]), _row(conv_ln_g[l]), _row(conv_ln_b[l]),
                      B, S, tm=512)
        km, vm = _memkv(mem.reshape(B * M, D), _row(norm_mem_g[l]),
                        xattn_wk[l].astype(BF16), xattn_wv[l].astype(BF16), M)

        rw = jnp.zeros((D, LANES), F32).at[:, :E].set(router_w[l]).astype(BF16)
        rb = jnp.zeros((1, LANES), F32).at[0, :E].set(router_b[l])
        x2, h3, idx, gates = _mid(
            xc, a_out, b_out, w_out[l].astype(BF16), _row(norm_xattn_g[l]),
            xattn_wq[l].astype(BF16), km, vm, xattn_wo[l].astype(BF16),
            _row(norm_ffn_g[l]), rw, rb, S, M, E, tm=MOE_TILE)

        tm = MOE_TILE
        pos, tile_cnt = _place(idx, tm=tm)
        n_te = tile_cnt.reshape(T // tm, SUBLANES, LANES)[:, 0, :E].astype(jnp.int32)
        counts = jnp.sum(n_te, axis=0)
        padded = ((counts + MOE_ROWS - 1) // MOE_ROWS) * MOE_ROWS
        pends = jnp.cumsum(padded)
        run_row = (pends - padded)[None, :] + jnp.cumsum(n_te, axis=0) - n_te
        run_loc = jnp.cumsum(n_te, axis=1) - n_te
        slots = max(E, LANES // 2)
        runs = [jnp.pad(a, ((0, 0), (0, slots - E))).reshape(-1).astype(jnp.int32)
                for a in (run_row, n_te, run_loc)]
        n_blocks = (T * TOP_K + E * (MOE_ROWS - 1)) // MOE_ROWS
        blk_start = jnp.arange(n_blocks, dtype=jnp.int32) * MOE_ROWS
        block_e = jnp.minimum(jnp.sum(pends[None, :] <= blk_start[:, None], axis=1), E - 1)
        n_used = (pends[E - 1] // MOE_ROWS).reshape(1)

        pos, gates = pos.reshape(-1), gates.reshape(-1)
        xs = _dispatch(pos, *runs, (pends - padded + counts).astype(jnp.int32),
                       pends.astype(jnp.int32), n_used.astype(jnp.int32), h3,
                       n_blocks * MOE_ROWS, tm=tm)
        group = MOE_ROWS // MOE_PARTS
        real = jnp.clip((pends - padded + counts)[block_e] - blk_start, 0, MOE_ROWS)
        live = jnp.where(blk_start < pends[E - 1], (real + group - 1) // group, 0)
        ys = _experts(block_e.astype(jnp.int32), live.astype(jnp.int32), xs,
                      exp_w_up[l], exp_b_up[l][:, None, :],
                      exp_w_down[l], exp_b_down[l][:, None, :])
        xc = _combine(pos, gates, *runs, x2,
                      _row(final_norm_g), ys, tm=tm, n_exp=E, apply_norm=(l == depth - 1))
    return xc.reshape(B, S, D)
```

```python
import functools

import numpy as np
import jax
import jax.numpy as jnp
from jax import lax
from jax.experimental import pallas as pl
from jax.experimental.pallas import tpu as pltpu

F32 = jnp.float32
BF16 = jnp.bfloat16

GLA_HEADS = 4
GLA_DK = 64
GLA_DV = 128
GLA_CHUNK = 64
GLA_TILE = 4
GLA_UNROLL = 4
GATE_TAU = 16.0
LOG2_E = 1.4426950408889634
CONV_KERNEL = 31
XATTN_HEADS = 4
TOP_K = 4
SWIGLU_ALPHA = 1.702
SWIGLU_LIMIT = 7.0
RMS_EPS = 1e-6
LN_EPS = 1e-5

LANES = 128
SUBLANES = 8
MOE_ROWS = 512
MOE_PARTS = 2
MOE_TILE = 512
VMEM_LIMIT = 56 * 1024 * 1024


def _dot(a, b):
    return jnp.dot(a, b, preferred_element_type=F32)


def _dot_nt(a, b):
    return lax.dot_general(a, b, (((1,), (1,)), ((), ())), preferred_element_type=F32)


def _dot_tn(a, b):
    return lax.dot_general(a, b, (((0,), (0,)), ((), ())), preferred_element_type=F32)


def _rms(x, g):
    return x * lax.rsqrt(jnp.mean(x * x, axis=-1, keepdims=True) + RMS_EPS) * g


def _params(sem):
    return pltpu.CompilerParams(dimension_semantics=sem, vmem_limit_bytes=VMEM_LIMIT)


def _store_rows_as_tiles(ref, val):
    n, d = val.shape
    assert d == SUBLANES * LANES and ref.shape == (n * SUBLANES, LANES)
    for s in range(SUBLANES):
        ref[pl.ds(s, n, stride=SUBLANES), :] = val[:, s * LANES:(s + 1) * LANES]


def _load_rows_from_tiles(ref):
    n = ref.shape[0] // SUBLANES
    return jnp.concatenate([ref[pl.ds(s, n, stride=SUBLANES), :] for s in range(SUBLANES)], axis=1)


def _inproj_kernel(x_ref, g_ref, w_qkvr_ref, w_glu_ref, w_gate_ref, w2_ref, b2_ref,
                   q_ref, k_ref, v_ref, gr_ref, la_ref, u_ref, *, kw, vw, cw, parts):
    n = x_ref.shape[0] // parts
    w_ref = (w_qkvr_ref, w_glu_ref, w_gate_ref)
    gens = [_inproj_rows(pl.ds(p * n, n), x_ref, g_ref, w_ref, w2_ref, b2_ref,
                         q_ref, k_ref, v_ref, gr_ref, la_ref, u_ref, kw, vw, cw)
            for p in range(parts)]
    for _ in zip(*gens):
        pass


def _inproj_rows(rows, x_ref, g_ref, w_ref, w2_ref, b2_ref,
                 q_ref, k_ref, v_ref, gr_ref, la_ref, u_ref, kw, vw, cw):
    w_qkvr_ref, w_glu_ref, w_gate_ref = w_ref
    h = _rms(x_ref[rows, :], g_ref[...]).astype(BF16)
    acc = _dot(h, w_qkvr_ref[...])
    glu = _dot(h, w_glu_ref[...])
    ranks = _dot(h, w_gate_ref[...])
    yield
    o = 0
    q_ref[rows, :] = (acc[:, o:o + kw] * (GLA_DK ** -0.5)).astype(BF16); o += kw
    k_ref[rows, :] = acc[:, o:o + kw].astype(BF16); o += kw
    v_ref[rows, :] = acc[:, o:o + vw].astype(BF16); o += vw
    r = acc[:, o:o + vw]
    gr_ref[rows, :] = (r * jax.nn.sigmoid(r)).astype(BF16)
    u_ref[rows, :] = glu[:, :cw] * jax.nn.sigmoid(glu[:, cw:])
    z = _dot(ranks.astype(BF16), w2_ref[...]) + b2_ref[...]
    yield
    la_ref[rows, :] = (jnp.minimum(z, 0.0) - jnp.log1p(jnp.exp(-jnp.abs(z)))) * (LOG2_E / GATE_TAU)
    yield


def _inproj(x2d, g, ws, w2, b2, kw, vw, cw, tm):
    T, D = x2d.shape
    row = lambda n: pl.BlockSpec((tm, n), lambda i: (i, 0))
    full = lambda a: pl.BlockSpec(a.shape, lambda i: (0,) * a.ndim)
    return pl.pallas_call(
        functools.partial(_inproj_kernel, kw=kw, vw=vw, cw=cw, parts=2),
        grid=(T // tm,),
        in_specs=[row(D), full(g)] + [full(w) for w in ws] + [full(w2), full(b2)],
        out_specs=[row(kw), row(kw), row(vw), row(vw), row(2 * kw), row(cw)],
        out_shape=[jax.ShapeDtypeStruct((T, kw), BF16), jax.ShapeDtypeStruct((T, kw), BF16),
                   jax.ShapeDtypeStruct((T, vw), BF16), jax.ShapeDtypeStruct((T, vw), BF16),
                   jax.ShapeDtypeStruct((T, 2 * kw), F32), jax.ShapeDtypeStruct((T, cw), F32)],
        compiler_params=_params(("parallel",)),
        name="inproj",
    )(x2d, g, *ws, w2, b2)


def _gla_levels():
    ms, m = [], GLA_CHUNK // 2
    while m >= GLA_TILE:
        ms.append(m)
        m //= 2
    return ms


def _gla_constants(bwd):
    C, H = GLA_CHUNK, GLA_HEADS
    t = np.arange(C)[:, None]
    s = np.arange(C)[None, :]
    lev_q, lev_mask = [], []
    for m in _gla_levels():
        g = 2 * m
        second = (t % g) >= m
        same = t // g == s // g
        if not bwd:
            qrow = second
            mask = same & ((t % g) >= m) & ((s % g) < m)
        else:
            qrow = ~second
            mask = same & ((t % g) < m) & ((s % g) >= m)
        lev_q.append(np.broadcast_to(qrow, (C, H * GLA_DK)))
        lev_mask.append(np.tile(mask, (1, H)))
    shifts = range(0, GLA_TILE) if not bwd else range(1, GLA_TILE)
    sh_mask = []
    for sh in shifts:
        j = t - sh if not bwd else t + sh
        sh_mask.append(np.tile((s == j) & (t // GLA_TILE == s // GLA_TILE), (1, H)))
    f = lambda xs: np.stack(xs).astype(np.float32)
    return f(lev_q), f(lev_mask), f(sh_mask)


def _gla_head_mask():
    C, H = GLA_CHUNK, GLA_HEADS
    assert GLA_DK == C
    blk = np.arange(H * C) // C
    return (blk[:, None] == blk[None, :]).astype(np.float32)


def _gla_chunk(la, q, k, v, st_ref, c_ref, tri_ref, lq_ref, lm_ref, sm_ref, hm_ref, bwd):
    C, H = GLA_CHUNK, GLA_HEADS
    kw = q.shape[1]
    hm = hm_ref[...]

    def stack(xb):
        return jnp.concatenate([xb] * H, axis=0) * hm

    tri = tri_ref[...]
    hi = la.astype(BF16)
    incl = _dot(tri, hi) + _dot(tri, (la - hi.astype(F32)).astype(BF16))
    yield None
    tot = incl[C - 1:C]
    cum = incl if not bwd else incl - la
    x_q, x_k = (cum, tot - cum) if not bwd else (tot - cum, cum)
    e_q, e_k, a_tot = jnp.exp2(x_q), jnp.exp2(x_k), jnp.exp2(tot)
    c_ref[...] = cum

    a = jnp.zeros((C, H * C), F32)
    for lv, m in enumerate(_gla_levels()):
        g = 2 * m
        edge = m - 1 if not bwd else m
        r = jnp.concatenate([jnp.broadcast_to(c_ref[s0 + edge:s0 + edge + 1, :], (g, kw))
                             for s0 in range(0, C, g)], axis=0)
        d = cum - r
        z = (jnp.where(lq_ref[lv] > 0, q, k) * jnp.exp2(jnp.minimum(d, -d))).astype(BF16)
        a = a + _dot_nt(z, stack(z)) * lm_ref[lv]
    ps = []
    for si in range(sm_ref.shape[0]):
        sh = si if not bwd else si + 1
        if sh == 0:
            ps.append((q * k).astype(BF16))
            continue
        amt = sh if not bwd else C - sh
        kr = pltpu.roll(k, amt, 0)
        cr = pltpu.roll(cum, amt, 0)
        d = (cum - cr) if not bwd else (cr - cum)
        ps.append((q * kr * jnp.exp2(jnp.minimum(d, 0.0))).astype(BF16))
    sc = _dot(jnp.concatenate(ps, axis=0), hm)
    yield None
    for si in range(sm_ref.shape[0]):
        a = a + sc[si * C:(si + 1) * C] * sm_ref[si]

    vst = jnp.concatenate([v[:, h * GLA_DV:(h + 1) * GLA_DV] for h in range(H)], axis=0)
    st = st_ref[...]
    o4 = _dot(stack(a.astype(BF16)), vst) + _dot_nt(stack((q * e_q).astype(BF16)), st.astype(BF16))
    st_ref[...] = st * a_tot + _dot_tn(vst, stack((k * e_k).astype(BF16)))
    yield jnp.concatenate([o4[h * C:(h + 1) * C] for h in range(H)], axis=1)


def _gla_kernel(q_ref, k_ref, v_ref, la_ref, gr_ref, ng_ref,
                lqf_ref, lmf_ref, smf_ref, lqb_ref, lmb_ref, smb_ref, tri_ref, hm_ref,
                out_ref, of_ref, ob_ref, stf_ref, stb_ref, cf_ref, cb_ref):
    C = GLA_CHUNK
    S = q_ref.shape[0]
    kw = q_ref.shape[1]
    n = S // C
    stf_ref[...] = jnp.zeros_like(stf_ref)
    stb_ref[...] = jnp.zeros_like(stb_ref)

    def body(i, carry):
        streams, outs = [], []
        for u in range(GLA_UNROLL):
            rf = pl.ds(pl.multiple_of((i * GLA_UNROLL + u) * C, C), C)
            rb = pl.ds(pl.multiple_of((n - 1 - i * GLA_UNROLL - u) * C, C), C)
            streams.append(_gla_chunk(
                la_ref[rf, 0:kw], q_ref[rf, :].astype(F32), k_ref[rf, :].astype(F32),
                v_ref[rf, :], stf_ref, cf_ref.at[u], tri_ref, lqf_ref, lmf_ref, smf_ref, hm_ref,
                False))
            streams.append(_gla_chunk(
                la_ref[rb, kw:2 * kw], q_ref[rb, :].astype(F32), k_ref[rb, :].astype(F32),
                v_ref[rb, :], stb_ref, cb_ref.at[u], tri_ref, lqb_ref, lmb_ref, smb_ref, hm_ref,
                True))
            outs += [(of_ref, rf), (ob_ref, rb)]
        for results in zip(*streams):
            pass
        for (ref, rows), o in zip(outs, results):
            ref[rows, :] = o
        return carry

    lax.fori_loop(0, n // GLA_UNROLL, body, 0)

    rows = 256
    def fin(i, carry):
        r = pl.ds(pl.multiple_of(i * rows, rows), rows)
        o = of_ref[r, :] + ob_ref[r, :]
        ys = []
        for h in range(GLA_HEADS):
            oh = o[:, h * GLA_DV:(h + 1) * GLA_DV]
            ys.append(_rms(oh, ng_ref[...]))
        y = jnp.concatenate(ys, axis=1) * gr_ref[r, :].astype(F32)
        out_ref[r, :] = y.astype(BF16)
        return carry

    lax.fori_loop(0, S // rows, fin, 0)


def _gla(q, k, v, la, gr, ng, B, S):
    kw, vw = q.shape[1], v.shape[1]
    cf = _gla_constants(False)
    cb = _gla_constants(True)
    tri = np.tril(np.ones((GLA_CHUNK, GLA_CHUNK), np.float32))
    consts = [jnp.asarray(c) for c in cf] + [jnp.asarray(c) for c in cb] + \
             [jnp.asarray(tri, BF16), jnp.asarray(_gla_head_mask(), BF16)]
    seq = lambda n: pl.BlockSpec((S, n), lambda b: (b, 0))
    full = lambda a: pl.BlockSpec(a.shape, lambda b: (0,) * a.ndim)
    return pl.pallas_call(
        _gla_kernel,
        grid=(B,),
        in_specs=[seq(kw), seq(kw), seq(vw), seq(2 * kw), seq(vw), full(ng)] + [full(c) for c in consts],
        out_specs=seq(vw),
        out_shape=jax.ShapeDtypeStruct((B * S, vw), BF16),
        scratch_shapes=[pltpu.VMEM((S, vw), F32), pltpu.VMEM((S, vw), F32),
                        pltpu.VMEM((GLA_DV, kw), F32), pltpu.VMEM((GLA_DV, kw), F32),
                        pltpu.VMEM((GLA_UNROLL, GLA_CHUNK, kw), F32),
                        pltpu.VMEM((GLA_UNROLL, GLA_CHUNK, kw), F32)],
        compiler_params=_params(("parallel",)),
        name="gla",
    )(q, k, v, la, gr, ng, *consts)


def _conv_kernel(u_ref, w_ref, cb_ref, lg_ref, lb_ref, out_ref, *, tm, halo):
    S = u_ref.shape[0]
    j = pl.program_id(1)
    t0 = pl.multiple_of(j * tm, tm)
    top = u_ref[pl.ds(pl.multiple_of(jnp.maximum(t0 - halo, 0), halo), halo), :]
    bot = u_ref[pl.ds(pl.multiple_of(jnp.minimum(t0 + tm, S - halo), halo), halo), :]
    top = jnp.where(j > 0, top, 0.0)
    bot = jnp.where(j < pl.num_programs(1) - 1, bot, 0.0)
    win = jnp.concatenate([top, u_ref[pl.ds(t0, tm), :], bot], axis=0)
    rows = tm + 2 * halo
    pad = (CONV_KERNEL - 1) // 2
    w = w_ref[...]
    acc = jnp.zeros((tm, u_ref.shape[1]), F32)
    for r in range(8):
        offs = [k for k in range(CONV_KERNEL) if (halo - pad + k) % 8 == r]
        if not offs:
            continue
        xr = win if r == 0 else pltpu.roll(win, rows - r, 0)
        for k in offs:
            a0 = (halo - pad + k) - r
            acc = acc + xr[a0:a0 + tm, :] * w[k:k + 1, :]
    y = acc + cb_ref[...]
    mu = jnp.mean(y, axis=-1, keepdims=True)
    yc = y - mu
    var = jnp.mean(yc * yc, axis=-1, keepdims=True)
    z = yc * lax.rsqrt(var + LN_EPS) * lg_ref[...] + lb_ref[...]
    out_ref[...] = (z * jax.nn.sigmoid(z)).astype(BF16)


def _conv(u, w, cb, lg, lb, B, S, tm):
    cw = u.shape[1]
    halo = 16
    full = lambda a: pl.BlockSpec(a.shape, lambda b, j: (0,) * a.ndim)
    nj = S // tm
    return pl.pallas_call(
        functools.partial(_conv_kernel, tm=tm, halo=halo),
        grid=(B, nj),
        in_specs=[pl.BlockSpec((S, cw), lambda b, j: (b, 0)), full(w), full(cb), full(lg), full(lb)],
        out_specs=pl.BlockSpec((tm, cw), lambda b, j: (b * nj + j, 0)),
        out_shape=jax.ShapeDtypeStruct((B * S, cw), BF16),
        compiler_params=_params(("parallel", "parallel")),
        name="conv",
    )(u, w, cb, lg, lb)


def _memkv_kernel(m_ref, g_ref, wk_ref, wv_ref, k_ref, v_ref):
    m = _rms(m_ref[...], g_ref[...]).astype(BF16)
    k_ref[...] = _dot(m, wk_ref[...]).astype(BF16)
    v_ref[...] = _dot(m, wv_ref[...]).astype(BF16)


def _memkv(mem2d, g, wk, wv, M):
    R, D = mem2d.shape
    full = lambda a: pl.BlockSpec(a.shape, lambda i: (0,) * a.ndim)
    blk = pl.BlockSpec((M, D), lambda i: (i, 0))
    return pl.pallas_call(
        _memkv_kernel,
        grid=(R // M,),
        in_specs=[blk, full(g), full(wk), full(wv)],
        out_specs=[blk, blk],
        out_shape=[jax.ShapeDtypeStruct((R, D), BF16)] * 2,
        compiler_params=_params(("parallel",)),
        name="memkv",
    )(mem2d, g, wk, wv)


def _mid_kernel(x_ref, a_ref, b_ref, wout_ref, gx_ref, wq_ref, km_ref, vm_ref, wo_ref,
                gf_ref, rw_ref, rb_ref, x2_ref, h3_ref, idx_ref, gate_ref, *, n_exp, parts):
    n = x_ref.shape[0] // parts
    gens = [_mid_rows(pl.ds(p * n, n), pl.ds(p * n * SUBLANES, n * SUBLANES),
                      x_ref, a_ref, b_ref, wout_ref, gx_ref, wq_ref, km_ref, vm_ref, wo_ref,
                      gf_ref, rw_ref, rb_ref, x2_ref, h3_ref, idx_ref, gate_ref, n_exp)
            for p in range(parts)]
    for _ in zip(*gens):
        pass


def _mid_rows(rows, tiles, x_ref, a_ref, b_ref, wout_ref, gx_ref, wq_ref, km_ref, vm_ref, wo_ref,
              gf_ref, rw_ref, rb_ref, x2_ref, h3_ref, idx_ref, gate_ref, n_exp):
    ab = jnp.concatenate([a_ref[rows, :], b_ref[rows, :]], axis=1)
    x1 = x_ref[rows, :] + _dot(ab, wout_ref[...])
    yield
    q2 = _dot(_rms(x1, gx_ref[...]).astype(BF16), wq_ref[...])
    yield
    D = q2.shape[1]
    dh = D // XATTN_HEADS
    ss = []
    for h in range(XATTN_HEADS):
        sl = slice(h * dh, (h + 1) * dh)
        ss.append(_dot_nt(q2[:, sl].astype(BF16), km_ref[:, sl]) * (dh ** -0.5 * LOG2_E))
    yield
    outs = []
    for h in range(XATTN_HEADS):
        sl = slice(h * dh, (h + 1) * dh)
        p = jnp.exp2(ss[h] - jnp.max(ss[h], axis=-1, keepdims=True))
        p = p / jnp.sum(p, axis=-1, keepdims=True)
        outs.append(_dot(p.astype(BF16), vm_ref[:, sl]))
    yield
    o = jnp.concatenate(outs, axis=1).astype(BF16)
    x2 = x1 + _dot(o, wo_ref[...])
    yield
    x2_ref[rows, :] = x2
    h3 = _rms(x2, gf_ref[...])
    _store_rows_as_tiles(h3_ref.at[tiles], h3)
    lane = lax.broadcasted_iota(jnp.int32, (x2.shape[0], LANES), 1)
    logits = _dot(h3.astype(BF16), rw_ref[...]) + rb_ref[...]
    yield
    cur = jnp.where(lane < n_exp, logits, -jnp.inf)
    lane_f = lane.astype(F32)
    vals, idxs = [], []
    for _ in range(TOP_K):
        m = jnp.max(cur, axis=-1, keepdims=True)
        ik = jnp.min(jnp.where(cur == m, lane_f, float(LANES)), axis=-1, keepdims=True)
        vals.append(m)
        idxs.append(ik)
        cur = jnp.where(lane_f == ik, -jnp.inf, cur)
    es = [jnp.exp(vk - vals[0]) for vk in vals]
    den = es[0]
    for ek in es[1:]:
        den = den + ek
    idx_out = jnp.zeros(lane.shape, F32)
    gate_out = jnp.zeros(lane.shape, F32)
    for kk in range(TOP_K):
        idx_out = jnp.where(lane == kk, idxs[kk], idx_out)
        gate_out = jnp.where(lane == kk, es[kk] / den, gate_out)
    idx_ref[rows, :] = idx_out.astype(jnp.int32)
    gate_ref[:, rows] = gate_out.T[0:SUBLANES, :]
    yield


def _mid(x2d, a, b, wout, gx, wq, km, vm, wo, gf, rw, rb, S, M, n_exp, tm):
    T, D = x2d.shape
    half = a.shape[1]
    per_b = S // tm
    row = lambda n: pl.BlockSpec((tm, n), lambda i: (i, 0))
    full = lambda arr: pl.BlockSpec(arr.shape, lambda i: (0,) * arr.ndim)
    memb = pl.BlockSpec((M, D), lambda i: (i // per_b, 0))
    return pl.pallas_call(
        functools.partial(_mid_kernel, n_exp=n_exp, parts=2),
        grid=(T // tm,),
        in_specs=[row(D), row(half), row(half), full(wout), full(gx), full(wq), memb, memb,
                  full(wo), full(gf), full(rw), full(rb)],
        out_specs=[row(D), pl.BlockSpec((tm * SUBLANES, LANES), lambda i: (i, 0)),
                   row(LANES), pl.BlockSpec((SUBLANES, tm), lambda i: (i // 2, i % 2))],
        out_shape=[jax.ShapeDtypeStruct((T, D), F32),
                   jax.ShapeDtypeStruct((T * SUBLANES, LANES), F32),
                   jax.ShapeDtypeStruct((T, LANES), jnp.int32),
                   jax.ShapeDtypeStruct((T // (2 * tm) * SUBLANES, 2 * tm), F32)],
        compiler_params=_params(("parallel",)),
        name="mid",
    )(x2d, a, b, wout, gx, wq, km, vm, wo, gf, rw, rb)


def _place_kernel(idx_ref, tri_ref, upper_ref, pos_ref, cnt_ref):
    tm = idx_ref.shape[0] // 2
    for _ in zip(*[_place_tile(half, tm, idx_ref, tri_ref, upper_ref, pos_ref, cnt_ref)
                   for half in range(2)]):
        pass


def _place_tile(half, tm, idx_ref, tri_ref, upper_ref, pos_ref, cnt_ref):
    rows = pl.ds(half * tm, tm)
    lane = lax.broadcasted_iota(jnp.int32, (tm, LANES), 1)
    hots = [(lane == idx_ref[rows, kk:kk + 1]).astype(F32) for kk in range(TOP_K)]
    mask = hots[0]
    for hk in hots[1:]:
        mask = mask + hk
    incl = _dot(tri_ref[...], mask.astype(BF16))
    yield
    cnt = jnp.broadcast_to(incl[tm - 1:tm, :], (SUBLANES, LANES))
    hi = jnp.floor(cnt * (1.0 / 16.0))
    lo = cnt - 16.0 * hi
    start = 16.0 * _dot(hi.astype(BF16), upper_ref[...]) + _dot(lo.astype(BF16), upper_ref[...])
    yield
    before = incl - mask + start[0:1, :]
    out = jnp.zeros((tm, LANES), F32)
    for kk in range(TOP_K):
        rk = jnp.sum(hots[kk] * before, axis=-1, keepdims=True)
        out = jnp.where(lane == kk, rk, out)
    pos_ref[:, rows] = (out.T[0:SUBLANES, :] * SUBLANES).astype(jnp.int32)
    cnt_ref[pl.ds(half * SUBLANES, SUBLANES), :] = cnt
    yield


def _place(idx, tm):
    T = idx.shape[0]
    assert tm <= 16 * 256
    tri = jnp.asarray(np.tril(np.ones((tm, tm), np.float32)), BF16)
    upper = jnp.asarray(np.triu(np.ones((LANES, LANES), np.float32), 1), BF16)
    return pl.pallas_call(
        _place_kernel,
        grid=(T // (2 * tm),),
        in_specs=[pl.BlockSpec((2 * tm, LANES), lambda i: (i, 0)),
                  pl.BlockSpec((tm, tm), lambda i: (0, 0)),
                  pl.BlockSpec((LANES, LANES), lambda i: (0, 0))],
        out_specs=[pl.BlockSpec((SUBLANES, 2 * tm), lambda i: (i, 0)),
                   pl.BlockSpec((2 * SUBLANES, LANES), lambda i: (i, 0))],
        out_shape=[jax.ShapeDtypeStruct((T // (2 * tm) * SUBLANES, 2 * tm), jnp.int32),
                   jax.ShapeDtypeStruct((T // tm * SUBLANES, LANES), F32)],
        compiler_params=_params(("parallel",)),
        name="place",
    )(idx, tri, upper)


ROW_UNROLL = 16


def _wait_rows(buf_ref, sem):
    pltpu.make_async_copy(buf_ref, buf_ref, sem).wait()


RUN_CHUNK = 32


def _rows(ref, row, n):
    return ref.at[pl.ds(pl.multiple_of(row * SUBLANES, SUBLANES), n * SUBLANES)]


def _for_run_pieces(n, fn):
    shift = RUN_CHUNK.bit_length() - 1
    whole = lax.shift_right_logical(n, shift)

    def body(c, carry):
        fn(c * RUN_CHUNK, RUN_CHUNK)
        return carry

    lax.fori_loop(0, whole, body, 0)
    off = whole * RUN_CHUNK
    for bit in reversed(range(shift)):
        take = (n & (1 << bit)) != 0

        @pl.when(take)
        def _(off=off, bit=bit):
            fn(off, 1 << bit)

        off = off + jnp.where(take, 1 << bit, 0)


def _dispatch_kernel(pos_ref, rdst_ref, rn_ref, rloc_ref, lo_ref, hi_ref, nb_ref, h_ref, xs_ref,
                     sb0, sb1, zero_ref, sem, zsem):
    j = pl.program_id(0)
    tm = h_ref.shape[0] // SUBLANES // 2
    slots = rn_ref.shape[0] // 2
    n_exp = lo_ref.shape[0]
    blk = zero_ref.shape[0]

    def sort(half, sb):
        def body(i, carry):
            for u in range(ROW_UNROLL):
                t = half * tm + i * ROW_UNROLL + u
                tile = h_ref[pl.ds(pl.multiple_of(t * SUBLANES, SUBLANES), SUBLANES), :]
                for kk in range(TOP_K):
                    p = pos_ref[kk * 2 * tm + t]
                    sb[pl.ds(pl.multiple_of(p, SUBLANES), SUBLANES), :] = tile
            return carry

        lax.fori_loop(0, tm // ROW_UNROLL, body, 0)

    def start_runs(half, sb, s):
        def expert(e, carry):
            r = half * slots + e
            loc, dst = rloc_ref[r], rdst_ref[r]
            _for_run_pieces(rn_ref[r], lambda off, n: pltpu.make_async_copy(
                _rows(sb, loc + off, n), _rows(xs_ref, dst + off, n), s).start())
            return carry

        lax.fori_loop(0, n_exp, expert, 0)

    @pl.when(j == 0)
    def _():
        zero_ref[...] = jnp.zeros_like(zero_ref)

        def each_fill(fn):
            def expert(e, carry):
                row = lo_ref[e]
                n = hi_ref[e] - row
                for bit in reversed(range(MOE_ROWS.bit_length() - 1)):
                    size = (1 << bit) * SUBLANES
                    take = (n & (1 << bit)) != 0

                    @pl.when(take)
                    def _(row=row, size=size):
                        dst = xs_ref.at[pl.ds(pl.multiple_of(row * SUBLANES, SUBLANES), size)]
                        fn(pltpu.make_async_copy(zero_ref.at[pl.ds(0, size)], dst, zsem))

                    row = row + jnp.where(take, 1 << bit, 0)
                return carry

            lax.fori_loop(0, lo_ref.shape[0], expert, 0)

            def block(b, carry):
                fn(pltpu.make_async_copy(
                    zero_ref, xs_ref.at[pl.ds(pl.multiple_of(b * blk, blk), blk)], zsem))
                return carry

            lax.fori_loop(nb_ref[0], xs_ref.shape[0] // blk, block, 0)

        each_fill(lambda cp: cp.start())
        each_fill(lambda cp: cp.wait())

    for half, sb in ((0, sb0), (1, sb1)):
        @pl.when(j > 0)
        def _(half=half, sb=sb):
            _wait_rows(sb, sem.at[half])

        sort(half, sb)
        start_runs(half, sb, sem.at[half])

    @pl.when(j == pl.num_programs(0) - 1)
    def _():
        _wait_rows(sb0, sem.at[0])
        _wait_rows(sb1, sem.at[1])


def _dispatch(pos, run_dst, run_n, run_loc, pad_lo, pad_hi, n_used, h3, n_rows, tm):
    T = h3.shape[0] // SUBLANES
    slots = run_n.shape[0] // (T // tm)
    smem = pl.BlockSpec(memory_space=pltpu.SMEM)
    per_step = lambda n: pl.BlockSpec((n,), lambda j: (j,), memory_space=pltpu.SMEM)
    sorted_rows = pltpu.VMEM((tm * TOP_K * SUBLANES, LANES), F32)
    return pl.pallas_call(
        _dispatch_kernel,
        grid=(T // (2 * tm),),
        in_specs=[per_step(SUBLANES * 2 * tm),
                  per_step(2 * slots), per_step(2 * slots),
                  per_step(2 * slots), smem, smem, smem,
                  pl.BlockSpec((2 * tm * SUBLANES, LANES), lambda j: (j, 0))],
        out_specs=pl.BlockSpec(memory_space=pl.ANY),
        out_shape=jax.ShapeDtypeStruct((n_rows * SUBLANES, LANES), F32),
        scratch_shapes=[sorted_rows, sorted_rows, pltpu.VMEM((MOE_ROWS * SUBLANES, LANES), F32),
                        pltpu.SemaphoreType.DMA((2,)), pltpu.SemaphoreType.DMA(())],
        compiler_params=_params(("arbitrary",)),
        name="dispatch",
    )(pos, run_dst, run_n, run_loc, pad_lo, pad_hi, n_used, h3)


def _experts_kernel(be_ref, live_ref, first_ref, next_ref, slot_ref,
                    xs_ref, wu_hbm, bu_ref, wd_hbm, bd_ref, ys_ref,
                    wu_f32, wd_f32, wu_bf, wd_bf, sem):
    i = pl.program_id(0)

    def fetch(e, s):
        return (pltpu.make_async_copy(wu_hbm.at[e], wu_f32.at[s], sem.at[s]),
                pltpu.make_async_copy(wd_hbm.at[e], wd_f32.at[s], sem.at[s]))

    @pl.when(i == 0)
    def _():
        for cp in fetch(be_ref[0], 0):
            cp.start()

    @pl.when(first_ref[i] == 1)
    def _():
        s = slot_ref[i]
        for cp in fetch(be_ref[i], s):
            cp.wait()

        @pl.when(next_ref[i] >= 0)
        def _():
            for cp in fetch(next_ref[i], 1 - s):
                cp.start()

        wu_bf[...] = wu_f32[s].astype(BF16)
        wd_bf[...] = wd_f32[s].astype(BF16)

    def mlp(tiles):
        F = wd_bf.shape[0]
        x = _load_rows_from_tiles(xs_ref.at[tiles]).astype(BF16)
        gu = _dot(x, wu_bf[...]) + bu_ref[0]
        yield
        x_glu = jnp.minimum(gu[:, :F], SWIGLU_LIMIT)
        x_lin = jnp.clip(gu[:, F:], -SWIGLU_LIMIT, SWIGLU_LIMIT)
        hid = x_glu * jax.nn.sigmoid(SWIGLU_ALPHA * x_glu) * (x_lin + 1.0)
        y = _dot(hid.astype(BF16), wd_bf[...]) + bd_ref[0]
        yield
        _store_rows_as_tiles(ys_ref.at[tiles], y)
        yield

    n = xs_ref.shape[0] // MOE_PARTS
    for live in range(MOE_PARTS + 1):
        @pl.when(live_ref[i] == live)
        def _(live=live):
            for _ in zip(*[mlp(pl.ds(p * n, n)) for p in range(live)]):
                pass
            if live < MOE_PARTS:
                rest = (MOE_PARTS - live) * n
                ys_ref[pl.ds(live * n, rest), :] = jnp.zeros((rest, ys_ref.shape[1]), F32)


def _expert_runs(block_e):
    nb = block_e.shape[0]
    ar = jnp.arange(nb, dtype=jnp.int32)
    first = jnp.concatenate([jnp.ones((1,), bool), block_e[1:] != block_e[:-1]])
    starts = jnp.where(first, ar, nb)
    later = lax.cummin(jnp.concatenate([starts[1:], jnp.full((1,), nb, jnp.int32)]), reverse=True)
    nxt = jnp.where(later < nb, block_e[jnp.minimum(later, nb - 1)], -1)
    slot = (jnp.cumsum(first.astype(jnp.int32)) - 1) & 1
    return first.astype(jnp.int32), nxt.astype(jnp.int32), slot.astype(jnp.int32)


def _experts(block_e, live, xs, wu, bu, wd, bd):
    E, D, F2 = wu.shape
    F = wd.shape[1]
    nb = xs.shape[0] // (MOE_ROWS * SUBLANES)
    rows = pl.BlockSpec((MOE_ROWS * SUBLANES, LANES), lambda i, *_: (i, 0))
    grid_spec = pltpu.PrefetchScalarGridSpec(
        num_scalar_prefetch=5,
        grid=(nb,),
        in_specs=[rows,
                  pl.BlockSpec(memory_space=pl.ANY),
                  pl.BlockSpec((1, 1, F2), lambda i, be, *_: (be[i], 0, 0)),
                  pl.BlockSpec(memory_space=pl.ANY),
                  pl.BlockSpec((1, 1, D), lambda i, be, *_: (be[i], 0, 0))],
        out_specs=rows,
        scratch_shapes=[pltpu.VMEM((2, D, F2), F32), pltpu.VMEM((2, F, D), F32),
                        pltpu.VMEM((D, F2), BF16), pltpu.VMEM((F, D), BF16),
                        pltpu.SemaphoreType.DMA((2,))],
    )
    return pl.pallas_call(
        _experts_kernel,
        grid_spec=grid_spec,
        out_shape=jax.ShapeDtypeStruct(xs.shape, F32),
        compiler_params=_params(("arbitrary",)),
        name="experts",
    )(block_e, live, *_expert_runs(block_e), xs, wu, bu, wd, bd)


def _combine_kernel(pos_ref, gate_ref, src_ref, rn_ref, rloc_ref, nsrc_ref, nrn_ref, nrloc_ref,
                    x2_ref, g_ref, ys_ref, out_ref, yb0, yb1, acc_ref, sem, *, n_exp, apply_norm):
    j = pl.program_id(0)
    tm = x2_ref.shape[0] // 2
    slots = rn_ref.shape[0] // 2

    def start(tbls, half, yb, s):
        src_t, n_t, loc_t = tbls

        def expert(e, carry):
            r = half * slots + e
            loc, src = loc_t[r], src_t[r]
            _for_run_pieces(n_t[r], lambda off, n: pltpu.make_async_copy(
                _rows(ys_ref, src + off, n), _rows(yb, loc + off, n), s).start())
            return carry

        lax.fori_loop(0, n_exp, expert, 0)

    def finish(half, yb, s):
        _wait_rows(yb, s)

        def body(i, carry):
            for u in range(ROW_UNROLL):
                t = i * ROW_UNROLL + u
                a = half * tm + t
                acc = None
                for kk in range(TOP_K):
                    p = pos_ref[kk * 2 * tm + a]
                    term = yb[pl.ds(pl.multiple_of(p, SUBLANES), SUBLANES), :] \
                        * gate_ref[kk * 2 * tm + a]
                    acc = term if acc is None else acc + term
                acc_ref[pl.ds(pl.multiple_of(t * SUBLANES, SUBLANES), SUBLANES), :] = acc
            return carry

        lax.fori_loop(0, tm // ROW_UNROLL, body, 0)
        rows = pl.ds(half * tm, tm)
        x = x2_ref[rows, :] + _load_rows_from_tiles(acc_ref)
        out_ref[rows, :] = _rms(x, g_ref[...]) if apply_norm else x

    cur = (src_ref, rn_ref, rloc_ref)

    @pl.when(j == 0)
    def _():
        start(cur, 0, yb0, sem.at[0])

    start(cur, 1, yb1, sem.at[1])
    finish(0, yb0, sem.at[0])

    @pl.when(j < pl.num_programs(0) - 1)
    def _():
        start((nsrc_ref, nrn_ref, nrloc_ref), 0, yb0, sem.at[0])

    finish(1, yb1, sem.at[1])


def _combine(pos, gates, run_src, run_n, run_loc, x2, g, ys, tm, n_exp, apply_norm):
    T, D = x2.shape
    steps = T // (2 * tm)
    slots = run_n.shape[0] // (T // tm)
    per_step = lambda n: pl.BlockSpec((n,), lambda j: (j,), memory_space=pltpu.SMEM)
    next_step = lambda n: pl.BlockSpec((n,), lambda j: (jnp.minimum(j + 1, steps - 1),),
                                       memory_space=pltpu.SMEM)
    sorted_rows = pltpu.VMEM((tm * TOP_K * SUBLANES, LANES), F32)
    return pl.pallas_call(
        functools.partial(_combine_kernel, n_exp=n_exp, apply_norm=apply_norm),
        grid=(steps,),
        in_specs=[per_step(SUBLANES * 2 * tm),
                  per_step(SUBLANES * 2 * tm),
                  per_step(2 * slots), per_step(2 * slots), per_step(2 * slots),
                  next_step(2 * slots), next_step(2 * slots), next_step(2 * slots),
                  pl.BlockSpec((2 * tm, D), lambda j: (j, 0)),
                  pl.BlockSpec(g.shape, lambda j: (0, 0)),
                  pl.BlockSpec(memory_space=pl.ANY)],
        out_specs=pl.BlockSpec((2 * tm, D), lambda j: (j, 0)),
        out_shape=jax.ShapeDtypeStruct((T, D), F32),
        scratch_shapes=[sorted_rows, sorted_rows, pltpu.VMEM((tm * SUBLANES, LANES), F32),
                        pltpu.SemaphoreType.DMA((2,))],
        compiler_params=_params(("arbitrary",)),
        name="combine",
    )(pos, gates, run_src, run_n, run_loc, run_src, run_n, run_loc, x2, g, ys)


def _row(v):
    return v.reshape(1, -1).astype(F32)


def kernel(x, mem, norm_mix_g, w_in, gate_w2_fwd, gate_b_fwd, gate_w2_bwd, gate_b_bwd,
           gla_norm_g, conv_w, conv_b, conv_ln_g, conv_ln_b, w_out, norm_xattn_g,
           norm_mem_g, xattn_wq, xattn_wk, xattn_wv, xattn_wo, norm_ffn_g, router_w,
           router_b, exp_w_up, exp_b_up, exp_w_down, exp_b_down, final_norm_g):
    B, S, D = x.shape
    M = mem.shape[1]
    T = B * S
    depth = w_in.shape[0]
    kw = GLA_HEADS * GLA_DK
    vw = GLA_HEADS * GLA_DV
    rank_w = gate_w2_fwd.shape[1]
    cw = conv_w.shape[2]
    E = router_w.shape[2]
    assert vw + cw == w_out.shape[1] and 2 * rank_w <= LANES and E <= LANES

    xc = x.reshape(T, D)
    for l in range(depth):
        o_gf = 2 * kw + 2 * vw
        o_glu = o_gf + 2 * rank_w
        wl = w_in[l]
        w_gate = jnp.pad(wl[:, o_gf:o_glu], ((0, 0), (0, LANES - 2 * rank_w)))
        ws = [wl[:, :o_gf].astype(BF16), wl[:, o_glu:].astype(BF16), w_gate.astype(BF16)]
        w2 = jnp.zeros((LANES, 2 * kw), F32)
        w2 = w2.at[:rank_w, :kw].set(gate_w2_fwd[l]).at[rank_w:2 * rank_w, kw:].set(gate_w2_bwd[l])
        b2 = jnp.concatenate([gate_b_fwd[l], gate_b_bwd[l]]).reshape(1, -1)

        q, k, v, gr, la, u = _inproj(xc, _row(norm_mix_g[l]), ws, w2.astype(BF16), b2,
                                     kw, vw, cw, tm=512)
        a_out = _gla(q, k, v, la, gr, _row(gla_norm_g[l]), B, S)
        b_out = _conv(u, conv_w[l], _row(conv_b[l]), _row(conv_ln_g[l]), _row(conv_ln_b[l]),
                      B, S, tm=512)
        km, vm = _memkv(mem.reshape(B * M, D), _row(norm_mem_g[l]),
                        xattn_wk[l].astype(BF16), xattn_wv[l].astype(BF16), M)

        rw = jnp.zeros((D, LANES), F32).at[:, :E].set(router_w[l]).astype(BF16)
        rb = jnp.zeros((1, LANES), F32).at[0, :E].set(router_b[l])
        x2, h3, idx, gates = _mid(
            xc, a_out, b_out, w_out[l].astype(BF16), _row(norm_xattn_g[l]),
            xattn_wq[l].astype(BF16), km, vm, xattn_wo[l].astype(BF16),
            _row(norm_ffn_g[l]), rw, rb, S, M, E, tm=MOE_TILE)

        tm = MOE_TILE
        pos, tile_cnt = _place(idx, tm=tm)
        n_te = tile_cnt.reshape(T // tm, SUBLANES, LANES)[:, 0, :E].astype(jnp.int32)
        counts = jnp.sum(n_te, axis=0)
        padded = ((counts + MOE_ROWS - 1) // MOE_ROWS) * MOE_ROWS
        pends = jnp.cumsum(padded)
        run_row = (pends - padded)[None, :] + jnp.cumsum(n_te, axis=0) - n_te
        run_loc = jnp.cumsum(n_te, axis=1) - n_te
        slots = max(E, LANES // 2)
        runs = [jnp.pad(a, ((0, 0), (0, slots - E))).reshape(-1).astype(jnp.int32)
                for a in (run_row, n_te, run_loc)]
        n_blocks = (T * TOP_K + E * (MOE_ROWS - 1)) // MOE_ROWS
        blk_start = jnp.arange(n_blocks, dtype=jnp.int32) * MOE_ROWS
        block_e = jnp.minimum(jnp.sum(pends[None, :] <= blk_start[:, None], axis=1), E - 1)
        n_used = (pends[E - 1] // MOE_ROWS).reshape(1)

        pos, gates = pos.reshape(-1), gates.reshape(-1)
        xs = _dispatch(pos, *runs, (pends - padded + counts).astype(jnp.int32),
                       pends.astype(jnp.int32), n_used.astype(jnp.int32), h3,
                       n_blocks * MOE_ROWS, tm=tm)
        group = MOE_ROWS // MOE_PARTS
        real = jnp.clip((pends - padded + counts)[block_e] - blk_start, 0, MOE_ROWS)
        live = jnp.where(blk_start < pends[E - 1], (real + group - 1) // group, 0)
        ys = _experts(block_e.astype(jnp.int32), live.astype(jnp.int32), xs,
                      exp_w_up[l], exp_b_up[l][:, None, :],
                      exp_w_down[l], exp_b_down[l][:, None, :])
        xc = _combine(pos, gates, *runs, x2,
                      _row(final_norm_g), ys, tm=tm, n_exp=E, apply_norm=(l == depth - 1))
    return xc.reshape(B, S, D)
```

```python
import functools

import numpy as np
import jax
import jax.numpy as jnp
from jax import lax
from jax.experimental import pallas as pl
from jax.experimental.pallas import tpu as pltpu

F32 = jnp.float32
BF16 = jnp.bfloat16

GLA_HEADS = 4
GLA_DK = 64
GLA_DV = 128
GLA_CHUNK = 64
GLA_TILE = 4
GLA_UNROLL = 4
GATE_TAU = 16.0
LOG2_E = 1.4426950408889634
CONV_KERNEL = 31
XATTN_HEADS = 4
TOP_K = 4
SWIGLU_ALPHA = 1.702
SWIGLU_LIMIT = 7.0
RMS_EPS = 1e-6
LN_EPS = 1e-5

LANES = 128
SUBLANES = 8
MOE_ROWS = 512
MOE_PARTS = 2
MOE_TILE = 512
VMEM_LIMIT = 56 * 1024 * 1024


def _dot(a, b):
    return jnp.dot(a, b, preferred_element_type=F32)


def _dot_nt(a, b):
    return lax.dot_general(a, b, (((1,), (1,)), ((), ())), preferred_element_type=F32)


def _dot_tn(a, b):
    return lax.dot_general(a, b, (((0,), (0,)), ((), ())), preferred_element_type=F32)


def _rms(x, g):
    return x * lax.rsqrt(jnp.mean(x * x, axis=-1, keepdims=True) + RMS_EPS) * g


def _params(sem):
    return pltpu.CompilerParams(dimension_semantics=sem, vmem_limit_bytes=VMEM_LIMIT)


def _store_rows_as_tiles(ref, val):
    n, d = val.shape
    assert d == SUBLANES * LANES and ref.shape == (n * SUBLANES, LANES)
    for s in range(SUBLANES):
        ref[pl.ds(s, n, stride=SUBLANES), :] = val[:, s * LANES:(s + 1) * LANES]


def _load_rows_from_tiles(ref):
    n = ref.shape[0] // SUBLANES
    return jnp.concatenate([ref[pl.ds(s, n, stride=SUBLANES), :] for s in range(SUBLANES)], axis=1)


def _inproj_kernel(x_ref, g_ref, w_qkvr_ref, w_glu_ref, w_gate_ref, w2_ref, b2_ref,
                   q_ref, k_ref, v_ref, gr_ref, la_ref, u_ref, *, kw, vw, cw, parts):
    n = x_ref.shape[0] // parts
    w_ref = (w_qkvr_ref, w_glu_ref, w_gate_ref)
    gens = [_inproj_rows(pl.ds(p * n, n), x_ref, g_ref, w_ref, w2_ref, b2_ref,
                         q_ref, k_ref, v_ref, gr_ref, la_ref, u_ref, kw, vw, cw)
            for p in range(parts)]
    for _ in zip(*gens):
        pass


def _inproj_rows(rows, x_ref, g_ref, w_ref, w2_ref, b2_ref,
                 q_ref, k_ref, v_ref, gr_ref, la_ref, u_ref, kw, vw, cw):
    w_qkvr_ref, w_glu_ref, w_gate_ref = w_ref
    h = _rms(x_ref[rows, :], g_ref[...]).astype(BF16)
    acc = _dot(h, w_qkvr_ref[...])
    glu = _dot(h, w_glu_ref[...])
    ranks = _dot(h, w_gate_ref[...])
    yield
    o = 0
    q_ref[rows, :] = (acc[:, o:o + kw] * (GLA_DK ** -0.5)).astype(BF16); o += kw
    k_ref[rows, :] = acc[:, o:o + kw].astype(BF16); o += kw
    v_ref[rows, :] = acc[:, o:o + vw].astype(BF16); o += vw
    r = acc[:, o:o + vw]
    gr_ref[rows, :] = (r * jax.nn.sigmoid(r)).astype(BF16)
    u_ref[rows, :] = glu[:, :cw] * jax.nn.sigmoid(glu[:, cw:])
    z = _dot(ranks.astype(BF16), w2_ref[...]) + b2_ref[...]
    yield
    la_ref[rows, :] = (jnp.minimum(z, 0.0) - jnp.log1p(jnp.exp(-jnp.abs(z)))) * (LOG2_E / GATE_TAU)
    yield


def _inproj(x2d, g, ws, w2, b2, kw, vw, cw, tm):
    T, D = x2d.shape
    row = lambda n: pl.BlockSpec((tm, n), lambda i: (i, 0))
    full = lambda a: pl.BlockSpec(a.shape, lambda i: (0,) * a.ndim)
    return pl.pallas_call(
        functools.partial(_inproj_kernel, kw=kw, vw=vw, cw=cw, parts=2),
        grid=(T // tm,),
        in_specs=[row(D), full(g)] + [full(w) for w in ws] + [full(w2), full(b2)],
        out_specs=[row(kw), row(kw), row(vw), row(vw), row(2 * kw), row(cw)],
        out_shape=[jax.ShapeDtypeStruct((T, kw), BF16), jax.ShapeDtypeStruct((T, kw), BF16),
                   jax.ShapeDtypeStruct((T, vw), BF16), jax.ShapeDtypeStruct((T, vw), BF16),
                   jax.ShapeDtypeStruct((T, 2 * kw), F32), jax.ShapeDtypeStruct((T, cw), F32)],
        compiler_params=_params(("parallel",)),
        name="inproj",
    )(x2d, g, *ws, w2, b2)


def _gla_levels():
    ms, m = [], GLA_CHUNK // 2
    while m >= GLA_TILE:
        ms.append(m)
        m //= 2
    return ms


def _gla_constants(bwd):
    C, H = GLA_CHUNK, GLA_HEADS
    t = np.arange(C)[:, None]
    s = np.arange(C)[None, :]
    lev_q, lev_mask = [], []
    for m in _gla_levels():
        g = 2 * m
        second = (t % g) >= m
        same = t // g == s // g
        if not bwd:
            qrow = second
            mask = same & ((t % g) >= m) & ((s % g) < m)
        else:
            qrow = ~second
            mask = same & ((t % g) < m) & ((s % g) >= m)
        lev_q.append(np.broadcast_to(qrow, (C, H * GLA_DK)))
        lev_mask.append(np.tile(mask, (1, H)))
    shifts = range(0, GLA_TILE) if not bwd else range(1, GLA_TILE)
    sh_mask = []
    for sh in shifts:
        j = t - sh if not bwd else t + sh
        sh_mask.append(np.tile((s == j) & (t // GLA_TILE == s // GLA_TILE), (1, H)))
    f = lambda xs: np.stack(xs).astype(np.float32)
    return f(lev_q), f(lev_mask), f(sh_mask)


def _gla_head_mask():
    C, H = GLA_CHUNK, GLA_HEADS
    assert GLA_DK == C
    blk = np.arange(H * C) // C
    return (blk[:, None] == blk[None, :]).astype(np.float32)


def _gla_chunk(la, q, k, v, st_ref, c_ref, tri_ref, lq_ref, lm_ref, sm_ref, hm_ref, bwd):
    C, H = GLA_CHUNK, GLA_HEADS
    kw = q.shape[1]
    hm = hm_ref[...]

    def stack(xb):
        return jnp.concatenate([xb] * H, axis=0) * hm

    tri = tri_ref[...]
    hi = la.astype(BF16)
    incl = _dot(tri, hi) + _dot(tri, (la - hi.astype(F32)).astype(BF16))
    yield None
    tot = incl[C - 1:C]
    cum = incl if not bwd else incl - la
    x_q, x_k = (cum, tot - cum) if not bwd else (tot - cum, cum)
    e_q, e_k, a_tot = jnp.exp2(x_q), jnp.exp2(x_k), jnp.exp2(tot)
    c_ref[...] = cum

    a = jnp.zeros((C, H * C), F32)
    for lv, m in enumerate(_gla_levels()):
        g = 2 * m
        edge = m - 1 if not bwd else m
        r = jnp.concatenate([jnp.broadcast_to(c_ref[s0 + edge:s0 + edge + 1, :], (g, kw))
                             for s0 in range(0, C, g)], axis=0)
        d = cum - r
        z = (jnp.where(lq_ref[lv] > 0, q, k) * jnp.exp2(jnp.minimum(d, -d))).astype(BF16)
        a = a + _dot_nt(z, stack(z)) * lm_ref[lv]
    ps = []
    for si in range(sm_ref.shape[0]):
        sh = si if not bwd else si + 1
        if sh == 0:
            ps.append((q * k).astype(BF16))
            continue
        amt = sh if not bwd else C - sh
        kr = pltpu.roll(k, amt, 0)
        cr = pltpu.roll(cum, amt, 0)
        d = (cum - cr) if not bwd else (cr - cum)
        ps.append((q * kr * jnp.exp2(jnp.minimum(d, 0.0))).astype(BF16))
    sc = _dot(jnp.concatenate(ps, axis=0), hm)
    yield None
    for si in range(sm_ref.shape[0]):
        a = a + sc[si * C:(si + 1) * C] * sm_ref[si]

    vst = jnp.concatenate([v[:, h * GLA_DV:(h + 1) * GLA_DV] for h in range(H)], axis=0)
    st = st_ref[...]
    o4 = _dot(stack(a.astype(BF16)), vst) + _dot_nt(stack((q * e_q).astype(BF16)), st.astype(BF16))
    st_ref[...] = st * a_tot + _dot_tn(vst, stack((k * e_k).astype(BF16)))
    yield jnp.concatenate([o4[h * C:(h + 1) * C] for h in range(H)], axis=1)


def _gla_kernel(q_ref, k_ref, v_ref, la_ref, gr_ref, ng_ref,
                lqf_ref, lmf_ref, smf_ref, lqb_ref, lmb_ref, smb_ref, tri_ref, hm_ref,
                out_ref, of_ref, ob_ref, stf_ref, stb_ref, cf_ref, cb_ref):
    C = GLA_CHUNK
    S = q_ref.shape[0]
    kw = q_ref.shape[1]
    n = S // C
    stf_ref[...] = jnp.zeros_like(stf_ref)
    stb_ref[...] = jnp.zeros_like(stb_ref)

    def body(i, carry):
        streams, outs = [], []
        for u in range(GLA_UNROLL):
            rf = pl.ds(pl.multiple_of((i * GLA_UNROLL + u) * C, C), C)
            rb = pl.ds(pl.multiple_of((n - 1 - i * GLA_UNROLL - u) * C, C), C)
            streams.append(_gla_chunk(
                la_ref[rf, 0:kw], q_ref[rf, :].astype(F32), k_ref[rf, :].astype(F32),
                v_ref[rf, :], stf_ref, cf_ref.at[u], tri_ref, lqf_ref, lmf_ref, smf_ref, hm_ref,
                False))
            streams.append(_gla_chunk(
                la_ref[rb, kw:2 * kw], q_ref[rb, :].astype(F32), k_ref[rb, :].astype(F32),
                v_ref[rb, :], stb_ref, cb_ref.at[u], tri_ref, lqb_ref, lmb_ref, smb_ref, hm_ref,
                True))
            outs += [(of_ref, rf), (ob_ref, rb)]
        for results in zip(*streams):
            pass
        for (ref, rows), o in zip(outs, results):
            ref[rows, :] = o
        return carry

    lax.fori_loop(0, n // GLA_UNROLL, body, 0)

    rows = 256
    def fin(i, carry):
        r = pl.ds(pl.multiple_of(i * rows, rows), rows)
        o = of_ref[r, :] + ob_ref[r, :]
        ys = []
        for h in range(GLA_HEADS):
            oh = o[:, h * GLA_DV:(h + 1) * GLA_DV]
            ys.append(_rms(oh, ng_ref[...]))
        y = jnp.concatenate(ys, axis=1) * gr_ref[r, :].astype(F32)
        out_ref[r, :] = y.astype(BF16)
        return carry

    lax.fori_loop(0, S // rows, fin, 0)


def _gla(q, k, v, la, gr, ng, B, S):
    kw, vw = q.shape[1], v.shape[1]
    cf = _gla_constants(False)
    cb = _gla_constants(True)
    tri = np.tril(np.ones((GLA_CHUNK, GLA_CHUNK), np.float32))
    consts = [jnp.asarray(c) for c in cf] + [jnp.asarray(c) for c in cb] + \
             [jnp.asarray(tri, BF16), jnp.asarray(_gla_head_mask(), BF16)]
    seq = lambda n: pl.BlockSpec((S, n), lambda b: (b, 0))
    full = lambda a: pl.BlockSpec(a.shape, lambda b: (0,) * a.ndim)
    return pl.pallas_call(
        _gla_kernel,
        grid=(B,),
        in_specs=[seq(kw), seq(kw), seq(vw), seq(2 * kw), seq(vw), full(ng)] + [full(c) for c in consts],
        out_specs=seq(vw),
        out_shape=jax.ShapeDtypeStruct((B * S, vw), BF16),
        scratch_shapes=[pltpu.VMEM((S, vw), F32), pltpu.VMEM((S, vw), F32),
                        pltpu.VMEM((GLA_DV, kw), F32), pltpu.VMEM((GLA_DV, kw), F32),
                        pltpu.VMEM((GLA_UNROLL, GLA_CHUNK, kw), F32),
                        pltpu.VMEM((GLA_UNROLL, GLA_CHUNK, kw), F32)],
        compiler_params=_params(("parallel",)),
        name="gla",
    )(q, k, v, la, gr, ng, *consts)


def _conv_kernel(u_ref, w_ref, cb_ref, lg_ref, lb_ref, out_ref, *, tm, halo):
    S = u_ref.shape[0]
    j = pl.program_id(1)
    t0 = pl.multiple_of(j * tm, tm)
    top = u_ref[pl.ds(pl.multiple_of(jnp.maximum(t0 - halo, 0), halo), halo), :]
    bot = u_ref[pl.ds(pl.multiple_of(jnp.minimum(t0 + tm, S - halo), halo), halo), :]
    top = jnp.where(j > 0, top, 0.0)
    bot = jnp.where(j < pl.num_programs(1) - 1, bot, 0.0)
    win = jnp.concatenate([top, u_ref[pl.ds(t0, tm), :], bot], axis=0)
    rows = tm + 2 * halo
    pad = (CONV_KERNEL - 1) // 2
    w = w_ref[...]
    acc = jnp.zeros((tm, u_ref.shape[1]), F32)
    for r in range(8):
        offs = [k for k in range(CONV_KERNEL) if (halo - pad + k) % 8 == r]
        if not offs:
            continue
        xr = win if r == 0 else pltpu.roll(win, rows - r, 0)
        for k in offs:
            a0 = (halo - pad + k) - r
            acc = acc + xr[a0:a0 + tm, :] * w[k:k + 1, :]
    y = acc + cb_ref[...]
    mu = jnp.mean(y, axis=-1, keepdims=True)
    yc = y - mu
    var = jnp.mean(yc * yc, axis=-1, keepdims=True)
    z = yc * lax.rsqrt(var + LN_EPS) * lg_ref[...] + lb_ref[...]
    out_ref[...] = (z * jax.nn.sigmoid(z)).astype(BF16)


def _conv(u, w, cb, lg, lb, B, S, tm):
    cw = u.shape[1]
    halo = 16
    full = lambda a: pl.BlockSpec(a.shape, lambda b, j: (0,) * a.ndim)
    nj = S // tm
    return pl.pallas_call(
        functools.partial(_conv_kernel, tm=tm, halo=halo),
        grid=(B, nj),
        in_specs=[pl.BlockSpec((S, cw), lambda b, j: (b, 0)), full(w), full(cb), full(lg), full(lb)],
        out_specs=pl.BlockSpec((tm, cw), lambda b, j: (b * nj + j, 0)),
        out_shape=jax.ShapeDtypeStruct((B * S, cw), BF16),
        compiler_params=_params(("parallel", "parallel")),
        name="conv",
    )(u, w, cb, lg, lb)


def _memkv_kernel(m_ref, g_ref, wk_ref, wv_ref, k_ref, v_ref):
    m = _rms(m_ref[...], g_ref[...]).astype(BF16)
    k_ref[...] = _dot(m, wk_ref[...]).astype(BF16)
    v_ref[...] = _dot(m, wv_ref[...]).astype(BF16)


def _memkv(mem2d, g, wk, wv, M):
    R, D = mem2d.shape
    full = lambda a: pl.BlockSpec(a.shape, lambda i: (0,) * a.ndim)
    blk = pl.BlockSpec((M, D), lambda i: (i, 0))
    return pl.pallas_call(
        _memkv_kernel,
        grid=(R // M,),
        in_specs=[blk, full(g), full(wk), full(wv)],
        out_specs=[blk, blk],
        out_shape=[jax.ShapeDtypeStruct((R, D), BF16)] * 2,
        compiler_params=_params(("parallel",)),
        name="memkv",
    )(mem2d, g, wk, wv)


def _mid_kernel(x_ref, a_ref, b_ref, wout_ref, gx_ref, wq_ref, km_ref, vm_ref, wo_ref,
                gf_ref, rw_ref, rb_ref, x2_ref, h3_ref, idx_ref, gate_ref, *, n_exp, parts):
    n = x_ref.shape[0] // parts
    gens = [_mid_rows(pl.ds(p * n, n), pl.ds(p * n * SUBLANES, n * SUBLANES),
                      x_ref, a_ref, b_ref, wout_ref, gx_ref, wq_ref, km_ref, vm_ref, wo_ref,
                      gf_ref, rw_ref, rb_ref, x2_ref, h3_ref, idx_ref, gate_ref, n_exp)
            for p in range(parts)]
    for _ in zip(*gens):
        pass


def _mid_rows(rows, tiles, x_ref, a_ref, b_ref, wout_ref, gx_ref, wq_ref, km_ref, vm_ref, wo_ref,
              gf_ref, rw_ref, rb_ref, x2_ref, h3_ref, idx_ref, gate_ref, n_exp):
    ab = jnp.concatenate([a_ref[rows, :], b_ref[rows, :]], axis=1)
    x1 = x_ref[rows, :] + _dot(ab, wout_ref[...])
    yield
    q2 = _dot(_rms(x1, gx_ref[...]).astype(BF16), wq_ref[...])
    yield
    D = q2.shape[1]
    dh = D // XATTN_HEADS
    ss = []
    for h in range(XATTN_HEADS):
        sl = slice(h * dh, (h + 1) * dh)
        ss.append(_dot_nt(q2[:, sl].astype(BF16), km_ref[:, sl]) * (dh ** -0.5 * LOG2_E))
    yield
    outs = []
    for h in range(XATTN_HEADS):
        sl = slice(h * dh, (h + 1) * dh)
        p = jnp.exp2(ss[h] - jnp.max(ss[h], axis=-1, keepdims=True))
        p = p / jnp.sum(p, axis=-1, keepdims=True)
        outs.append(_dot(p.astype(BF16), vm_ref[:, sl]))
    yield
    o = jnp.concatenate(outs, axis=1).astype(BF16)
    x2 = x1 + _dot(o, wo_ref[...])
    yield
    x2_ref[rows, :] = x2
    h3 = _rms(x2, gf_ref[...])
    _store_rows_as_tiles(h3_ref.at[tiles], h3)
    lane = lax.broadcasted_iota(jnp.int32, (x2.shape[0], LANES), 1)
    logits = _dot(h3.astype(BF16), rw_ref[...]) + rb_ref[...]
    yield
    cur = jnp.where(lane < n_exp, logits, -jnp.inf)
    lane_f = lane.astype(F32)
    vals, idxs = [], []
    for _ in range(TOP_K):
        m = jnp.max(cur, axis=-1, keepdims=True)
        ik = jnp.min(jnp.where(cur == m, lane_f, float(LANES)), axis=-1, keepdims=True)
        vals.append(m)
        idxs.append(ik)
        cur = jnp.where(lane_f == ik, -jnp.inf, cur)
    es = [jnp.exp(vk - vals[0]) for vk in vals]
    den = es[0]
    for ek in es[1:]:
        den = den + ek
    idx_out = jnp.zeros(lane.shape, F32)
    gate_out = jnp.zeros(lane.shape, F32)
    for kk in range(TOP_K):
        idx_out = jnp.where(lane == kk, idxs[kk], idx_out)
        gate_out = jnp.where(lane == kk, es[kk] / den, gate_out)
    idx_ref[rows, :] = idx_out.astype(jnp.int32)
    gate_ref[:, rows] = gate_out.T[0:SUBLANES, :]
    yield


def _mid(x2d, a, b, wout, gx, wq, km, vm, wo, gf, rw, rb, S, M, n_exp, tm):
    T, D = x2d.shape
    half = a.shape[1]
    per_b = S // tm
    row = lambda n: pl.BlockSpec((tm, n), lambda i: (i, 0))
    full = lambda arr: pl.BlockSpec(arr.shape, lambda i: (0,) * arr.ndim)
    memb = pl.BlockSpec((M, D), lambda i: (i // per_b, 0))
    return pl.pallas_call(
        functools.partial(_mid_kernel, n_exp=n_exp, parts=2),
        grid=(T // tm,),
        in_specs=[row(D), row(half), row(half), full(wout), full(gx), full(wq), memb, memb,
                  full(wo), full(gf), full(rw), full(rb)],
        out_specs=[row(D), pl.BlockSpec((tm * SUBLANES, LANES), lambda i: (i, 0)),
                   row(LANES), pl.BlockSpec((SUBLANES, tm), lambda i: (i // 2, i % 2))],
        out_shape=[jax.ShapeDtypeStruct((T, D), F32),
                   jax.ShapeDtypeStruct((T * SUBLANES, LANES), F32),
                   jax.ShapeDtypeStruct((T, LANES), jnp.int32),
                   jax.ShapeDtypeStruct((T // (2 * tm) * SUBLANES, 2 * tm), F32)],
        compiler_params=_params(("parallel",)),
        name="mid",
    )(x2d, a, b, wout, gx, wq, km, vm, wo, gf, rw, rb)


def _place_kernel(idx_ref, tri_ref, upper_ref, pos_ref, cnt_ref):
    tm = idx_ref.shape[0] // 2
    for _ in zip(*[_place_tile(half, tm, idx_ref, tri_ref, upper_ref, pos_ref, cnt_ref)
                   for half in range(2)]):
        pass


def _place_tile(half, tm, idx_ref, tri_ref, upper_ref, pos_ref, cnt_ref):
    rows = pl.ds(half * tm, tm)
    lane = lax.broadcasted_iota(jnp.int32, (tm, LANES), 1)
    hots = [(lane == idx_ref[rows, kk:kk + 1]).astype(F32) for kk in range(TOP_K)]
    mask = hots[0]
    for hk in hots[1:]:
        mask = mask + hk
    incl = _dot(tri_ref[...], mask.astype(BF16))
    yield
    cnt = jnp.broadcast_to(incl[tm - 1:tm, :], (SUBLANES, LANES))
    hi = jnp.floor(cnt * (1.0 / 16.0))
    lo = cnt - 16.0 * hi
    start = 16.0 * _dot(hi.astype(BF16), upper_ref[...]) + _dot(lo.astype(BF16), upper_ref[...])
    yield
    before = incl - mask + start[0:1, :]
    out = jnp.zeros((tm, LANES), F32)
    for kk in range(TOP_K):
        rk = jnp.sum(hots[kk] * before, axis=-1, keepdims=True)
        out = jnp.where(lane == kk, rk, out)
    pos_ref[:, rows] = (out.T[0:SUBLANES, :] * SUBLANES).astype(jnp.int32)
    cnt_ref[pl.ds(half * SUBLANES, SUBLANES), :] = cnt
    yield


def _place(idx, tm):
    T = idx.shape[0]
    assert tm <= 16 * 256
    tri = jnp.asarray(np.tril(np.ones((tm, tm), np.float32)), BF16)
    upper = jnp.asarray(np.triu(np.ones((LANES, LANES), np.float32), 1), BF16)
    return pl.pallas_call(
        _place_kernel,
        grid=(T // (2 * tm),),
        in_specs=[pl.BlockSpec((2 * tm, LANES), lambda i: (i, 0)),
                  pl.BlockSpec((tm, tm), lambda i: (0, 0)),
                  pl.BlockSpec((LANES, LANES), lambda i: (0, 0))],
        out_specs=[pl.BlockSpec((SUBLANES, 2 * tm), lambda i: (i, 0)),
                   pl.BlockSpec((2 * SUBLANES, LANES), lambda i: (i, 0))],
        out_shape=[jax.ShapeDtypeStruct((T // (2 * tm) * SUBLANES, 2 * tm), jnp.int32),
                   jax.ShapeDtypeStruct((T // tm * SUBLANES, LANES), F32)],
        compiler_params=_params(("parallel",)),
        name="place",
    )(idx, tri, upper)


ROW_UNROLL = 16


def _wait_rows(buf_ref, sem):
    pltpu.make_async_copy(buf_ref, buf_ref, sem).wait()


RUN_CHUNK = 32


def _rows(ref, row, n):
    return ref.at[pl.ds(pl.multiple_of(row * SUBLANES, SUBLANES), n * SUBLANES)]


def _for_run_pieces(n, fn):
    shift = RUN_CHUNK.bit_length() - 1
    whole = lax.shift_right_logical(n, shift)

    def body(c, carry):
        fn(c * RUN_CHUNK, RUN_CHUNK)
        return carry

    lax.fori_loop(0, whole, body, 0)
    off = whole * RUN_CHUNK
    for bit in reversed(range(shift)):
        take = (n & (1 << bit)) != 0

        @pl.when(take)
        def _(off=off, bit=bit):
            fn(off, 1 << bit)

        off = off + jnp.where(take, 1 << bit, 0)


def _dispatch_kernel(pos_ref, rdst_ref, rn_ref, rloc_ref, lo_ref, hi_ref, nb_ref, h_ref, xs_ref,
                     sb0, sb1, zero_ref, sem, zsem):
    j = pl.program_id(0)
    tm = h_ref.shape[0] // SUBLANES // 2
    slots = rn_ref.shape[0] // 2
    n_exp = lo_ref.shape[0]
    blk = zero_ref.shape[0]

    def sort(half, sb):
        def body(i, carry):
            for u in range(ROW_UNROLL):
                t = half * tm + i * ROW_UNROLL + u
                tile = h_ref[pl.ds(pl.multiple_of(t * SUBLANES, SUBLANES), SUBLANES), :]
                for kk in range(TOP_K):
                    p = pos_ref[kk * 2 * tm + t]
                    sb[pl.ds(pl.multiple_of(p, SUBLANES), SUBLANES), :] = tile
            return carry

        lax.fori_loop(0, tm // ROW_UNROLL, body, 0)

    def start_runs(half, sb, s):
        def expert_pair(e2, carry):
            for prio in range(2):
                r = half * slots + 2 * e2 + prio
                loc, dst = rloc_ref[r], rdst_ref[r]
                _for_run_pieces(rn_ref[r], lambda off, n, loc=loc, dst=dst, prio=prio:
                                pltpu.make_async_copy(_rows(sb, loc + off, n),
                                                      _rows(xs_ref, dst + off, n),
                                                      s).start(priority=prio))
            return carry

        lax.fori_loop(0, (n_exp + 1) // 2, expert_pair, 0)

    @pl.when(j == 0)
    def _():
        zero_ref[...] = jnp.zeros_like(zero_ref)

        def each_fill(fn):
            def expert(e, carry):
                row = lo_ref[e]
                n = hi_ref[e] - row
                for bit in reversed(range(MOE_ROWS.bit_length() - 1)):
                    size = (1 << bit) * SUBLANES
                    take = (n & (1 << bit)) != 0

                    @pl.when(take)
                    def _(row=row, size=size):
                        dst = xs_ref.at[pl.ds(pl.multiple_of(row * SUBLANES, SUBLANES), size)]
                        fn(pltpu.make_async_copy(zero_ref.at[pl.ds(0, size)], dst, zsem))

                    row = row + jnp.where(take, 1 << bit, 0)
                return carry

            lax.fori_loop(0, lo_ref.shape[0], expert, 0)

            def block(b, carry):
                fn(pltpu.make_async_copy(
                    zero_ref, xs_ref.at[pl.ds(pl.multiple_of(b * blk, blk), blk)], zsem))
                return carry

            lax.fori_loop(nb_ref[0], xs_ref.shape[0] // blk, block, 0)

        each_fill(lambda cp: cp.start())
        each_fill(lambda cp: cp.wait())

    for half, sb in ((0, sb0), (1, sb1)):
        @pl.when(j > 0)
        def _(half=half, sb=sb):
            _wait_rows(sb, sem.at[half])

        sort(half, sb)
        start_runs(half, sb, sem.at[half])

    @pl.when(j == pl.num_programs(0) - 1)
    def _():
        _wait_rows(sb0, sem.at[0])
        _wait_rows(sb1, sem.at[1])


def _dispatch(pos, run_dst, run_n, run_loc, pad_lo, pad_hi, n_used, h3, n_rows, tm):
    T = h3.shape[0] // SUBLANES
    slots = run_n.shape[0] // (T // tm)
    smem = pl.BlockSpec(memory_space=pltpu.SMEM)
    per_step = lambda n: pl.BlockSpec((n,), lambda j: (j,), memory_space=pltpu.SMEM)
    sorted_rows = pltpu.VMEM((tm * TOP_K * SUBLANES, LANES), F32)
    return pl.pallas_call(
        _dispatch_kernel,
        grid=(T // (2 * tm),),
        in_specs=[per_step(SUBLANES * 2 * tm),
                  per_step(2 * slots), per_step(2 * slots),
                  per_step(2 * slots), smem, smem, smem,
                  pl.BlockSpec((2 * tm * SUBLANES, LANES), lambda j: (j, 0))],
        out_specs=pl.BlockSpec(memory_space=pl.ANY),
        out_shape=jax.ShapeDtypeStruct((n_rows * SUBLANES, LANES), F32),
        scratch_shapes=[sorted_rows, sorted_rows, pltpu.VMEM((MOE_ROWS * SUBLANES, LANES), F32),
                        pltpu.SemaphoreType.DMA((2,)), pltpu.SemaphoreType.DMA(())],
        compiler_params=_params(("arbitrary",)),
        name="dispatch",
    )(pos, run_dst, run_n, run_loc, pad_lo, pad_hi, n_used, h3)


def _experts_kernel(be_ref, live_ref, first_ref, next_ref, slot_ref,
                    xs_ref, wu_hbm, bu_ref, wd_hbm, bd_ref, ys_ref,
                    wu_f32, wd_f32, wu_bf, wd_bf, sem):
    i = pl.program_id(0)

    def fetch(e, s):
        return (pltpu.make_async_copy(wu_hbm.at[e], wu_f32.at[s], sem.at[s]),
                pltpu.make_async_copy(wd_hbm.at[e], wd_f32.at[s], sem.at[s]))

    @pl.when(i == 0)
    def _():
        for cp in fetch(be_ref[0], 0):
            cp.start()

    @pl.when(first_ref[i] == 1)
    def _():
        s = slot_ref[i]
        for cp in fetch(be_ref[i], s):
            cp.wait()

        @pl.when(next_ref[i] >= 0)
        def _():
            for cp in fetch(next_ref[i], 1 - s):
                cp.start()

        wu_bf[...] = wu_f32[s].astype(BF16)
        wd_bf[...] = wd_f32[s].astype(BF16)

    def mlp(tiles):
        F = wd_bf.shape[0]
        x = _load_rows_from_tiles(xs_ref.at[tiles]).astype(BF16)
        gu = _dot(x, wu_bf[...]) + bu_ref[0]
        yield
        x_glu = jnp.minimum(gu[:, :F], SWIGLU_LIMIT)
        x_lin = jnp.clip(gu[:, F:], -SWIGLU_LIMIT, SWIGLU_LIMIT)
        hid = x_glu * jax.nn.sigmoid(SWIGLU_ALPHA * x_glu) * (x_lin + 1.0)
        y = _dot(hid.astype(BF16), wd_bf[...]) + bd_ref[0]
        yield
        _store_rows_as_tiles(ys_ref.at[tiles], y)
        yield

    n = xs_ref.shape[0] // MOE_PARTS
    for live in range(MOE_PARTS + 1):
        @pl.when(live_ref[i] == live)
        def _(live=live):
            for _ in zip(*[mlp(pl.ds(p * n, n)) for p in range(live)]):
                pass
            if live < MOE_PARTS:
                rest = (MOE_PARTS - live) * n
                ys_ref[pl.ds(live * n, rest), :] = jnp.zeros((rest, ys_ref.shape[1]), F32)


def _expert_runs(block_e):
    nb = block_e.shape[0]
    ar = jnp.arange(nb, dtype=jnp.int32)
    first = jnp.concatenate([jnp.ones((1,), bool), block_e[1:] != block_e[:-1]])
    starts = jnp.where(first, ar, nb)
    later = lax.cummin(jnp.concatenate([starts[1:], jnp.full((1,), nb, jnp.int32)]), reverse=True)
    nxt = jnp.where(later < nb, block_e[jnp.minimum(later, nb - 1)], -1)
    slot = (jnp.cumsum(first.astype(jnp.int32)) - 1) & 1
    return first.astype(jnp.int32), nxt.astype(jnp.int32), slot.astype(jnp.int32)


def _experts(block_e, live, xs, wu, bu, wd, bd):
    E, D, F2 = wu.shape
    F = wd.shape[1]
    nb = xs.shape[0] // (MOE_ROWS * SUBLANES)
    rows = pl.BlockSpec((MOE_ROWS * SUBLANES, LANES), lambda i, *_: (i, 0))
    grid_spec = pltpu.PrefetchScalarGridSpec(
        num_scalar_prefetch=5,
        grid=(nb,),
        in_specs=[rows,
                  pl.BlockSpec(memory_space=pl.ANY),
                  pl.BlockSpec((1, 1, F2), lambda i, be, *_: (be[i], 0, 0)),
                  pl.BlockSpec(memory_space=pl.ANY),
                  pl.BlockSpec((1, 1, D), lambda i, be, *_: (be[i], 0, 0))],
        out_specs=rows,
        scratch_shapes=[pltpu.VMEM((2, D, F2), F32), pltpu.VMEM((2, F, D), F32),
                        pltpu.VMEM((D, F2), BF16), pltpu.VMEM((F, D), BF16),
                        pltpu.SemaphoreType.DMA((2,))],
    )
    return pl.pallas_call(
        _experts_kernel,
        grid_spec=grid_spec,
        out_shape=jax.ShapeDtypeStruct(xs.shape, F32),
        compiler_params=_params(("arbitrary",)),
        name="experts",
    )(block_e, live, *_expert_runs(block_e), xs, wu, bu, wd, bd)


def _combine_kernel(pos_ref, gate_ref, src_ref, rn_ref, rloc_ref, nsrc_ref, nrn_ref, nrloc_ref,
                    x2_ref, g_ref, ys_ref, out_ref, yb0, yb1, acc_ref, sem, *, n_exp, apply_norm):
    j = pl.program_id(0)
    tm = x2_ref.shape[0] // 2
    slots = rn_ref.shape[0] // 2

    def start(tbls, half, yb, s):
        src_t, n_t, loc_t = tbls

        def expert_pair(e2, carry):
            for prio in range(2):
                r = half * slots + 2 * e2 + prio
                loc, src = loc_t[r], src_t[r]
                _for_run_pieces(n_t[r], lambda off, n, loc=loc, src=src, prio=prio:
                                pltpu.make_async_copy(_rows(ys_ref, src + off, n),
                                                      _rows(yb, loc + off, n),
                                                      s).start(priority=prio))
            return carry

        lax.fori_loop(0, (n_exp + 1) // 2, expert_pair, 0)

    def finish(half, yb, s):
        _wait_rows(yb, s)

        def body(i, carry):
            for u in range(ROW_UNROLL):
                t = i * ROW_UNROLL + u
                a = half * tm + t
                acc = None
                for kk in range(TOP_K):
                    p = pos_ref[kk * 2 * tm + a]
                    term = yb[pl.ds(pl.multiple_of(p, SUBLANES), SUBLANES), :] \
                        * gate_ref[kk * 2 * tm + a]
                    acc = term if acc is None else acc + term
                acc_ref[pl.ds(pl.multiple_of(t * SUBLANES, SUBLANES), SUBLANES), :] = acc
            return carry

        lax.fori_loop(0, tm // ROW_UNROLL, body, 0)
        rows = pl.ds(half * tm, tm)
        x = x2_ref[rows, :] + _load_rows_from_tiles(acc_ref)
        out_ref[rows, :] = _rms(x, g_ref[...]) if apply_norm else x

    cur = (src_ref, rn_ref, rloc_ref)

    @pl.when(j == 0)
    def _():
        start(cur, 0, yb0, sem.at[0])

    start(cur, 1, yb1, sem.at[1])
    finish(0, yb0, sem.at[0])

    @pl.when(j < pl.num_programs(0) - 1)
    def _():
        start((nsrc_ref, nrn_ref, nrloc_ref), 0, yb0, sem.at[0])

    finish(1, yb1, sem.at[1])


def _combine(pos, gates, run_src, run_n, run_loc, x2, g, ys, tm, n_exp, apply_norm):
    T, D = x2.shape
    steps = T // (2 * tm)
    slots = run_n.shape[0] // (T // tm)
    per_step = lambda n: pl.BlockSpec((n,), lambda j: (j,), memory_space=pltpu.SMEM)
    next_step = lambda n: pl.BlockSpec((n,), lambda j: (jnp.minimum(j + 1, steps - 1),),
                                       memory_space=pltpu.SMEM)
    sorted_rows = pltpu.VMEM((tm * TOP_K * SUBLANES, LANES), F32)
    return pl.pallas_call(
        functools.partial(_combine_kernel, n_exp=n_exp, apply_norm=apply_norm),
        grid=(steps,),
        in_specs=[per_step(SUBLANES * 2 * tm),
                  per_step(SUBLANES * 2 * tm),
                  per_step(2 * slots), per_step(2 * slots), per_step(2 * slots),
                  next_step(2 * slots), next_step(2 * slots), next_step(2 * slots),
                  pl.BlockSpec((2 * tm, D), lambda j: (j, 0)),
                  pl.BlockSpec(g.shape, lambda j: (0, 0)),
                  pl.BlockSpec(memory_space=pl.ANY)],
        out_specs=pl.BlockSpec((2 * tm, D), lambda j: (j, 0)),
        out_shape=jax.ShapeDtypeStruct((T, D), F32),
        scratch_shapes=[sorted_rows, sorted_rows, pltpu.VMEM((tm * SUBLANES, LANES), F32),
                        pltpu.SemaphoreType.DMA((2,))],
        compiler_params=_params(("arbitrary",)),
        name="combine",
    )(pos, gates, run_src, run_n, run_loc, run_src, run_n, run_loc, x2, g, ys)


def _row(v):
    return v.reshape(1, -1).astype(F32)


def kernel(x, mem, norm_mix_g, w_in, gate_w2_fwd, gate_b_fwd, gate_w2_bwd, gate_b_bwd,
           gla_norm_g, conv_w, conv_b, conv_ln_g, conv_ln_b, w_out, norm_xattn_g,
           norm_mem_g, xattn_wq, xattn_wk, xattn_wv, xattn_wo, norm_ffn_g, router_w,
           router_b, exp_w_up, exp_b_up, exp_w_down, exp_b_down, final_norm_g):
    B, S, D = x.shape
    M = mem.shape[1]
    T = B * S
    depth = w_in.shape[0]
    kw = GLA_HEADS * GLA_DK
    vw = GLA_HEADS * GLA_DV
    rank_w = gate_w2_fwd.shape[1]
    cw = conv_w.shape[2]
    E = router_w.shape[2]
    assert vw + cw == w_out.shape[1] and 2 * rank_w <= LANES and E <= LANES

    xc = x.reshape(T, D)
    for l in range(depth):
        o_gf = 2 * kw + 2 * vw
        o_glu = o_gf + 2 * rank_w
        wl = w_in[l]
        w_gate = jnp.pad(wl[:, o_gf:o_glu], ((0, 0), (0, LANES - 2 * rank_w)))
        ws = [wl[:, :o_gf].astype(BF16), wl[:, o_glu:].astype(BF16), w_gate.astype(BF16)]
        w2 = jnp.zeros((LANES, 2 * kw), F32)
        w2 = w2.at[:rank_w, :kw].set(gate_w2_fwd[l]).at[rank_w:2 * rank_w, kw:].set(gate_w2_bwd[l])
        b2 = jnp.concatenate([gate_b_fwd[l], gate_b_bwd[l]]).reshape(1, -1)

        q, k, v, gr, la, u = _inproj(xc, _row(norm_mix_g[l]), ws, w2.astype(BF16), b2,
                                     kw, vw, cw, tm=512)
        a_out = _gla(q, k, v, la, gr, _row(gla_norm_g[l]), B, S)
        b_out = _conv(u, conv_w[l], _row(conv_b[l]), _row(conv_ln_g[l]), _row(conv_ln_b[l]),
                      B, S, tm=512)
        km, vm = _memkv(mem.reshape(B * M, D), _row(norm_mem_g[l]),
                        xattn_wk[l].astype(BF16), xattn_wv[l].astype(BF16), M)

        rw = jnp.zeros((D, LANES), F32).at[:, :E].set(router_w[l]).astype(BF16)
        rb = jnp.zeros((1, LANES), F32).at[0, :E].set(router_b[l])
        x2, h3, idx, gates = _mid(
            xc, a_out, b_out, w_out[l].astype(BF16), _row(norm_xattn_g[l]),
            xattn_wq[l].astype(BF16), km, vm, xattn_wo[l].astype(BF16),
            _row(norm_ffn_g[l]), rw, rb, S, M, E, tm=MOE_TILE)

        tm = MOE_TILE
        pos, tile_cnt = _place(idx, tm=tm)
        n_te = tile_cnt.reshape(T // tm, SUBLANES, LANES)[:, 0, :E].astype(jnp.int32)
        counts = jnp.sum(n_te, axis=0)
        padded = ((counts + MOE_ROWS - 1) // MOE_ROWS) * MOE_ROWS
        pends = jnp.cumsum(padded)
        run_row = (pends - padded)[None, :] + jnp.cumsum(n_te, axis=0) - n_te
        run_loc = jnp.cumsum(n_te, axis=1) - n_te
        slots = max(E, LANES // 2)
        runs = [jnp.pad(a, ((0, 0), (0, slots - E))).reshape(-1).astype(jnp.int32)
                for a in (run_row, n_te, run_loc)]
        n_blocks = (T * TOP_K + E * (MOE_ROWS - 1)) // MOE_ROWS
        blk_start = jnp.arange(n_blocks, dtype=jnp.int32) * MOE_ROWS
        block_e = jnp.minimum(jnp.sum(pends[None, :] <= blk_start[:, None], axis=1), E - 1)
        n_used = (pends[E - 1] // MOE_ROWS).reshape(1)

        pos, gates = pos.reshape(-1), gates.reshape(-1)
        xs = _dispatch(pos, *runs, (pends - padded + counts).astype(jnp.int32),
                       pends.astype(jnp.int32), n_used.astype(jnp.int32), h3,
                       n_blocks * MOE_ROWS, tm=tm)
        group = MOE_ROWS // MOE_PARTS
        real = jnp.clip((pends - padded + counts)[block_e] - blk_start, 0, MOE_ROWS)
        live = jnp.where(blk_start < pends[E - 1], (real + group - 1) // group, 0)
        ys = _experts(block_e.astype(jnp.int32), live.astype(jnp.int32), xs,
                      exp_w_up[l], exp_b_up[l][:, None, :],
                      exp_w_down[l], exp_b_down[l][:, None, :])
        xc = _combine(pos, gates, *runs, x2,
                      _row(final_norm_g), ys, tm=tm, n_exp=E, apply_norm=(l == depth - 1))
    return xc.reshape(B, S, D)
```
